```python
import math
import jax, jax.numpy as jnp
from jax import lax
import numpy as np

D_MODEL = 1024
BATCH = 16
SEQ = 2048
DEPTH = 4

N_A_LAYERS = DEPTH // 2
N_B_LAYERS = DEPTH - N_A_LAYERS
PLE_DIM = 256
SSM_DIM = D_MODEL
SSM_GROUP_DIM = 16
SSM_GROUPS = SSM_DIM // SSM_GROUP_DIM
SSM_STATE = 64
SSM_CHUNK = 128
N_HEADS = 8
HEAD_DIM = 64
Q_BLOCK = 128
REL_BUCKETS = 32
REL_MAX_EXACT = REL_BUCKETS // 2
REL_MAX_DIST = 128
D_FF = 2816
CONV_WIDTH = 3
NEG_INF = -1e30
EPS = 1e-6

kernel_name = "yoco_s5_diffattn_convffn_hybrid"


def rms_norm(x, g):
    xf = x.astype(jnp.float32)
    y = xf * lax.rsqrt(jnp.mean(xf * xf, axis=-1, keepdims=True) + EPS)
    return (y * g.astype(jnp.float32)).astype(x.dtype)


def _ssm_combine(e_i, e_j):
    ar_i, ai_i, br_i, bi_i = e_i
    ar_j, ai_j, br_j, bi_j = e_j
    return (ar_j * ar_i - ai_j * ai_i,
            ar_j * ai_i + ai_j * ar_i,
            ar_j * br_i - ai_j * bi_i + br_j,
            ar_j * bi_i + ai_j * br_i + bi_j)


def s5_mixer(h, w_in, log_dt, lam_re, lam_im, b_re, b_im, c_re, c_im, d_skip, w_glu):
    f32 = jnp.float32
    bsz, seqlen, _ = h.shape
    u = h @ w_in
    dt = jnp.exp(log_dt.astype(f32))[:, None]
    lr = lam_re.astype(f32)
    li = lam_im.astype(f32)
    mag = jnp.exp(lr * dt)
    ab_r = mag * jnp.cos(li * dt)
    ab_i = mag * jnp.sin(li * dt)
    den = lr * lr + li * li
    f_r = ((ab_r - 1.0) * lr + ab_i * li) / den
    f_i = (ab_i * lr - (ab_r - 1.0) * li) / den
    br = b_re.astype(f32)
    bi = b_im.astype(f32)
    bb_r = f_r[..., None] * br - f_i[..., None] * bi
    bb_i = f_r[..., None] * bi + f_i[..., None] * br
    cr = c_re.astype(f32)
    ci = c_im.astype(f32)
    n_chunks = seqlen // SSM_CHUNK
    uc = u.astype(f32).reshape(bsz, n_chunks, SSM_CHUNK, SSM_GROUPS, SSM_GROUP_DIM)
    uc = uc.transpose(1, 2, 0, 3, 4)
    a_shape = (SSM_CHUNK, bsz, SSM_GROUPS, SSM_STATE)
    a_r = jnp.broadcast_to(ab_r, a_shape)
    a_i = jnp.broadcast_to(ab_i, a_shape)

    def chunk_step(carry, u_chunk):
        x_r0, x_i0 = carry
        bu_r = jnp.einsum("tbgc,gpc->tbgp", u_chunk, bb_r)
        bu_i = jnp.einsum("tbgc,gpc->tbgp", u_chunk, bb_i)
        pw_r, pw_i, s_r, s_i = lax.associative_scan(
            _ssm_combine, (a_r, a_i, bu_r, bu_i), axis=0)
        x_r = s_r + pw_r * x_r0 - pw_i * x_i0
        x_i = s_i + pw_r * x_i0 + pw_i * x_r0
        y = (jnp.einsum("tbgp,gcp->tbgc", x_r, cr)
             - jnp.einsum("tbgp,gcp->tbgc", x_i, ci))
        return (x_r[-1], x_i[-1]), y

    zeros = jnp.zeros((bsz, SSM_GROUPS, SSM_STATE), f32)
    _, ys = lax.scan(chunk_step, (zeros, zeros), uc)
    y = ys.transpose(2, 0, 1, 3, 4).reshape(bsz, seqlen, SSM_DIM)
    y = y + d_skip.astype(f32) * u.astype(f32)
    g = jax.nn.gelu(y).astype(h.dtype)
    val, gate = jnp.split(g @ w_glu, 2, axis=-1)
    return val * jax.nn.sigmoid(gate)


def t5_causal_bucket(rel):
    n = jnp.maximum(-rel, 0)
    nf = jnp.maximum(n, REL_MAX_EXACT).astype(jnp.float32)
    large = REL_MAX_EXACT + (jnp.log(nf / REL_MAX_EXACT)
                             / math.log(REL_MAX_DIST / REL_MAX_EXACT)
                             * (REL_BUCKETS - REL_MAX_EXACT)).astype(jnp.int32)
    large = jnp.minimum(large, REL_BUCKETS - 1)
    return jnp.where(n < REL_MAX_EXACT, n, large)


def shared_kv(x, kv_norm, kv_w, k_norm):
    bsz, seqlen, _ = x.shape
    kv = rms_norm(x, kv_norm) @ kv_w
    nk = 2 * N_HEADS * HEAD_DIM
    k = rms_norm(kv[..., :nk].reshape(bsz, seqlen, N_HEADS, 2, HEAD_DIM), k_norm)
    v = kv[..., nk:].reshape(bsz, seqlen, N_HEADS, 2 * HEAD_DIM)
    return k, v


def diff_attention(h, k, v, rel_bias, w_q, q_norm, lq1, lk1, lq2, lk2, subln, w_o, lam_init):
    f32 = jnp.float32
    bsz, seqlen, _ = h.shape
    q = rms_norm((h @ w_q).reshape(bsz, seqlen, N_HEADS, 2, HEAD_DIM), q_norm)
    lam = (jnp.exp(jnp.sum(lq1.astype(f32) * lk1.astype(f32)))
           - jnp.exp(jnp.sum(lq2.astype(f32) * lk2.astype(f32))) + lam_init)
    scale = HEAD_DIM ** -0.5
    n_blocks = seqlen // Q_BLOCK
    qb = q.reshape(bsz, n_blocks, Q_BLOCK, N_HEADS, 2, HEAD_DIM).transpose(1, 0, 2, 3, 4, 5)
    k_pos = jnp.arange(seqlen)

    def block(args):
        idx, qc = args
        q_pos = idx * Q_BLOCK + jnp.arange(Q_BLOCK)
        rel = k_pos[None, :] - q_pos[:, None]
        bias = rel_bias[t5_causal_bucket(rel)].astype(f32).transpose(2, 0, 1)
        s = jnp.einsum("bqhcd,bkhcd->bhcqk", qc, k).astype(f32) * scale
        s = s + bias[None, :, None]
        s = jnp.where(rel <= 0, s, NEG_INF)
        pm = jax.nn.softmax(s, axis=-1)
        attn = pm[:, :, 0] - lam * pm[:, :, 1]
        return jnp.einsum("bhqk,bkhe->bqhe", attn.astype(v.dtype), v)

    o = lax.map(block, (jnp.arange(n_blocks), qb))
    o = o.transpose(1, 0, 2, 3, 4).reshape(bsz, seqlen, N_HEADS, 2 * HEAD_DIM)
    o = rms_norm(o, subln) * (1.0 - lam_init)
    return o.reshape(bsz, seqlen, N_HEADS * 2 * HEAD_DIM) @ w_o


def conv_ffn(h, w_up, conv_w, conv_b, w_down):
    up = h @ w_up
    seqlen = up.shape[1]
    padded = jnp.pad(up, ((0, 0), (CONV_WIDTH - 1, 0), (0, 0)))
    c = conv_b
    for j in range(CONV_WIDTH):
        c = c + conv_w[j] * padded[:, j:j + seqlen]
    g, val = jnp.split(c, 2, axis=-1)
    return (jax.nn.gelu(g) * val) @ w_down


def setup_inputs(seed: int = 0) -> dict:
    key = jax.random.key(seed)
    ks = iter(jax.random.split(key, 40))
    f32 = jnp.float32

    def nrm(shape, scale):
        return jax.random.normal(next(ks), shape, f32) * scale

    def gain(shape):
        return 1.0 + nrm(shape, 0.05)

    na, nb = N_A_LAYERS, N_B_LAYERS
    lam_im = jnp.pi * jnp.arange(SSM_STATE, dtype=f32)
    return {
        "x": nrm((BATCH, SEQ, D_MODEL), 1.0),
        "p": nrm((DEPTH, BATCH, SEQ, PLE_DIM), 1.0),
        "ssm_norm": gain((na, D_MODEL)),
        "ssm_w_in": nrm((na, D_MODEL, SSM_DIM), D_MODEL ** -0.5),
        "ssm_log_dt": jax.random.uniform(next(ks), (na, SSM_GROUPS), f32,
                                         math.log(1e-3), math.log(1e-1)),
        "ssm_lambda_re": -0.5 + nrm((na, SSM_GROUPS, SSM_STATE), 0.01),
        "ssm_lambda_im": lam_im + nrm((na, SSM_GROUPS, SSM_STATE), 0.01),
        "ssm_b_re": nrm((na, SSM_GROUPS, SSM_STATE, SSM_GROUP_DIM), SSM_GROUP_DIM ** -0.5),
        "ssm_b_im": nrm((na, SSM_GROUPS, SSM_STATE, SSM_GROUP_DIM), SSM_GROUP_DIM ** -0.5),
        "ssm_c_re": nrm((na, SSM_GROUPS, SSM_GROUP_DIM, SSM_STATE), SSM_STATE ** -0.5),
        "ssm_c_im": nrm((na, SSM_GROUPS, SSM_GROUP_DIM, SSM_STATE), SSM_STATE ** -0.5),
        "ssm_d": nrm((na, SSM_DIM), 0.5),
        "ssm_w_glu": nrm((na, SSM_DIM, 2 * D_MODEL), SSM_DIM ** -0.5),
        "kv_norm": gain((D_MODEL,)),
        "kv_w": nrm((D_MODEL, 2 * N_HEADS * HEAD_DIM + N_HEADS * 2 * HEAD_DIM), D_MODEL ** -0.5),
        "k_norm": gain((HEAD_DIM,)),
        "attn_norm": gain((nb, D_MODEL)),
        "attn_w_q": nrm((nb, D_MODEL, 2 * N_HEADS * HEAD_DIM), D_MODEL ** -0.5),
        "q_norm": gain((nb, HEAD_DIM)),
        "lambda_q1": nrm((nb, HEAD_DIM), 0.1),
        "lambda_k1": nrm((nb, HEAD_DIM), 0.1),
        "lambda_q2": nrm((nb, HEAD_DIM), 0.1),
        "lambda_k2": nrm((nb, HEAD_DIM), 0.1),
        "subln": gain((nb, 2 * HEAD_DIM)),
        "attn_w_o": nrm((nb, N_HEADS * 2 * HEAD_DIM, D_MODEL), (N_HEADS * 2 * HEAD_DIM) ** -0.5),
        "rel_bias": nrm((REL_BUCKETS, N_HEADS), 0.5),
        "ffn_norm": gain((DEPTH, D_MODEL)),
        "ffn_w_up": nrm((DEPTH, D_MODEL, 2 * D_FF), D_MODEL ** -0.5),
        "ffn_conv_w": nrm((DEPTH, CONV_WIDTH, 2 * D_FF), CONV_WIDTH ** -0.5),
        "ffn_conv_b": nrm((DEPTH, 2 * D_FF), 0.02),
        "ffn_w_down": nrm((DEPTH, D_FF, D_MODEL), D_FF ** -0.5),
        "ple_norm": gain((DEPTH, D_MODEL)),
        "ple_w_gate": nrm((DEPTH, D_MODEL, D_MODEL), D_MODEL ** -0.5),
        "ple_w_proj": nrm((DEPTH, PLE_DIM, D_MODEL), PLE_DIM ** -0.5),
    }


def reference(x, p, ssm_norm, ssm_w_in, ssm_log_dt, ssm_lambda_re, ssm_lambda_im,
              ssm_b_re, ssm_b_im, ssm_c_re, ssm_c_im, ssm_d, ssm_w_glu,
              kv_norm, kv_w, k_norm, attn_norm, attn_w_q, q_norm,
              lambda_q1, lambda_k1, lambda_q2, lambda_k2, subln, attn_w_o, rel_bias,
              ffn_norm, ffn_w_up, ffn_conv_w, ffn_conv_b, ffn_w_down,
              ple_norm, ple_w_gate, ple_w_proj):
    k = v = None
    for i in range(DEPTH):
        if i < N_A_LAYERS:
            h = rms_norm(x, ssm_norm[i])
            x = x + s5_mixer(h, ssm_w_in[i], ssm_log_dt[i], ssm_lambda_re[i], ssm_lambda_im[i],
                             ssm_b_re[i], ssm_b_im[i], ssm_c_re[i], ssm_c_im[i],
                             ssm_d[i], ssm_w_glu[i])
        else:
            if i == N_A_LAYERS:
                k, v = shared_kv(x, kv_norm, kv_w, k_norm)
            j = i - N_A_LAYERS
            lam_init = 0.8 - 0.6 * math.exp(-0.3 * i)
            h = rms_norm(x, attn_norm[j])
            x = x + diff_attention(h, k, v, rel_bias, attn_w_q[j], q_norm[j],
                                   lambda_q1[j], lambda_k1[j], lambda_q2[j], lambda_k2[j],
                                   subln[j], attn_w_o[j], lam_init)
        h = rms_norm(x, ffn_norm[i])
        x = x + conv_ffn(h, ffn_w_up[i], ffn_conv_w[i], ffn_conv_b[i], ffn_w_down[i])
        gate = jax.nn.sigmoid(rms_norm(x, ple_norm[i]) @ ple_w_gate[i])
        x = x + gate * (p[i] @ ple_w_proj[i])
    return x
```

```python
import functools
import math

import jax
import jax.numpy as jnp
from jax import lax
from jax.experimental import pallas as pl
from jax.experimental.pallas import tpu as pltpu

EPS = 1e-6
NEG_INF = -1e30

SSM_GROUP_DIM = 16
SSM_STATE = 64
N_HEADS = 8
HEAD_DIM = 64
REL_BUCKETS = 32
REL_MAX_EXACT = REL_BUCKETS // 2
REL_MAX_DIST = 128
CONV_WIDTH = 3

LANES = 128
SUBLANES = 8
MXU_DIM = 256
VMEM_LIMIT_BYTES = 56 * 1024 * 1024

SSM_T = MXU_DIM // SSM_GROUP_DIM
SEG_CHUNK = MXU_DIM

bf16 = jnp.bfloat16
f32 = jnp.float32


def _cparams(*sem):
    return pltpu.CompilerParams(dimension_semantics=sem, vmem_limit_bytes=VMEM_LIMIT_BYTES)


def _rms(xf, g):
    return xf * lax.rsqrt(jnp.mean(xf * xf, axis=-1, keepdims=True) + EPS) * g


def _seg_norm(y, seg_ref, gain_ref):
    cols = []
    for c in range(y.shape[1] // SEG_CHUNK):
        yc = y[:, c * SEG_CHUNK:(c + 1) * SEG_CHUNK]
        ss = jnp.dot((yc * yc).astype(bf16), seg_ref[...], preferred_element_type=f32)
        cols.append(yc * lax.rsqrt(ss * (1.0 / HEAD_DIM) + EPS) * gain_ref[...])
    return jnp.concatenate(cols, axis=1)


def _norm_proj_kernel(n_seg_cols, x_ref, g_ref, w_ref, seg_ref, gain_ref, *out_refs):
    h = _rms(x_ref[...], g_ref[...]).astype(bf16)
    y = jnp.dot(h, w_ref[...], preferred_element_type=f32)
    width = out_refs[0].shape[1]
    for i, o_ref in enumerate(out_refs):
        yi = y[:, i * width:(i + 1) * width]
        if (i + 1) * width <= n_seg_cols:
            yi = _seg_norm(yi, seg_ref, gain_ref)
        o_ref[...] = yi.astype(o_ref.dtype)


def _norm_proj(x, g, w, seg_gain, n_seg_cols, n_out, tm=512):
    n, d = x.shape
    width = w.shape[1] // n_out
    seg = (jnp.arange(SEG_CHUNK)[:, None] // HEAD_DIM == jnp.arange(SEG_CHUNK)[None, :] // HEAD_DIM)
    gain = jnp.tile(seg_gain.astype(f32), SEG_CHUNK // HEAD_DIM)[None, :]
    const = lambda i: (0, 0)
    outs = pl.pallas_call(
        functools.partial(_norm_proj_kernel, n_seg_cols),
        grid=(n // tm,),
        in_specs=[pl.BlockSpec((tm, d), lambda i: (i, 0)),
                  pl.BlockSpec((1, d), const),
                  pl.BlockSpec(w.shape, const),
                  pl.BlockSpec((SEG_CHUNK, SEG_CHUNK), const),
                  pl.BlockSpec((1, SEG_CHUNK), const)],
        out_specs=[pl.BlockSpec((tm, width), lambda i: (i, 0))] * n_out,
        out_shape=[jax.ShapeDtypeStruct((n, width), bf16)] * n_out,
        compiler_params=_cparams("parallel"),
        name="norm_proj",
    )(x, g.astype(f32)[None, :], w, seg.astype(bf16), gain)
    return outs


def _ssm_kernel(n_rounds, rows_per_chunk, u_ref, mt_ref, w_ref, v_ref, pw_ref, y_ref):
    u0 = u_ref[0]
    u1 = u_ref[1]
    n_rows = u0.shape[0]
    half = LANES
    s = (jnp.dot(u0, w_ref[0], preferred_element_type=f32)
         + jnp.dot(u1, w_ref[1], preferred_element_type=f32))
    xr = s[:, :half]
    xi = s[:, half:]
    for j in range(n_rounds):
        d = rows_per_chunk << j
        ar = pw_ref[j, 0:1, :]
        ai = pw_ref[j, 1:2, :]
        sr = xr[:n_rows - d]
        si = xi[:n_rows - d]
        zr = jnp.zeros((d, half), f32)
        xr, xi = (xr + jnp.concatenate([zr, ar * sr - ai * si], axis=0),
                  xi + jnp.concatenate([zr, ar * si + ai * sr], axis=0))
    zr = jnp.zeros((rows_per_chunk, half), f32)
    xin = jnp.concatenate([jnp.concatenate([zr, xr[:n_rows - rows_per_chunk]], axis=0),
                           jnp.concatenate([zr, xi[:n_rows - rows_per_chunk]], axis=0)], axis=1).astype(bf16)
    ys = jnp.dot(xin, v_ref[0], preferred_element_type=f32)
    y_ref[0] = (jnp.dot(u0, mt_ref[0], preferred_element_type=f32) + ys[:, :MXU_DIM]).astype(y_ref.dtype)
    y_ref[1] = (jnp.dot(u1, mt_ref[1], preferred_element_type=f32) + ys[:, MXU_DIM:]).astype(y_ref.dtype)


def _ssm_tables(log_dt, lam_re, lam_im, b_re, b_im, c_re, c_im, n_chunks):
    g_n, p_n = lam_re.shape
    t_n, c_n = SSM_T, SSM_GROUP_DIM
    dt = jnp.exp(log_dt.astype(f32))[:, None]
    lr = lam_re.astype(f32)
    li = lam_im.astype(f32)

    def power(n):
        nn = jnp.asarray(n, f32).reshape((-1, 1, 1))
        mag = jnp.exp(lr * dt * nn)
        return mag * jnp.cos(li * dt * nn), mag * jnp.sin(li * dt * nn)

    ab_r, ab_i = power([1])
    ab_r, ab_i = ab_r[0], ab_i[0]
    den = lr * lr + li * li
    f_r = ((ab_r - 1.0) * lr + ab_i * li) / den
    f_i = (ab_i * lr - (ab_r - 1.0) * li) / den
    br = b_re.astype(f32)
    bi = b_im.astype(f32)
    bb_r = f_r[..., None] * br - f_i[..., None] * bi
    bb_i = f_r[..., None] * bi + f_i[..., None] * br
    cr = c_re.astype(f32)
    ci = c_im.astype(f32)

    pr, pi = power(jnp.arange(t_n + 1))
    t_r = pr[:t_n, :, :, None] * bb_r[None] - pi[:t_n, :, :, None] * bb_i[None]
    t_i = pr[:t_n, :, :, None] * bb_i[None] + pi[:t_n, :, :, None] * bb_r[None]
    kern = (jnp.einsum("gcp,dgpk->gdkc", cr, t_r, precision=lax.Precision.HIGHEST)
            - jnp.einsum("gcp,dgpk->gdkc", ci, t_i, precision=lax.Precision.HIGHEST))
    kern = jnp.concatenate([kern, jnp.zeros_like(kern[:, :1])], axis=1)
    lag = jnp.arange(t_n)[None, :] - jnp.arange(t_n)[:, None]
    lag = jnp.where(lag >= 0, lag, t_n)
    mt = kern[:, lag]
    mt = mt.transpose(0, 1, 3, 2, 4).reshape(g_n, t_n * c_n, t_n * c_n)
    w_r = t_r[::-1].transpose(1, 0, 3, 2).reshape(g_n, t_n * c_n, p_n)
    w_i = t_i[::-1].transpose(1, 0, 3, 2).reshape(g_n, t_n * c_n, p_n)
    pr1 = pr[1:].transpose(1, 2, 0)
    pi1 = pi[1:].transpose(1, 2, 0)
    crt = cr.transpose(0, 2, 1)
    cit = ci.transpose(0, 2, 1)
    v_r = (crt[:, :, None, :] * pr1[..., None] - cit[:, :, None, :] * pi1[..., None]).reshape(g_n, p_n, t_n * c_n)
    v_i = -(crt[:, :, None, :] * pi1[..., None] + cit[:, :, None, :] * pr1[..., None]).reshape(g_n, p_n, t_n * c_n)

    gp = g_n // 2
    z = jnp.zeros((gp, t_n * c_n, p_n), f32)
    w_r = w_r.reshape(gp, 2, t_n * c_n, p_n)
    w_i = w_i.reshape(gp, 2, t_n * c_n, p_n)
    w0 = jnp.concatenate([w_r[:, 0], z, w_i[:, 0], z], axis=-1)
    w1 = jnp.concatenate([z, w_r[:, 1], z, w_i[:, 1]], axis=-1)
    w_pack = jnp.stack([w0, w1], axis=1)
    zv = jnp.zeros((gp, p_n, t_n * c_n), f32)
    v_r = v_r.reshape(gp, 2, p_n, t_n * c_n)
    v_i = v_i.reshape(gp, 2, p_n, t_n * c_n)
    v_pack = jnp.concatenate([
        jnp.concatenate([v_r[:, 0], zv], axis=-1),
        jnp.concatenate([zv, v_r[:, 1]], axis=-1),
        jnp.concatenate([v_i[:, 0], zv], axis=-1),
        jnp.concatenate([zv, v_i[:, 1]], axis=-1)], axis=1)[:, None]
    n_rounds = (n_chunks - 1).bit_length()
    qr, qi = power(t_n * (2 ** jnp.arange(max(n_rounds, 1))))
    pw = jnp.stack([qr.reshape(-1, gp, 2 * p_n), qi.reshape(-1, gp, 2 * p_n)], axis=2)
    pw = pw.transpose(1, 0, 2, 3)
    return mt.reshape(gp, 2, t_n * c_n, t_n * c_n).astype(bf16), w_pack.astype(bf16), v_pack.astype(bf16), pw, n_rounds


def _ssm(u, tables, bsz, seqlen):
    mt, w_pack, v_pack, pw, n_rounds = tables
    n, d = u.shape
    g_n = d // SSM_GROUP_DIM
    gp = g_n // 2
    n_chunks = seqlen // SSM_T
    rows = n_chunks * bsz
    k = SSM_T * SSM_GROUP_DIM
    ut = u.reshape(bsz, n_chunks, SSM_T, g_n, SSM_GROUP_DIM).transpose(3, 1, 0, 2, 4).reshape(g_n, rows, k)
    yt = pl.pallas_call(
        functools.partial(_ssm_kernel, n_rounds, bsz),
        grid=(gp,),
        in_specs=[pl.BlockSpec((2, rows, k), lambda g: (g, 0, 0)),
                  pl.BlockSpec((None, 2, k, k), lambda g: (g, 0, 0, 0)),
                  pl.BlockSpec((None, 2, k, k), lambda g: (g, 0, 0, 0)),
                  pl.BlockSpec((None, 1, k, 2 * k), lambda g: (g, 0, 0, 0)),
                  pl.BlockSpec((None,) + pw.shape[1:], lambda g: (g, 0, 0, 0))],
        out_specs=pl.BlockSpec((2, rows, k), lambda g: (g, 0, 0)),
        out_shape=jax.ShapeDtypeStruct((g_n, rows, k), bf16),
        compiler_params=_cparams("parallel"),
        name="ssm",
    )(ut, mt, w_pack, v_pack, pw)
    y = yt.reshape(g_n, n_chunks, bsz, SSM_T, SSM_GROUP_DIM).transpose(2, 1, 3, 0, 4).reshape(n, d)
    return y


def _glu_kernel(x_ref, y_ref, u_ref, d_ref, w_ref, o_ref):
    y = y_ref[...].astype(f32) + d_ref[...] * u_ref[...].astype(f32)
    g = jax.nn.gelu(y).astype(bf16)
    r = jnp.dot(g, w_ref[...], preferred_element_type=f32)
    dm = o_ref.shape[1]
    o_ref[...] = x_ref[...] + r[:, :dm] * jax.nn.sigmoid(r[:, dm:])


def _glu_res(x, y, u, d_skip, w_glu, tm=512):
    n, d = x.shape
    row = lambda i: (i, 0)
    const = lambda i: (0, 0)
    return pl.pallas_call(
        _glu_kernel,
        grid=(n // tm,),
        in_specs=[pl.BlockSpec((tm, d), row), pl.BlockSpec((tm, d), row), pl.BlockSpec((tm, d), row),
                  pl.BlockSpec((1, d), const), pl.BlockSpec(w_glu.shape, const)],
        out_specs=pl.BlockSpec((tm, d), row),
        out_shape=jax.ShapeDtypeStruct((n, d), f32),
        compiler_params=_cparams("parallel"),
        name="glu_res",
    )(x, y, u, d_skip.astype(f32)[None, :], w_glu)


def _attn_kernel(tq, tk, out_scale, lam_ref, q_ref, k_ref, v_ref, bias_ref, sub_ref, o_ref, qm_ref):
    qi = pl.program_id(2)
    q = q_ref[0].astype(f32) * (HEAD_DIM ** -0.5)
    lane = lax.broadcasted_iota(jnp.int32, q.shape, 1)
    qm_ref[:tq] = jnp.where(lane < HEAD_DIM, q, 0.0).astype(bf16)
    qm_ref[tq:] = jnp.where(lane >= HEAD_DIM, q, 0.0).astype(bf16)

    def step(ki, carry):
        m, l, acc = carry
        start = pl.multiple_of(ki * tk, tk)
        k = k_ref[0, pl.ds(start, tk), :]
        v = v_ref[0, pl.ds(start, tk), :]
        s = lax.dot_general(qm_ref[...], k, (((1,), (1,)), ((), ())), preferred_element_type=f32)
        b = bias_ref[0, jnp.minimum(qi - ki, 2)]
        s = s + jnp.concatenate([b, b], axis=0)
        m_new = jnp.maximum(m, jnp.max(s, axis=1, keepdims=True))
        alpha = jnp.exp(m - m_new)
        p = jnp.exp(s - m_new)
        l = alpha * l + jnp.sum(p, axis=1, keepdims=True)
        acc = alpha * acc + jnp.dot(p.astype(bf16), v, preferred_element_type=f32)
        return m_new, l, acc

    init = (jnp.full((2 * tq, 1), NEG_INF, f32), jnp.zeros((2 * tq, 1), f32),
            jnp.zeros((2 * tq, 2 * HEAD_DIM), f32))
    m, l, acc = lax.fori_loop(0, qi + 1, step, init)
    o = acc / l
    o = o[:tq] - lam_ref[0] * o[tq:]
    o = _rms(o, sub_ref[...]) * out_scale
    o_ref[0] = o.astype(o_ref.dtype)


def _bias_tiles(rel_bias, tq, tk):
    n = jnp.arange(3 * tq)
    nf = jnp.maximum(n, REL_MAX_EXACT).astype(f32)
    large = REL_MAX_EXACT + (jnp.log(nf / REL_MAX_EXACT) / math.log(REL_MAX_DIST / REL_MAX_EXACT)
                             * (REL_BUCKETS - REL_MAX_EXACT)).astype(jnp.int32)
    large = jnp.minimum(large, REL_BUCKETS - 1)
    bucket = jnp.where(n < REL_MAX_EXACT, n, large)
    by_dist = rel_bias.astype(f32)[bucket]
    dist = (jnp.arange(3)[:, None, None] * tq + jnp.arange(tq)[None, :, None]
            - jnp.arange(tk)[None, None, :])
    tiles = by_dist[jnp.maximum(dist, 0)].transpose(3, 0, 1, 2)
    return jnp.where(dist[None] >= 0, tiles, NEG_INF)


def _diff_attention(q, k, v, bias, lam, subln, out_scale, bsz, seqlen, tq=256, tk=256):
    hd2 = 2 * HEAD_DIM
    q3 = q.reshape(bsz, seqlen, N_HEADS * hd2)
    k3 = k.reshape(bsz, seqlen, N_HEADS * hd2)
    v3 = v.reshape(bsz, seqlen, N_HEADS * hd2)
    o = pl.pallas_call(
        functools.partial(_attn_kernel, tq, tk, out_scale),
        grid=(bsz, N_HEADS, seqlen // tq),
        in_specs=[pl.BlockSpec(memory_space=pltpu.SMEM),
                  pl.BlockSpec((1, tq, hd2), lambda b, h, i: (b, i, h)),
                  pl.BlockSpec((1, seqlen, hd2), lambda b, h, i: (b, 0, h)),
                  pl.BlockSpec((1, seqlen, hd2), lambda b, h, i: (b, 0, h)),
                  pl.BlockSpec((1, 3, tq, tk), lambda b, h, i: (h, 0, 0, 0)),
                  pl.BlockSpec((1, hd2), lambda b, h, i: (0, 0))],
        out_specs=pl.BlockSpec((1, tq, hd2), lambda b, h, i: (b, i, h)),
        out_shape=jax.ShapeDtypeStruct((bsz, seqlen, N_HEADS * hd2), bf16),
        scratch_shapes=[pltpu.VMEM((2 * tq, hd2), bf16)],
        compiler_params=_cparams("parallel", "parallel", "parallel"),
        name="diff_attn",
    )(lam.reshape(1).astype(f32), q3, k3, v3, bias, subln.astype(f32)[None, :])
    return o.reshape(bsz * seqlen, N_HEADS * hd2)


def _oproj_kernel(x_ref, o_ref_in, w_ref, out_ref):
    out_ref[...] = x_ref[...] + jnp.dot(o_ref_in[...], w_ref[...], preferred_element_type=f32)


def _oproj_res(x, o, w_o, tm=512):
    n, d = x.shape
    row = lambda i: (i, 0)
    return pl.pallas_call(
        _oproj_kernel,
        grid=(n // tm,),
        in_specs=[pl.BlockSpec((tm, d), row), pl.BlockSpec((tm, o.shape[1]), row),
                  pl.BlockSpec(w_o.shape, lambda i: (0, 0))],
        out_specs=pl.BlockSpec((tm, d), row),
        out_shape=jax.ShapeDtypeStruct((n, d), f32),
        compiler_params=_cparams("parallel"),
        name="oproj_res",
    )(x, o, w_o)


def _ffn_ple_kernel(tiles_per_seq, fc, x_ref, p_ref, fg_ref, wup_ref, cw_ref, cb_ref, wdn_ref,
                    pg_ref, wgate_ref, wproj_ref, o_ref, ext_ref, carry_ref, act_ref):
    i = pl.program_id(0)
    tm = x_ref.shape[0]
    d_ff = wdn_ref.shape[0]
    halo = SUBLANES

    @pl.when(i % tiles_per_seq == 0)
    def _():
        carry_ref[...] = jnp.zeros_like(carry_ref)

    x = x_ref[...]
    h = _rms(x, fg_ref[...]).astype(bf16)

    def conv(col):
        up = jnp.dot(h, wup_ref[:, col:col + fc], preferred_element_type=f32)
        ext_ref[:halo] = carry_ref[:, col:col + fc]
        ext_ref[halo:] = up
        carry_ref[:, col:col + fc] = up[tm - halo:]
        c = cb_ref[:, col:col + fc] + cw_ref[2:3, col:col + fc] * up
        c = c + cw_ref[1:2, col:col + fc] * ext_ref[pl.ds(halo - 1, tm), :]
        c = c + cw_ref[0:1, col:col + fc] * ext_ref[pl.ds(halo - 2, tm), :]
        return c

    for j in range(d_ff // fc):
        gate = conv(j * fc)
        val = conv(d_ff + j * fc)
        act_ref[:, j * fc:(j + 1) * fc] = (jax.nn.gelu(gate) * val).astype(bf16)

    x = x + jnp.dot(act_ref[...], wdn_ref[...], preferred_element_type=f32)
    hg = _rms(x, pg_ref[...]).astype(bf16)
    gate = jax.nn.sigmoid(jnp.dot(hg, wgate_ref[...], preferred_element_type=f32))
    proj = jnp.dot(p_ref[...].astype(bf16), wproj_ref[...], preferred_element_type=f32)
    o_ref[...] = x + gate * proj


def _ffn_ple(x, p, ffn_g, w_up, conv_w, conv_b, w_down, ple_g, w_gate, w_proj, seqlen, tm=512, fc=256):
    n, d = x.shape
    d_ff = w_down.shape[0]
    row = lambda i: (i, 0)
    const = lambda i: (0, 0)
    full = lambda a: pl.BlockSpec(a.shape, const, pipeline_mode=pl.Buffered(1))
    cw = conv_w.astype(f32)
    cb = conv_b.astype(f32)[None, :]
    fg = ffn_g.astype(f32)[None, :]
    pg = ple_g.astype(f32)[None, :]
    return pl.pallas_call(
        functools.partial(_ffn_ple_kernel, seqlen // tm, fc),
        grid=(n // tm,),
        in_specs=[pl.BlockSpec((tm, d), row), pl.BlockSpec((tm, p.shape[1]), row),
                  full(fg), full(w_up), full(cw), full(cb), full(w_down), full(pg), full(w_gate), full(w_proj)],
        out_specs=pl.BlockSpec((tm, d), row),
        out_shape=jax.ShapeDtypeStruct((n, d), f32),
        scratch_shapes=[pltpu.VMEM((tm + SUBLANES, fc), f32),
                        pltpu.VMEM((SUBLANES, 2 * d_ff), f32),
                        pltpu.VMEM((tm, d_ff), bf16)],
        compiler_params=_cparams("arbitrary"),
        name="ffn_ple",
    )(x, p, fg, w_up, cw, cb, w_down, pg, w_gate, w_proj)


def kernel(x, p, ssm_norm, ssm_w_in, ssm_log_dt, ssm_lambda_re, ssm_lambda_im, ssm_b_re, ssm_b_im, ssm_c_re, ssm_c_im, ssm_d, ssm_w_glu, kv_norm, kv_w, k_norm, attn_norm, attn_w_q, q_norm, lambda_q1, lambda_k1, lambda_q2, lambda_k2, subln, attn_w_o, rel_bias, ffn_norm, ffn_w_up, ffn_conv_w, ffn_conv_b, ffn_w_down, ple_norm, ple_w_gate, ple_w_proj):
    bsz, seqlen, d = x.shape
    depth = p.shape[0]
    n_a = ssm_norm.shape[0]
    n = bsz * seqlen
    x = x.reshape(n, d).astype(f32)
    p = p.reshape(depth, n, p.shape[-1])
    nk = N_HEADS * 2 * HEAD_DIM
    k = v = bias = None
    for i in range(depth):
        if i < n_a:
            tables = _ssm_tables(ssm_log_dt[i], ssm_lambda_re[i], ssm_lambda_im[i], ssm_b_re[i], ssm_b_im[i],
                                 ssm_c_re[i], ssm_c_im[i], seqlen // SSM_T)
            (u,) = _norm_proj(x, ssm_norm[i], ssm_w_in[i].astype(bf16), jnp.ones((HEAD_DIM,), f32), 0, 1)
            y = _ssm(u, tables, bsz, seqlen)
            x = _glu_res(x, y, u, ssm_d[i], ssm_w_glu[i].astype(bf16))
        else:
            if i == n_a:
                k, v = _norm_proj(x, kv_norm, kv_w.astype(bf16), k_norm, nk, 2)
                bias = _bias_tiles(rel_bias, 256, 256)
            j = i - n_a
            lam_init = 0.8 - 0.6 * math.exp(-0.3 * i)
            lam = (jnp.exp(jnp.sum(lambda_q1[j].astype(f32) * lambda_k1[j].astype(f32)))
                   - jnp.exp(jnp.sum(lambda_q2[j].astype(f32) * lambda_k2[j].astype(f32))) + lam_init)
            (q,) = _norm_proj(x, attn_norm[j], attn_w_q[j].astype(bf16), q_norm[j], nk, 1)
            o = _diff_attention(q, k, v, bias, lam, subln[j], 1.0 - lam_init, bsz, seqlen)
            x = _oproj_res(x, o, attn_w_o[j].astype(bf16))
        x = _ffn_ple(x, p[i], ffn_norm[i], ffn_w_up[i].astype(bf16), ffn_conv_w[i], ffn_conv_b[i],
                     ffn_w_down[i].astype(bf16), ple_norm[i], ple_w_gate[i].astype(bf16),
                     ple_w_proj[i].astype(bf16), seqlen)
    return x.reshape(bsz, seqlen, d)
```

```python
import functools
import math

import jax
import jax.numpy as jnp
from jax import lax
from jax.experimental import pallas as pl
from jax.experimental.pallas import tpu as pltpu

EPS = 1e-6
NEG_INF = -1e30

SSM_GROUP_DIM = 16
SSM_STATE = 64
N_HEADS = 8
HEAD_DIM = 64
REL_BUCKETS = 32
REL_MAX_EXACT = REL_BUCKETS // 2
REL_MAX_DIST = 128
CONV_WIDTH = 3

LANES = 128
SUBLANES = 8
MXU_DIM = 256
VMEM_LIMIT_BYTES = 56 * 1024 * 1024

SSM_T = MXU_DIM // SSM_GROUP_DIM
SEG_CHUNK = MXU_DIM
ATTN_BLOCK = 256
ATTN_HEADS_PER_STEP = 4
CORNER = LANES
V_ROWS = 2 * HEAD_DIM + 16
LOG2E = math.log2(math.e)

bf16 = jnp.bfloat16
f32 = jnp.float32


def _cparams(*sem):
    return pltpu.CompilerParams(dimension_semantics=sem, vmem_limit_bytes=VMEM_LIMIT_BYTES)


def _rms(xf, g):
    return xf * lax.rsqrt(jnp.mean(xf * xf, axis=-1, keepdims=True) + EPS) * g


def _seg_norm(y, seg_ref, gain_ref):
    cols = []
    for c in range(y.shape[1] // SEG_CHUNK):
        yc = y[:, c * SEG_CHUNK:(c + 1) * SEG_CHUNK]
        ss = jnp.dot((yc * yc).astype(bf16), seg_ref[...], preferred_element_type=f32)
        cols.append(yc * lax.rsqrt(ss * (1.0 / HEAD_DIM) + EPS) * gain_ref[...])
    return jnp.concatenate(cols, axis=1)


def _norm_proj_kernel(n_seg_cols, x_ref, g_ref, w_ref, seg_ref, gain_ref, *out_refs):
    h = _rms(x_ref[...], g_ref[...]).astype(bf16)
    y = jnp.dot(h, w_ref[...], preferred_element_type=f32)
    width = out_refs[0].shape[1]
    for i, o_ref in enumerate(out_refs):
        yi = y[:, i * width:(i + 1) * width]
        if (i + 1) * width <= n_seg_cols:
            yi = _seg_norm(yi, seg_ref, gain_ref)
        o_ref[...] = yi.astype(o_ref.dtype)


def _norm_proj(x, g, w, seg_gain, n_seg_cols, n_out, tm=512):
    n, d = x.shape
    width = w.shape[1] // n_out
    seg = (jnp.arange(SEG_CHUNK)[:, None] // HEAD_DIM == jnp.arange(SEG_CHUNK)[None, :] // HEAD_DIM)
    gain = jnp.tile(seg_gain.astype(f32), SEG_CHUNK // HEAD_DIM)[None, :]
    const = lambda i: (0, 0)
    outs = pl.pallas_call(
        functools.partial(_norm_proj_kernel, n_seg_cols),
        grid=(n // tm,),
        in_specs=[pl.BlockSpec((tm, d), lambda i: (i, 0)),
                  pl.BlockSpec((1, d), const),
                  pl.BlockSpec(w.shape, const),
                  pl.BlockSpec((SEG_CHUNK, SEG_CHUNK), const),
                  pl.BlockSpec((1, SEG_CHUNK), const)],
        out_specs=[pl.BlockSpec((tm, width), lambda i: (i, 0))] * n_out,
        out_shape=[jax.ShapeDtypeStruct((n, width), bf16)] * n_out,
        compiler_params=_cparams("parallel"),
        name="norm_proj",
    )(x, g.astype(f32)[None, :], w, seg.astype(bf16), gain)
    return outs


def _kv_proj_kernel(x_ref, g_ref, wk_ref, wvt_ref, seg_ref, gain_ref, k_ref, vt_ref):
    h = _rms(x_ref[...], g_ref[...]).astype(bf16)
    k = jnp.dot(h, wk_ref[...], preferred_element_type=f32)
    k_ref[...] = _seg_norm(k, seg_ref, gain_ref).astype(k_ref.dtype)
    vt = lax.dot_general(wvt_ref[...], h, (((1,), (1,)), ((), ())), preferred_element_type=f32)
    hd2 = 2 * HEAD_DIM
    for hh in range(vt_ref.shape[1]):
        vt_ref[0, hh, :hd2, :] = vt[hh * hd2:(hh + 1) * hd2].astype(vt_ref.dtype)
        vt_ref[0, hh, hd2:, :] = jnp.ones((V_ROWS - hd2, vt.shape[1]), vt_ref.dtype)


def _kv_proj(x, g, w_k, w_vt, seg_gain, bsz, seqlen, tm=512):
    n, d = x.shape
    tps = seqlen // tm
    seg = (jnp.arange(SEG_CHUNK)[:, None] // HEAD_DIM == jnp.arange(SEG_CHUNK)[None, :] // HEAD_DIM)
    gain = jnp.tile(seg_gain.astype(f32), SEG_CHUNK // HEAD_DIM)[None, :]
    const = lambda i: (0, 0)
    return pl.pallas_call(
        _kv_proj_kernel,
        grid=(n // tm,),
        in_specs=[pl.BlockSpec((tm, d), lambda i: (i, 0)),
                  pl.BlockSpec((1, d), const),
                  pl.BlockSpec(w_k.shape, const),
                  pl.BlockSpec(w_vt.shape, const),
                  pl.BlockSpec((SEG_CHUNK, SEG_CHUNK), const),
                  pl.BlockSpec((1, SEG_CHUNK), const)],
        out_specs=[pl.BlockSpec((tm, w_k.shape[1]), lambda i: (i, 0)),
                   pl.BlockSpec((1, N_HEADS, V_ROWS, tm), lambda i: (i // tps, 0, 0, i % tps))],
        out_shape=[jax.ShapeDtypeStruct((n, w_k.shape[1]), bf16),
                   jax.ShapeDtypeStruct((bsz, N_HEADS, V_ROWS, seqlen), bf16)],
        compiler_params=_cparams("parallel"),
        name="kv_proj",
    )(x, g.astype(f32)[None, :], w_k, w_vt, seg.astype(bf16), gain)


def _win_kernel(x_ref, g_ref, wt_ref, ut_ref):
    n_chunks, nt, d = x_ref.shape
    g_n = ut_ref.shape[0]
    xt = pltpu.einshape("ktd->tkd", x_ref[...]).reshape(nt * n_chunks, d)
    h = _rms(xt, g_ref[...]).astype(bf16)
    ut = lax.dot_general(wt_ref[...], h, (((1,), (1,)), ((), ())), preferred_element_type=f32)
    for tt in range(nt):
        blk = ut[:, tt * n_chunks:(tt + 1) * n_chunks].reshape(g_n, SSM_GROUP_DIM, n_chunks)
        ut_ref[:, tt * SSM_GROUP_DIM:(tt + 1) * SSM_GROUP_DIM, :] = blk.astype(ut_ref.dtype)


def _by_chunk(x3):
    bsz, seqlen, d = x3.shape
    return x3.reshape(bsz, seqlen // SSM_T, SSM_T // SUBLANES, SUBLANES, d)


def _chunk_spec(x5):
    _, n_chunks, _, nt, d = x5.shape
    return pl.BlockSpec((None, n_chunks, None, nt, d), lambda b, t: (b, 0, t, 0, 0))


def _win_t(x3, g, w_t):
    bsz, seqlen, d = x3.shape
    n_chunks = seqlen // SSM_T
    assert n_chunks == LANES
    g_n = d // SSM_GROUP_DIM
    x5 = _by_chunk(x3)
    return pl.pallas_call(
        _win_kernel,
        grid=(bsz, SSM_T // SUBLANES),
        in_specs=[_chunk_spec(x5),
                  pl.BlockSpec((1, d), lambda b, t: (0, 0)),
                  pl.BlockSpec(w_t.shape, lambda b, t: (0, 0))],
        out_specs=pl.BlockSpec((g_n, SUBLANES * SSM_GROUP_DIM, n_chunks), lambda b, t: (0, t, b)),
        out_shape=jax.ShapeDtypeStruct((g_n, SSM_T * SSM_GROUP_DIM, bsz * n_chunks), bf16),
        compiler_params=_cparams("parallel", "parallel"),
        name="win_t",
    )(x5, g.astype(f32)[None, :], w_t)


def _ssm_kernel(n_rounds, u_ref, mt_ref, w_ref, v_ref, pw_ref, y_ref, xin_ref):
    u0 = u_ref[0]
    u1 = u_ref[1]
    half = LANES
    seg = LANES
    st = (jnp.dot(w_ref[0], u0, preferred_element_type=f32)
          + jnp.dot(w_ref[1], u1, preferred_element_type=f32))
    chunk = lax.broadcasted_iota(jnp.int32, (half, seg), 1)
    for b in range(st.shape[1] // seg):
        xr = st[:half, b * seg:(b + 1) * seg]
        xi = st[half:, b * seg:(b + 1) * seg]
        for j in range(n_rounds):
            d = 1 << j
            ar = pw_ref[j, 0]
            ai = pw_ref[j, 1]
            sr = jnp.where(chunk >= d, pltpu.roll(xr, d, 1), 0.0)
            si = jnp.where(chunk >= d, pltpu.roll(xi, d, 1), 0.0)
            xr, xi = xr + (ar * sr - ai * si), xi + (ar * si + ai * sr)
        xin_ref[:half, b * seg:(b + 1) * seg] = jnp.where(chunk >= 1, pltpu.roll(xr, 1, 1), 0.0).astype(bf16)
        xin_ref[half:, b * seg:(b + 1) * seg] = jnp.where(chunk >= 1, pltpu.roll(xi, 1, 1), 0.0).astype(bf16)
    ys = jnp.dot(v_ref[...], xin_ref[...], preferred_element_type=f32)
    y_ref[0] = (jnp.dot(mt_ref[0], u0, preferred_element_type=f32) + ys[:MXU_DIM]).astype(y_ref.dtype)
    y_ref[1] = (jnp.dot(mt_ref[1], u1, preferred_element_type=f32) + ys[MXU_DIM:]).astype(y_ref.dtype)


def _ssm_tables(log_dt, lam_re, lam_im, b_re, b_im, c_re, c_im, n_chunks):
    g_n, p_n = lam_re.shape
    t_n, c_n = SSM_T, SSM_GROUP_DIM
    dt = jnp.exp(log_dt.astype(f32))[:, None]
    lr = lam_re.astype(f32)
    li = lam_im.astype(f32)

    def power(n):
        nn = jnp.asarray(n, f32).reshape((-1, 1, 1))
        mag = jnp.exp(lr * dt * nn)
        return mag * jnp.cos(li * dt * nn), mag * jnp.sin(li * dt * nn)

    ab_r, ab_i = power([1])
    ab_r, ab_i = ab_r[0], ab_i[0]
    den = lr * lr + li * li
    f_r = ((ab_r - 1.0) * lr + ab_i * li) / den
    f_i = (ab_i * lr - (ab_r - 1.0) * li) / den
    br = b_re.astype(f32)
    bi = b_im.astype(f32)
    bb_r = f_r[..., None] * br - f_i[..., None] * bi
    bb_i = f_r[..., None] * bi + f_i[..., None] * br
    cr = c_re.astype(f32)
    ci = c_im.astype(f32)

    pr, pi = power(jnp.arange(t_n + 1))
    t_r = pr[:t_n, :, :, None] * bb_r[None] - pi[:t_n, :, :, None] * bb_i[None]
    t_i = pr[:t_n, :, :, None] * bb_i[None] + pi[:t_n, :, :, None] * bb_r[None]
    kern = (jnp.einsum("gcp,dgpk->gdkc", cr, t_r, precision=lax.Precision.HIGHEST)
            - jnp.einsum("gcp,dgpk->gdkc", ci, t_i, precision=lax.Precision.HIGHEST))
    lag = jnp.arange(t_n)[None, :] - jnp.arange(t_n)[:, None]
    place = (lag[None] == jnp.arange(t_n)[:, None, None]).astype(f32)
    mt = jnp.einsum("dst,gdkc->gtcsk", place, kern, precision=lax.Precision.HIGHEST)
    mt = mt.reshape(g_n, t_n * c_n, t_n * c_n)
    w_r = t_r[::-1].transpose(1, 2, 0, 3).reshape(g_n, p_n, t_n * c_n)
    w_i = t_i[::-1].transpose(1, 2, 0, 3).reshape(g_n, p_n, t_n * c_n)
    pr1 = pr[1:].transpose(1, 0, 2)
    pi1 = pi[1:].transpose(1, 0, 2)
    v_r = (cr[:, None] * pr1[:, :, None, :] - ci[:, None] * pi1[:, :, None, :]).reshape(g_n, t_n * c_n, p_n)
    v_i = -(cr[:, None] * pi1[:, :, None, :] + ci[:, None] * pr1[:, :, None, :]).reshape(g_n, t_n * c_n, p_n)

    gp = g_n // 2
    z = jnp.zeros((gp, p_n, t_n * c_n), f32)
    w_r = w_r.reshape(gp, 2, p_n, t_n * c_n)
    w_i = w_i.reshape(gp, 2, p_n, t_n * c_n)
    w0 = jnp.concatenate([w_r[:, 0], z, w_i[:, 0], z], axis=1)
    w1 = jnp.concatenate([z, w_r[:, 1], z, w_i[:, 1]], axis=1)
    w_pack = jnp.stack([w0, w1], axis=1)
    zv = jnp.zeros((gp, t_n * c_n, p_n), f32)
    v_r = v_r.reshape(gp, 2, t_n * c_n, p_n)
    v_i = v_i.reshape(gp, 2, t_n * c_n, p_n)
    v_pack = jnp.concatenate([
        jnp.concatenate([v_r[:, 0], zv, v_i[:, 0], zv], axis=-1),
        jnp.concatenate([zv, v_r[:, 1], zv, v_i[:, 1]], axis=-1)], axis=1)
    n_rounds = (n_chunks - 1).bit_length()
    qr, qi = power(t_n * (2 ** jnp.arange(max(n_rounds, 1))))
    pw = jnp.stack([qr.reshape(-1, gp, 2 * p_n), qi.reshape(-1, gp, 2 * p_n)], axis=2)
    pw = pw.transpose(1, 0, 2, 3)[..., None]
    return mt.reshape(gp, 2, t_n * c_n, t_n * c_n).astype(bf16), w_pack.astype(bf16), v_pack.astype(bf16), pw, n_rounds


def _ssm(ut, tables):
    mt, w_pack, v_pack, pw, n_rounds = tables
    g_n, k, cols = ut.shape
    gp = g_n // 2
    return pl.pallas_call(
        functools.partial(_ssm_kernel, n_rounds),
        grid=(gp,),
        in_specs=[pl.BlockSpec((2, k, cols), lambda g: (g, 0, 0)),
                  pl.BlockSpec((None, 2, k, k), lambda g: (g, 0, 0, 0)),
                  pl.BlockSpec((None, 2, k, k), lambda g: (g, 0, 0, 0)),
                  pl.BlockSpec((None, 2 * k, k), lambda g: (g, 0, 0)),
                  pl.BlockSpec((None,) + pw.shape[1:], lambda g: (g, 0, 0, 0, 0))],
        out_specs=pl.BlockSpec((2, k, cols), lambda g: (g, 0, 0)),
        out_shape=jax.ShapeDtypeStruct((g_n, k, cols), bf16),
        scratch_shapes=[pltpu.VMEM((k, cols), bf16)],
        compiler_params=_cparams("parallel"),
        name="ssm",
    )(ut, mt, w_pack, v_pack, pw)


def _glu_kernel(x_ref, y_ref, u_ref, d_ref, w_ref, o_ref):
    n_chunks, nt, dm = x_ref.shape
    g_n = y_ref.shape[0]
    cols = []
    for tt in range(nt):
        sl = slice(tt * SSM_GROUP_DIM, (tt + 1) * SSM_GROUP_DIM)
        y = y_ref[:, sl, :].astype(f32) + d_ref[...] * u_ref[:, sl, :].astype(f32)
        cols.append(jax.nn.gelu(y).reshape(g_n * SSM_GROUP_DIM, n_chunks))
    gt = jnp.concatenate(cols, axis=1)
    r = jnp.dot(gt.T.astype(bf16), w_ref[...], preferred_element_type=f32)
    upd = r[:, :dm] * jax.nn.sigmoid(r[:, dm:])
    o_ref[...] = x_ref[...] + pltpu.einshape("tkd->ktd", upd.reshape(nt, n_chunks, dm))


def _glu_res(x3, yt, ut, d_skip, w_glu):
    bsz, seqlen, d = x3.shape
    g_n, _, cols = yt.shape
    n_chunks = cols // bsz
    d_tab = jnp.broadcast_to(d_skip.astype(f32).reshape(g_n, SSM_GROUP_DIM, 1), (g_n, SSM_GROUP_DIM, n_chunks))
    feat = pl.BlockSpec((g_n, SUBLANES * SSM_GROUP_DIM, n_chunks), lambda b, t: (0, t, b))
    x5 = _by_chunk(x3)
    out = pl.pallas_call(
        _glu_kernel,
        grid=(bsz, SSM_T // SUBLANES),
        in_specs=[_chunk_spec(x5), feat, feat,
                  pl.BlockSpec(d_tab.shape, lambda b, t: (0, 0, 0)),
                  pl.BlockSpec(w_glu.shape, lambda b, t: (0, 0))],
        out_specs=_chunk_spec(x5),
        out_shape=jax.ShapeDtypeStruct(x5.shape, f32),
        compiler_params=_cparams("parallel", "parallel"),
        name="glu_res",
    )(x5, yt, ut, d_tab, w_glu)
    return out.reshape(bsz, seqlen, d)


def _attn_kernel(tq, tk, out_scale, lam_ref, far_ref, q_ref, k_ref, vt_ref, diag_ref, corner_ref, sub_ref,
                 o_ref, qm_ref, m_ref, acc_ref):
    qi = pl.program_id(2)
    hd2 = 2 * HEAD_DIM
    heads = range(vt_ref.shape[1])
    lane = lax.broadcasted_iota(jnp.int32, (tq, hd2), 1)
    for hh in heads:
        q = q_ref[0, :, hh * hd2:(hh + 1) * hd2].astype(f32) * (HEAD_DIM ** -0.5 * LOG2E)
        qm_ref[hh, :tq] = jnp.where(lane < HEAD_DIM, q, 0.0).astype(bf16)
        qm_ref[hh, tq:] = jnp.where(lane >= HEAD_DIM, q, 0.0).astype(bf16)
    m_ref[...] = jnp.full_like(m_ref, NEG_INF)
    acc_ref[...] = jnp.zeros_like(acc_ref)

    def scores(hh, ki):
        k = k_ref[0, pl.ds(pl.multiple_of(ki * tk, tk), tk), hh * hd2:(hh + 1) * hd2]
        return lax.dot_general(k, qm_ref[hh], (((1,), (1,)), ((), ())), preferred_element_type=f32)

    def accumulate(hh, ki, s, shift):
        vt = vt_ref[0, hh, :, pl.ds(pl.multiple_of(ki * tk, tk), tk)]
        m_prev = m_ref[hh]
        m_new = jnp.maximum(m_prev, jnp.max(s, axis=0, keepdims=True) + shift)
        p = jnp.exp2(s - (m_new - shift))
        acc_ref[hh] = (jnp.exp2(m_prev - m_new) * acc_ref[hh]
                       + jnp.dot(vt, p.astype(bf16), preferred_element_type=f32))
        m_ref[hh] = m_new

    def earlier_block(ki, carry):
        nxt = tuple(scores(hh, ki + 1) for hh in heads)
        near = jnp.where(ki == qi - 1, 1.0, 0.0)
        for hh, s in zip(heads, carry):
            c = near * corner_ref[hh]
            lo = s[tk - CORNER:]
            lo = jnp.concatenate([lo[:, :CORNER] + c, lo[:, CORNER:tq], lo[:, tq:tq + CORNER] + c,
                                  lo[:, tq + CORNER:]], axis=1)
            accumulate(hh, ki, jnp.concatenate([s[:tk - CORNER], lo], axis=0),
                       far_ref[pl.program_id(1) * len(heads) + hh])
        return nxt

    carry = lax.fori_loop(0, qi, earlier_block, tuple(scores(hh, 0) for hh in heads))
    for hh, s in zip(heads, carry):
        b = diag_ref[hh]
        accumulate(hh, qi, s + jnp.concatenate([b, b], axis=1), 0.0)
        acc = acc_ref[hh]
        ot = acc[:hd2] / acc[hd2:hd2 + 1]
        ot = ot[:, :tq] - lam_ref[0] * ot[:, tq:]
        o = _rms(ot.T, sub_ref[...]) * out_scale
        o_ref[0, :, hh * hd2:(hh + 1) * hd2] = o.astype(o_ref.dtype)


def _bias_tables(rel_bias, tq, tk):
    n = jnp.arange(2 * tq)
    nf = jnp.maximum(n, REL_MAX_EXACT).astype(f32)
    large = REL_MAX_EXACT + (jnp.log(nf / REL_MAX_EXACT) / math.log(REL_MAX_DIST / REL_MAX_EXACT)
                             * (REL_BUCKETS - REL_MAX_EXACT)).astype(jnp.int32)
    large = jnp.minimum(large, REL_BUCKETS - 1)
    bucket = jnp.where(n < REL_MAX_EXACT, n, large)
    onehot = (bucket[:, None] == jnp.arange(REL_BUCKETS)[None, :]).astype(f32)
    by_dist = jnp.dot(onehot, rel_bias.astype(f32), precision=lax.Precision.HIGHEST).T * LOG2E
    n_heads = by_dist.shape[0]
    neg = jnp.full((n_heads, tk), NEG_INF, f32)
    period = tq + tk
    tiles = []
    for d, tail in ((0, neg), (1, by_dist[:, tq - tk:tq])):
        r = jnp.concatenate([by_dist[:, d * tq:(d + 1) * tq], tail], axis=1)
        t = jnp.tile(r, (1, tk))[:, :tk * (period - 1)].reshape(n_heads, tk, period - 1)
        tiles.append(t[:, :, :tq])
    far = rel_bias.astype(f32)[REL_BUCKETS - 1] * LOG2E
    corner = tiles[1][:, tk - CORNER:, :CORNER] - far[:, None, None]
    return tiles[0], corner, far


def _far_bucket_from():
    n = REL_MAX_EXACT
    while REL_MAX_EXACT + int(math.log(n / REL_MAX_EXACT) / math.log(REL_MAX_DIST / REL_MAX_EXACT)
                              * (REL_BUCKETS - REL_MAX_EXACT)) < REL_BUCKETS - 1:
        n += 1
    return n


def _diff_attention(q, k, vt, diag, corner, far, lam, subln, out_scale, bsz, seqlen, tq, tk):
    assert tq == tk and tq - CORNER + 1 >= _far_bucket_from()
    hd2 = 2 * HEAD_DIM
    q3 = q.reshape(bsz, seqlen, N_HEADS * hd2)
    k3 = k.reshape(bsz, seqlen, N_HEADS * hd2)
    smem = pl.BlockSpec(memory_space=pltpu.SMEM)
    hps = ATTN_HEADS_PER_STEP
    o = pl.pallas_call(
        functools.partial(_attn_kernel, tq, tk, out_scale),
        grid=(bsz, N_HEADS // hps, seqlen // tq),
        in_specs=[smem, smem,
                  pl.BlockSpec((1, tq, hps * hd2), lambda b, h, i: (b, i, h)),
                  pl.BlockSpec((1, seqlen, hps * hd2), lambda b, h, i: (b, 0, h)),
                  pl.BlockSpec((1, hps, V_ROWS, seqlen), lambda b, h, i: (b, h, 0, 0)),
                  pl.BlockSpec((hps, tk, tq), lambda b, h, i: (h, 0, 0)),
                  pl.BlockSpec((hps, CORNER, CORNER), lambda b, h, i: (h, 0, 0)),
                  pl.BlockSpec((1, hd2), lambda b, h, i: (0, 0))],
        out_specs=pl.BlockSpec((1, tq, hps * hd2), lambda b, h, i: (b, i, h)),
        out_shape=jax.ShapeDtypeStruct((bsz, seqlen, N_HEADS * hd2), bf16),
        scratch_shapes=[pltpu.VMEM((hps, 2 * tq, hd2), bf16), pltpu.VMEM((hps, 1, 2 * tq), f32),
                        pltpu.VMEM((hps, V_ROWS, 2 * tq), f32)],
        compiler_params=_cparams("parallel", "parallel", "parallel"),
        name="diff_attn",
    )(lam.reshape(1).astype(f32), far, q3, k3, vt, diag, corner, subln.astype(f32)[None, :])
    return o.reshape(bsz * seqlen, N_HEADS * hd2)


def _oproj_kernel(x_ref, o_ref_in, w_ref, out_ref):
    out_ref[...] = x_ref[...] + jnp.dot(o_ref_in[...], w_ref[...], preferred_element_type=f32)


def _oproj_res(x, o, w_o, tm=512):
    n, d = x.shape
    row = lambda i: (i, 0)
    return pl.pallas_call(
        _oproj_kernel,
        grid=(n // tm,),
        in_specs=[pl.BlockSpec((tm, d), row), pl.BlockSpec((tm, o.shape[1]), row),
                  pl.BlockSpec(w_o.shape, lambda i: (0, 0))],
        out_specs=pl.BlockSpec((tm, d), row),
        out_shape=jax.ShapeDtypeStruct((n, d), f32),
        compiler_params=_cparams("parallel"),
        name="oproj_res",
    )(x, o, w_o)


def _ffn_ple_kernel(tiles_per_seq, fc, x_ref, p_ref, fg_ref, wup_ref, cw_ref, cb_ref, wdn_ref,
                    pg_ref, wgate_ref, wproj_ref, o_ref, ext_ref, carry_ref, act_ref):
    i = pl.program_id(0)
    tm = x_ref.shape[0]
    d_ff = wdn_ref.shape[0]
    halo = SUBLANES

    @pl.when(i % tiles_per_seq == 0)
    def _():
        carry_ref[...] = jnp.zeros_like(carry_ref)

    x = x_ref[...]
    h = _rms(x, fg_ref[...]).astype(bf16)

    def conv(col):
        up = jnp.dot(h, wup_ref[:, col:col + fc], preferred_element_type=f32)
        ext_ref[:halo] = carry_ref[:, col:col + fc]
        ext_ref[halo:] = up
        carry_ref[:, col:col + fc] = up[tm - halo:]
        c = cb_ref[:, col:col + fc] + cw_ref[2:3, col:col + fc] * up
        c = c + cw_ref[1:2, col:col + fc] * ext_ref[pl.ds(halo - 1, tm), :]
        c = c + cw_ref[0:1, col:col + fc] * ext_ref[pl.ds(halo - 2, tm), :]
        return c

    for j in range(d_ff // fc):
        gate = conv(j * fc)
        val = conv(d_ff + j * fc)
        act_ref[:, j * fc:(j + 1) * fc] = (jax.nn.gelu(gate) * val).astype(bf16)

    x = x + jnp.dot(act_ref[...], wdn_ref[...], preferred_element_type=f32)
    hg = _rms(x, pg_ref[...]).astype(bf16)
    gate = jax.nn.sigmoid(jnp.dot(hg, wgate_ref[...], preferred_element_type=f32))
    proj = jnp.dot(p_ref[...].astype(bf16), wproj_ref[...], preferred_element_type=f32)
    o_ref[...] = x + gate * proj


def _ffn_ple(x, p, ffn_g, w_up, conv_w, conv_b, w_down, ple_g, w_gate, w_proj, seqlen, tm=512, fc=256):
    n, d = x.shape
    d_ff = w_down.shape[0]
    row = lambda i: (i, 0)
    const = lambda i: (0, 0)
    full = lambda a: pl.BlockSpec(a.shape, const, pipeline_mode=pl.Buffered(1))
    cw = conv_w.astype(f32)
    cb = conv_b.astype(f32)[None, :]
    fg = ffn_g.astype(f32)[None, :]
    pg = ple_g.astype(f32)[None, :]
    return pl.pallas_call(
        functools.partial(_ffn_ple_kernel, seqlen // tm, fc),
        grid=(n // tm,),
        in_specs=[pl.BlockSpec((tm, d), row), pl.BlockSpec((tm, p.shape[1]), row),
                  full(fg), full(w_up), full(cw), full(cb), full(w_down), full(pg), full(w_gate), full(w_proj)],
        out_specs=pl.BlockSpec((tm, d), row),
        out_shape=jax.ShapeDtypeStruct((n, d), f32),
        scratch_shapes=[pltpu.VMEM((tm + SUBLANES, fc), f32),
                        pltpu.VMEM((SUBLANES, 2 * d_ff), f32),
                        pltpu.VMEM((tm, d_ff), bf16)],
        compiler_params=_cparams("arbitrary"),
        name="ffn_ple",
    )(x, p, fg, w_up, cw, cb, w_down, pg, w_gate, w_proj)


def kernel(x, p, ssm_norm, ssm_w_in, ssm_log_dt, ssm_lambda_re, ssm_lambda_im, ssm_b_re, ssm_b_im, ssm_c_re, ssm_c_im, ssm_d, ssm_w_glu, kv_norm, kv_w, k_norm, attn_norm, attn_w_q, q_norm, lambda_q1, lambda_k1, lambda_q2, lambda_k2, subln, attn_w_o, rel_bias, ffn_norm, ffn_w_up, ffn_conv_w, ffn_conv_b, ffn_w_down, ple_norm, ple_w_gate, ple_w_proj):
    bsz, seqlen, d = x.shape
    depth = p.shape[0]
    n_a = ssm_norm.shape[0]
    n = bsz * seqlen
    x = x.reshape(n, d).astype(f32)
    p = p.reshape(depth, n, p.shape[-1])
    nk = N_HEADS * 2 * HEAD_DIM
    k = v = bias = None
    for i in range(depth):
        if i < n_a:
            tables = _ssm_tables(ssm_log_dt[i], ssm_lambda_re[i], ssm_lambda_im[i], ssm_b_re[i], ssm_b_im[i],
                                 ssm_c_re[i], ssm_c_im[i], seqlen // SSM_T)
            x3 = x.reshape(bsz, seqlen, d)
            ut = _win_t(x3, ssm_norm[i], ssm_w_in[i].T.astype(bf16))
            yt = _ssm(ut, tables)
            x = _glu_res(x3, yt, ut, ssm_d[i], ssm_w_glu[i].astype(bf16)).reshape(n, d)
        else:
            if i == n_a:
                k, vt = _kv_proj(x, kv_norm, kv_w[:, :nk].astype(bf16), kv_w[:, nk:].T.astype(bf16), k_norm,
                                 bsz, seqlen)
                diag, corner, far = _bias_tables(rel_bias, ATTN_BLOCK, ATTN_BLOCK)
            j = i - n_a
            lam_init = 0.8 - 0.6 * math.exp(-0.3 * i)
            lam = (jnp.exp(jnp.sum(lambda_q1[j].astype(f32) * lambda_k1[j].astype(f32)))
                   - jnp.exp(jnp.sum(lambda_q2[j].astype(f32) * lambda_k2[j].astype(f32))) + lam_init)
            (q,) = _norm_proj(x, attn_norm[j], attn_w_q[j].astype(bf16), q_norm[j], nk, 1)
            o = _diff_attention(q, k, vt, diag, corner, far, lam, subln[j], 1.0 - lam_init, bsz, seqlen,
                                ATTN_BLOCK, ATTN_BLOCK)
            x = _oproj_res(x, o, attn_w_o[j].astype(bf16))
        x = _ffn_ple(x, p[i], ffn_norm[i], ffn_w_up[i].astype(bf16), ffn_conv_w[i], ffn_conv_b[i],
                     ffn_w_down[i].astype(bf16), ple_norm[i], ple_w_gate[i].astype(bf16),
                     ple_w_proj[i].astype(bf16), seqlen)
    return x.reshape(bsz, seqlen, d)
```

```python
import functools
import math

import jax
import jax.numpy as jnp
from jax import lax
from jax.experimental import pallas as pl
from jax.experimental.pallas import tpu as pltpu

EPS = 1e-6
NEG_INF = -1e30

SSM_GROUP_DIM = 16
SSM_STATE = 64
N_HEADS = 8
HEAD_DIM = 64
REL_BUCKETS = 32
REL_MAX_EXACT = REL_BUCKETS // 2
REL_MAX_DIST = 128
CONV_WIDTH = 3

LANES = 128
SUBLANES = 8
MXU_DIM = 256
VMEM_LIMIT_BYTES = 56 * 1024 * 1024

SSM_T = MXU_DIM // SSM_GROUP_DIM
SEG_CHUNK = MXU_DIM
ATTN_BLOCK = 256
ATTN_HEADS_PER_STEP = 4
CORNER = LANES
V_ROWS = 2 * HEAD_DIM + 16
LOG2E = math.log2(math.e)
DOWN_SPLITS = 1

bf16 = jnp.bfloat16
f32 = jnp.float32


def _cparams(*sem):
    return pltpu.CompilerParams(dimension_semantics=sem, vmem_limit_bytes=VMEM_LIMIT_BYTES)


def _rms(xf, g):
    return xf * lax.rsqrt(jnp.mean(xf * xf, axis=-1, keepdims=True) + EPS) * g


def _seg_norm(y, seg_ref, gain_ref):
    cols = []
    for c in range(y.shape[1] // SEG_CHUNK):
        yc = y[:, c * SEG_CHUNK:(c + 1) * SEG_CHUNK]
        ss = jnp.dot((yc * yc).astype(bf16), seg_ref[...], preferred_element_type=f32)
        cols.append(yc * lax.rsqrt(ss * (1.0 / HEAD_DIM) + EPS) * gain_ref[...])
    return jnp.concatenate(cols, axis=1)


def _segment_ones():
    idx = jnp.arange(SEG_CHUNK) // HEAD_DIM
    return (idx[:, None] == idx[None, :]).astype(bf16)


def _segment_gain(g):
    return jnp.tile(g.astype(f32), SEG_CHUNK // HEAD_DIM)[None, :]


def _store_vt(vt_ref, vt):
    hd2 = 2 * HEAD_DIM
    for hh in range(vt_ref.shape[1]):
        vt_ref[0, hh, :hd2, :] = vt[hh * hd2:(hh + 1) * hd2].astype(vt_ref.dtype)
        vt_ref[0, hh, hd2:, :] = jnp.ones((V_ROWS - hd2, vt.shape[1]), vt_ref.dtype)


def _win_kernel(x_ref, g_ref, wt_ref, ut_ref):
    n_chunks, nt, d = x_ref.shape
    g_n = ut_ref.shape[0]
    xt = pltpu.einshape("ktd->tkd", x_ref[...]).reshape(nt * n_chunks, d)
    h = _rms(xt, g_ref[...]).astype(bf16)
    ut = lax.dot_general(wt_ref[...], h, (((1,), (1,)), ((), ())), preferred_element_type=f32)
    for tt in range(nt):
        blk = ut[:, tt * n_chunks:(tt + 1) * n_chunks].reshape(g_n, SSM_GROUP_DIM, n_chunks)
        ut_ref[:, tt * SSM_GROUP_DIM:(tt + 1) * SSM_GROUP_DIM, :] = blk.astype(ut_ref.dtype)


def _by_chunk(x3):
    bsz, seqlen, d = x3.shape
    return x3.reshape(bsz, seqlen // SSM_T, SSM_T // SUBLANES, SUBLANES, d)


def _chunk_spec(x5):
    _, n_chunks, _, nt, d = x5.shape
    return pl.BlockSpec((None, n_chunks, None, nt, d), lambda b, t: (b, 0, t, 0, 0))


def _win_t(x3, g, w_t):
    bsz, seqlen, d = x3.shape
    n_chunks = seqlen // SSM_T
    assert n_chunks == LANES
    g_n = d // SSM_GROUP_DIM
    x5 = _by_chunk(x3)
    return pl.pallas_call(
        _win_kernel,
        grid=(bsz, SSM_T // SUBLANES),
        in_specs=[_chunk_spec(x5),
                  pl.BlockSpec((1, d), lambda b, t: (0, 0)),
                  pl.BlockSpec(w_t.shape, lambda b, t: (0, 0))],
        out_specs=pl.BlockSpec((g_n, SUBLANES * SSM_GROUP_DIM, n_chunks), lambda b, t: (0, t, b)),
        out_shape=jax.ShapeDtypeStruct((g_n, SSM_T * SSM_GROUP_DIM, bsz * n_chunks), bf16),
        compiler_params=_cparams("parallel", "parallel"),
        name="win_t",
    )(x5, g.astype(f32)[None, :], w_t)


def _ssm_kernel(n_rounds, u_ref, mt_ref, w_ref, v_ref, pw_ref, y_ref, xin_ref):
    u0 = u_ref[0]
    u1 = u_ref[1]
    half = LANES
    seg = LANES
    st = (jnp.dot(w_ref[0], u0, preferred_element_type=f32)
          + jnp.dot(w_ref[1], u1, preferred_element_type=f32))
    n_seq = st.shape[1] // seg
    xr = jnp.stack([st[:half, b * seg:(b + 1) * seg] for b in range(n_seq)])
    xi = jnp.stack([st[half:, b * seg:(b + 1) * seg] for b in range(n_seq)])
    chunk = lax.broadcasted_iota(jnp.int32, xr.shape, 2)
    for j in range(n_rounds):
        d = 1 << j
        ar = pw_ref[j, 0]
        ai = pw_ref[j, 1]
        sr = jnp.where(chunk >= d, pltpu.roll(xr, d, 2), 0.0)
        si = jnp.where(chunk >= d, pltpu.roll(xi, d, 2), 0.0)
        xr, xi = xr + (ar * sr - ai * si), xi + (ar * si + ai * sr)
    xr = jnp.where(chunk >= 1, pltpu.roll(xr, 1, 2), 0.0).astype(bf16)
    xi = jnp.where(chunk >= 1, pltpu.roll(xi, 1, 2), 0.0).astype(bf16)
    for b in range(n_seq):
        xin_ref[:half, b * seg:(b + 1) * seg] = xr[b]
        xin_ref[half:, b * seg:(b + 1) * seg] = xi[b]
    ys = jnp.dot(v_ref[...], xin_ref[...], preferred_element_type=f32)
    y_ref[0] = (jnp.dot(mt_ref[0], u0, preferred_element_type=f32) + ys[:MXU_DIM]).astype(y_ref.dtype)
    y_ref[1] = (jnp.dot(mt_ref[1], u1, preferred_element_type=f32) + ys[MXU_DIM:]).astype(y_ref.dtype)


def _ssm_tables(log_dt, lam_re, lam_im, b_re, b_im, c_re, c_im, n_chunks):
    g_n, p_n = lam_re.shape
    t_n, c_n = SSM_T, SSM_GROUP_DIM
    dt = jnp.exp(log_dt.astype(f32))[:, None]
    lr = lam_re.astype(f32)
    li = lam_im.astype(f32)

    def power(n):
        nn = jnp.asarray(n, f32).reshape((-1, 1, 1))
        mag = jnp.exp(lr * dt * nn)
        return mag * jnp.cos(li * dt * nn), mag * jnp.sin(li * dt * nn)

    ab_r, ab_i = power([1])
    ab_r, ab_i = ab_r[0], ab_i[0]
    den = lr * lr + li * li
    f_r = ((ab_r - 1.0) * lr + ab_i * li) / den
    f_i = (ab_i * lr - (ab_r - 1.0) * li) / den
    br = b_re.astype(f32)
    bi = b_im.astype(f32)
    bb_r = f_r[..., None] * br - f_i[..., None] * bi
    bb_i = f_r[..., None] * bi + f_i[..., None] * br
    cr = c_re.astype(f32)
    ci = c_im.astype(f32)

    pr, pi = power(jnp.arange(t_n + 1))
    t_r = pr[:t_n, :, :, None] * bb_r[None] - pi[:t_n, :, :, None] * bb_i[None]
    t_i = pr[:t_n, :, :, None] * bb_i[None] + pi[:t_n, :, :, None] * bb_r[None]
    kern = (jnp.einsum("gcp,dgpk->gdkc", cr, t_r, precision=lax.Precision.HIGHEST)
            - jnp.einsum("gcp,dgpk->gdkc", ci, t_i, precision=lax.Precision.HIGHEST))
    lag = jnp.arange(t_n)[None, :] - jnp.arange(t_n)[:, None]
    place = (lag[None] == jnp.arange(t_n)[:, None, None]).astype(f32)
    mt = jnp.einsum("dst,gdkc->gtcsk", place, kern, precision=lax.Precision.HIGHEST)
    mt = mt.reshape(g_n, t_n * c_n, t_n * c_n)
    w_r = t_r[::-1].transpose(1, 2, 0, 3).reshape(g_n, p_n, t_n * c_n)
    w_i = t_i[::-1].transpose(1, 2, 0, 3).reshape(g_n, p_n, t_n * c_n)
    pr1 = pr[1:].transpose(1, 0, 2)
    pi1 = pi[1:].transpose(1, 0, 2)
    v_r = (cr[:, None] * pr1[:, :, None, :] - ci[:, None] * pi1[:, :, None, :]).reshape(g_n, t_n * c_n, p_n)
    v_i = -(cr[:, None] * pi1[:, :, None, :] + ci[:, None] * pr1[:, :, None, :]).reshape(g_n, t_n * c_n, p_n)

    gp = g_n // 2
    z = jnp.zeros((gp, p_n, t_n * c_n), f32)
    w_r = w_r.reshape(gp, 2, p_n, t_n * c_n)
    w_i = w_i.reshape(gp, 2, p_n, t_n * c_n)
    w0 = jnp.concatenate([w_r[:, 0], z, w_i[:, 0], z], axis=1)
    w1 = jnp.concatenate([z, w_r[:, 1], z, w_i[:, 1]], axis=1)
    w_pack = jnp.stack([w0, w1], axis=1)
    zv = jnp.zeros((gp, t_n * c_n, p_n), f32)
    v_r = v_r.reshape(gp, 2, t_n * c_n, p_n)
    v_i = v_i.reshape(gp, 2, t_n * c_n, p_n)
    v_pack = jnp.concatenate([
        jnp.concatenate([v_r[:, 0], zv, v_i[:, 0], zv], axis=-1),
        jnp.concatenate([zv, v_r[:, 1], zv, v_i[:, 1]], axis=-1)], axis=1)
    n_rounds = (n_chunks - 1).bit_length()
    qr, qi = power(t_n * (2 ** jnp.arange(max(n_rounds, 1))))
    pw = jnp.stack([qr.reshape(-1, gp, 2 * p_n), qi.reshape(-1, gp, 2 * p_n)], axis=2)
    pw = jnp.broadcast_to(pw.transpose(1, 0, 2, 3)[..., None], (gp, pw.shape[0], 2, 2 * p_n, LANES))
    return mt.reshape(gp, 2, t_n * c_n, t_n * c_n).astype(bf16), w_pack.astype(bf16), v_pack.astype(bf16), pw, n_rounds


def _ssm(ut, tables):
    mt, w_pack, v_pack, pw, n_rounds = tables
    g_n, k, cols = ut.shape
    gp = g_n // 2
    return pl.pallas_call(
        functools.partial(_ssm_kernel, n_rounds),
        grid=(gp,),
        in_specs=[pl.BlockSpec((2, k, cols), lambda g: (g, 0, 0)),
                  pl.BlockSpec((None, 2, k, k), lambda g: (g, 0, 0, 0)),
                  pl.BlockSpec((None, 2, k, k), lambda g: (g, 0, 0, 0)),
                  pl.BlockSpec((None, 2 * k, k), lambda g: (g, 0, 0)),
                  pl.BlockSpec((None,) + pw.shape[1:], lambda g: (g, 0, 0, 0, 0))],
        out_specs=pl.BlockSpec((2, k, cols), lambda g: (g, 0, 0)),
        out_shape=jax.ShapeDtypeStruct((g_n, k, cols), bf16),
        scratch_shapes=[pltpu.VMEM((k, cols), bf16)],
        compiler_params=_cparams("parallel"),
        name="ssm",
    )(ut, mt, w_pack, v_pack, pw)


def _glu_kernel(x_ref, y_ref, u_ref, d_ref, w_ref, o_ref):
    n_chunks, nt, dm = x_ref.shape
    g_n = y_ref.shape[0]
    cols = []
    for tt in range(nt):
        sl = slice(tt * SSM_GROUP_DIM, (tt + 1) * SSM_GROUP_DIM)
        y = y_ref[:, sl, :].astype(f32) + d_ref[...] * u_ref[:, sl, :].astype(f32)
        cols.append(jax.nn.gelu(y).reshape(g_n * SSM_GROUP_DIM, n_chunks))
    gt = jnp.concatenate(cols, axis=1)
    r = jnp.dot(gt.T.astype(bf16), w_ref[...], preferred_element_type=f32)
    upd = r[:, :dm] * jax.nn.sigmoid(r[:, dm:])
    o_ref[...] = x_ref[...] + pltpu.einshape("tkd->ktd", upd.reshape(nt, n_chunks, dm))


def _glu_res(x3, yt, ut, d_skip, w_glu):
    bsz, seqlen, d = x3.shape
    g_n, _, cols = yt.shape
    n_chunks = cols // bsz
    d_tab = jnp.broadcast_to(d_skip.astype(f32).reshape(g_n, SSM_GROUP_DIM, 1), (g_n, SSM_GROUP_DIM, n_chunks))
    feat = pl.BlockSpec((g_n, SUBLANES * SSM_GROUP_DIM, n_chunks), lambda b, t: (0, t, b))
    x5 = _by_chunk(x3)
    out = pl.pallas_call(
        _glu_kernel,
        grid=(bsz, SSM_T // SUBLANES),
        in_specs=[_chunk_spec(x5), feat, feat,
                  pl.BlockSpec(d_tab.shape, lambda b, t: (0, 0, 0)),
                  pl.BlockSpec(w_glu.shape, lambda b, t: (0, 0))],
        out_specs=_chunk_spec(x5),
        out_shape=jax.ShapeDtypeStruct(x5.shape, f32),
        compiler_params=_cparams("parallel", "parallel"),
        name="glu_res",
    )(x5, yt, ut, d_tab, w_glu)
    return out.reshape(bsz, seqlen, d)


def _attn_kernel(tq, tk, out_scale, lam_ref, far_ref, q_ref, k_ref, vt_ref, diag_ref, corner_ref, sub_ref,
                 o_ref, qm_ref, m_ref, acc_ref, s_ref):
    qi = pl.program_id(2)
    hd2 = 2 * HEAD_DIM
    heads = range(vt_ref.shape[1])
    lane = lax.broadcasted_iota(jnp.int32, (tq, hd2), 1)
    for hh in heads:
        q = q_ref[0, :, hh * hd2:(hh + 1) * hd2].astype(f32) * (HEAD_DIM ** -0.5 * LOG2E)
        qm_ref[hh, :tq] = jnp.where(lane < HEAD_DIM, q, 0.0).astype(bf16)
        qm_ref[hh, tq:] = jnp.where(lane >= HEAD_DIM, q, 0.0).astype(bf16)
    m_ref[...] = jnp.full_like(m_ref, NEG_INF)
    acc_ref[...] = jnp.zeros_like(acc_ref)

    def fill(slot, ki):
        for hh in heads:
            k = k_ref[0, pl.ds(pl.multiple_of(ki * tk, tk), tk), hh * hd2:(hh + 1) * hd2]
            s_ref[slot, hh] = lax.dot_general(k, qm_ref[hh], (((1,), (1,)), ((), ())),
                                              preferred_element_type=f32)

    def accumulate(hh, ki, s, shift):
        vt = vt_ref[0, hh, :, pl.ds(pl.multiple_of(ki * tk, tk), tk)]
        m_prev = m_ref[hh]
        m_new = jnp.maximum(m_prev, jnp.max(s, axis=0, keepdims=True) + shift)
        p = jnp.exp2(s - (m_new - shift))
        acc_ref[hh] = (jnp.exp2(m_prev - m_new) * acc_ref[hh]
                       + jnp.dot(vt, p.astype(bf16), preferred_element_type=f32))
        m_ref[hh] = m_new

    def earlier(slot, ki):
        near = jnp.where(ki == qi - 1, 1.0, 0.0)
        for hh in heads:
            s = s_ref[slot, hh]
            c = near * corner_ref[hh]
            lo = s[tk - CORNER:]
            lo = jnp.concatenate([lo[:, :CORNER] + c, lo[:, CORNER:tq], lo[:, tq:tq + CORNER] + c,
                                  lo[:, tq + CORNER:]], axis=1)
            accumulate(hh, ki, jnp.concatenate([s[:tk - CORNER], lo], axis=0),
                       far_ref[pl.program_id(1) * len(heads) + hh])

    def diagonal(slot):
        for hh in heads:
            b = diag_ref[hh]
            accumulate(hh, qi, s_ref[slot, hh] + jnp.concatenate([b, b], axis=1), 0.0)

    def pair(i, carry):
        fill(1, 2 * i + 1)
        earlier(0, 2 * i)
        fill(0, 2 * i + 2)
        earlier(1, 2 * i + 1)
        return carry

    fill(0, 0)
    lax.fori_loop(0, qi // 2, pair, 0)

    @pl.when(qi % 2 == 1)
    def _():
        fill(1, qi)
        earlier(0, qi - 1)
        diagonal(1)

    @pl.when(qi % 2 == 0)
    def _():
        diagonal(0)

    for hh in heads:
        acc = acc_ref[hh]
        ot = acc[:hd2] / acc[hd2:hd2 + 1]
        ot = ot[:, :tq] - lam_ref[0] * ot[:, tq:]
        o = _rms(ot.T, sub_ref[...]) * out_scale
        o_ref[0, :, hh * hd2:(hh + 1) * hd2] = o.astype(o_ref.dtype)


def _bias_tables(rel_bias, tq, tk):
    n = jnp.arange(2 * tq)
    nf = jnp.maximum(n, REL_MAX_EXACT).astype(f32)
    large = REL_MAX_EXACT + (jnp.log(nf / REL_MAX_EXACT) / math.log(REL_MAX_DIST / REL_MAX_EXACT)
                             * (REL_BUCKETS - REL_MAX_EXACT)).astype(jnp.int32)
    large = jnp.minimum(large, REL_BUCKETS - 1)
    bucket = jnp.where(n < REL_MAX_EXACT, n, large)
    onehot = (bucket[:, None] == jnp.arange(REL_BUCKETS)[None, :]).astype(f32)
    by_dist = jnp.dot(onehot, rel_bias.astype(f32), precision=lax.Precision.HIGHEST).T * LOG2E
    n_heads = by_dist.shape[0]
    neg = jnp.full((n_heads, tk), NEG_INF, f32)
    period = tq + tk
    tiles = []
    for d, tail in ((0, neg), (1, by_dist[:, tq - tk:tq])):
        r = jnp.concatenate([by_dist[:, d * tq:(d + 1) * tq], tail], axis=1)
        t = jnp.tile(r, (1, tk))[:, :tk * (period - 1)].reshape(n_heads, tk, period - 1)
        tiles.append(t[:, :, :tq])
    far = rel_bias.astype(f32)[REL_BUCKETS - 1] * LOG2E
    corner = tiles[1][:, tk - CORNER:, :CORNER] - far[:, None, None]
    return tiles[0], corner, far


def _far_bucket_from():
    n = REL_MAX_EXACT
    while REL_MAX_EXACT + int(math.log(n / REL_MAX_EXACT) / math.log(REL_MAX_DIST / REL_MAX_EXACT)
                              * (REL_BUCKETS - REL_MAX_EXACT)) < REL_BUCKETS - 1:
        n += 1
    return n


def _diff_attention(q, k, vt, diag, corner, far, lam, subln, out_scale, bsz, seqlen, tq, tk):
    assert tq == tk and tq - CORNER + 1 >= _far_bucket_from()
    hd2 = 2 * HEAD_DIM
    q3 = q.reshape(bsz, seqlen, N_HEADS * hd2)
    k3 = k.reshape(bsz, seqlen, N_HEADS * hd2)
    smem = pl.BlockSpec(memory_space=pltpu.SMEM)
    hps = ATTN_HEADS_PER_STEP
    o = pl.pallas_call(
        functools.partial(_attn_kernel, tq, tk, out_scale),
        grid=(bsz, N_HEADS // hps, seqlen // tq),
        in_specs=[smem, smem,
                  pl.BlockSpec((1, tq, hps * hd2), lambda b, h, i: (b, i, h)),
                  pl.BlockSpec((1, seqlen, hps * hd2), lambda b, h, i: (b, 0, h)),
                  pl.BlockSpec((1, hps, V_ROWS, seqlen), lambda b, h, i: (b, h, 0, 0)),
                  pl.BlockSpec((hps, tk, tq), lambda b, h, i: (h, 0, 0)),
                  pl.BlockSpec((hps, CORNER, CORNER), lambda b, h, i: (h, 0, 0)),
                  pl.BlockSpec((1, hd2), lambda b, h, i: (0, 0))],
        out_specs=pl.BlockSpec((1, tq, hps * hd2), lambda b, h, i: (b, i, h)),
        out_shape=jax.ShapeDtypeStruct((bsz, seqlen, N_HEADS * hd2), bf16),
        scratch_shapes=[pltpu.VMEM((hps, 2 * tq, hd2), bf16), pltpu.VMEM((hps, 1, 2 * tq), f32),
                        pltpu.VMEM((hps, V_ROWS, 2 * tq), f32), pltpu.VMEM((2, hps, tk, 2 * tq), f32)],
        compiler_params=_cparams("parallel", "parallel", "parallel"),
        name="diff_attn",
    )(lam.reshape(1).astype(f32), far, q3, k3, vt, diag, corner, subln.astype(f32)[None, :])
    return o.reshape(bsz * seqlen, N_HEADS * hd2)


def _gelu_tanh(v):
    c = 2.0 * math.sqrt(2.0 / math.pi)
    return v * jax.nn.sigmoid(v * (c + (c * 0.044715) * (v * v)))


def _ffn_ple_kernel(tiles_per_seq, fc, has_attn, has_q, has_kv, *refs):
    refs = list(refs)
    x_ref, p_ref = refs[:2]
    del refs[:2]
    if has_attn:
        a_ref, wo_ref = refs[:2]
        del refs[:2]
    fg_ref, wup_ref, cw_ref, cb_ref, wdn_ref, pg_ref, wgate_ref, wproj_ref = refs[:8]
    del refs[:8]
    if has_q or has_kv:
        seg_ref = refs.pop(0)
    if has_q:
        qg_ref, wq_ref, qgain_ref = refs[:3]
        del refs[:3]
    if has_kv:
        kvg_ref, wk_ref, wvt_ref, kgain_ref = refs[:4]
        del refs[:4]
    o_ref = refs.pop(0)
    if has_q:
        q_ref = refs.pop(0)
    if has_kv:
        k_ref, vt_ref = refs[:2]
        del refs[:2]
    carry_ref, act_ref = refs

    i = pl.program_id(0)
    tm = x_ref.shape[0]
    d_ff = wdn_ref.shape[0]
    halo = SUBLANES

    @pl.when(i % tiles_per_seq == 0)
    def _():
        carry_ref[...] = jnp.zeros_like(carry_ref)

    x = x_ref[...]
    if has_attn:
        x = x + jnp.dot(a_ref[...], wo_ref[...], preferred_element_type=f32)
    h = _rms(x, fg_ref[...]).astype(bf16)
    row = lax.broadcasted_iota(jnp.int32, (halo, fc), 0)

    def up_proj(col):
        return jnp.dot(h, wup_ref[:, col:col + fc], preferred_element_type=f32)

    def conv(col, up):
        prev = carry_ref[:, col:col + fc]
        carry_ref[:, col:col + fc] = up[tm - halo:]
        c = cb_ref[:, col:col + fc] + cw_ref[2:3, col:col + fc] * up
        for back in range(1, CONV_WIDTH):
            r = pltpu.roll(up, back, 0)
            head = jnp.where(row < back, pltpu.roll(prev, back, 0), r[:halo])
            c = c + cw_ref[2 - back:3 - back, col:col + fc] * jnp.concatenate([head, r[halo:]], axis=0)
        return c

    cols = list(range(0, d_ff, fc))
    bounds = [round(s * len(cols) / DOWN_SPLITS) * fc for s in range(DOWN_SPLITS + 1)]
    pending = None
    ups = (up_proj(cols[0]), up_proj(d_ff + cols[0]))
    for idx, col in enumerate(cols):
        nxt = (up_proj(cols[idx + 1]), up_proj(d_ff + cols[idx + 1])) if idx + 1 < len(cols) else None
        if pending is not None:
            x = x + jnp.dot(act_ref[:, pending[0]:pending[1]], wdn_ref[pending[0]:pending[1], :],
                            preferred_element_type=f32)
            pending = None
        act_ref[:, col:col + fc] = (_gelu_tanh(conv(col, ups[0])) * conv(d_ff + col, ups[1])).astype(bf16)
        if col + fc in bounds:
            pending = (bounds[bounds.index(col + fc) - 1], col + fc)
        ups = nxt
    x = x + jnp.dot(act_ref[:, pending[0]:pending[1]], wdn_ref[pending[0]:pending[1], :],
                    preferred_element_type=f32)
    hg = _rms(x, pg_ref[...]).astype(bf16)
    gate = jax.nn.sigmoid(jnp.dot(hg, wgate_ref[...], preferred_element_type=f32))
    proj = jnp.dot(p_ref[...].astype(bf16), wproj_ref[...], preferred_element_type=f32)
    x = x + gate * proj
    o_ref[...] = x
    if has_q:
        hq = _rms(x, qg_ref[...]).astype(bf16)
        q = jnp.dot(hq, wq_ref[...], preferred_element_type=f32)
        q_ref[...] = _seg_norm(q, seg_ref, qgain_ref).astype(q_ref.dtype)
    if has_kv:
        hk = _rms(x, kvg_ref[...]).astype(bf16)
        k = jnp.dot(hk, wk_ref[...], preferred_element_type=f32)
        k_ref[...] = _seg_norm(k, seg_ref, kgain_ref).astype(k_ref.dtype)
        _store_vt(vt_ref, lax.dot_general(wvt_ref[...], hk, (((1,), (1,)), ((), ())),
                                          preferred_element_type=f32))


def _ffn_ple(x, p, layer, ffn, ple, seqlen, attn=None, next_q=None, next_kv=None, tm=512, fc=256):
    n, d = x.shape
    ffn_g, w_up, conv_w, conv_b, w_down = ffn
    ple_g, w_gate, w_proj = ple
    d_ff = w_down.shape[0]
    bsz = n // seqlen
    tps = seqlen // tm
    row = lambda i: (i, 0)
    const = lambda i: (0, 0)
    full = lambda a: pl.BlockSpec(a.shape, const, pipeline_mode=pl.Buffered(1))
    vec = lambda a: a.astype(f32)[None, :]
    args = [x, p]
    in_specs = [pl.BlockSpec((tm, d), row), pl.BlockSpec((None, tm, p.shape[2]), lambda i: (layer, i, 0))]
    if attn is not None:
        args += [attn[0], attn[1]]
        in_specs += [pl.BlockSpec((tm, attn[0].shape[1]), row), full(attn[1])]
    weights = [vec(ffn_g), w_up, conv_w.astype(f32), vec(conv_b), w_down, vec(ple_g), w_gate, w_proj]
    out_specs = [pl.BlockSpec((tm, d), row)]
    out_shape = [jax.ShapeDtypeStruct((n, d), f32)]
    if next_q is not None or next_kv is not None:
        weights.append(_segment_ones())
    if next_q is not None:
        weights += [vec(next_q[0]), next_q[1], _segment_gain(next_q[2])]
        out_specs.append(pl.BlockSpec((tm, next_q[1].shape[1]), row))
        out_shape.append(jax.ShapeDtypeStruct((n, next_q[1].shape[1]), bf16))
    if next_kv is not None:
        weights += [vec(next_kv[0]), next_kv[1], next_kv[2], _segment_gain(next_kv[3])]
        out_specs += [pl.BlockSpec((tm, next_kv[1].shape[1]), row),
                      pl.BlockSpec((1, N_HEADS, V_ROWS, tm), lambda i: (i // tps, 0, 0, i % tps))]
        out_shape += [jax.ShapeDtypeStruct((n, next_kv[1].shape[1]), bf16),
                      jax.ShapeDtypeStruct((bsz, N_HEADS, V_ROWS, seqlen), bf16)]
    args += weights
    in_specs += [full(w) for w in weights]
    return pl.pallas_call(
        functools.partial(_ffn_ple_kernel, tps, fc, attn is not None, next_q is not None, next_kv is not None),
        grid=(n // tm,),
        in_specs=in_specs,
        out_specs=out_specs,
        out_shape=out_shape,
        scratch_shapes=[pltpu.VMEM((SUBLANES, 2 * d_ff), f32),
                        pltpu.VMEM((tm, d_ff), bf16)],
        compiler_params=_cparams("arbitrary"),
        name="ffn_ple",
    )(*args)


def kernel(x, p, ssm_norm, ssm_w_in, ssm_log_dt, ssm_lambda_re, ssm_lambda_im, ssm_b_re, ssm_b_im, ssm_c_re, ssm_c_im, ssm_d, ssm_w_glu, kv_norm, kv_w, k_norm, attn_norm, attn_w_q, q_norm, lambda_q1, lambda_k1, lambda_q2, lambda_k2, subln, attn_w_o, rel_bias, ffn_norm, ffn_w_up, ffn_conv_w, ffn_conv_b, ffn_w_down, ple_norm, ple_w_gate, ple_w_proj):
    bsz, seqlen, d = x.shape
    depth = p.shape[0]
    n_a = ssm_norm.shape[0]
    n = bsz * seqlen
    x = x.reshape(n, d).astype(f32)
    p = p.reshape(depth, n, p.shape[-1])
    nk = N_HEADS * 2 * HEAD_DIM
    assert 1 <= n_a < depth
    diag, corner, far = _bias_tables(rel_bias, ATTN_BLOCK, ATTN_BLOCK)
    q = k = vt = None
    for i in range(depth):
        attn = None
        if i < n_a:
            tables = _ssm_tables(ssm_log_dt[i], ssm_lambda_re[i], ssm_lambda_im[i], ssm_b_re[i], ssm_b_im[i],
                                 ssm_c_re[i], ssm_c_im[i], seqlen // SSM_T)
            x3 = x.reshape(bsz, seqlen, d)
            ut = _win_t(x3, ssm_norm[i], ssm_w_in[i].T.astype(bf16))
            yt = _ssm(ut, tables)
            x = _glu_res(x3, yt, ut, ssm_d[i], ssm_w_glu[i].astype(bf16)).reshape(n, d)
        else:
            j = i - n_a
            lam_init = 0.8 - 0.6 * math.exp(-0.3 * i)
            lam = (jnp.exp(jnp.sum(lambda_q1[j].astype(f32) * lambda_k1[j].astype(f32)))
                   - jnp.exp(jnp.sum(lambda_q2[j].astype(f32) * lambda_k2[j].astype(f32))) + lam_init)
            o = _diff_attention(q, k, vt, diag, corner, far, lam, subln[j], 1.0 - lam_init, bsz, seqlen,
                                ATTN_BLOCK, ATTN_BLOCK)
            attn = (o, attn_w_o[j].astype(bf16))
        j_next = i + 1 - n_a
        next_q = next_kv = None
        if 0 <= j_next < depth - n_a:
            next_q = (attn_norm[j_next], attn_w_q[j_next].astype(bf16), q_norm[j_next])
            if j_next == 0:
                next_kv = (kv_norm, kv_w[:, :nk].astype(bf16), kv_w[:, nk:].T.astype(bf16), k_norm)
        outs = _ffn_ple(x, p, i,
                        (ffn_norm[i], ffn_w_up[i].astype(bf16), ffn_conv_w[i], ffn_conv_b[i],
                         ffn_w_down[i].astype(bf16)),
                        (ple_norm[i], ple_w_gate[i].astype(bf16), ple_w_proj[i].astype(bf16)),
                        seqlen, attn=attn, next_q=next_q, next_kv=next_kv)
        x = outs[0]
        if next_q is not None:
            q = outs[1]
        if next_kv is not None:
            k, vt = outs[2], outs[3]
    return x.reshape(bsz, seqlen, d)
```

```python
import functools
import math

import jax
import jax.numpy as jnp
from jax import lax
from jax.experimental import pallas as pl
from jax.experimental.pallas import tpu as pltpu

EPS = 1e-6
NEG_INF = -1e30

SSM_GROUP_DIM = 16
SSM_STATE = 64
N_HEADS = 8
HEAD_DIM = 64
REL_BUCKETS = 32
REL_MAX_EXACT = REL_BUCKETS // 2
REL_MAX_DIST = 128
CONV_WIDTH = 3

LANES = 128
SUBLANES = 8
MXU_DIM = 256
VMEM_LIMIT_BYTES = 56 * 1024 * 1024

SSM_T = MXU_DIM // SSM_GROUP_DIM
SEG_CHUNK = MXU_DIM
ATTN_BLOCK = 256
ATTN_HEADS_PER_STEP = 4
CORNER = LANES
V_ROWS = 2 * HEAD_DIM + 16
LOG2E = math.log2(math.e)
DOWN_SPLITS = 1

bf16 = jnp.bfloat16
f32 = jnp.float32


def _cparams(*sem):
    return pltpu.CompilerParams(dimension_semantics=sem, vmem_limit_bytes=VMEM_LIMIT_BYTES)


def _rms(xf, g):
    return xf * lax.rsqrt(jnp.mean(xf * xf, axis=-1, keepdims=True) + EPS) * g


def _seg_norm(y, seg_ref, gain_ref):
    cols = []
    for c in range(y.shape[1] // SEG_CHUNK):
        yc = y[:, c * SEG_CHUNK:(c + 1) * SEG_CHUNK]
        ss = jnp.dot((yc * yc).astype(bf16), seg_ref[...], preferred_element_type=f32)
        cols.append(yc * lax.rsqrt(ss * (1.0 / HEAD_DIM) + EPS) * gain_ref[...])
    return jnp.concatenate(cols, axis=1)


def _segment_ones():
    idx = jnp.arange(SEG_CHUNK) // HEAD_DIM
    return (idx[:, None] == idx[None, :]).astype(bf16)


def _segment_gain(g):
    return jnp.tile(g.astype(f32), SEG_CHUNK // HEAD_DIM)[None, :]


def _store_vt(vt_ref, vt):
    hd2 = 2 * HEAD_DIM
    for hh in range(vt_ref.shape[1]):
        vt_ref[0, hh, :hd2, :] = vt[hh * hd2:(hh + 1) * hd2].astype(vt_ref.dtype)
        vt_ref[0, hh, hd2:, :] = jnp.ones((V_ROWS - hd2, vt.shape[1]), vt_ref.dtype)


def _win_kernel(x_ref, g_ref, wt_ref, ut_ref):
    n_chunks, nt, d = x_ref.shape
    g_n = ut_ref.shape[0]
    xt = pltpu.einshape("ktd->tkd", x_ref[...]).reshape(nt * n_chunks, d)
    h = _rms(xt, g_ref[...]).astype(bf16)
    ut = lax.dot_general(wt_ref[...], h, (((1,), (1,)), ((), ())), preferred_element_type=f32)
    for tt in range(nt):
        blk = ut[:, tt * n_chunks:(tt + 1) * n_chunks].reshape(g_n, SSM_GROUP_DIM, n_chunks)
        ut_ref[:, tt * SSM_GROUP_DIM:(tt + 1) * SSM_GROUP_DIM, :] = blk.astype(ut_ref.dtype)


def _by_chunk(x3):
    bsz, seqlen, d = x3.shape
    return x3.reshape(bsz, seqlen // SSM_T, SSM_T // SUBLANES, SUBLANES, d)


def _chunk_spec(x5):
    _, n_chunks, _, nt, d = x5.shape
    return pl.BlockSpec((None, n_chunks, None, nt, d), lambda b, t: (b, 0, t, 0, 0))


def _win_t(x3, g, w_t):
    bsz, seqlen, d = x3.shape
    n_chunks = seqlen // SSM_T
    assert n_chunks == LANES
    g_n = d // SSM_GROUP_DIM
    x5 = _by_chunk(x3)
    return pl.pallas_call(
        _win_kernel,
        grid=(bsz, SSM_T // SUBLANES),
        in_specs=[_chunk_spec(x5),
                  pl.BlockSpec((1, d), lambda b, t: (0, 0)),
                  pl.BlockSpec(w_t.shape, lambda b, t: (0, 0))],
        out_specs=pl.BlockSpec((g_n, SUBLANES * SSM_GROUP_DIM, n_chunks), lambda b, t: (0, t, b)),
        out_shape=jax.ShapeDtypeStruct((g_n, SSM_T * SSM_GROUP_DIM, bsz * n_chunks), bf16),
        compiler_params=_cparams("parallel", "parallel"),
        name="win_t",
    )(x5, g.astype(f32)[None, :], w_t)


def _ssm_kernel(n_rounds, u_ref, mt_ref, w_ref, v_ref, pw_ref, y_ref, xin_ref):
    u0 = u_ref[0]
    u1 = u_ref[1]
    half = LANES
    seg = LANES
    st = (jnp.dot(w_ref[0], u0, preferred_element_type=f32)
          + jnp.dot(w_ref[1], u1, preferred_element_type=f32))
    n_seq = st.shape[1] // seg
    xr = jnp.stack([st[:half, b * seg:(b + 1) * seg] for b in range(n_seq)])
    xi = jnp.stack([st[half:, b * seg:(b + 1) * seg] for b in range(n_seq)])
    chunk = lax.broadcasted_iota(jnp.int32, xr.shape, 2)
    for j in range(n_rounds):
        d = 1 << j
        ar = pw_ref[j, 0]
        ai = pw_ref[j, 1]
        sr = jnp.where(chunk >= d, pltpu.roll(xr, d, 2), 0.0)
        si = jnp.where(chunk >= d, pltpu.roll(xi, d, 2), 0.0)
        xr, xi = xr + (ar * sr - ai * si), xi + (ar * si + ai * sr)
    xr = jnp.where(chunk >= 1, pltpu.roll(xr, 1, 2), 0.0).astype(bf16)
    xi = jnp.where(chunk >= 1, pltpu.roll(xi, 1, 2), 0.0).astype(bf16)
    for b in range(n_seq):
        xin_ref[:half, b * seg:(b + 1) * seg] = xr[b]
        xin_ref[half:, b * seg:(b + 1) * seg] = xi[b]
    ys = jnp.dot(v_ref[...], xin_ref[...], preferred_element_type=f32)
    y_ref[0] = (jnp.dot(mt_ref[0], u0, preferred_element_type=f32) + ys[:MXU_DIM]).astype(y_ref.dtype)
    y_ref[1] = (jnp.dot(mt_ref[1], u1, preferred_element_type=f32) + ys[MXU_DIM:]).astype(y_ref.dtype)


def _ssm_tables(log_dt, lam_re, lam_im, b_re, b_im, c_re, c_im, n_chunks):
    g_n, p_n = lam_re.shape
    t_n, c_n = SSM_T, SSM_GROUP_DIM
    dt = jnp.exp(log_dt.astype(f32))[:, None]
    lr = lam_re.astype(f32)
    li = lam_im.astype(f32)

    def power(n):
        nn = jnp.asarray(n, f32).reshape((-1, 1, 1))
        mag = jnp.exp(lr * dt * nn)
        return mag * jnp.cos(li * dt * nn), mag * jnp.sin(li * dt * nn)

    ab_r, ab_i = power([1])
    ab_r, ab_i = ab_r[0], ab_i[0]
    den = lr * lr + li * li
    f_r = ((ab_r - 1.0) * lr + ab_i * li) / den
    f_i = (ab_i * lr - (ab_r - 1.0) * li) / den
    br = b_re.astype(f32)
    bi = b_im.astype(f32)
    bb_r = f_r[..., None] * br - f_i[..., None] * bi
    bb_i = f_r[..., None] * bi + f_i[..., None] * br
    cr = c_re.astype(f32)
    ci = c_im.astype(f32)

    pr, pi = power(jnp.arange(t_n + 1))
    t_r = pr[:t_n, :, :, None] * bb_r[None] - pi[:t_n, :, :, None] * bb_i[None]
    t_i = pr[:t_n, :, :, None] * bb_i[None] + pi[:t_n, :, :, None] * bb_r[None]
    kern = (jnp.einsum("gcp,dgpk->gdkc", cr, t_r, precision=lax.Precision.HIGHEST)
            - jnp.einsum("gcp,dgpk->gdkc", ci, t_i, precision=lax.Precision.HIGHEST))
    lag = jnp.arange(t_n)[None, :] - jnp.arange(t_n)[:, None]
    place = (lag[None] == jnp.arange(t_n)[:, None, None]).astype(f32)
    mt = jnp.einsum("dst,gdkc->gtcsk", place, kern, precision=lax.Precision.HIGHEST)
    mt = mt.reshape(g_n, t_n * c_n, t_n * c_n)
    w_r = t_r[::-1].transpose(1, 2, 0, 3).reshape(g_n, p_n, t_n * c_n)
    w_i = t_i[::-1].transpose(1, 2, 0, 3).reshape(g_n, p_n, t_n * c_n)
    pr1 = pr[1:].transpose(1, 0, 2)
    pi1 = pi[1:].transpose(1, 0, 2)
    v_r = (cr[:, None] * pr1[:, :, None, :] - ci[:, None] * pi1[:, :, None, :]).reshape(g_n, t_n * c_n, p_n)
    v_i = -(cr[:, None] * pi1[:, :, None, :] + ci[:, None] * pr1[:, :, None, :]).reshape(g_n, t_n * c_n, p_n)

    gp = g_n // 2
    z = jnp.zeros((gp, p_n, t_n * c_n), f32)
    w_r = w_r.reshape(gp, 2, p_n, t_n * c_n)
    w_i = w_i.reshape(gp, 2, p_n, t_n * c_n)
    w0 = jnp.concatenate([w_r[:, 0], z, w_i[:, 0], z], axis=1)
    w1 = jnp.concatenate([z, w_r[:, 1], z, w_i[:, 1]], axis=1)
    w_pack = jnp.stack([w0, w1], axis=1)
    zv = jnp.zeros((gp, t_n * c_n, p_n), f32)
    v_r = v_r.reshape(gp, 2, t_n * c_n, p_n)
    v_i = v_i.reshape(gp, 2, t_n * c_n, p_n)
    v_pack = jnp.concatenate([
        jnp.concatenate([v_r[:, 0], zv, v_i[:, 0], zv], axis=-1),
        jnp.concatenate([zv, v_r[:, 1], zv, v_i[:, 1]], axis=-1)], axis=1)
    n_rounds = (n_chunks - 1).bit_length()
    qr, qi = power(t_n * (2 ** jnp.arange(max(n_rounds, 1))))
    pw = jnp.stack([qr.reshape(-1, gp, 2 * p_n), qi.reshape(-1, gp, 2 * p_n)], axis=2)
    pw = jnp.broadcast_to(pw.transpose(1, 0, 2, 3)[..., None], (gp, pw.shape[0], 2, 2 * p_n, LANES))
    return mt.reshape(gp, 2, t_n * c_n, t_n * c_n).astype(bf16), w_pack.astype(bf16), v_pack.astype(bf16), pw


def _ssm(ut, tables):
    mt, w_pack, v_pack, pw = tables
    n_rounds = pw.shape[1]
    g_n, k, cols = ut.shape
    gp = g_n // 2
    return pl.pallas_call(
        functools.partial(_ssm_kernel, n_rounds),
        grid=(gp,),
        in_specs=[pl.BlockSpec((2, k, cols), lambda g: (g, 0, 0)),
                  pl.BlockSpec((None, 2, k, k), lambda g: (g, 0, 0, 0)),
                  pl.BlockSpec((None, 2, k, k), lambda g: (g, 0, 0, 0)),
                  pl.BlockSpec((None, 2 * k, k), lambda g: (g, 0, 0)),
                  pl.BlockSpec((None,) + pw.shape[1:], lambda g: (g, 0, 0, 0, 0))],
        out_specs=pl.BlockSpec((2, k, cols), lambda g: (g, 0, 0)),
        out_shape=jax.ShapeDtypeStruct((g_n, k, cols), bf16),
        scratch_shapes=[pltpu.VMEM((k, cols), bf16)],
        compiler_params=_cparams("parallel"),
        name="ssm",
    )(ut, mt, w_pack, v_pack, pw)


def _glu_kernel(x_ref, y_ref, u_ref, d_ref, w_ref, o_ref):
    n_chunks, nt, dm = x_ref.shape
    g_n = y_ref.shape[0]
    cols = []
    for tt in range(nt):
        sl = slice(tt * SSM_GROUP_DIM, (tt + 1) * SSM_GROUP_DIM)
        y = y_ref[:, sl, :].astype(f32) + d_ref[...] * u_ref[:, sl, :].astype(f32)
        cols.append(jax.nn.gelu(y).reshape(g_n * SSM_GROUP_DIM, n_chunks))
    gt = jnp.concatenate(cols, axis=1)
    r = jnp.dot(gt.T.astype(bf16), w_ref[...], preferred_element_type=f32)
    upd = r[:, :dm] * jax.nn.sigmoid(r[:, dm:])
    o_ref[...] = x_ref[...] + pltpu.einshape("tkd->ktd", upd.reshape(nt, n_chunks, dm))


def _glu_res(x3, yt, ut, d_skip, w_glu):
    bsz, seqlen, d = x3.shape
    g_n, _, cols = yt.shape
    n_chunks = cols // bsz
    d_tab = jnp.broadcast_to(d_skip.astype(f32).reshape(g_n, SSM_GROUP_DIM, 1), (g_n, SSM_GROUP_DIM, n_chunks))
    feat = pl.BlockSpec((g_n, SUBLANES * SSM_GROUP_DIM, n_chunks), lambda b, t: (0, t, b))
    x5 = _by_chunk(x3)
    out = pl.pallas_call(
        _glu_kernel,
        grid=(bsz, SSM_T // SUBLANES),
        in_specs=[_chunk_spec(x5), feat, feat,
                  pl.BlockSpec(d_tab.shape, lambda b, t: (0, 0, 0)),
                  pl.BlockSpec(w_glu.shape, lambda b, t: (0, 0))],
        out_specs=_chunk_spec(x5),
        out_shape=jax.ShapeDtypeStruct(x5.shape, f32),
        compiler_params=_cparams("parallel", "parallel"),
        name="glu_res",
    )(x5, yt, ut, d_tab, w_glu)
    return out.reshape(bsz, seqlen, d)


def _attn_kernel(tq, tk, out_scale, lam_ref, far_ref, q_ref, k_ref, vt_ref, diag_ref, corner_ref, sub_ref,
                 o_ref, qm_ref, m_ref, acc_ref, s_ref):
    qi = pl.program_id(2)
    hd2 = 2 * HEAD_DIM
    heads = range(vt_ref.shape[1])
    lane = lax.broadcasted_iota(jnp.int32, (tq, hd2), 1)
    for hh in heads:
        q = q_ref[0, :, hh * hd2:(hh + 1) * hd2].astype(f32) * (HEAD_DIM ** -0.5 * LOG2E)
        qm_ref[hh, :tq] = jnp.where(lane < HEAD_DIM, q, 0.0).astype(bf16)
        qm_ref[hh, tq:] = jnp.where(lane >= HEAD_DIM, q, 0.0).astype(bf16)
    m_ref[...] = jnp.full_like(m_ref, NEG_INF)
    acc_ref[...] = jnp.zeros_like(acc_ref)

    def fill(slot, ki):
        for hh in heads:
            k = k_ref[0, pl.ds(pl.multiple_of(ki * tk, tk), tk), hh * hd2:(hh + 1) * hd2]
            s_ref[slot, hh] = lax.dot_general(k, qm_ref[hh], (((1,), (1,)), ((), ())),
                                              preferred_element_type=f32)

    def accumulate(hh, ki, s, shift):
        vt = vt_ref[0, hh, :, pl.ds(pl.multiple_of(ki * tk, tk), tk)]
        m_prev = m_ref[hh]
        m_new = jnp.maximum(m_prev, jnp.max(s, axis=0, keepdims=True) + shift)
        p = jnp.exp2(s - (m_new - shift))
        acc_ref[hh] = (jnp.exp2(m_prev - m_new) * acc_ref[hh]
                       + jnp.dot(vt, p.astype(bf16), preferred_element_type=f32))
        m_ref[hh] = m_new

    def earlier(slot, ki):
        near = jnp.where(ki == qi - 1, 1.0, 0.0)
        for hh in heads:
            s = s_ref[slot, hh]
            c = near * corner_ref[hh]
            lo = s[tk - CORNER:]
            lo = jnp.concatenate([lo[:, :CORNER] + c, lo[:, CORNER:tq], lo[:, tq:tq + CORNER] + c,
                                  lo[:, tq + CORNER:]], axis=1)
            accumulate(hh, ki, jnp.concatenate([s[:tk - CORNER], lo], axis=0),
                       far_ref[pl.program_id(1) * len(heads) + hh])

    def diagonal(slot):
        for hh in heads:
            b = diag_ref[hh]
            accumulate(hh, qi, s_ref[slot, hh] + jnp.concatenate([b, b], axis=1), 0.0)

    def pair(i, carry):
        fill(1, 2 * i + 1)
        earlier(0, 2 * i)
        fill(0, 2 * i + 2)
        earlier(1, 2 * i + 1)
        return carry

    fill(0, 0)
    lax.fori_loop(0, qi // 2, pair, 0)

    @pl.when(qi % 2 == 1)
    def _():
        fill(1, qi)
        earlier(0, qi - 1)
        diagonal(1)

    @pl.when(qi % 2 == 0)
    def _():
        diagonal(0)

    for hh in heads:
        acc = acc_ref[hh]
        ot = acc[:hd2] / acc[hd2:hd2 + 1]
        ot = ot[:, :tq] - lam_ref[0] * ot[:, tq:]
        ot = ot * lax.rsqrt(jnp.mean(ot * ot, axis=0, keepdims=True) + EPS)
        o_ref[0, :, hh * hd2:(hh + 1) * hd2] = (ot.T * sub_ref[...] * out_scale).astype(o_ref.dtype)


def _bias_tables(rel_bias, tq, tk):
    n = jnp.arange(2 * tq)
    nf = jnp.maximum(n, REL_MAX_EXACT).astype(f32)
    large = REL_MAX_EXACT + (jnp.log(nf / REL_MAX_EXACT) / math.log(REL_MAX_DIST / REL_MAX_EXACT)
                             * (REL_BUCKETS - REL_MAX_EXACT)).astype(jnp.int32)
    large = jnp.minimum(large, REL_BUCKETS - 1)
    bucket = jnp.where(n < REL_MAX_EXACT, n, large)
    onehot = (bucket[:, None] == jnp.arange(REL_BUCKETS)[None, :]).astype(f32)
    by_dist = jnp.dot(onehot, rel_bias.astype(f32), precision=lax.Precision.HIGHEST).T * LOG2E
    n_heads = by_dist.shape[0]
    neg = jnp.full((n_heads, tk), NEG_INF, f32)
    period = tq + tk
    tiles = []
    for d, tail in ((0, neg), (1, by_dist[:, tq - tk:tq])):
        r = jnp.concatenate([by_dist[:, d * tq:(d + 1) * tq], tail], axis=1)
        t = jnp.tile(r, (1, tk))[:, :tk * (period - 1)].reshape(n_heads, tk, period - 1)
        tiles.append(t[:, :, :tq])
    far = rel_bias.astype(f32)[REL_BUCKETS - 1] * LOG2E
    corner = tiles[1][:, tk - CORNER:, :CORNER] - far[:, None, None]
    return tiles[0], corner, far


def _far_bucket_from():
    n = REL_MAX_EXACT
    while REL_MAX_EXACT + int(math.log(n / REL_MAX_EXACT) / math.log(REL_MAX_DIST / REL_MAX_EXACT)
                              * (REL_BUCKETS - REL_MAX_EXACT)) < REL_BUCKETS - 1:
        n += 1
    return n


def _diff_attention(q, k, vt, diag, corner, far, lam, subln, out_scale, bsz, seqlen, tq, tk):
    assert tq == tk and tq - CORNER + 1 >= _far_bucket_from()
    hd2 = 2 * HEAD_DIM
    q3 = q.reshape(bsz, seqlen, N_HEADS * hd2)
    k3 = k.reshape(bsz, seqlen, N_HEADS * hd2)
    smem = pl.BlockSpec(memory_space=pltpu.SMEM)
    hps = ATTN_HEADS_PER_STEP
    o = pl.pallas_call(
        functools.partial(_attn_kernel, tq, tk, out_scale),
        grid=(bsz, N_HEADS // hps, seqlen // tq),
        in_specs=[smem, smem,
                  pl.BlockSpec((1, tq, hps * hd2), lambda b, h, i: (b, i, h)),
                  pl.BlockSpec((1, seqlen, hps * hd2), lambda b, h, i: (b, 0, h)),
                  pl.BlockSpec((1, hps, V_ROWS, seqlen), lambda b, h, i: (b, h, 0, 0)),
                  pl.BlockSpec((hps, tk, tq), lambda b, h, i: (h, 0, 0)),
                  pl.BlockSpec((hps, CORNER, CORNER), lambda b, h, i: (h, 0, 0)),
                  pl.BlockSpec((1, hd2), lambda b, h, i: (0, 0))],
        out_specs=pl.BlockSpec((1, tq, hps * hd2), lambda b, h, i: (b, i, h)),
        out_shape=jax.ShapeDtypeStruct((bsz, seqlen, N_HEADS * hd2), bf16),
        scratch_shapes=[pltpu.VMEM((hps, 2 * tq, hd2), bf16), pltpu.VMEM((hps, 1, 2 * tq), f32),
                        pltpu.VMEM((hps, V_ROWS, 2 * tq), f32), pltpu.VMEM((2, hps, tk, 2 * tq), f32)],
        compiler_params=_cparams("parallel", "parallel", "parallel"),
        name="diff_attn",
    )(lam.reshape(1).astype(f32), far, q3, k3, vt, diag, corner, subln.astype(f32)[None, :])
    return o.reshape(bsz * seqlen, N_HEADS * hd2)


def _gelu_tanh(v):
    c = 2.0 * math.sqrt(2.0 / math.pi)
    return v * jax.nn.sigmoid(v * (c + (c * 0.044715) * (v * v)))


def _ffn_ple_kernel(tiles_per_seq, fc, has_attn, has_q, has_kv, *refs):
    refs = list(refs)
    x_ref, p_ref = refs[:2]
    del refs[:2]
    if has_attn:
        a_ref, wo_ref = refs[:2]
        del refs[:2]
    fg_ref, wup_ref, cw_ref, cb_ref, wdn_ref, pg_ref, wgate_ref, wproj_ref = refs[:8]
    del refs[:8]
    if has_q or has_kv:
        seg_ref = refs.pop(0)
    if has_q:
        qg_ref, wq_ref, qgain_ref = refs[:3]
        del refs[:3]
    if has_kv:
        kvg_ref, wk_ref, wvt_ref, kgain_ref = refs[:4]
        del refs[:4]
    o_ref = refs.pop(0)
    if has_q:
        q_ref = refs.pop(0)
    if has_kv:
        k_ref, vt_ref = refs[:2]
        del refs[:2]
    carry_ref, act_ref, h_ref = refs

    i = pl.program_id(0)
    tm = x_ref.shape[0]
    d_ff = wdn_ref.shape[0]
    halo = SUBLANES

    @pl.when(i % tiles_per_seq == 0)
    def _():
        carry_ref[...] = jnp.zeros_like(carry_ref)

    x = x_ref[...]
    if has_attn:
        x = x + jnp.dot(a_ref[...], wo_ref[...], preferred_element_type=f32)
    h_ref[...] = _rms(x, fg_ref[...]).astype(bf16)
    row = lax.broadcasted_iota(jnp.int32, (halo, fc), 0)

    def up_proj(col):
        return jnp.dot(h_ref[...], wup_ref[:, col:col + fc], preferred_element_type=f32)

    def conv(col, up):
        prev = carry_ref[:, col:col + fc]
        carry_ref[:, col:col + fc] = up[tm - halo:]
        c = cb_ref[:, col:col + fc] + cw_ref[2:3, col:col + fc] * up
        for back in range(1, CONV_WIDTH):
            r = pltpu.roll(up, back, 0)
            head = jnp.where(row < back, pltpu.roll(prev, back, 0), r[:halo])
            c = c + cw_ref[2 - back:3 - back, col:col + fc] * jnp.concatenate([head, r[halo:]], axis=0)
        return c

    cols = list(range(0, d_ff, fc))
    bounds = [round(s * len(cols) / DOWN_SPLITS) * fc for s in range(DOWN_SPLITS + 1)]
    pending = None
    ups = (up_proj(cols[0]), up_proj(d_ff + cols[0]))
    for idx, col in enumerate(cols):
        nxt = (up_proj(cols[idx + 1]), up_proj(d_ff + cols[idx + 1])) if idx + 1 < len(cols) else None
        if pending is not None:
            x = x + jnp.dot(act_ref[:, pending[0]:pending[1]], wdn_ref[pending[0]:pending[1], :],
                            preferred_element_type=f32)
            pending = None
        act_ref[:, col:col + fc] = (_gelu_tanh(conv(col, ups[0])) * conv(d_ff + col, ups[1])).astype(bf16)
        if col + fc in bounds:
            pending = (bounds[bounds.index(col + fc) - 1], col + fc)
        ups = nxt
    lo, hi = pending
    halves = [slice(0, tm // 2), slice(tm // 2, tm)]
    xs = [x[r] + jnp.dot(act_ref[r, lo:hi], wdn_ref[lo:hi, :], preferred_element_type=f32) for r in halves]
    projs = [jnp.dot(p_ref[r, :].astype(bf16), wproj_ref[...], preferred_element_type=f32) for r in halves]
    gates = [jax.nn.sigmoid(jnp.dot(_rms(xh, pg_ref[...]).astype(bf16), wgate_ref[...],
                                    preferred_element_type=f32)) for xh in xs]
    xs = [xh + g * pr for xh, g, pr in zip(xs, gates, projs)]
    for r, xh in zip(halves, xs):
        o_ref[r, :] = xh
    if has_q or has_kv:
        xn = jnp.concatenate([xh * lax.rsqrt(jnp.mean(xh * xh, axis=-1, keepdims=True) + EPS) for xh in xs],
                             axis=0)
    if has_q:
        q = jnp.dot((xn * qg_ref[...]).astype(bf16), wq_ref[...], preferred_element_type=f32)
        q_ref[...] = _seg_norm(q, seg_ref, qgain_ref).astype(q_ref.dtype)
    if has_kv:
        hk = (xn * kvg_ref[...]).astype(bf16)
        k = jnp.dot(hk, wk_ref[...], preferred_element_type=f32)
        k_ref[...] = _seg_norm(k, seg_ref, kgain_ref).astype(k_ref.dtype)
        _store_vt(vt_ref, lax.dot_general(wvt_ref[...], hk, (((1,), (1,)), ((), ())),
                                          preferred_element_type=f32))


def _ffn_ple(x, p, layer, ffn, ple, seqlen, attn=None, next_q=None, next_kv=None, tm=512, fc=256):
    n, d = x.shape
    ffn_g, w_up, conv_w, conv_b, w_down = ffn
    ple_g, w_gate, w_proj = ple
    d_ff = w_down.shape[-2]
    bsz = n // seqlen
    tps = seqlen // tm
    row = lambda i: (i, 0)

    def full(a):
        if a.ndim == 3:
            return pl.BlockSpec((None,) + a.shape[1:], lambda i: (layer, 0, 0), pipeline_mode=pl.Buffered(1))
        return pl.BlockSpec(a.shape, lambda i: (0, 0), pipeline_mode=pl.Buffered(1))

    vec = lambda a: a.astype(f32)[None, :]
    args = [x, p]
    in_specs = [pl.BlockSpec((tm, d), row), pl.BlockSpec((None, tm, p.shape[2]), lambda i: (layer, i, 0))]
    if attn is not None:
        args += [attn[0], attn[1]]
        in_specs += [pl.BlockSpec((tm, attn[0].shape[1]), row), full(attn[1])]
    weights = [vec(ffn_g), w_up, conv_w.astype(f32), vec(conv_b), w_down, vec(ple_g), w_gate, w_proj]
    out_specs = [pl.BlockSpec((tm, d), row)]
    out_shape = [jax.ShapeDtypeStruct((n, d), f32)]
    if next_q is not None or next_kv is not None:
        weights.append(_segment_ones())
    if next_q is not None:
        weights += [vec(next_q[0]), next_q[1], _segment_gain(next_q[2])]
        out_specs.append(pl.BlockSpec((tm, next_q[1].shape[1]), row))
        out_shape.append(jax.ShapeDtypeStruct((n, next_q[1].shape[1]), bf16))
    if next_kv is not None:
        weights += [vec(next_kv[0]), next_kv[1], next_kv[2], _segment_gain(next_kv[3])]
        out_specs += [pl.BlockSpec((tm, next_kv[1].shape[1]), row),
                      pl.BlockSpec((1, N_HEADS, V_ROWS, tm), lambda i: (i // tps, 0, 0, i % tps))]
        out_shape += [jax.ShapeDtypeStruct((n, next_kv[1].shape[1]), bf16),
                      jax.ShapeDtypeStruct((bsz, N_HEADS, V_ROWS, seqlen), bf16)]
    args += weights
    in_specs += [full(w) for w in weights]
    return pl.pallas_call(
        functools.partial(_ffn_ple_kernel, tps, fc, attn is not None, next_q is not None, next_kv is not None),
        grid=(n // tm,),
        in_specs=in_specs,
        out_specs=out_specs,
        out_shape=out_shape,
        scratch_shapes=[pltpu.VMEM((SUBLANES, 2 * d_ff), f32),
                        pltpu.VMEM((tm, d_ff), bf16),
                        pltpu.VMEM((tm, d), bf16)],
        compiler_params=_cparams("arbitrary"),
        name="ffn_ple",
    )(*args)


def kernel(x, p, ssm_norm, ssm_w_in, ssm_log_dt, ssm_lambda_re, ssm_lambda_im, ssm_b_re, ssm_b_im, ssm_c_re, ssm_c_im, ssm_d, ssm_w_glu, kv_norm, kv_w, k_norm, attn_norm, attn_w_q, q_norm, lambda_q1, lambda_k1, lambda_q2, lambda_k2, subln, attn_w_o, rel_bias, ffn_norm, ffn_w_up, ffn_conv_w, ffn_conv_b, ffn_w_down, ple_norm, ple_w_gate, ple_w_proj):
    bsz, seqlen, d = x.shape
    depth = p.shape[0]
    n_a = ssm_norm.shape[0]
    n = bsz * seqlen
    x = x.reshape(n, d).astype(f32)
    p = p.reshape(depth, n, p.shape[-1])
    nk = N_HEADS * 2 * HEAD_DIM
    assert 1 <= n_a < depth
    diag, corner, far = _bias_tables(rel_bias, ATTN_BLOCK, ATTN_BLOCK)
    tables = jax.vmap(functools.partial(_ssm_tables, n_chunks=seqlen // SSM_T))(
        ssm_log_dt, ssm_lambda_re, ssm_lambda_im, ssm_b_re, ssm_b_im, ssm_c_re, ssm_c_im)
    w_in_t = jnp.swapaxes(ssm_w_in, 1, 2).astype(bf16)
    w_glu, w_q, w_o = ssm_w_glu.astype(bf16), attn_w_q.astype(bf16), attn_w_o.astype(bf16)
    w_up, w_down = ffn_w_up.astype(bf16), ffn_w_down.astype(bf16)
    w_gate, w_proj = ple_w_gate.astype(bf16), ple_w_proj.astype(bf16)
    lam_all = (jnp.exp(jnp.sum(lambda_q1.astype(f32) * lambda_k1.astype(f32), axis=-1))
               - jnp.exp(jnp.sum(lambda_q2.astype(f32) * lambda_k2.astype(f32), axis=-1)))
    q = k = vt = None
    for i in range(depth):
        attn = None
        if i < n_a:
            x3 = x.reshape(bsz, seqlen, d)
            ut = _win_t(x3, ssm_norm[i], w_in_t[i])
            yt = _ssm(ut, tuple(t[i] for t in tables))
            x = _glu_res(x3, yt, ut, ssm_d[i], w_glu[i]).reshape(n, d)
        else:
            j = i - n_a
            lam_init = 0.8 - 0.6 * math.exp(-0.3 * i)
            o = _diff_attention(q, k, vt, diag, corner, far, lam_all[j] + lam_init, subln[j], 1.0 - lam_init,
                                bsz, seqlen, ATTN_BLOCK, ATTN_BLOCK)
            attn = (o, w_o[j])
        j_next = i + 1 - n_a
        next_q = next_kv = None
        if 0 <= j_next < depth - n_a:
            next_q = (attn_norm[j_next], w_q[j_next], q_norm[j_next])
            if j_next == 0:
                next_kv = (kv_norm, kv_w[:, :nk].astype(bf16), kv_w[:, nk:].T.astype(bf16), k_norm)
        outs = _ffn_ple(x, p, i,
                        (ffn_norm[i], w_up, ffn_conv_w[i], ffn_conv_b[i], w_down),
                        (ple_norm[i], w_gate, w_proj),
                        seqlen, attn=attn, next_q=next_q, next_kv=next_kv)
        x = outs[0]
        if next_q is not None:
            q = outs[1]
        if next_kv is not None:
            k, vt = outs[2], outs[3]
    return x.reshape(bsz, seqlen, d)
```

```python
import functools
import math

import jax
import jax.numpy as jnp
from jax import lax
from jax.experimental import pallas as pl
from jax.experimental.pallas import tpu as pltpu

EPS = 1e-6
NEG_INF = -1e30

SSM_GROUP_DIM = 16
SSM_STATE = 64
N_HEADS = 8
HEAD_DIM = 64
REL_BUCKETS = 32
REL_MAX_EXACT = REL_BUCKETS // 2
REL_MAX_DIST = 128
CONV_WIDTH = 3

LANES = 128
SUBLANES = 8
MXU_DIM = 256
VMEM_LIMIT_BYTES = 56 * 1024 * 1024

SSM_T = MXU_DIM // SSM_GROUP_DIM
SEG_CHUNK = MXU_DIM
ATTN_BLOCK = 256
ATTN_HEADS_PER_STEP = 4
CORNER = LANES
V_ROWS = 2 * HEAD_DIM + 16
LOG2E = math.log2(math.e)
DOWN_SPLITS = 1

bf16 = jnp.bfloat16
f32 = jnp.float32


def _cparams(*sem):
    return pltpu.CompilerParams(dimension_semantics=sem, vmem_limit_bytes=VMEM_LIMIT_BYTES)


def _rms(xf, g):
    return xf * lax.rsqrt(jnp.mean(xf * xf, axis=-1, keepdims=True) + EPS) * g


def _seg_norm(y, seg_ref, gain_ref):
    cols = []
    for c in range(y.shape[1] // SEG_CHUNK):
        yc = y[:, c * SEG_CHUNK:(c + 1) * SEG_CHUNK]
        ss = jnp.dot((yc * yc).astype(bf16), seg_ref[...], preferred_element_type=f32)
        cols.append(yc * lax.rsqrt(ss * (1.0 / HEAD_DIM) + EPS) * gain_ref[...])
    return jnp.concatenate(cols, axis=1)


def _segment_ones():
    idx = jnp.arange(SEG_CHUNK) // HEAD_DIM
    return (idx[:, None] == idx[None, :]).astype(bf16)


def _segment_gain(g):
    return jnp.tile(g.astype(f32), SEG_CHUNK // HEAD_DIM)[None, :]


def _store_vt(vt_ref, vt):
    hd2 = 2 * HEAD_DIM
    for hh in range(vt_ref.shape[1]):
        vt_ref[0, hh, :hd2, :] = vt[hh * hd2:(hh + 1) * hd2].astype(vt_ref.dtype)
        vt_ref[0, hh, hd2:, :] = jnp.ones((V_ROWS - hd2, vt.shape[1]), vt_ref.dtype)


def _win_kernel(x_ref, g_ref, wt_ref, ut_ref):
    n_chunks, nt, d = x_ref.shape
    g_n = ut_ref.shape[0]
    xt = pltpu.einshape("ktd->tkd", x_ref[...]).reshape(nt * n_chunks, d)
    h = _rms(xt, g_ref[...]).astype(bf16)
    ut = lax.dot_general(wt_ref[...], h, (((1,), (1,)), ((), ())), preferred_element_type=f32)
    for tt in range(nt):
        blk = ut[:, tt * n_chunks:(tt + 1) * n_chunks].reshape(g_n, SSM_GROUP_DIM, n_chunks)
        ut_ref[:, tt * SSM_GROUP_DIM:(tt + 1) * SSM_GROUP_DIM, :] = blk.astype(ut_ref.dtype)


def _by_chunk(x3):
    bsz, seqlen, d = x3.shape
    return x3.reshape(bsz, seqlen // SSM_T, SSM_T // SUBLANES, SUBLANES, d)


def _chunk_spec(x5):
    _, n_chunks, _, nt, d = x5.shape
    return pl.BlockSpec((None, n_chunks, None, nt, d), lambda b, t: (b, 0, t, 0, 0))


def _win_t(x3, g, w_t):
    bsz, seqlen, d = x3.shape
    n_chunks = seqlen // SSM_T
    assert n_chunks == LANES
    g_n = d // SSM_GROUP_DIM
    x5 = _by_chunk(x3)
    return pl.pallas_call(
        _win_kernel,
        grid=(bsz, SSM_T // SUBLANES),
        in_specs=[_chunk_spec(x5),
                  pl.BlockSpec((1, d), lambda b, t: (0, 0)),
                  pl.BlockSpec(w_t.shape, lambda b, t: (0, 0))],
        out_specs=pl.BlockSpec((g_n, SUBLANES * SSM_GROUP_DIM, n_chunks), lambda b, t: (0, t, b)),
        out_shape=jax.ShapeDtypeStruct((g_n, SSM_T * SSM_GROUP_DIM, bsz * n_chunks), bf16),
        compiler_params=_cparams("parallel", "parallel"),
        name="win_t",
    )(x5, g.astype(f32)[None, :], w_t)


def _ssm_kernel(n_rounds, u_ref, mt_ref, w_ref, v_ref, pw_ref, y_ref, xin_ref):
    u0 = u_ref[0]
    u1 = u_ref[1]
    half = LANES
    seg = LANES
    st = (jnp.dot(w_ref[0], u0, preferred_element_type=f32)
          + jnp.dot(w_ref[1], u1, preferred_element_type=f32))
    n_seq = st.shape[1] // seg
    xr = jnp.stack([st[:half, b * seg:(b + 1) * seg] for b in range(n_seq)])
    xi = jnp.stack([st[half:, b * seg:(b + 1) * seg] for b in range(n_seq)])
    chunk = lax.broadcasted_iota(jnp.int32, xr.shape, 2)
    for j in range(n_rounds):
        d = 1 << j
        ar = pw_ref[j, 0]
        ai = pw_ref[j, 1]
        sr = pltpu.roll(xr, d, 2)
        si = pltpu.roll(xi, d, 2)
        xr, xi = xr + (ar * sr - ai * si), xi + (ar * si + ai * sr)
    xr = jnp.where(chunk >= 1, pltpu.roll(xr, 1, 2), 0.0).astype(bf16)
    xi = jnp.where(chunk >= 1, pltpu.roll(xi, 1, 2), 0.0).astype(bf16)
    for b in range(n_seq):
        xin_ref[:half, b * seg:(b + 1) * seg] = xr[b]
        xin_ref[half:, b * seg:(b + 1) * seg] = xi[b]
    ys = jnp.dot(v_ref[...], xin_ref[...], preferred_element_type=f32)
    y_ref[0] = (jnp.dot(mt_ref[0], u0, preferred_element_type=f32) + ys[:MXU_DIM]).astype(y_ref.dtype)
    y_ref[1] = (jnp.dot(mt_ref[1], u1, preferred_element_type=f32) + ys[MXU_DIM:]).astype(y_ref.dtype)


def _ssm_tables(log_dt, lam_re, lam_im, b_re, b_im, c_re, c_im, n_chunks):
    g_n, p_n = lam_re.shape
    t_n, c_n = SSM_T, SSM_GROUP_DIM
    dt = jnp.exp(log_dt.astype(f32))[:, None]
    lr = lam_re.astype(f32)
    li = lam_im.astype(f32)

    def power(n):
        nn = jnp.asarray(n, f32).reshape((-1, 1, 1))
        mag = jnp.exp(lr * dt * nn)
        return mag * jnp.cos(li * dt * nn), mag * jnp.sin(li * dt * nn)

    ab_r, ab_i = power([1])
    ab_r, ab_i = ab_r[0], ab_i[0]
    den = lr * lr + li * li
    f_r = ((ab_r - 1.0) * lr + ab_i * li) / den
    f_i = (ab_i * lr - (ab_r - 1.0) * li) / den
    br = b_re.astype(f32)
    bi = b_im.astype(f32)
    bb_r = f_r[..., None] * br - f_i[..., None] * bi
    bb_i = f_r[..., None] * bi + f_i[..., None] * br
    cr = c_re.astype(f32)
    ci = c_im.astype(f32)

    pr, pi = power(jnp.arange(t_n + 1))
    t_r = pr[:t_n, :, :, None] * bb_r[None] - pi[:t_n, :, :, None] * bb_i[None]
    t_i = pr[:t_n, :, :, None] * bb_i[None] + pi[:t_n, :, :, None] * bb_r[None]
    kern = (jnp.einsum("gcp,dgpk->gcdk", cr, t_r[::-1], precision=lax.Precision.HIGHEST)
            - jnp.einsum("gcp,dgpk->gcdk", ci, t_i[::-1], precision=lax.Precision.HIGHEST))
    kern = kern.reshape(g_n, c_n, t_n * c_n)
    kern = jnp.concatenate([kern, jnp.zeros_like(kern)], axis=-1)
    mt = jnp.concatenate([kern[:, :, (t_n - 1 - t) * c_n:(2 * t_n - 1 - t) * c_n] for t in range(t_n)], axis=1)
    w_r = t_r[::-1].transpose(1, 2, 0, 3).reshape(g_n, p_n, t_n * c_n)
    w_i = t_i[::-1].transpose(1, 2, 0, 3).reshape(g_n, p_n, t_n * c_n)
    pr1 = pr[1:].transpose(1, 0, 2)
    pi1 = pi[1:].transpose(1, 0, 2)
    v_r = (cr[:, None] * pr1[:, :, None, :] - ci[:, None] * pi1[:, :, None, :]).reshape(g_n, t_n * c_n, p_n)
    v_i = -(cr[:, None] * pi1[:, :, None, :] + ci[:, None] * pr1[:, :, None, :]).reshape(g_n, t_n * c_n, p_n)

    gp = g_n // 2
    z = jnp.zeros((gp, p_n, t_n * c_n), f32)
    w_r = w_r.reshape(gp, 2, p_n, t_n * c_n)
    w_i = w_i.reshape(gp, 2, p_n, t_n * c_n)
    w0 = jnp.concatenate([w_r[:, 0], z, w_i[:, 0], z], axis=1)
    w1 = jnp.concatenate([z, w_r[:, 1], z, w_i[:, 1]], axis=1)
    w_pack = jnp.stack([w0, w1], axis=1)
    zv = jnp.zeros((gp, t_n * c_n, p_n), f32)
    v_r = v_r.reshape(gp, 2, t_n * c_n, p_n)
    v_i = v_i.reshape(gp, 2, t_n * c_n, p_n)
    v_pack = jnp.concatenate([
        jnp.concatenate([v_r[:, 0], zv, v_i[:, 0], zv], axis=-1),
        jnp.concatenate([zv, v_r[:, 1], zv, v_i[:, 1]], axis=-1)], axis=1)
    n_rounds = (n_chunks - 1).bit_length()
    qr, qi = power(t_n * (2 ** jnp.arange(max(n_rounds, 1))))
    pw = jnp.stack([qr.reshape(-1, gp, 2 * p_n), qi.reshape(-1, gp, 2 * p_n)], axis=2)
    reach = (jnp.arange(LANES)[None, :] >= (2 ** jnp.arange(pw.shape[0]))[:, None]).astype(f32)
    pw = pw.transpose(1, 0, 2, 3)[..., None] * reach[None, :, None, None, :]
    return mt.reshape(gp, 2, t_n * c_n, t_n * c_n).astype(bf16), w_pack.astype(bf16), v_pack.astype(bf16), pw


def _ssm(ut, tables, layer):
    mt, w_pack, v_pack, pw = tables
    n_rounds = pw.shape[2]
    g_n, k, cols = ut.shape
    gp = g_n // 2
    return pl.pallas_call(
        functools.partial(_ssm_kernel, n_rounds),
        grid=(gp,),
        in_specs=[pl.BlockSpec((2, k, cols), lambda g: (g, 0, 0)),
                  pl.BlockSpec((None, None, 2, k, k), lambda g: (layer, g, 0, 0, 0)),
                  pl.BlockSpec((None, None, 2, k, k), lambda g: (layer, g, 0, 0, 0)),
                  pl.BlockSpec((None, None, 2 * k, k), lambda g: (layer, g, 0, 0)),
                  pl.BlockSpec((None, None) + pw.shape[2:], lambda g: (layer, g, 0, 0, 0, 0))],
        out_specs=pl.BlockSpec((2, k, cols), lambda g: (g, 0, 0)),
        out_shape=jax.ShapeDtypeStruct((g_n, k, cols), bf16),
        scratch_shapes=[pltpu.VMEM((k, cols), bf16)],
        compiler_params=_cparams("parallel"),
        name="ssm",
    )(ut, mt, w_pack, v_pack, pw)


def _glu_kernel(x_ref, y_ref, u_ref, d_ref, w_ref, o_ref):
    n_chunks, nt, dm = x_ref.shape
    g_n = y_ref.shape[0]
    cols = []
    for tt in range(nt):
        sl = slice(tt * SSM_GROUP_DIM, (tt + 1) * SSM_GROUP_DIM)
        y = y_ref[:, sl, :].astype(f32) + d_ref[...] * u_ref[:, sl, :].astype(f32)
        cols.append(jax.nn.gelu(y).reshape(g_n * SSM_GROUP_DIM, n_chunks))
    gt = jnp.concatenate(cols, axis=1)
    r = jnp.dot(gt.astype(bf16).T, w_ref[...], preferred_element_type=f32)
    upd = r[:, :dm] * jax.nn.sigmoid(r[:, dm:])
    o_ref[...] = x_ref[...] + pltpu.einshape("tkd->ktd", upd.reshape(nt, n_chunks, dm))


def _glu_res(x3, yt, ut, d_skip, w_glu):
    bsz, seqlen, d = x3.shape
    g_n, _, cols = yt.shape
    n_chunks = cols // bsz
    d_tab = jnp.broadcast_to(d_skip.astype(f32).reshape(g_n, SSM_GROUP_DIM, 1), (g_n, SSM_GROUP_DIM, n_chunks))
    feat = pl.BlockSpec((g_n, SUBLANES * SSM_GROUP_DIM, n_chunks), lambda b, t: (0, t, b))
    x5 = _by_chunk(x3)
    out = pl.pallas_call(
        _glu_kernel,
        grid=(bsz, SSM_T // SUBLANES),
        in_specs=[_chunk_spec(x5), feat, feat,
                  pl.BlockSpec(d_tab.shape, lambda b, t: (0, 0, 0)),
                  pl.BlockSpec(w_glu.shape, lambda b, t: (0, 0))],
        out_specs=_chunk_spec(x5),
        out_shape=jax.ShapeDtypeStruct(x5.shape, f32),
        compiler_params=_cparams("parallel", "parallel"),
        name="glu_res",
    )(x5, yt, ut, d_tab, w_glu)
    return out.reshape(bsz, seqlen, d)


def _attn_kernel(tq, tk, out_scale, lam_ref, far_ref, q_ref, k_ref, vt_ref, diag_ref, corner_ref, sub_ref,
                 o_ref, qm_ref, m_ref, acc_ref, s_ref):
    qi = pl.program_id(2)
    hd2 = 2 * HEAD_DIM
    heads = range(vt_ref.shape[1])
    lane = lax.broadcasted_iota(jnp.int32, (tq, hd2), 1)
    for hh in heads:
        q = q_ref[0, :, hh * hd2:(hh + 1) * hd2].astype(f32) * (HEAD_DIM ** -0.5 * LOG2E)
        qm_ref[hh, :tq] = jnp.where(lane < HEAD_DIM, q, 0.0).astype(bf16)
        qm_ref[hh, tq:] = jnp.where(lane >= HEAD_DIM, q, 0.0).astype(bf16)
    m_ref[...] = jnp.full_like(m_ref, NEG_INF)
    acc_ref[...] = jnp.zeros_like(acc_ref)

    def fill(slot, ki):
        for hh in heads:
            k = k_ref[0, pl.ds(pl.multiple_of(ki * tk, tk), tk), hh * hd2:(hh + 1) * hd2]
            s_ref[slot, hh] = lax.dot_general(k, qm_ref[hh], (((1,), (1,)), ((), ())),
                                              preferred_element_type=f32)

    def accumulate(hh, ki, s, shift):
        vt = vt_ref[0, hh, :, pl.ds(pl.multiple_of(ki * tk, tk), tk)]
        m_prev = m_ref[hh]
        m_new = jnp.maximum(m_prev, jnp.max(s, axis=0, keepdims=True) + shift)
        p = jnp.exp2(s - (m_new - shift))
        acc_ref[hh] = (jnp.exp2(m_prev - m_new) * acc_ref[hh]
                       + jnp.dot(vt, p.astype(bf16), preferred_element_type=f32))
        m_ref[hh] = m_new

    def earlier(slot, ki):
        near = jnp.where(ki == qi - 1, 1.0, 0.0)
        for hh in heads:
            s = s_ref[slot, hh]
            c = near * corner_ref[hh]
            lo = s[tk - CORNER:]
            lo = jnp.concatenate([lo[:, :CORNER] + c, lo[:, CORNER:tq], lo[:, tq:tq + CORNER] + c,
                                  lo[:, tq + CORNER:]], axis=1)
            accumulate(hh, ki, jnp.concatenate([s[:tk - CORNER], lo], axis=0),
                       far_ref[pl.program_id(1) * len(heads) + hh])

    def pair(i, carry):
        fill(1, 2 * i + 1)
        earlier(0, 2 * i)
        fill(0, 2 * i + 2)
        earlier(1, 2 * i + 1)
        return carry

    fill(0, 0)
    lax.fori_loop(0, qi // 2, pair, 0)

    @pl.when(qi % 2 == 1)
    def _():
        fill(1, qi)
        earlier(0, qi - 1)

    for hh in heads:
        b = diag_ref[hh]
        accumulate(hh, qi, s_ref[qi % 2, hh] + jnp.concatenate([b, b], axis=1), 0.0)
        acc = acc_ref[hh]
        ot = acc[:hd2] / acc[hd2:hd2 + 1]
        ot = ot[:, :tq] - lam_ref[0] * ot[:, tq:]
        ot = ot * lax.rsqrt(jnp.mean(ot * ot, axis=0, keepdims=True) + EPS)
        o_ref[0, :, hh * hd2:(hh + 1) * hd2] = (ot.T * sub_ref[...] * out_scale).astype(o_ref.dtype)


def _bias_tables(rel_bias, tq, tk):
    n = jnp.arange(2 * tq)
    nf = jnp.maximum(n, REL_MAX_EXACT).astype(f32)
    large = REL_MAX_EXACT + (jnp.log(nf / REL_MAX_EXACT) / math.log(REL_MAX_DIST / REL_MAX_EXACT)
                             * (REL_BUCKETS - REL_MAX_EXACT)).astype(jnp.int32)
    large = jnp.minimum(large, REL_BUCKETS - 1)
    bucket = jnp.where(n < REL_MAX_EXACT, n, large)
    onehot = (bucket[:, None] == jnp.arange(REL_BUCKETS)[None, :]).astype(f32)
    by_dist = jnp.dot(onehot, rel_bias.astype(f32), precision=lax.Precision.HIGHEST).T * LOG2E
    n_heads = by_dist.shape[0]
    neg = jnp.full((n_heads, tk), NEG_INF, f32)
    period = tq + tk
    tiles = []
    for d, tail in ((0, neg), (1, by_dist[:, tq - tk:tq])):
        r = jnp.concatenate([by_dist[:, d * tq:(d + 1) * tq], tail], axis=1)
        t = jnp.tile(r, (1, tk))[:, :tk * (period - 1)].reshape(n_heads, tk, period - 1)
        tiles.append(t[:, :, :tq])
    far = rel_bias.astype(f32)[REL_BUCKETS - 1] * LOG2E
    corner = tiles[1][:, tk - CORNER:, :CORNER] - far[:, None, None]
    return tiles[0], corner, far


def _far_bucket_from():
    n = REL_MAX_EXACT
    while REL_MAX_EXACT + int(math.log(n / REL_MAX_EXACT) / math.log(REL_MAX_DIST / REL_MAX_EXACT)
                              * (REL_BUCKETS - REL_MAX_EXACT)) < REL_BUCKETS - 1:
        n += 1
    return n


def _diff_attention(q, k, vt, diag, corner, far, lam, subln, out_scale, bsz, seqlen, tq, tk):
    assert tq == tk and tq - CORNER + 1 >= _far_bucket_from()
    hd2 = 2 * HEAD_DIM
    q3 = q.reshape(bsz, seqlen, N_HEADS * hd2)
    k3 = k.reshape(bsz, seqlen, N_HEADS * hd2)
    smem = pl.BlockSpec(memory_space=pltpu.SMEM)
    hps = ATTN_HEADS_PER_STEP
    o = pl.pallas_call(
        functools.partial(_attn_kernel, tq, tk, out_scale),
        grid=(bsz, N_HEADS // hps, seqlen // tq),
        in_specs=[smem, smem,
                  pl.BlockSpec((1, tq, hps * hd2), lambda b, h, i: (b, i, h)),
                  pl.BlockSpec((1, seqlen, hps * hd2), lambda b, h, i: (b, 0, h)),
                  pl.BlockSpec((1, hps, V_ROWS, seqlen), lambda b, h, i: (b, h, 0, 0)),
                  pl.BlockSpec((hps, tk, tq), lambda b, h, i: (h, 0, 0)),
                  pl.BlockSpec((hps, CORNER, CORNER), lambda b, h, i: (h, 0, 0)),
                  pl.BlockSpec((1, hd2), lambda b, h, i: (0, 0))],
        out_specs=pl.BlockSpec((1, tq, hps * hd2), lambda b, h, i: (b, i, h)),
        out_shape=jax.ShapeDtypeStruct((bsz, seqlen, N_HEADS * hd2), bf16),
        scratch_shapes=[pltpu.VMEM((hps, 2 * tq, hd2), bf16), pltpu.VMEM((hps, 1, 2 * tq), f32),
                        pltpu.VMEM((hps, V_ROWS, 2 * tq), f32), pltpu.VMEM((2, hps, tk, 2 * tq), f32)],
        compiler_params=_cparams("parallel", "parallel", "parallel"),
        name="diff_attn",
    )(lam.reshape(1).astype(f32), far, q3, k3, vt, diag, corner, subln.astype(f32)[None, :])
    return o.reshape(bsz * seqlen, N_HEADS * hd2)


def _gelu_tanh(v):
    c = 2.0 * math.sqrt(2.0 / math.pi)
    return v * jax.nn.sigmoid(v * (c + (c * 0.044715) * (v * v)))


def _ffn_ple_kernel(tiles_per_seq, fc, has_attn, has_q, has_kv, *refs):
    refs = list(refs)
    x_ref, p_ref = refs[:2]
    del refs[:2]
    if has_attn:
        a_ref, wo_ref = refs[:2]
        del refs[:2]
    fg_ref, wup_ref, cw_ref, cb_ref, wdn_ref, pg_ref, wgate_ref, wproj_ref = refs[:8]
    del refs[:8]
    if has_q or has_kv:
        seg_ref = refs.pop(0)
    if has_q:
        qg_ref, wq_ref, qgain_ref = refs[:3]
        del refs[:3]
    if has_kv:
        kvg_ref, wk_ref, wvt_ref, kgain_ref = refs[:4]
        del refs[:4]
    o_ref = refs.pop(0)
    if has_q:
        q_ref = refs.pop(0)
    if has_kv:
        k_ref, vt_ref = refs[:2]
        del refs[:2]
    carry_ref, act_ref, h_ref = refs

    i = pl.program_id(0)
    tm = x_ref.shape[0]
    d_ff = wdn_ref.shape[0]
    halo = SUBLANES

    @pl.when(i % tiles_per_seq == 0)
    def _():
        carry_ref[...] = jnp.zeros_like(carry_ref)

    x = x_ref[...]
    if has_attn:
        x = x + jnp.dot(a_ref[...], wo_ref[...], preferred_element_type=f32)
    h_ref[...] = _rms(x, fg_ref[...]).astype(bf16)
    row = lax.broadcasted_iota(jnp.int32, (halo, fc), 0)

    def up_proj(col):
        return jnp.dot(h_ref[...], wup_ref[:, col:col + fc], preferred_element_type=f32)

    def conv(col, up):
        prev = carry_ref[:, col:col + fc]
        carry_ref[:, col:col + fc] = up[tm - halo:]
        c = cb_ref[:, col:col + fc] + cw_ref[2:3, col:col + fc] * up
        for back in range(1, CONV_WIDTH):
            r = pltpu.roll(up, back, 0)
            head = jnp.where(row < back, pltpu.roll(prev, back, 0), r[:halo])
            c = c + cw_ref[2 - back:3 - back, col:col + fc] * jnp.concatenate([head, r[halo:]], axis=0)
        return c

    cols = list(range(0, d_ff, fc))
    bounds = [round(s * len(cols) / DOWN_SPLITS) * fc for s in range(DOWN_SPLITS + 1)]
    pending = None
    ups = (up_proj(cols[0]), up_proj(d_ff + cols[0]))
    for idx, col in enumerate(cols):
        nxt = (up_proj(cols[idx + 1]), up_proj(d_ff + cols[idx + 1])) if idx + 1 < len(cols) else None
        if pending is not None:
            x = x + jnp.dot(act_ref[:, pending[0]:pending[1]], wdn_ref[pending[0]:pending[1], :],
                            preferred_element_type=f32)
            pending = None
        act_ref[:, col:col + fc] = (_gelu_tanh(conv(col, ups[0])) * conv(d_ff + col, ups[1])).astype(bf16)
        if col + fc in bounds:
            pending = (bounds[bounds.index(col + fc) - 1], col + fc)
        ups = nxt
    lo, hi = pending
    halves = [slice(0, tm // 2), slice(tm // 2, tm)]
    xs = [x[r] + jnp.dot(act_ref[r, lo:hi], wdn_ref[lo:hi, :], preferred_element_type=f32) for r in halves]
    projs = [jnp.dot(p_ref[r, :].astype(bf16), wproj_ref[...], preferred_element_type=f32) for r in halves]
    gates = [jax.nn.sigmoid(jnp.dot(_rms(xh, pg_ref[...]).astype(bf16), wgate_ref[...],
                                    preferred_element_type=f32)) for xh in xs]
    xs = [xh + g * pr for xh, g, pr in zip(xs, gates, projs)]
    for r, xh in zip(halves, xs):
        o_ref[r, :] = xh
    if has_q or has_kv:
        xn = jnp.concatenate([xh * lax.rsqrt(jnp.mean(xh * xh, axis=-1, keepdims=True) + EPS) for xh in xs],
                             axis=0)
    if has_q:
        q = jnp.dot((xn * qg_ref[...]).astype(bf16), wq_ref[...], preferred_element_type=f32)
        q_ref[...] = _seg_norm(q, seg_ref, qgain_ref).astype(q_ref.dtype)
    if has_kv:
        hk = (xn * kvg_ref[...]).astype(bf16)
        k = jnp.dot(hk, wk_ref[...], preferred_element_type=f32)
        k_ref[...] = _seg_norm(k, seg_ref, kgain_ref).astype(k_ref.dtype)
        _store_vt(vt_ref, lax.dot_general(wvt_ref[...], hk, (((1,), (1,)), ((), ())),
                                          preferred_element_type=f32))


def _ffn_ple(x, p, layer, ffn, ple, seqlen, attn=None, next_q=None, next_kv=None, tm=512, fc=256):
    n, d = x.shape
    ffn_g, w_up, conv_w, conv_b, w_down = ffn
    ple_g, w_gate, w_proj = ple
    d_ff = w_down.shape[-2]
    bsz = n // seqlen
    tps = seqlen // tm
    row = lambda i: (i, 0)

    def full(a):
        if a.ndim == 3:
            return pl.BlockSpec((None,) + a.shape[1:], lambda i: (layer, 0, 0), pipeline_mode=pl.Buffered(1))
        return pl.BlockSpec(a.shape, lambda i: (0, 0), pipeline_mode=pl.Buffered(1))

    vec = lambda a: a.astype(f32)[None, :]
    args = [x, p]
    in_specs = [pl.BlockSpec((tm, d), row), pl.BlockSpec((None, tm, p.shape[2]), lambda i: (layer, i, 0))]
    if attn is not None:
        args += [attn[0], attn[1]]
        in_specs += [pl.BlockSpec((tm, attn[0].shape[1]), row), full(attn[1])]
    weights = [vec(ffn_g), w_up, conv_w.astype(f32), vec(conv_b), w_down, vec(ple_g), w_gate, w_proj]
    out_specs = [pl.BlockSpec((tm, d), row)]
    out_shape = [jax.ShapeDtypeStruct((n, d), f32)]
    if next_q is not None or next_kv is not None:
        weights.append(_segment_ones())
    if next_q is not None:
        weights += [vec(next_q[0]), next_q[1], _segment_gain(next_q[2])]
        out_specs.append(pl.BlockSpec((tm, next_q[1].shape[1]), row))
        out_shape.append(jax.ShapeDtypeStruct((n, next_q[1].shape[1]), bf16))
    if next_kv is not None:
        weights += [vec(next_kv[0]), next_kv[1], next_kv[2], _segment_gain(next_kv[3])]
        out_specs += [pl.BlockSpec((tm, next_kv[1].shape[1]), row),
                      pl.BlockSpec((1, N_HEADS, V_ROWS, tm), lambda i: (i // tps, 0, 0, i % tps))]
        out_shape += [jax.ShapeDtypeStruct((n, next_kv[1].shape[1]), bf16),
                      jax.ShapeDtypeStruct((bsz, N_HEADS, V_ROWS, seqlen), bf16)]
    args += weights
    in_specs += [full(w) for w in weights]
    return pl.pallas_call(
        functools.partial(_ffn_ple_kernel, tps, fc, attn is not None, next_q is not None, next_kv is not None),
        grid=(n // tm,),
        in_specs=in_specs,
        out_specs=out_specs,
        out_shape=out_shape,
        scratch_shapes=[pltpu.VMEM((SUBLANES, 2 * d_ff), f32),
                        pltpu.VMEM((tm, d_ff), bf16),
                        pltpu.VMEM((tm, d), bf16)],
        compiler_params=_cparams("arbitrary"),
        name="ffn_ple",
    )(*args)


def kernel(x, p, ssm_norm, ssm_w_in, ssm_log_dt, ssm_lambda_re, ssm_lambda_im, ssm_b_re, ssm_b_im, ssm_c_re, ssm_c_im, ssm_d, ssm_w_glu, kv_norm, kv_w, k_norm, attn_norm, attn_w_q, q_norm, lambda_q1, lambda_k1, lambda_q2, lambda_k2, subln, attn_w_o, rel_bias, ffn_norm, ffn_w_up, ffn_conv_w, ffn_conv_b, ffn_w_down, ple_norm, ple_w_gate, ple_w_proj):
    bsz, seqlen, d = x.shape
    depth = p.shape[0]
    n_a = ssm_norm.shape[0]
    n = bsz * seqlen
    x = x.reshape(n, d).astype(f32)
    p = p.reshape(depth, n, p.shape[-1])
    nk = N_HEADS * 2 * HEAD_DIM
    assert 1 <= n_a < depth
    diag, corner, far = _bias_tables(rel_bias, ATTN_BLOCK, ATTN_BLOCK)
    tables = jax.vmap(functools.partial(_ssm_tables, n_chunks=seqlen // SSM_T))(
        ssm_log_dt, ssm_lambda_re, ssm_lambda_im, ssm_b_re, ssm_b_im, ssm_c_re, ssm_c_im)
    w_in_t = jnp.swapaxes(ssm_w_in, 1, 2).astype(bf16)
    w_glu, w_q, w_o = ssm_w_glu.astype(bf16), attn_w_q.astype(bf16), attn_w_o.astype(bf16)
    w_up, w_down = ffn_w_up.astype(bf16), ffn_w_down.astype(bf16)
    w_gate, w_proj = ple_w_gate.astype(bf16), ple_w_proj.astype(bf16)
    lam_all = (jnp.exp(jnp.sum(lambda_q1.astype(f32) * lambda_k1.astype(f32), axis=-1))
               - jnp.exp(jnp.sum(lambda_q2.astype(f32) * lambda_k2.astype(f32), axis=-1)))
    q = k = vt = None
    for i in range(depth):
        attn = None
        if i < n_a:
            x3 = x.reshape(bsz, seqlen, d)
            ut = _win_t(x3, ssm_norm[i], w_in_t[i])
            yt = _ssm(ut, tables, i)
            x = _glu_res(x3, yt, ut, ssm_d[i], w_glu[i]).reshape(n, d)
        else:
            j = i - n_a
            lam_init = 0.8 - 0.6 * math.exp(-0.3 * i)
            o = _diff_attention(q, k, vt, diag, corner, far, lam_all[j] + lam_init, subln[j], 1.0 - lam_init,
                                bsz, seqlen, ATTN_BLOCK, ATTN_BLOCK)
            attn = (o, w_o[j])
        j_next = i + 1 - n_a
        next_q = next_kv = None
        if 0 <= j_next < depth - n_a:
            next_q = (attn_norm[j_next], w_q[j_next], q_norm[j_next])
            if j_next == 0:
                next_kv = (kv_norm, kv_w[:, :nk].astype(bf16), kv_w[:, nk:].T.astype(bf16), k_norm)
        outs = _ffn_ple(x, p, i,
                        (ffn_norm[i], w_up, ffn_conv_w[i], ffn_conv_b[i], w_down),
                        (ple_norm[i], w_gate, w_proj),
                        seqlen, attn=attn, next_q=next_q, next_kv=next_kv)
        x = outs[0]
        if next_q is not None:
            q = outs[1]
        if next_kv is not None:
            k, vt = outs[2], outs[3]
    return x.reshape(bsz, seqlen, d)
```

```python
import functools
import math

import jax
import jax.numpy as jnp
from jax import lax
from jax.experimental import pallas as pl
from jax.experimental.pallas import tpu as pltpu

EPS = 1e-6
NEG_INF = -1e30

SSM_GROUP_DIM = 16
SSM_STATE = 64
N_HEADS = 8
HEAD_DIM = 64
REL_BUCKETS = 32
REL_MAX_EXACT = REL_BUCKETS // 2
REL_MAX_DIST = 128
CONV_WIDTH = 3

LANES = 128
SUBLANES = 8
MXU_DIM = 256
VMEM_LIMIT_BYTES = 56 * 1024 * 1024

SSM_T = MXU_DIM // SSM_GROUP_DIM
SEG_CHUNK = MXU_DIM
ATTN_BLOCK = 256
ATTN_HEADS_PER_STEP = 4
CORNER = LANES
V_ROWS = 2 * HEAD_DIM + 16
LOG2E = math.log2(math.e)
GLU_SPLIT = 4
DOWN_SPLITS = 1

bf16 = jnp.bfloat16
f32 = jnp.float32


def _cparams(*sem):
    return pltpu.CompilerParams(dimension_semantics=sem, vmem_limit_bytes=VMEM_LIMIT_BYTES)


def _rms(xf, g):
    return xf * lax.rsqrt(jnp.mean(xf * xf, axis=-1, keepdims=True) + EPS) * g


def _seg_norm(y, seg_ref, gain_ref):
    cols = []
    for c in range(y.shape[1] // SEG_CHUNK):
        yc = y[:, c * SEG_CHUNK:(c + 1) * SEG_CHUNK]
        ss = jnp.dot((yc * yc).astype(bf16), seg_ref[...], preferred_element_type=f32)
        cols.append(yc * lax.rsqrt(ss * (1.0 / HEAD_DIM) + EPS) * gain_ref[...])
    return jnp.concatenate(cols, axis=1)


def _segment_ones():
    idx = jnp.arange(SEG_CHUNK) // HEAD_DIM
    return (idx[:, None] == idx[None, :]).astype(bf16)


def _segment_gain(g):
    return jnp.tile(g.astype(f32), SEG_CHUNK // HEAD_DIM)[None, :]


def _store_vt(vt_ref, vt):
    hd2 = 2 * HEAD_DIM
    for hh in range(vt_ref.shape[1]):
        vt_ref[0, hh, :hd2, :] = vt[hh * hd2:(hh + 1) * hd2].astype(vt_ref.dtype)
        vt_ref[0, hh, hd2:, :] = jnp.ones((V_ROWS - hd2, vt.shape[1]), vt_ref.dtype)


def _win_kernel(x_ref, g_ref, wt_ref, ut_ref):
    n_chunks, nt, d = x_ref.shape
    g_n = ut_ref.shape[0]
    xt = pltpu.einshape("ktd->tkd", x_ref[...]).reshape(nt * n_chunks, d)
    h = _rms(xt, g_ref[...]).astype(bf16)
    ut = lax.dot_general(wt_ref[...], h, (((1,), (1,)), ((), ())), preferred_element_type=f32)
    for tt in range(nt):
        blk = ut[:, tt * n_chunks:(tt + 1) * n_chunks].reshape(g_n, SSM_GROUP_DIM, n_chunks)
        ut_ref[:, tt * SSM_GROUP_DIM:(tt + 1) * SSM_GROUP_DIM, :] = blk.astype(ut_ref.dtype)


def _by_chunk(x3):
    bsz, seqlen, d = x3.shape
    return x3.reshape(bsz, seqlen // SSM_T, SSM_T // SUBLANES, SUBLANES, d)


def _chunk_spec(x5):
    _, n_chunks, _, nt, d = x5.shape
    return pl.BlockSpec((None, n_chunks, None, nt, d), lambda b, t: (b, 0, t, 0, 0))


def _win_t(x3, g, w_t):
    bsz, seqlen, d = x3.shape
    n_chunks = seqlen // SSM_T
    assert n_chunks == LANES
    g_n = d // SSM_GROUP_DIM
    x5 = _by_chunk(x3)
    return pl.pallas_call(
        _win_kernel,
        grid=(bsz, SSM_T // SUBLANES),
        in_specs=[_chunk_spec(x5),
                  pl.BlockSpec((1, d), lambda b, t: (0, 0)),
                  pl.BlockSpec(w_t.shape, lambda b, t: (0, 0))],
        out_specs=pl.BlockSpec((g_n, SUBLANES * SSM_GROUP_DIM, n_chunks), lambda b, t: (0, t, b)),
        out_shape=jax.ShapeDtypeStruct((g_n, SSM_T * SSM_GROUP_DIM, bsz * n_chunks), bf16),
        compiler_params=_cparams("parallel", "parallel"),
        name="win_t",
    )(x5, g.astype(f32)[None, :], w_t)


def _ssm_kernel(n_rounds, u_ref, mt_ref, w_ref, v_ref, pw_ref, y_ref, xin_ref):
    u0 = u_ref[0]
    u1 = u_ref[1]
    half = LANES
    seg = LANES
    st = (jnp.dot(w_ref[0], u0, preferred_element_type=f32)
          + jnp.dot(w_ref[1], u1, preferred_element_type=f32))
    n_seq = st.shape[1] // seg
    xr = jnp.stack([st[:half, b * seg:(b + 1) * seg] for b in range(n_seq)])
    xi = jnp.stack([st[half:, b * seg:(b + 1) * seg] for b in range(n_seq)])
    chunk = lax.broadcasted_iota(jnp.int32, xr.shape, 2)
    for j in range(n_rounds):
        d = 1 << j
        ar = pw_ref[j, 0]
        ai = pw_ref[j, 1]
        sr = pltpu.roll(xr, d, 2)
        si = pltpu.roll(xi, d, 2)
        xr, xi = xr + (ar * sr - ai * si), xi + (ar * si + ai * sr)
    xr = jnp.where(chunk >= 1, pltpu.roll(xr, 1, 2), 0.0).astype(bf16)
    xi = jnp.where(chunk >= 1, pltpu.roll(xi, 1, 2), 0.0).astype(bf16)
    for b in range(n_seq):
        xin_ref[:half, b * seg:(b + 1) * seg] = xr[b]
        xin_ref[half:, b * seg:(b + 1) * seg] = xi[b]
    ys = jnp.dot(v_ref[...], xin_ref[...], preferred_element_type=f32)
    y_ref[0] = (jnp.dot(mt_ref[0], u0, preferred_element_type=f32) + ys[:MXU_DIM]).astype(y_ref.dtype)
    y_ref[1] = (jnp.dot(mt_ref[1], u1, preferred_element_type=f32) + ys[MXU_DIM:]).astype(y_ref.dtype)


def _ssm_tables(log_dt, lam_re, lam_im, b_re, b_im, c_re, c_im, n_chunks):
    g_n, p_n = lam_re.shape
    t_n, c_n = SSM_T, SSM_GROUP_DIM
    dt = jnp.exp(log_dt.astype(f32))[:, None]
    lr = lam_re.astype(f32)
    li = lam_im.astype(f32)

    def power(n):
        nn = jnp.asarray(n, f32).reshape((-1, 1, 1))
        mag = jnp.exp(lr * dt * nn)
        return mag * jnp.cos(li * dt * nn), mag * jnp.sin(li * dt * nn)

    ab_r, ab_i = power([1])
    ab_r, ab_i = ab_r[0], ab_i[0]
    den = lr * lr + li * li
    f_r = ((ab_r - 1.0) * lr + ab_i * li) / den
    f_i = (ab_i * lr - (ab_r - 1.0) * li) / den
    br = b_re.astype(f32)
    bi = b_im.astype(f32)
    bb_r = f_r[..., None] * br - f_i[..., None] * bi
    bb_i = f_r[..., None] * bi + f_i[..., None] * br
    cr = c_re.astype(f32)
    ci = c_im.astype(f32)

    qr, qi = power(t_n - 1 - jnp.arange(t_n))
    t_r = qr[..., None] * bb_r[None] - qi[..., None] * bb_i[None]
    t_i = qr[..., None] * bb_i[None] + qi[..., None] * bb_r[None]
    kern = (jnp.einsum("gcp,dgpk->gcdk", cr, t_r, precision=lax.Precision.HIGHEST)
            - jnp.einsum("gcp,dgpk->gcdk", ci, t_i, precision=lax.Precision.HIGHEST))
    kern = kern.reshape(g_n, c_n, t_n * c_n)
    kern = jnp.concatenate([kern, jnp.zeros_like(kern)], axis=-1)
    mt = jnp.concatenate([kern[:, :, (t_n - 1 - t) * c_n:(2 * t_n - 1 - t) * c_n] for t in range(t_n)], axis=1)
    w_r = t_r.transpose(1, 2, 0, 3).reshape(g_n, p_n, t_n * c_n)
    w_i = t_i.transpose(1, 2, 0, 3).reshape(g_n, p_n, t_n * c_n)
    pr1, pi1 = power(1 + jnp.arange(t_n))
    pr1 = pr1.transpose(1, 0, 2)
    pi1 = pi1.transpose(1, 0, 2)
    v_r = (cr[:, None] * pr1[:, :, None, :] - ci[:, None] * pi1[:, :, None, :]).reshape(g_n, t_n * c_n, p_n)
    v_i = -(cr[:, None] * pi1[:, :, None, :] + ci[:, None] * pr1[:, :, None, :]).reshape(g_n, t_n * c_n, p_n)

    gp = g_n // 2
    z = jnp.zeros((gp, p_n, t_n * c_n), f32)
    w_r = w_r.reshape(gp, 2, p_n, t_n * c_n)
    w_i = w_i.reshape(gp, 2, p_n, t_n * c_n)
    w0 = jnp.concatenate([w_r[:, 0], z, w_i[:, 0], z], axis=1)
    w1 = jnp.concatenate([z, w_r[:, 1], z, w_i[:, 1]], axis=1)
    w_pack = jnp.stack([w0, w1], axis=1)
    zv = jnp.zeros((gp, t_n * c_n, p_n), f32)
    v_r = v_r.reshape(gp, 2, t_n * c_n, p_n)
    v_i = v_i.reshape(gp, 2, t_n * c_n, p_n)
    v_pack = jnp.concatenate([
        jnp.concatenate([v_r[:, 0], zv, v_i[:, 0], zv], axis=-1),
        jnp.concatenate([zv, v_r[:, 1], zv, v_i[:, 1]], axis=-1)], axis=1)
    n_rounds = (n_chunks - 1).bit_length()
    qr, qi = power(t_n * (2 ** jnp.arange(max(n_rounds, 1))))
    pw = jnp.stack([qr.reshape(-1, gp, 2 * p_n), qi.reshape(-1, gp, 2 * p_n)], axis=2)
    reach = (jnp.arange(LANES)[None, :] >= (2 ** jnp.arange(pw.shape[0]))[:, None]).astype(f32)
    pw = pw.transpose(1, 0, 2, 3)[..., None] * reach[None, :, None, None, :]
    return mt.reshape(gp, 2, t_n * c_n, t_n * c_n).astype(bf16), w_pack.astype(bf16), v_pack.astype(bf16), pw


def _ssm(ut, tables, layer):
    mt, w_pack, v_pack, pw = tables
    n_rounds = pw.shape[2]
    g_n, k, cols = ut.shape
    gp = g_n // 2
    return pl.pallas_call(
        functools.partial(_ssm_kernel, n_rounds),
        grid=(gp,),
        in_specs=[pl.BlockSpec((2, k, cols), lambda g: (g, 0, 0)),
                  pl.BlockSpec((None, None, 2, k, k), lambda g: (layer, g, 0, 0, 0)),
                  pl.BlockSpec((None, None, 2, k, k), lambda g: (layer, g, 0, 0, 0)),
                  pl.BlockSpec((None, None, 2 * k, k), lambda g: (layer, g, 0, 0)),
                  pl.BlockSpec((None, None) + pw.shape[2:], lambda g: (layer, g, 0, 0, 0, 0))],
        out_specs=pl.BlockSpec((2, k, cols), lambda g: (g, 0, 0)),
        out_shape=jax.ShapeDtypeStruct((g_n, k, cols), bf16),
        scratch_shapes=[pltpu.VMEM((k, cols), bf16)],
        compiler_params=_cparams("parallel"),
        name="ssm",
    )(ut, mt, w_pack, v_pack, pw)


def _glu_kernel(x_ref, y_ref, u_ref, d_ref, w_ref, o_ref):
    n_chunks, nt, dm = x_ref.shape
    g_n = y_ref.shape[0]
    per = nt // GLU_SPLIT
    upd = []
    for s in range(GLU_SPLIT):
        cols = []
        for tt in range(s * per, (s + 1) * per):
            sl = slice(tt * SSM_GROUP_DIM, (tt + 1) * SSM_GROUP_DIM)
            y = y_ref[:, sl, :].astype(f32) + d_ref[...] * u_ref[:, sl, :].astype(f32)
            cols.append(_gelu_tanh(y).reshape(g_n * SSM_GROUP_DIM, n_chunks))
        gt = jnp.concatenate(cols, axis=1)
        r = jnp.dot(gt.T.astype(bf16), w_ref[...], preferred_element_type=f32)
        upd.append(r[:, :dm] * jax.nn.sigmoid(r[:, dm:]))
    upd = jnp.concatenate(upd, axis=0).reshape(nt, n_chunks, dm)
    o_ref[...] = x_ref[...] + pltpu.einshape("tkd->ktd", upd)


def _glu_res(x3, yt, ut, d_skip, w_glu):
    bsz, seqlen, d = x3.shape
    g_n, _, cols = yt.shape
    n_chunks = cols // bsz
    d_tab = jnp.broadcast_to(d_skip.astype(f32).reshape(g_n, SSM_GROUP_DIM, 1), (g_n, SSM_GROUP_DIM, n_chunks))
    feat = pl.BlockSpec((g_n, SUBLANES * SSM_GROUP_DIM, n_chunks), lambda b, t: (0, t, b))
    x5 = _by_chunk(x3)
    out = pl.pallas_call(
        _glu_kernel,
        grid=(bsz, SSM_T // SUBLANES),
        in_specs=[_chunk_spec(x5), feat, feat,
                  pl.BlockSpec(d_tab.shape, lambda b, t: (0, 0, 0)),
                  pl.BlockSpec(w_glu.shape, lambda b, t: (0, 0))],
        out_specs=_chunk_spec(x5),
        out_shape=jax.ShapeDtypeStruct(x5.shape, f32),
        compiler_params=_cparams("parallel", "parallel"),
        name="glu_res",
    )(x5, yt, ut, d_tab, w_glu)
    return out.reshape(bsz, seqlen, d)


def _attn_kernel(tq, tk, out_scale, lam_ref, far_ref, q_ref, k_ref, vt_ref, diag_ref, corner_ref, sub_ref,
                 o_ref, qm_ref, m_ref, acc_ref, s_ref):
    qi = pl.program_id(2)
    hd2 = 2 * HEAD_DIM
    heads = range(vt_ref.shape[1])
    lane = lax.broadcasted_iota(jnp.int32, (tq, hd2), 1)
    for hh in heads:
        q = q_ref[0, :, hh * hd2:(hh + 1) * hd2].astype(f32) * (HEAD_DIM ** -0.5 * LOG2E)
        qm_ref[hh, :tq] = jnp.where(lane < HEAD_DIM, q, 0.0).astype(bf16)
        qm_ref[hh, tq:] = jnp.where(lane >= HEAD_DIM, q, 0.0).astype(bf16)
    m_ref[...] = jnp.full_like(m_ref, NEG_INF)
    acc_ref[...] = jnp.zeros_like(acc_ref)

    def fill(slot, ki):
        for hh in heads:
            k = k_ref[0, pl.ds(pl.multiple_of(ki * tk, tk), tk), hh * hd2:(hh + 1) * hd2]
            s_ref[slot, hh] = lax.dot_general(k, qm_ref[hh], (((1,), (1,)), ((), ())),
                                              preferred_element_type=f32)

    def accumulate(hh, ki, s, shift):
        vt = vt_ref[0, hh, :, pl.ds(pl.multiple_of(ki * tk, tk), tk)]
        m_prev = m_ref[hh]
        m_new = jnp.maximum(m_prev, jnp.max(s, axis=0, keepdims=True) + shift)
        p = jnp.exp2(s - (m_new - shift))
        acc_ref[hh] = (jnp.exp2(m_prev - m_new) * acc_ref[hh]
                       + jnp.dot(vt, p.astype(bf16), preferred_element_type=f32))
        m_ref[hh] = m_new

    def earlier(slot, ki):
        near = jnp.where(ki == qi - 1, 1.0, 0.0)
        for hh in heads:
            s = s_ref[slot, hh]
            c = near * corner_ref[hh]
            lo = s[tk - CORNER:]
            lo = jnp.concatenate([lo[:, :CORNER] + c, lo[:, CORNER:tq], lo[:, tq:tq + CORNER] + c,
                                  lo[:, tq + CORNER:]], axis=1)
            accumulate(hh, ki, jnp.concatenate([s[:tk - CORNER], lo], axis=0),
                       far_ref[pl.program_id(1) * len(heads) + hh])

    def pair(i, carry):
        fill(1, 2 * i + 1)
        earlier(0, 2 * i)
        fill(0, 2 * i + 2)
        earlier(1, 2 * i + 1)
        return carry

    fill(0, 0)
    lax.fori_loop(0, qi // 2, pair, 0)

    def diagonal(slot):
        for hh in heads:
            b = diag_ref[hh]
            accumulate(hh, qi, s_ref[slot, hh] + jnp.concatenate([b, b], axis=1), 0.0)

    @pl.when(qi % 2 == 1)
    def _():
        fill(1, qi)
        earlier(0, qi - 1)
        diagonal(1)

    @pl.when(qi % 2 == 0)
    def _():
        diagonal(0)

    for hh in heads:
        acc = acc_ref[hh]
        ot = acc[:hd2] / acc[hd2:hd2 + 1]
        ot = ot[:, :tq] - lam_ref[0] * ot[:, tq:]
        ot = ot * lax.rsqrt(jnp.mean(ot * ot, axis=0, keepdims=True) + EPS)
        o_ref[0, :, hh * hd2:(hh + 1) * hd2] = (ot.T * sub_ref[...] * out_scale).astype(o_ref.dtype)


def _bias_tables(rel_bias, tq, tk):
    n = jnp.arange(2 * tq)
    nf = jnp.maximum(n, REL_MAX_EXACT).astype(f32)
    large = REL_MAX_EXACT + (jnp.log(nf / REL_MAX_EXACT) / math.log(REL_MAX_DIST / REL_MAX_EXACT)
                             * (REL_BUCKETS - REL_MAX_EXACT)).astype(jnp.int32)
    large = jnp.minimum(large, REL_BUCKETS - 1)
    bucket = jnp.where(n < REL_MAX_EXACT, n, large)
    onehot = (bucket[:, None] == jnp.arange(REL_BUCKETS)[None, :]).astype(f32)
    by_dist = jnp.dot(onehot, rel_bias.astype(f32), precision=lax.Precision.HIGHEST).T * LOG2E
    n_heads = by_dist.shape[0]
    neg = jnp.full((n_heads, tk), NEG_INF, f32)
    period = tq + tk
    tiles = []
    for d, tail in ((0, neg), (1, by_dist[:, tq - tk:tq])):
        r = jnp.concatenate([by_dist[:, d * tq:(d + 1) * tq], tail], axis=1)
        t = jnp.tile(r, (1, tk))[:, :tk * (period - 1)].reshape(n_heads, tk, period - 1)
        tiles.append(t[:, :, :tq])
    far = rel_bias.astype(f32)[REL_BUCKETS - 1] * LOG2E
    corner = tiles[1][:, tk - CORNER:, :CORNER] - far[:, None, None]
    return tiles[0], corner, far


def _far_bucket_from():
    n = REL_MAX_EXACT
    while REL_MAX_EXACT + int(math.log(n / REL_MAX_EXACT) / math.log(REL_MAX_DIST / REL_MAX_EXACT)
                              * (REL_BUCKETS - REL_MAX_EXACT)) < REL_BUCKETS - 1:
        n += 1
    return n


def _diff_attention(q, k, vt, diag, corner, far, lam, subln, out_scale, bsz, seqlen, tq, tk):
    assert tq == tk and tq - CORNER + 1 >= _far_bucket_from()
    hd2 = 2 * HEAD_DIM
    q3 = q.reshape(bsz, seqlen, N_HEADS * hd2)
    k3 = k.reshape(bsz, seqlen, N_HEADS * hd2)
    smem = pl.BlockSpec(memory_space=pltpu.SMEM)
    hps = ATTN_HEADS_PER_STEP
    o = pl.pallas_call(
        functools.partial(_attn_kernel, tq, tk, out_scale),
        grid=(bsz, N_HEADS // hps, seqlen // tq),
        in_specs=[smem, smem,
                  pl.BlockSpec((1, tq, hps * hd2), lambda b, h, i: (b, i, h)),
                  pl.BlockSpec((1, seqlen, hps * hd2), lambda b, h, i: (b, 0, h)),
                  pl.BlockSpec((1, hps, V_ROWS, seqlen), lambda b, h, i: (b, h, 0, 0)),
                  pl.BlockSpec((hps, tk, tq), lambda b, h, i: (h, 0, 0)),
                  pl.BlockSpec((hps, CORNER, CORNER), lambda b, h, i: (h, 0, 0)),
                  pl.BlockSpec((1, hd2), lambda b, h, i: (0, 0))],
        out_specs=pl.BlockSpec((1, tq, hps * hd2), lambda b, h, i: (b, i, h)),
        out_shape=jax.ShapeDtypeStruct((bsz, seqlen, N_HEADS * hd2), bf16),
        scratch_shapes=[pltpu.VMEM((hps, 2 * tq, hd2), bf16), pltpu.VMEM((hps, 1, 2 * tq), f32),
                        pltpu.VMEM((hps, V_ROWS, 2 * tq), f32), pltpu.VMEM((2, hps, tk, 2 * tq), f32)],
        compiler_params=_cparams("parallel", "parallel", "parallel"),
        name="diff_attn",
    )(lam.reshape(1).astype(f32), far, q3, k3, vt, diag, corner, subln.astype(f32)[None, :])
    return o.reshape(bsz * seqlen, N_HEADS * hd2)


def _gelu_tanh(v):
    c = 2.0 * math.sqrt(2.0 / math.pi)
    return v * jax.nn.sigmoid(v * (c + (c * 0.044715) * (v * v)))


def _ffn_ple_kernel(tiles_per_seq, fc, has_attn, has_q, has_kv, *refs):
    refs = list(refs)
    x_ref, p_ref = refs[:2]
    del refs[:2]
    if has_attn:
        a_ref, wo_ref = refs[:2]
        del refs[:2]
    fg_ref, wup_ref, cw_ref, cb_ref, wdn_ref, pg_ref, wgate_ref, wproj_ref = refs[:8]
    del refs[:8]
    if has_q or has_kv:
        seg_ref = refs.pop(0)
    if has_q:
        qg_ref, wq_ref, qgain_ref = refs[:3]
        del refs[:3]
    if has_kv:
        kvg_ref, wk_ref, wvt_ref, kgain_ref = refs[:4]
        del refs[:4]
    o_ref = refs.pop(0)
    if has_q:
        q_ref = refs.pop(0)
    if has_kv:
        k_ref, vt_ref = refs[:2]
        del refs[:2]
    carry_ref, act_ref, h_ref = refs

    i = pl.program_id(0)
    tm = x_ref.shape[0]
    d_ff = wdn_ref.shape[0]
    halo = SUBLANES

    @pl.when(i % tiles_per_seq == 0)
    def _():
        carry_ref[...] = jnp.zeros_like(carry_ref)

    x = x_ref[...]
    if has_attn:
        x = x + jnp.dot(a_ref[...], wo_ref[...], preferred_element_type=f32)
    h_ref[...] = _rms(x, fg_ref[...]).astype(bf16)
    row = lax.broadcasted_iota(jnp.int32, (halo, fc), 0)

    def up_proj(col):
        return jnp.dot(h_ref[...], wup_ref[:, col:col + fc], preferred_element_type=f32)

    def conv(col, up):
        prev = carry_ref[:, col:col + fc]
        carry_ref[:, col:col + fc] = up[tm - halo:]
        c = cb_ref[:, col:col + fc] + cw_ref[2:3, col:col + fc] * up
        for back in range(1, CONV_WIDTH):
            r = pltpu.roll(up, back, 0)
            head = jnp.where(row < back, pltpu.roll(prev, back, 0), r[:halo])
            c = c + cw_ref[2 - back:3 - back, col:col + fc] * jnp.concatenate([head, r[halo:]], axis=0)
        return c

    cols = list(range(0, d_ff, fc))
    bounds = [round(s * len(cols) / DOWN_SPLITS) * fc for s in range(DOWN_SPLITS + 1)]
    pending = None
    ups = (up_proj(cols[0]), up_proj(d_ff + cols[0]))
    for idx, col in enumerate(cols):
        nxt = (up_proj(cols[idx + 1]), up_proj(d_ff + cols[idx + 1])) if idx + 1 < len(cols) else None
        if pending is not None:
            x = x + jnp.dot(act_ref[:, pending[0]:pending[1]], wdn_ref[pending[0]:pending[1], :],
                            preferred_element_type=f32)
            pending = None
        act_ref[:, col:col + fc] = (_gelu_tanh(conv(col, ups[0])) * conv(d_ff + col, ups[1])).astype(bf16)
        if col + fc in bounds:
            pending = (bounds[bounds.index(col + fc) - 1], col + fc)
        ups = nxt
    lo, hi = pending
    halves = [slice(0, tm // 2), slice(tm // 2, tm)]
    xs = [x[r] + jnp.dot(act_ref[r, lo:hi], wdn_ref[lo:hi, :], preferred_element_type=f32) for r in halves]
    projs = [jnp.dot(p_ref[r, :].astype(bf16), wproj_ref[...], preferred_element_type=f32) for r in halves]
    gates = [jax.nn.sigmoid(jnp.dot(_rms(xh, pg_ref[...]).astype(bf16), wgate_ref[...],
                                    preferred_element_type=f32)) for xh in xs]
    xs = [xh + g * pr for xh, g, pr in zip(xs, gates, projs)]
    for r, xh in zip(halves, xs):
        o_ref[r, :] = xh
    if has_q or has_kv:
        xn = jnp.concatenate([xh * lax.rsqrt(jnp.mean(xh * xh, axis=-1, keepdims=True) + EPS) for xh in xs],
                             axis=0)
    if has_q:
        q = jnp.dot((xn * qg_ref[...]).astype(bf16), wq_ref[...], preferred_element_type=f32)
        q_ref[...] = _seg_norm(q, seg_ref, qgain_ref).astype(q_ref.dtype)
    if has_kv:
        hk = (xn * kvg_ref[...]).astype(bf16)
        k = jnp.dot(hk, wk_ref[...], preferred_element_type=f32)
        k_ref[...] = _seg_norm(k, seg_ref, kgain_ref).astype(k_ref.dtype)
        _store_vt(vt_ref, lax.dot_general(wvt_ref[...], hk, (((1,), (1,)), ((), ())),
                                          preferred_element_type=f32))


def _ffn_ple(x, p, layer, ffn, ple, seqlen, attn=None, next_q=None, next_kv=None, tm=512, fc=256):
    n, d = x.shape
    ffn_g, w_up, conv_w, conv_b, w_down = ffn
    ple_g, w_gate, w_proj = ple
    d_ff = w_down.shape[-2]
    bsz = n // seqlen
    tps = seqlen // tm
    row = lambda i: (i, 0)

    def full(a):
        if a.ndim == 3:
            return pl.BlockSpec((None,) + a.shape[1:], lambda i: (layer, 0, 0), pipeline_mode=pl.Buffered(1))
        return pl.BlockSpec(a.shape, lambda i: (0, 0), pipeline_mode=pl.Buffered(1))

    vec = lambda a: a.astype(f32)[None, :]
    args = [x, p]
    in_specs = [pl.BlockSpec((tm, d), row), pl.BlockSpec((None, tm, p.shape[2]), lambda i: (layer, i, 0))]
    if attn is not None:
        args += [attn[0], attn[1]]
        in_specs += [pl.BlockSpec((tm, attn[0].shape[1]), row), full(attn[1])]
    weights = [vec(ffn_g), w_up, conv_w.astype(f32), vec(conv_b), w_down, vec(ple_g), w_gate, w_proj]
    out_specs = [pl.BlockSpec((tm, d), row)]
    out_shape = [jax.ShapeDtypeStruct((n, d), f32)]
    if next_q is not None or next_kv is not None:
        weights.append(_segment_ones())
    if next_q is not None:
        weights += [vec(next_q[0]), next_q[1], _segment_gain(next_q[2])]
        out_specs.append(pl.BlockSpec((tm, next_q[1].shape[1]), row))
        out_shape.append(jax.ShapeDtypeStruct((n, next_q[1].shape[1]), bf16))
    if next_kv is not None:
        weights += [vec(next_kv[0]), next_kv[1], next_kv[2], _segment_gain(next_kv[3])]
        out_specs += [pl.BlockSpec((tm, next_kv[1].shape[1]), row),
                      pl.BlockSpec((1, N_HEADS, V_ROWS, tm), lambda i: (i // tps, 0, 0, i % tps))]
        out_shape += [jax.ShapeDtypeStruct((n, next_kv[1].shape[1]), bf16),
                      jax.ShapeDtypeStruct((bsz, N_HEADS, V_ROWS, seqlen), bf16)]
    args += weights
    in_specs += [full(w) for w in weights]
    return pl.pallas_call(
        functools.partial(_ffn_ple_kernel, tps, fc, attn is not None, next_q is not None, next_kv is not None),
        grid=(n // tm,),
        in_specs=in_specs,
        out_specs=out_specs,
        out_shape=out_shape,
        scratch_shapes=[pltpu.VMEM((SUBLANES, 2 * d_ff), f32),
                        pltpu.VMEM((tm, d_ff), bf16),
                        pltpu.VMEM((tm, d), bf16)],
        compiler_params=_cparams("arbitrary"),
        name="ffn_ple",
    )(*args)


def kernel(x, p, ssm_norm, ssm_w_in, ssm_log_dt, ssm_lambda_re, ssm_lambda_im, ssm_b_re, ssm_b_im, ssm_c_re, ssm_c_im, ssm_d, ssm_w_glu, kv_norm, kv_w, k_norm, attn_norm, attn_w_q, q_norm, lambda_q1, lambda_k1, lambda_q2, lambda_k2, subln, attn_w_o, rel_bias, ffn_norm, ffn_w_up, ffn_conv_w, ffn_conv_b, ffn_w_down, ple_norm, ple_w_gate, ple_w_proj):
    bsz, seqlen, d = x.shape
    depth = p.shape[0]
    n_a = ssm_norm.shape[0]
    n = bsz * seqlen
    x = x.reshape(n, d).astype(f32)
    p = p.reshape(depth, n, p.shape[-1])
    nk = N_HEADS * 2 * HEAD_DIM
    assert 1 <= n_a < depth
    diag, corner, far = _bias_tables(rel_bias, ATTN_BLOCK, ATTN_BLOCK)
    tables = jax.vmap(functools.partial(_ssm_tables, n_chunks=seqlen // SSM_T))(
        ssm_log_dt, ssm_lambda_re, ssm_lambda_im, ssm_b_re, ssm_b_im, ssm_c_re, ssm_c_im)
    w_in_t = jnp.swapaxes(ssm_w_in, 1, 2).astype(bf16)
    w_glu, w_q, w_o = ssm_w_glu.astype(bf16), attn_w_q.astype(bf16), attn_w_o.astype(bf16)
    w_up, w_down = ffn_w_up.astype(bf16), ffn_w_down.astype(bf16)
    w_gate, w_proj = ple_w_gate.astype(bf16), ple_w_proj.astype(bf16)
    lam_all = (jnp.exp(jnp.sum(lambda_q1.astype(f32) * lambda_k1.astype(f32), axis=-1))
               - jnp.exp(jnp.sum(lambda_q2.astype(f32) * lambda_k2.astype(f32), axis=-1)))
    q = k = vt = None
    for i in range(depth):
        attn = None
        if i < n_a:
            x3 = x.reshape(bsz, seqlen, d)
            ut = _win_t(x3, ssm_norm[i], w_in_t[i])
            yt = _ssm(ut, tables, i)
            x = _glu_res(x3, yt, ut, ssm_d[i], w_glu[i]).reshape(n, d)
        else:
            j = i - n_a
            lam_init = 0.8 - 0.6 * math.exp(-0.3 * i)
            o = _diff_attention(q, k, vt, diag, corner, far, lam_all[j] + lam_init, subln[j], 1.0 - lam_init,
                                bsz, seqlen, ATTN_BLOCK, ATTN_BLOCK)
            attn = (o, w_o[j])
        j_next = i + 1 - n_a
        next_q = next_kv = None
        if 0 <= j_next < depth - n_a:
            next_q = (attn_norm[j_next], w_q[j_next], q_norm[j_next])
            if j_next == 0:
                next_kv = (kv_norm, kv_w[:, :nk].astype(bf16), kv_w[:, nk:].T.astype(bf16), k_norm)
        outs = _ffn_ple(x, p, i,
                        (ffn_norm[i], w_up, ffn_conv_w[i], ffn_conv_b[i], w_down),
                        (ple_norm[i], w_gate, w_proj),
                        seqlen, attn=attn, next_q=next_q, next_kv=next_kv)
        x = outs[0]
        if next_q is not None:
            q = outs[1]
        if next_kv is not None:
            k, vt = outs[2], outs[3]
    return x.reshape(bsz, seqlen, d)
```

```python
import functools
import math

import jax
import jax.numpy as jnp
from jax import lax
from jax.experimental import pallas as pl
from jax.experimental.pallas import tpu as pltpu

EPS = 1e-6
NEG_INF = -1e30

SSM_GROUP_DIM = 16
SSM_STATE = 64
N_HEADS = 8
HEAD_DIM = 64
REL_BUCKETS = 32
REL_MAX_EXACT = REL_BUCKETS // 2
REL_MAX_DIST = 128
CONV_WIDTH = 3

LANES = 128
SUBLANES = 8
MXU_DIM = 256
VMEM_LIMIT_BYTES = 56 * 1024 * 1024

SSM_T = MXU_DIM // SSM_GROUP_DIM
SEG_CHUNK = MXU_DIM
ATTN_BLOCK = 256
ATTN_HEADS_PER_STEP = 4
CORNER = LANES
V_ROWS = 2 * HEAD_DIM + 16
LOG2E = math.log2(math.e)
GLU_SPLIT = 4
DOWN_SPLITS = 1

bf16 = jnp.bfloat16
f32 = jnp.float32


def _cparams(*sem):
    return pltpu.CompilerParams(dimension_semantics=sem, vmem_limit_bytes=VMEM_LIMIT_BYTES)


def _rms(xf, g):
    return xf * lax.rsqrt(jnp.mean(xf * xf, axis=-1, keepdims=True) + EPS) * g


def _seg_norm(y, seg_ref, gain_ref):
    cols = []
    for c in range(y.shape[1] // SEG_CHUNK):
        yc = y[:, c * SEG_CHUNK:(c + 1) * SEG_CHUNK]
        ss = jnp.dot((yc * yc).astype(bf16), seg_ref[...], preferred_element_type=f32)
        cols.append(yc * lax.rsqrt(ss * (1.0 / HEAD_DIM) + EPS) * gain_ref[...])
    return jnp.concatenate(cols, axis=1)


def _segment_ones():
    idx = jnp.arange(SEG_CHUNK) // HEAD_DIM
    return (idx[:, None] == idx[None, :]).astype(bf16)


def _segment_gain(g):
    return jnp.tile(g.astype(f32), SEG_CHUNK // HEAD_DIM)[None, :]


def _store_vt(vt_ref, vt):
    hd2 = 2 * HEAD_DIM
    for hh in range(vt_ref.shape[1]):
        vt_ref[0, hh, :hd2, :] = vt[hh * hd2:(hh + 1) * hd2].astype(vt_ref.dtype)
        vt_ref[0, hh, hd2:, :] = jnp.ones((V_ROWS - hd2, vt.shape[1]), vt_ref.dtype)


def _win_kernel(x_ref, g_ref, wt_ref, ut_ref):
    n_chunks, nt, d = x_ref.shape
    g_n = ut_ref.shape[0]
    xt = pltpu.einshape("ktd->tkd", x_ref[...]).reshape(nt * n_chunks, d)
    h = _rms(xt, g_ref[...]).astype(bf16)
    ut = lax.dot_general(wt_ref[...], h, (((1,), (1,)), ((), ())), preferred_element_type=f32)
    for tt in range(nt):
        blk = ut[:, tt * n_chunks:(tt + 1) * n_chunks].reshape(g_n, SSM_GROUP_DIM, n_chunks)
        ut_ref[:, tt * SSM_GROUP_DIM:(tt + 1) * SSM_GROUP_DIM, :] = blk.astype(ut_ref.dtype)


def _by_chunk(x3):
    bsz, seqlen, d = x3.shape
    return x3.reshape(bsz, seqlen // SSM_T, SSM_T // SUBLANES, SUBLANES, d)


def _chunk_spec(x5):
    _, n_chunks, _, nt, d = x5.shape
    return pl.BlockSpec((None, n_chunks, None, nt, d), lambda b, t: (b, 0, t, 0, 0))


def _win_t(x3, g, w_t):
    bsz, seqlen, d = x3.shape
    n_chunks = seqlen // SSM_T
    assert n_chunks == LANES
    g_n = d // SSM_GROUP_DIM
    x5 = _by_chunk(x3)
    return pl.pallas_call(
        _win_kernel,
        grid=(bsz, SSM_T // SUBLANES),
        in_specs=[_chunk_spec(x5),
                  pl.BlockSpec((1, d), lambda b, t: (0, 0)),
                  pl.BlockSpec(w_t.shape, lambda b, t: (0, 0))],
        out_specs=pl.BlockSpec((g_n, SUBLANES * SSM_GROUP_DIM, n_chunks), lambda b, t: (0, t, b)),
        out_shape=jax.ShapeDtypeStruct((g_n, SSM_T * SSM_GROUP_DIM, bsz * n_chunks), bf16),
        compiler_params=_cparams("parallel", "parallel"),
        name="win_t",
    )(x5, g.astype(f32)[None, :], w_t)


def _ssm_kernel(n_rounds, u_ref, mt_ref, w_ref, v_ref, pw_ref, y_ref, xin_ref):
    u0 = u_ref[0]
    u1 = u_ref[1]
    half = LANES
    seg = LANES
    st = (jnp.dot(w_ref[0], u0, preferred_element_type=f32)
          + jnp.dot(w_ref[1], u1, preferred_element_type=f32))
    n_seq = st.shape[1] // seg
    xr = jnp.stack([st[:half, b * seg:(b + 1) * seg] for b in range(n_seq)])
    xi = jnp.stack([st[half:, b * seg:(b + 1) * seg] for b in range(n_seq)])
    chunk = lax.broadcasted_iota(jnp.int32, xr.shape, 2)
    for j in range(n_rounds):
        d = 1 << j
        ar = pw_ref[j, 0]
        ai = pw_ref[j, 1]
        sr = pltpu.roll(xr, d, 2)
        si = pltpu.roll(xi, d, 2)
        xr, xi = xr + (ar * sr - ai * si), xi + (ar * si + ai * sr)
    xr = jnp.where(chunk >= 1, pltpu.roll(xr, 1, 2), 0.0).astype(bf16)
    xi = jnp.where(chunk >= 1, pltpu.roll(xi, 1, 2), 0.0).astype(bf16)
    for b in range(n_seq):
        xin_ref[:half, b * seg:(b + 1) * seg] = xr[b]
        xin_ref[half:, b * seg:(b + 1) * seg] = xi[b]
    ys = jnp.dot(v_ref[...], xin_ref[...], preferred_element_type=f32)
    y_ref[0] = (jnp.dot(mt_ref[0], u0, preferred_element_type=f32) + ys[:MXU_DIM]).astype(y_ref.dtype)
    y_ref[1] = (jnp.dot(mt_ref[1], u1, preferred_element_type=f32) + ys[MXU_DIM:]).astype(y_ref.dtype)


def _ssm_tables(log_dt, lam_re, lam_im, b_re, b_im, c_re, c_im, n_chunks):
    g_n, p_n = lam_re.shape
    t_n, c_n = SSM_T, SSM_GROUP_DIM
    dt = jnp.exp(log_dt.astype(f32))[:, None]
    lr = lam_re.astype(f32)
    li = lam_im.astype(f32)

    def power(n):
        nn = jnp.asarray(n, f32).reshape((-1, 1, 1))
        mag = jnp.exp(lr * dt * nn)
        return mag * jnp.cos(li * dt * nn), mag * jnp.sin(li * dt * nn)

    ab_r, ab_i = power([1])
    ab_r, ab_i = ab_r[0], ab_i[0]
    den = lr * lr + li * li
    f_r = ((ab_r - 1.0) * lr + ab_i * li) / den
    f_i = (ab_i * lr - (ab_r - 1.0) * li) / den
    br = b_re.astype(f32)
    bi = b_im.astype(f32)
    bb_r = f_r[..., None] * br - f_i[..., None] * bi
    bb_i = f_r[..., None] * bi + f_i[..., None] * br
    cr = c_re.astype(f32)
    ci = c_im.astype(f32)

    qr, qi = power(t_n - 1 - jnp.arange(t_n))
    t_r = qr[..., None] * bb_r[None] - qi[..., None] * bb_i[None]
    t_i = qr[..., None] * bb_i[None] + qi[..., None] * bb_r[None]
    kern = (jnp.einsum("gcp,dgpk->gcdk", cr, t_r, precision=lax.Precision.HIGHEST)
            - jnp.einsum("gcp,dgpk->gcdk", ci, t_i, precision=lax.Precision.HIGHEST))
    kern = kern.reshape(g_n, c_n, t_n * c_n)
    kern = jnp.concatenate([kern, jnp.zeros_like(kern)], axis=-1)
    mt = jnp.concatenate([kern[:, :, (t_n - 1 - t) * c_n:(2 * t_n - 1 - t) * c_n] for t in range(t_n)], axis=1)
    w_r = t_r.transpose(1, 2, 0, 3).reshape(g_n, p_n, t_n * c_n)
    w_i = t_i.transpose(1, 2, 0, 3).reshape(g_n, p_n, t_n * c_n)
    pr1, pi1 = power(1 + jnp.arange(t_n))
    pr1 = pr1.transpose(1, 0, 2)
    pi1 = pi1.transpose(1, 0, 2)
    v_r = (cr[:, None] * pr1[:, :, None, :] - ci[:, None] * pi1[:, :, None, :]).reshape(g_n, t_n * c_n, p_n)
    v_i = -(cr[:, None] * pi1[:, :, None, :] + ci[:, None] * pr1[:, :, None, :]).reshape(g_n, t_n * c_n, p_n)

    gp = g_n // 2
    z = jnp.zeros((gp, p_n, t_n * c_n), f32)
    w_r = w_r.reshape(gp, 2, p_n, t_n * c_n)
    w_i = w_i.reshape(gp, 2, p_n, t_n * c_n)
    w0 = jnp.concatenate([w_r[:, 0], z, w_i[:, 0], z], axis=1)
    w1 = jnp.concatenate([z, w_r[:, 1], z, w_i[:, 1]], axis=1)
    w_pack = jnp.stack([w0, w1], axis=1)
    zv = jnp.zeros((gp, t_n * c_n, p_n), f32)
    v_r = v_r.reshape(gp, 2, t_n * c_n, p_n)
    v_i = v_i.reshape(gp, 2, t_n * c_n, p_n)
    v_pack = jnp.concatenate([
        jnp.concatenate([v_r[:, 0], zv, v_i[:, 0], zv], axis=-1),
        jnp.concatenate([zv, v_r[:, 1], zv, v_i[:, 1]], axis=-1)], axis=1)
    n_rounds = (n_chunks - 1).bit_length()
    qr, qi = power(t_n * (2 ** jnp.arange(max(n_rounds, 1))))
    pw = jnp.stack([qr.reshape(-1, gp, 2 * p_n), qi.reshape(-1, gp, 2 * p_n)], axis=2)
    reach = (jnp.arange(LANES)[None, :] >= (2 ** jnp.arange(pw.shape[0]))[:, None]).astype(f32)
    pw = pw.transpose(1, 0, 2, 3)[..., None] * reach[None, :, None, None, :]
    return mt.reshape(gp, 2, t_n * c_n, t_n * c_n).astype(bf16), w_pack.astype(bf16), v_pack.astype(bf16), pw


def _ssm(ut, tables, layer):
    mt, w_pack, v_pack, pw = tables
    n_rounds = pw.shape[2]
    g_n, k, cols = ut.shape
    gp = g_n // 2
    return pl.pallas_call(
        functools.partial(_ssm_kernel, n_rounds),
        grid=(gp,),
        in_specs=[pl.BlockSpec((2, k, cols), lambda g: (g, 0, 0)),
                  pl.BlockSpec((None, None, 2, k, k), lambda g: (layer, g, 0, 0, 0)),
                  pl.BlockSpec((None, None, 2, k, k), lambda g: (layer, g, 0, 0, 0)),
                  pl.BlockSpec((None, None, 2 * k, k), lambda g: (layer, g, 0, 0)),
                  pl.BlockSpec((None, None) + pw.shape[2:], lambda g: (layer, g, 0, 0, 0, 0))],
        out_specs=pl.BlockSpec((2, k, cols), lambda g: (g, 0, 0)),
        out_shape=jax.ShapeDtypeStruct((g_n, k, cols), bf16),
        scratch_shapes=[pltpu.VMEM((k, cols), bf16)],
        compiler_params=_cparams("parallel"),
        name="ssm",
    )(ut, mt, w_pack, v_pack, pw)


def _glu_kernel(x_ref, y_ref, u_ref, d_ref, w_ref, o_ref):
    n_chunks, nt, dm = x_ref.shape
    g_n = y_ref.shape[0]
    per = nt // GLU_SPLIT
    upd = []
    for s in range(GLU_SPLIT):
        cols = []
        for tt in range(s * per, (s + 1) * per):
            sl = slice(tt * SSM_GROUP_DIM, (tt + 1) * SSM_GROUP_DIM)
            y = y_ref[:, sl, :].astype(f32) + d_ref[...] * u_ref[:, sl, :].astype(f32)
            cols.append(_gelu_tanh(y).reshape(g_n * SSM_GROUP_DIM, n_chunks))
        gt = jnp.concatenate(cols, axis=1)
        r = jnp.dot(gt.T.astype(bf16), w_ref[...], preferred_element_type=f32)
        upd.append(r[:, :dm] * jax.nn.sigmoid(r[:, dm:]))
    upd = jnp.concatenate(upd, axis=0).reshape(nt, n_chunks, dm)
    o_ref[...] = x_ref[...] + pltpu.einshape("tkd->ktd", upd)


def _glu_res(x3, yt, ut, d_skip, w_glu):
    bsz, seqlen, d = x3.shape
    g_n, _, cols = yt.shape
    n_chunks = cols // bsz
    d_tab = jnp.broadcast_to(d_skip.astype(f32).reshape(g_n, SSM_GROUP_DIM, 1), (g_n, SSM_GROUP_DIM, n_chunks))
    feat = pl.BlockSpec((g_n, SUBLANES * SSM_GROUP_DIM, n_chunks), lambda b, t: (0, t, b))
    x5 = _by_chunk(x3)
    out = pl.pallas_call(
        _glu_kernel,
        grid=(bsz, SSM_T // SUBLANES),
        in_specs=[_chunk_spec(x5), feat, feat,
                  pl.BlockSpec(d_tab.shape, lambda b, t: (0, 0, 0)),
                  pl.BlockSpec(w_glu.shape, lambda b, t: (0, 0))],
        out_specs=_chunk_spec(x5),
        out_shape=jax.ShapeDtypeStruct(x5.shape, f32),
        compiler_params=_cparams("parallel", "parallel"),
        name="glu_res",
    )(x5, yt, ut, d_tab, w_glu)
    return out.reshape(bsz, seqlen, d)


def _attn_kernel(tq, tk, out_scale, lam_ref, far_ref, q_ref, k_ref, vt_ref, diag_ref, corner_ref, sub_ref,
                 o_ref, qm_ref, m_ref, acc_ref, s_ref):
    hd2 = 2 * HEAD_DIM
    heads = range(vt_ref.shape[1])
    lane = lax.broadcasted_iota(jnp.int32, (tq, hd2), 1)
    far = [far_ref[pl.program_id(1) * len(heads) + hh] for hh in heads]

    def q_block(qi, carry):
        rows = pl.ds(pl.multiple_of(qi * tq, tq), tq)
        for hh in heads:
            q = q_ref[0, rows, hh * hd2:(hh + 1) * hd2].astype(f32) * (HEAD_DIM ** -0.5 * LOG2E)
            qm_ref[hh, :tq] = jnp.where(lane < HEAD_DIM, q, 0.0).astype(bf16)
            qm_ref[hh, tq:] = jnp.where(lane >= HEAD_DIM, q, 0.0).astype(bf16)
        m_ref[...] = jnp.full_like(m_ref, NEG_INF)
        acc_ref[...] = jnp.zeros_like(acc_ref)

        def fill(slot, ki):
            for hh in heads:
                k = k_ref[0, pl.ds(pl.multiple_of(ki * tk, tk), tk), hh * hd2:(hh + 1) * hd2]
                s_ref[slot, hh] = lax.dot_general(k, qm_ref[hh], (((1,), (1,)), ((), ())),
                                                  preferred_element_type=f32)

        def accumulate(hh, ki, s, shift):
            vt = vt_ref[0, hh, :, pl.ds(pl.multiple_of(ki * tk, tk), tk)]
            m_prev = m_ref[hh]
            m_new = jnp.maximum(m_prev, jnp.max(s, axis=0, keepdims=True) + shift)
            p = jnp.exp2(s - (m_new - shift))
            acc_ref[hh] = (jnp.exp2(m_prev - m_new) * acc_ref[hh]
                           + jnp.dot(vt, p.astype(bf16), preferred_element_type=f32))
            m_ref[hh] = m_new

        def earlier(slot, ki):
            near = jnp.where(ki == qi - 1, 1.0, 0.0)
            for hh in heads:
                s = s_ref[slot, hh]
                c = near * corner_ref[hh]
                lo = s[tk - CORNER:]
                lo = jnp.concatenate([lo[:, :CORNER] + c, lo[:, CORNER:tq], lo[:, tq:tq + CORNER] + c,
                                      lo[:, tq + CORNER:]], axis=1)
                accumulate(hh, ki, jnp.concatenate([s[:tk - CORNER], lo], axis=0), far[hh])

        def pair(i, carry):
            fill(1, 2 * i + 1)
            earlier(0, 2 * i)
            fill(0, 2 * i + 2)
            earlier(1, 2 * i + 1)
            return carry

        fill(0, 0)
        lax.fori_loop(0, qi // 2, pair, 0)

        def diagonal(slot):
            for hh in heads:
                b = diag_ref[hh]
                accumulate(hh, qi, s_ref[slot, hh] + jnp.concatenate([b, b], axis=1), 0.0)

        @pl.when(qi % 2 == 1)
        def _():
            fill(1, qi)
            earlier(0, qi - 1)
            diagonal(1)

        @pl.when(qi % 2 == 0)
        def _():
            diagonal(0)

        for hh in heads:
            acc = acc_ref[hh]
            ot = acc[:hd2] / acc[hd2:hd2 + 1]
            ot = ot[:, :tq] - lam_ref[0] * ot[:, tq:]
            ot = ot * lax.rsqrt(jnp.mean(ot * ot, axis=0, keepdims=True) + EPS)
            o_ref[0, rows, hh * hd2:(hh + 1) * hd2] = (ot.T * sub_ref[...] * out_scale).astype(o_ref.dtype)
        return carry

    lax.fori_loop(0, k_ref.shape[1] // tq, q_block, 0)


def _bias_tables(rel_bias, tq, tk):
    n = jnp.arange(2 * tq)
    nf = jnp.maximum(n, REL_MAX_EXACT).astype(f32)
    large = REL_MAX_EXACT + (jnp.log(nf / REL_MAX_EXACT) / math.log(REL_MAX_DIST / REL_MAX_EXACT)
                             * (REL_BUCKETS - REL_MAX_EXACT)).astype(jnp.int32)
    large = jnp.minimum(large, REL_BUCKETS - 1)
    bucket = jnp.where(n < REL_MAX_EXACT, n, large)
    onehot = (bucket[:, None] == jnp.arange(REL_BUCKETS)[None, :]).astype(f32)
    by_dist = jnp.dot(onehot, rel_bias.astype(f32), precision=lax.Precision.HIGHEST).T * LOG2E
    n_heads = by_dist.shape[0]
    neg = jnp.full((n_heads, tk), NEG_INF, f32)
    period = tq + tk
    tiles = []
    for d, tail in ((0, neg), (1, by_dist[:, tq - tk:tq])):
        r = jnp.concatenate([by_dist[:, d * tq:(d + 1) * tq], tail], axis=1)
        t = jnp.tile(r, (1, tk))[:, :tk * (period - 1)].reshape(n_heads, tk, period - 1)
        tiles.append(t[:, :, :tq])
    far = rel_bias.astype(f32)[REL_BUCKETS - 1] * LOG2E
    corner = tiles[1][:, tk - CORNER:, :CORNER] - far[:, None, None]
    return tiles[0], corner, far


def _far_bucket_from():
    n = REL_MAX_EXACT
    while REL_MAX_EXACT + int(math.log(n / REL_MAX_EXACT) / math.log(REL_MAX_DIST / REL_MAX_EXACT)
                              * (REL_BUCKETS - REL_MAX_EXACT)) < REL_BUCKETS - 1:
        n += 1
    return n


def _diff_attention(q, k, vt, diag, corner, far, lam, subln, out_scale, bsz, seqlen, tq, tk):
    assert tq == tk and tq - CORNER + 1 >= _far_bucket_from()
    hd2 = 2 * HEAD_DIM
    q3 = q.reshape(bsz, seqlen, N_HEADS * hd2)
    k3 = k.reshape(bsz, seqlen, N_HEADS * hd2)
    smem = pl.BlockSpec(memory_space=pltpu.SMEM)
    hps = ATTN_HEADS_PER_STEP
    o = pl.pallas_call(
        functools.partial(_attn_kernel, tq, tk, out_scale),
        grid=(bsz, N_HEADS // hps),
        in_specs=[smem, smem,
                  pl.BlockSpec((1, seqlen, hps * hd2), lambda b, h: (b, 0, h)),
                  pl.BlockSpec((1, seqlen, hps * hd2), lambda b, h: (b, 0, h)),
                  pl.BlockSpec((1, hps, V_ROWS, seqlen), lambda b, h: (b, h, 0, 0)),
                  pl.BlockSpec((hps, tk, tq), lambda b, h: (h, 0, 0)),
                  pl.BlockSpec((hps, CORNER, CORNER), lambda b, h: (h, 0, 0)),
                  pl.BlockSpec((1, hd2), lambda b, h: (0, 0))],
        out_specs=pl.BlockSpec((1, seqlen, hps * hd2), lambda b, h: (b, 0, h)),
        out_shape=jax.ShapeDtypeStruct((bsz, seqlen, N_HEADS * hd2), bf16),
        scratch_shapes=[pltpu.VMEM((hps, 2 * tq, hd2), bf16), pltpu.VMEM((hps, 1, 2 * tq), f32),
                        pltpu.VMEM((hps, V_ROWS, 2 * tq), f32), pltpu.VMEM((2, hps, tk, 2 * tq), f32)],
        compiler_params=_cparams("parallel", "parallel"),
        name="diff_attn",
    )(lam.reshape(1).astype(f32), far, q3, k3, vt, diag, corner, subln.astype(f32)[None, :])
    return o.reshape(bsz * seqlen, N_HEADS * hd2)


def _gelu_tanh(v):
    c = 2.0 * math.sqrt(2.0 / math.pi)
    return v * jax.nn.sigmoid(v * (c + (c * 0.044715) * (v * v)))


def _ffn_ple_kernel(tiles_per_seq, fc, has_attn, has_q, has_kv, *refs):
    refs = list(refs)
    x_ref, p_ref = refs[:2]
    del refs[:2]
    if has_attn:
        a_ref, wo_ref = refs[:2]
        del refs[:2]
    fg_ref, wup_ref, cw_ref, cb_ref, wdn_ref, pg_ref, wgate_ref, wproj_ref = refs[:8]
    del refs[:8]
    if has_q or has_kv:
        seg_ref = refs.pop(0)
    if has_q:
        qg_ref, wq_ref, qgain_ref = refs[:3]
        del refs[:3]
    if has_kv:
        kvg_ref, wk_ref, wvt_ref, kgain_ref = refs[:4]
        del refs[:4]
    o_ref = refs.pop(0)
    if has_q:
        q_ref = refs.pop(0)
    if has_kv:
        k_ref, vt_ref = refs[:2]
        del refs[:2]
    carry_ref, act_ref, h_ref = refs

    i = pl.program_id(0)
    tm = x_ref.shape[0]
    d_ff = wdn_ref.shape[0]
    halo = SUBLANES

    @pl.when(i % tiles_per_seq == 0)
    def _():
        carry_ref[...] = jnp.zeros_like(carry_ref)

    x = x_ref[...]
    if has_attn:
        x = x + jnp.dot(a_ref[...], wo_ref[...], preferred_element_type=f32)
    h_ref[...] = _rms(x, fg_ref[...]).astype(bf16)
    row = lax.broadcasted_iota(jnp.int32, (halo, fc), 0)

    def up_proj(col):
        return jnp.dot(h_ref[...], wup_ref[:, col:col + fc], preferred_element_type=f32)

    def conv(col, up):
        prev = carry_ref[:, col:col + fc]
        carry_ref[:, col:col + fc] = up[tm - halo:]
        c = cb_ref[:, col:col + fc] + cw_ref[2:3, col:col + fc] * up
        for back in range(1, CONV_WIDTH):
            r = pltpu.roll(up, back, 0)
            head = jnp.where(row < back, pltpu.roll(prev, back, 0), r[:halo])
            c = c + cw_ref[2 - back:3 - back, col:col + fc] * jnp.concatenate([head, r[halo:]], axis=0)
        return c

    cols = list(range(0, d_ff, fc))
    bounds = [round(s * len(cols) / DOWN_SPLITS) * fc for s in range(DOWN_SPLITS + 1)]
    pending = None
    ups = (up_proj(cols[0]), up_proj(d_ff + cols[0]))
    for idx, col in enumerate(cols):
        nxt = (up_proj(cols[idx + 1]), up_proj(d_ff + cols[idx + 1])) if idx + 1 < len(cols) else None
        if pending is not None:
            x = x + jnp.dot(act_ref[:, pending[0]:pending[1]], wdn_ref[pending[0]:pending[1], :],
                            preferred_element_type=f32)
            pending = None
        act_ref[:, col:col + fc] = (_gelu_tanh(conv(col, ups[0])) * conv(d_ff + col, ups[1])).astype(bf16)
        if col + fc in bounds:
            pending = (bounds[bounds.index(col + fc) - 1], col + fc)
        ups = nxt
    lo, hi = pending
    halves = [slice(0, tm // 2), slice(tm // 2, tm)]
    xs = [x[r] + jnp.dot(act_ref[r, lo:hi], wdn_ref[lo:hi, :], preferred_element_type=f32) for r in halves]
    projs = [jnp.dot(p_ref[r, :].astype(bf16), wproj_ref[...], preferred_element_type=f32) for r in halves]
    gates = [jax.nn.sigmoid(jnp.dot(_rms(xh, pg_ref[...]).astype(bf16), wgate_ref[...],
                                    preferred_element_type=f32)) for xh in xs]
    xs = [xh + g * pr for xh, g, pr in zip(xs, gates, projs)]
    for r, xh in zip(halves, xs):
        o_ref[r, :] = xh
    if has_q or has_kv:
        xn = jnp.concatenate([xh * lax.rsqrt(jnp.mean(xh * xh, axis=-1, keepdims=True) + EPS) for xh in xs],
                             axis=0)
    if has_q:
        q = jnp.dot((xn * qg_ref[...]).astype(bf16), wq_ref[...], preferred_element_type=f32)
        q_ref[...] = _seg_norm(q, seg_ref, qgain_ref).astype(q_ref.dtype)
    if has_kv:
        hk = (xn * kvg_ref[...]).astype(bf16)
        k = jnp.dot(hk, wk_ref[...], preferred_element_type=f32)
        k_ref[...] = _seg_norm(k, seg_ref, kgain_ref).astype(k_ref.dtype)
        _store_vt(vt_ref, lax.dot_general(wvt_ref[...], hk, (((1,), (1,)), ((), ())),
                                          preferred_element_type=f32))


def _ffn_ple(x, p, layer, ffn, ple, seqlen, attn=None, next_q=None, next_kv=None, tm=512, fc=256):
    n, d = x.shape
    ffn_g, w_up, conv_w, conv_b, w_down = ffn
    ple_g, w_gate, w_proj = ple
    d_ff = w_down.shape[-2]
    bsz = n // seqlen
    tps = seqlen // tm
    row = lambda i: (i, 0)

    def full(a):
        if a.ndim == 3:
            return pl.BlockSpec((None,) + a.shape[1:], lambda i: (layer, 0, 0), pipeline_mode=pl.Buffered(1))
        return pl.BlockSpec(a.shape, lambda i: (0, 0), pipeline_mode=pl.Buffered(1))

    vec = lambda a: a.astype(f32)[None, :]
    args = [x, p]
    in_specs = [pl.BlockSpec((tm, d), row), pl.BlockSpec((None, tm, p.shape[2]), lambda i: (layer, i, 0))]
    if attn is not None:
        args += [attn[0], attn[1]]
        in_specs += [pl.BlockSpec((tm, attn[0].shape[1]), row), full(attn[1])]
    weights = [vec(ffn_g), w_up, conv_w.astype(f32), vec(conv_b), w_down, vec(ple_g), w_gate, w_proj]
    out_specs = [pl.BlockSpec((tm, d), row)]
    out_shape = [jax.ShapeDtypeStruct((n, d), f32)]
    if next_q is not None or next_kv is not None:
        weights.append(_segment_ones())
    if next_q is not None:
        weights += [vec(next_q[0]), next_q[1], _segment_gain(next_q[2])]
        out_specs.append(pl.BlockSpec((tm, next_q[1].shape[1]), row))
        out_shape.append(jax.ShapeDtypeStruct((n, next_q[1].shape[1]), bf16))
    if next_kv is not None:
        weights += [vec(next_kv[0]), next_kv[1], next_kv[2], _segment_gain(next_kv[3])]
        out_specs += [pl.BlockSpec((tm, next_kv[1].shape[1]), row),
                      pl.BlockSpec((1, N_HEADS, V_ROWS, tm), lambda i: (i // tps, 0, 0, i % tps))]
        out_shape += [jax.ShapeDtypeStruct((n, next_kv[1].shape[1]), bf16),
                      jax.ShapeDtypeStruct((bsz, N_HEADS, V_ROWS, seqlen), bf16)]
    args += weights
    in_specs += [full(w) for w in weights]
    return pl.pallas_call(
        functools.partial(_ffn_ple_kernel, tps, fc, attn is not None, next_q is not None, next_kv is not None),
        grid=(n // tm,),
        in_specs=in_specs,
        out_specs=out_specs,
        out_shape=out_shape,
        scratch_shapes=[pltpu.VMEM((SUBLANES, 2 * d_ff), f32),
                        pltpu.VMEM((tm, d_ff), bf16),
                        pltpu.VMEM((tm, d), bf16)],
        compiler_params=_cparams("arbitrary"),
        name="ffn_ple",
    )(*args)


def kernel(x, p, ssm_norm, ssm_w_in, ssm_log_dt, ssm_lambda_re, ssm_lambda_im, ssm_b_re, ssm_b_im, ssm_c_re, ssm_c_im, ssm_d, ssm_w_glu, kv_norm, kv_w, k_norm, attn_norm, attn_w_q, q_norm, lambda_q1, lambda_k1, lambda_q2, lambda_k2, subln, attn_w_o, rel_bias, ffn_norm, ffn_w_up, ffn_conv_w, ffn_conv_b, ffn_w_down, ple_norm, ple_w_gate, ple_w_proj):
    bsz, seqlen, d = x.shape
    depth = p.shape[0]
    n_a = ssm_norm.shape[0]
    n = bsz * seqlen
    x = x.reshape(n, d).astype(f32)
    p = p.reshape(depth, n, p.shape[-1])
    nk = N_HEADS * 2 * HEAD_DIM
    assert 1 <= n_a < depth
    diag, corner, far = _bias_tables(rel_bias, ATTN_BLOCK, ATTN_BLOCK)
    tables = jax.vmap(functools.partial(_ssm_tables, n_chunks=seqlen // SSM_T))(
        ssm_log_dt, ssm_lambda_re, ssm_lambda_im, ssm_b_re, ssm_b_im, ssm_c_re, ssm_c_im)
    w_in_t = jnp.swapaxes(ssm_w_in, 1, 2).astype(bf16)
    w_glu, w_q, w_o = ssm_w_glu.astype(bf16), attn_w_q.astype(bf16), attn_w_o.astype(bf16)
    w_up, w_down = ffn_w_up.astype(bf16), ffn_w_down.astype(bf16)
    w_gate, w_proj = ple_w_gate.astype(bf16), ple_w_proj.astype(bf16)
    lam_all = (jnp.exp(jnp.sum(lambda_q1.astype(f32) * lambda_k1.astype(f32), axis=-1))
               - jnp.exp(jnp.sum(lambda_q2.astype(f32) * lambda_k2.astype(f32), axis=-1)))
    q = k = vt = None
    for i in range(depth):
        attn = None
        if i < n_a:
            x3 = x.reshape(bsz, seqlen, d)
            ut = _win_t(x3, ssm_norm[i], w_in_t[i])
            yt = _ssm(ut, tables, i)
            x = _glu_res(x3, yt, ut, ssm_d[i], w_glu[i]).reshape(n, d)
        else:
            j = i - n_a
            lam_init = 0.8 - 0.6 * math.exp(-0.3 * i)
            o = _diff_attention(q, k, vt, diag, corner, far, lam_all[j] + lam_init, subln[j], 1.0 - lam_init,
                                bsz, seqlen, ATTN_BLOCK, ATTN_BLOCK)
            attn = (o, w_o[j])
        j_next = i + 1 - n_a
        next_q = next_kv = None
        if 0 <= j_next < depth - n_a:
            next_q = (attn_norm[j_next], w_q[j_next], q_norm[j_next])
            if j_next == 0:
                next_kv = (kv_norm, kv_w[:, :nk].astype(bf16), kv_w[:, nk:].T.astype(bf16), k_norm)
        outs = _ffn_ple(x, p, i,
                        (ffn_norm[i], w_up, ffn_conv_w[i], ffn_conv_b[i], w_down),
                        (ple_norm[i], w_gate, w_proj),
                        seqlen, attn=attn, next_q=next_q, next_kv=next_kv)
        x = outs[0]
        if next_q is not None:
            q = outs[1]
        if next_kv is not None:
            k, vt = outs[2], outs[3]
    return x.reshape(bsz, seqlen, d)
```

```python
import functools
import math

import jax
import jax.numpy as jnp
from jax import lax
from jax.experimental import pallas as pl
from jax.experimental.pallas import tpu as pltpu

EPS = 1e-6
NEG_INF = -1e30

SSM_GROUP_DIM = 16
SSM_STATE = 64
N_HEADS = 8
HEAD_DIM = 64
REL_BUCKETS = 32
REL_MAX_EXACT = REL_BUCKETS // 2
REL_MAX_DIST = 128
CONV_WIDTH = 3

LANES = 128
SUBLANES = 8
MXU_DIM = 256
VMEM_LIMIT_BYTES = 56 * 1024 * 1024

SSM_T = MXU_DIM // SSM_GROUP_DIM
SEG_CHUNK = MXU_DIM
ATTN_BLOCK = 256
ATTN_HEADS_PER_STEP = 2
CORNER = LANES
V_ROWS = 2 * HEAD_DIM + 16
LOG2E = math.log2(math.e)
GLU_SPLIT = 4
DOWN_SPLITS = 1

bf16 = jnp.bfloat16
f32 = jnp.float32


def _cparams(*sem):
    return pltpu.CompilerParams(dimension_semantics=sem, vmem_limit_bytes=VMEM_LIMIT_BYTES)


def _rms(xf, g):
    return xf * lax.rsqrt(jnp.mean(xf * xf, axis=-1, keepdims=True) + EPS) * g


def _seg_norm(y, seg_ref, gain_ref):
    cols = []
    for c in range(y.shape[1] // SEG_CHUNK):
        yc = y[:, c * SEG_CHUNK:(c + 1) * SEG_CHUNK]
        ss = jnp.dot((yc * yc).astype(bf16), seg_ref[...], preferred_element_type=f32)
        cols.append(yc * lax.rsqrt(ss * (1.0 / HEAD_DIM) + EPS) * gain_ref[...])
    return jnp.concatenate(cols, axis=1)


def _segment_ones():
    idx = jnp.arange(SEG_CHUNK) // HEAD_DIM
    return (idx[:, None] == idx[None, :]).astype(bf16)


def _segment_gain(g):
    return jnp.tile(g.astype(f32), SEG_CHUNK // HEAD_DIM)[None, :]


def _store_vt(vt_ref, vt):
    hd2 = 2 * HEAD_DIM
    for hh in range(vt_ref.shape[1]):
        vt_ref[0, hh, :hd2, :] = vt[hh * hd2:(hh + 1) * hd2].astype(vt_ref.dtype)
        vt_ref[0, hh, hd2:, :] = jnp.ones((V_ROWS - hd2, vt.shape[1]), vt_ref.dtype)


def _win_kernel(x_ref, g_ref, wt_ref, ut_ref):
    n_chunks, nt, d = x_ref.shape
    g_n = ut_ref.shape[0]
    xt = pltpu.einshape("ktd->tkd", x_ref[...]).reshape(nt * n_chunks, d)
    h = _rms(xt, g_ref[...]).astype(bf16)
    ut = lax.dot_general(wt_ref[...], h, (((1,), (1,)), ((), ())), preferred_element_type=f32)
    for tt in range(nt):
        blk = ut[:, tt * n_chunks:(tt + 1) * n_chunks].reshape(g_n, SSM_GROUP_DIM, n_chunks)
        ut_ref[:, tt * SSM_GROUP_DIM:(tt + 1) * SSM_GROUP_DIM, :] = blk.astype(ut_ref.dtype)


def _by_chunk(x3):
    bsz, seqlen, d = x3.shape
    return x3.reshape(bsz, seqlen // SSM_T, SSM_T // SUBLANES, SUBLANES, d)


def _chunk_spec(x5):
    _, n_chunks, _, nt, d = x5.shape
    return pl.BlockSpec((None, n_chunks, None, nt, d), lambda b, t: (b, 0, t, 0, 0))


def _win_t(x3, g, w_t):
    bsz, seqlen, d = x3.shape
    n_chunks = seqlen // SSM_T
    assert n_chunks == LANES
    g_n = d // SSM_GROUP_DIM
    x5 = _by_chunk(x3)
    return pl.pallas_call(
        _win_kernel,
        grid=(bsz, SSM_T // SUBLANES),
        in_specs=[_chunk_spec(x5),
                  pl.BlockSpec((1, d), lambda b, t: (0, 0)),
                  pl.BlockSpec(w_t.shape, lambda b, t: (0, 0))],
        out_specs=pl.BlockSpec((g_n, SUBLANES * SSM_GROUP_DIM, n_chunks), lambda b, t: (0, t, b)),
        out_shape=jax.ShapeDtypeStruct((g_n, SSM_T * SSM_GROUP_DIM, bsz * n_chunks), bf16),
        compiler_params=_cparams("parallel", "parallel"),
        name="win_t",
    )(x5, g.astype(f32)[None, :], w_t)


def _ssm_kernel(n_rounds, u_ref, mt_ref, w_ref, v_ref, pw_ref, y_ref, xin_ref):
    u0 = u_ref[0]
    u1 = u_ref[1]
    half = LANES
    seg = LANES
    st = (jnp.dot(w_ref[0], u0, preferred_element_type=f32)
          + jnp.dot(w_ref[1], u1, preferred_element_type=f32))
    n_seq = st.shape[1] // seg
    xr = jnp.stack([st[:half, b * seg:(b + 1) * seg] for b in range(n_seq)])
    xi = jnp.stack([st[half:, b * seg:(b + 1) * seg] for b in range(n_seq)])
    chunk = lax.broadcasted_iota(jnp.int32, xr.shape, 2)
    for j in range(n_rounds):
        d = 1 << j
        ar = pw_ref[j, 0]
        ai = pw_ref[j, 1]
        sr = pltpu.roll(xr, d, 2)
        si = pltpu.roll(xi, d, 2)
        xr, xi = xr + (ar * sr - ai * si), xi + (ar * si + ai * sr)
    xr = jnp.where(chunk >= 1, pltpu.roll(xr, 1, 2), 0.0).astype(bf16)
    xi = jnp.where(chunk >= 1, pltpu.roll(xi, 1, 2), 0.0).astype(bf16)
    for b in range(n_seq):
        xin_ref[:half, b * seg:(b + 1) * seg] = xr[b]
        xin_ref[half:, b * seg:(b + 1) * seg] = xi[b]
    ys = jnp.dot(v_ref[...], xin_ref[...], preferred_element_type=f32)
    y_ref[0] = (jnp.dot(mt_ref[0], u0, preferred_element_type=f32) + ys[:MXU_DIM]).astype(y_ref.dtype)
    y_ref[1] = (jnp.dot(mt_ref[1], u1, preferred_element_type=f32) + ys[MXU_DIM:]).astype(y_ref.dtype)


def _ssm_tables(log_dt, lam_re, lam_im, b_re, b_im, c_re, c_im, n_chunks):
    g_n, p_n = lam_re.shape
    t_n, c_n = SSM_T, SSM_GROUP_DIM
    dt = jnp.exp(log_dt.astype(f32))[:, None]
    lr = lam_re.astype(f32)
    li = lam_im.astype(f32)

    def power(n):
        nn = jnp.asarray(n, f32).reshape((-1, 1, 1))
        mag = jnp.exp(lr * dt * nn)
        return mag * jnp.cos(li * dt * nn), mag * jnp.sin(li * dt * nn)

    ab_r, ab_i = power([1])
    ab_r, ab_i = ab_r[0], ab_i[0]
    den = lr * lr + li * li
    f_r = ((ab_r - 1.0) * lr + ab_i * li) / den
    f_i = (ab_i * lr - (ab_r - 1.0) * li) / den
    br = b_re.astype(f32)
    bi = b_im.astype(f32)
    bb_r = f_r[..., None] * br - f_i[..., None] * bi
    bb_i = f_r[..., None] * bi + f_i[..., None] * br
    cr = c_re.astype(f32)
    ci = c_im.astype(f32)

    qr, qi = power(t_n - 1 - jnp.arange(t_n))
    t_r = qr[..., None] * bb_r[None] - qi[..., None] * bb_i[None]
    t_i = qr[..., None] * bb_i[None] + qi[..., None] * bb_r[None]
    kern = (jnp.einsum("gcp,dgpk->gcdk", cr, t_r, precision=lax.Precision.HIGHEST)
            - jnp.einsum("gcp,dgpk->gcdk", ci, t_i, precision=lax.Precision.HIGHEST))
    kern = kern.reshape(g_n, c_n, t_n * c_n)
    kern = jnp.concatenate([kern, jnp.zeros_like(kern)], axis=-1)
    mt = jnp.concatenate([kern[:, :, (t_n - 1 - t) * c_n:(2 * t_n - 1 - t) * c_n] for t in range(t_n)], axis=1)
    w_r = t_r.transpose(1, 2, 0, 3).reshape(g_n, p_n, t_n * c_n)
    w_i = t_i.transpose(1, 2, 0, 3).reshape(g_n, p_n, t_n * c_n)
    pr1, pi1 = power(1 + jnp.arange(t_n))
    pr1 = pr1.transpose(1, 0, 2)
    pi1 = pi1.transpose(1, 0, 2)
    v_r = (cr[:, None] * pr1[:, :, None, :] - ci[:, None] * pi1[:, :, None, :]).reshape(g_n, t_n * c_n, p_n)
    v_i = -(cr[:, None] * pi1[:, :, None, :] + ci[:, None] * pr1[:, :, None, :]).reshape(g_n, t_n * c_n, p_n)

    gp = g_n // 2
    z = jnp.zeros((gp, p_n, t_n * c_n), f32)
    w_r = w_r.reshape(gp, 2, p_n, t_n * c_n)
    w_i = w_i.reshape(gp, 2, p_n, t_n * c_n)
    w0 = jnp.concatenate([w_r[:, 0], z, w_i[:, 0], z], axis=1)
    w1 = jnp.concatenate([z, w_r[:, 1], z, w_i[:, 1]], axis=1)
    w_pack = jnp.stack([w0, w1], axis=1)
    zv = jnp.zeros((gp, t_n * c_n, p_n), f32)
    v_r = v_r.reshape(gp, 2, t_n * c_n, p_n)
    v_i = v_i.reshape(gp, 2, t_n * c_n, p_n)
    v_pack = jnp.concatenate([
        jnp.concatenate([v_r[:, 0], zv, v_i[:, 0], zv], axis=-1),
        jnp.concatenate([zv, v_r[:, 1], zv, v_i[:, 1]], axis=-1)], axis=1)
    n_rounds = (n_chunks - 1).bit_length()
    qr, qi = power(t_n * (2 ** jnp.arange(max(n_rounds, 1))))
    pw = jnp.stack([qr.reshape(-1, gp, 2 * p_n), qi.reshape(-1, gp, 2 * p_n)], axis=2)
    reach = (jnp.arange(LANES)[None, :] >= (2 ** jnp.arange(pw.shape[0]))[:, None]).astype(f32)
    pw = pw.transpose(1, 0, 2, 3)[..., None] * reach[None, :, None, None, :]
    return mt.reshape(gp, 2, t_n * c_n, t_n * c_n).astype(bf16), w_pack.astype(bf16), v_pack.astype(bf16), pw


def _ssm(ut, tables, layer):
    mt, w_pack, v_pack, pw = tables
    n_rounds = pw.shape[2]
    g_n, k, cols = ut.shape
    gp = g_n // 2
    return pl.pallas_call(
        functools.partial(_ssm_kernel, n_rounds),
        grid=(gp,),
        in_specs=[pl.BlockSpec((2, k, cols), lambda g: (g, 0, 0)),
                  pl.BlockSpec((None, None, 2, k, k), lambda g: (layer, g, 0, 0, 0)),
                  pl.BlockSpec((None, None, 2, k, k), lambda g: (layer, g, 0, 0, 0)),
                  pl.BlockSpec((None, None, 2 * k, k), lambda g: (layer, g, 0, 0)),
                  pl.BlockSpec((None, None) + pw.shape[2:], lambda g: (layer, g, 0, 0, 0, 0))],
        out_specs=pl.BlockSpec((2, k, cols), lambda g: (g, 0, 0)),
        out_shape=jax.ShapeDtypeStruct((g_n, k, cols), bf16),
        scratch_shapes=[pltpu.VMEM((k, cols), bf16)],
        compiler_params=_cparams("parallel"),
        name="ssm",
    )(ut, mt, w_pack, v_pack, pw)


def _glu_kernel(x_ref, y_ref, u_ref, d_ref, w_ref, o_ref):
    n_chunks, nt, dm = x_ref.shape
    g_n = y_ref.shape[0]
    per = nt // GLU_SPLIT
    upd = []
    for s in range(GLU_SPLIT):
        cols = []
        for tt in range(s * per, (s + 1) * per):
            sl = slice(tt * SSM_GROUP_DIM, (tt + 1) * SSM_GROUP_DIM)
            y = y_ref[:, sl, :].astype(f32) + d_ref[...] * u_ref[:, sl, :].astype(f32)
            cols.append(_gelu_tanh(y).reshape(g_n * SSM_GROUP_DIM, n_chunks))
        gt = jnp.concatenate(cols, axis=1)
        r = jnp.dot(gt.T.astype(bf16), w_ref[...], preferred_element_type=f32)
        upd.append(r[:, :dm] * jax.nn.sigmoid(r[:, dm:]))
    upd = jnp.concatenate(upd, axis=0).reshape(nt, n_chunks, dm)
    o_ref[...] = x_ref[...] + pltpu.einshape("tkd->ktd", upd)


def _glu_res(x3, yt, ut, d_skip, w_glu):
    bsz, seqlen, d = x3.shape
    g_n, _, cols = yt.shape
    n_chunks = cols // bsz
    d_tab = jnp.broadcast_to(d_skip.astype(f32).reshape(g_n, SSM_GROUP_DIM, 1), (g_n, SSM_GROUP_DIM, n_chunks))
    feat = pl.BlockSpec((g_n, SUBLANES * SSM_GROUP_DIM, n_chunks), lambda b, t: (0, t, b))
    x5 = _by_chunk(x3)
    out = pl.pallas_call(
        _glu_kernel,
        grid=(bsz, SSM_T // SUBLANES),
        in_specs=[_chunk_spec(x5), feat, feat,
                  pl.BlockSpec(d_tab.shape, lambda b, t: (0, 0, 0)),
                  pl.BlockSpec(w_glu.shape, lambda b, t: (0, 0))],
        out_specs=_chunk_spec(x5),
        out_shape=jax.ShapeDtypeStruct(x5.shape, f32),
        compiler_params=_cparams("parallel", "parallel"),
        name="glu_res",
    )(x5, yt, ut, d_tab, w_glu)
    return out.reshape(bsz, seqlen, d)


def _attn_kernel(tq, tk, out_scale, lam_ref, far_ref, q_ref, k_ref, vt_ref, diag_ref, corner_ref, sub_ref,
                 o_ref, qm_ref, m_ref, acc_ref, s_ref):
    hd2 = 2 * HEAD_DIM
    heads = range(vt_ref.shape[1])
    lane = lax.broadcasted_iota(jnp.int32, (tq, hd2), 1)
    far = [far_ref[pl.program_id(1) * len(heads) + hh] for hh in heads]

    def q_block(qi, carry):
        rows = pl.ds(pl.multiple_of(qi * tq, tq), tq)
        for hh in heads:
            q = q_ref[0, rows, hh * hd2:(hh + 1) * hd2].astype(f32) * (HEAD_DIM ** -0.5 * LOG2E)
            qm_ref[hh, :tq] = jnp.where(lane < HEAD_DIM, q, 0.0).astype(bf16)
            qm_ref[hh, tq:] = jnp.where(lane >= HEAD_DIM, q, 0.0).astype(bf16)
        m_ref[...] = jnp.full_like(m_ref, NEG_INF)
        acc_ref[...] = jnp.zeros_like(acc_ref)

        def fill(slot, ki):
            for hh in heads:
                k = k_ref[0, pl.ds(pl.multiple_of(ki * tk, tk), tk), hh * hd2:(hh + 1) * hd2]
                s_ref[slot, hh] = lax.dot_general(k, qm_ref[hh], (((1,), (1,)), ((), ())),
                                                  preferred_element_type=f32)

        def accumulate(hh, ki, s, shift):
            vt = vt_ref[0, hh, :, pl.ds(pl.multiple_of(ki * tk, tk), tk)]
            m_prev = m_ref[hh]
            m_new = jnp.maximum(m_prev, jnp.max(s, axis=0, keepdims=True) + shift)
            p = jnp.exp2(s - (m_new - shift))
            acc_ref[hh] = (jnp.exp2(m_prev - m_new) * acc_ref[hh]
                           + jnp.dot(vt, p.astype(bf16), preferred_element_type=f32))
            m_ref[hh] = m_new

        def earlier(slot, ki):
            near = jnp.where(ki == qi - 1, 1.0, 0.0)
            for hh in heads:
                s = s_ref[slot, hh]
                c = near * corner_ref[hh]
                lo = s[tk - CORNER:]
                lo = jnp.concatenate([lo[:, :CORNER] + c, lo[:, CORNER:tq], lo[:, tq:tq + CORNER] + c,
                                      lo[:, tq + CORNER:]], axis=1)
                accumulate(hh, ki, jnp.concatenate([s[:tk - CORNER], lo], axis=0), far[hh])

        def pair(i, carry):
            fill(1, 2 * i + 1)
            earlier(0, 2 * i)
            fill(0, 2 * i + 2)
            earlier(1, 2 * i + 1)
            return carry

        fill(0, 0)
        lax.fori_loop(0, qi // 2, pair, 0)

        def diagonal(slot):
            for hh in heads:
                b = diag_ref[hh]
                accumulate(hh, qi, s_ref[slot, hh] + jnp.concatenate([b, b], axis=1), 0.0)

        @pl.when(qi % 2 == 1)
        def _():
            fill(1, qi)
            earlier(0, qi - 1)
            diagonal(1)

        @pl.when(qi % 2 == 0)
        def _():
            diagonal(0)

        for hh in heads:
            acc = acc_ref[hh]
            ot = acc[:hd2] / acc[hd2:hd2 + 1]
            ot = ot[:, :tq] - lam_ref[0] * ot[:, tq:]
            ot = ot * lax.rsqrt(jnp.mean(ot * ot, axis=0, keepdims=True) + EPS)
            o_ref[0, rows, hh * hd2:(hh + 1) * hd2] = (ot.T * sub_ref[...] * out_scale).astype(o_ref.dtype)
        return carry

    lax.fori_loop(0, k_ref.shape[1] // tq, q_block, 0)


def _bias_tables(rel_bias, tq, tk):
    n = jnp.arange(2 * tq)
    nf = jnp.maximum(n, REL_MAX_EXACT).astype(f32)
    large = REL_MAX_EXACT + (jnp.log(nf / REL_MAX_EXACT) / math.log(REL_MAX_DIST / REL_MAX_EXACT)
                             * (REL_BUCKETS - REL_MAX_EXACT)).astype(jnp.int32)
    large = jnp.minimum(large, REL_BUCKETS - 1)
    bucket = jnp.where(n < REL_MAX_EXACT, n, large)
    onehot = (bucket[:, None] == jnp.arange(REL_BUCKETS)[None, :]).astype(f32)
    by_dist = jnp.dot(onehot, rel_bias.astype(f32), precision=lax.Precision.HIGHEST).T * LOG2E
    n_heads = by_dist.shape[0]
    neg = jnp.full((n_heads, tk), NEG_INF, f32)
    period = tq + tk
    tiles = []
    for d, tail in ((0, neg), (1, by_dist[:, tq - tk:tq])):
        r = jnp.concatenate([by_dist[:, d * tq:(d + 1) * tq], tail], axis=1)
        t = jnp.tile(r, (1, tk))[:, :tk * (period - 1)].reshape(n_heads, tk, period - 1)
        tiles.append(t[:, :, :tq])
    far = rel_bias.astype(f32)[REL_BUCKETS - 1] * LOG2E
    corner = tiles[1][:, tk - CORNER:, :CORNER] - far[:, None, None]
    return tiles[0], corner, far


def _far_bucket_from():
    n = REL_MAX_EXACT
    while REL_MAX_EXACT + int(math.log(n / REL_MAX_EXACT) / math.log(REL_MAX_DIST / REL_MAX_EXACT)
                              * (REL_BUCKETS - REL_MAX_EXACT)) < REL_BUCKETS - 1:
        n += 1
    return n


def _diff_attention(q, k, vt, diag, corner, far, lam, subln, out_scale, bsz, seqlen, tq, tk):
    assert tq == tk and tq - CORNER + 1 >= _far_bucket_from()
    hd2 = 2 * HEAD_DIM
    q3 = q.reshape(bsz, seqlen, N_HEADS * hd2)
    k3 = k.reshape(bsz, seqlen, N_HEADS * hd2)
    smem = pl.BlockSpec(memory_space=pltpu.SMEM)
    hps = ATTN_HEADS_PER_STEP
    o = pl.pallas_call(
        functools.partial(_attn_kernel, tq, tk, out_scale),
        grid=(bsz, N_HEADS // hps),
        in_specs=[smem, smem,
                  pl.BlockSpec((1, seqlen, hps * hd2), lambda b, h: (b, 0, h)),
                  pl.BlockSpec((1, seqlen, hps * hd2), lambda b, h: (b, 0, h)),
                  pl.BlockSpec((1, hps, V_ROWS, seqlen), lambda b, h: (b, h, 0, 0)),
                  pl.BlockSpec((hps, tk, tq), lambda b, h: (h, 0, 0)),
                  pl.BlockSpec((hps, CORNER, CORNER), lambda b, h: (h, 0, 0)),
                  pl.BlockSpec((1, hd2), lambda b, h: (0, 0))],
        out_specs=pl.BlockSpec((1, seqlen, hps * hd2), lambda b, h: (b, 0, h)),
        out_shape=jax.ShapeDtypeStruct((bsz, seqlen, N_HEADS * hd2), bf16),
        scratch_shapes=[pltpu.VMEM((hps, 2 * tq, hd2), bf16), pltpu.VMEM((hps, 1, 2 * tq), f32),
                        pltpu.VMEM((hps, V_ROWS, 2 * tq), f32), pltpu.VMEM((2, hps, tk, 2 * tq), f32)],
        compiler_params=_cparams("parallel", "parallel"),
        name="diff_attn",
    )(lam.reshape(1).astype(f32), far, q3, k3, vt, diag, corner, subln.astype(f32)[None, :])
    return o.reshape(bsz * seqlen, N_HEADS * hd2)


def _gelu_tanh(v):
    c = 2.0 * math.sqrt(2.0 / math.pi)
    return v * jax.nn.sigmoid(v * (c + (c * 0.044715) * (v * v)))


def _ffn_ple_kernel(tiles_per_seq, fc, has_attn, has_q, has_kv, *refs):
    refs = list(refs)
    x_ref, p_ref = refs[:2]
    del refs[:2]
    if has_attn:
        a_ref, wo_ref = refs[:2]
        del refs[:2]
    fg_ref, wup_ref, cw_ref, cb_ref, wdn_ref, pg_ref, wgate_ref, wproj_ref = refs[:8]
    del refs[:8]
    if has_q or has_kv:
        seg_ref = refs.pop(0)
    if has_q:
        qg_ref, wq_ref, qgain_ref = refs[:3]
        del refs[:3]
    if has_kv:
        kvg_ref, wk_ref, wvt_ref, kgain_ref = refs[:4]
        del refs[:4]
    o_ref = refs.pop(0)
    if has_q:
        q_ref = refs.pop(0)
    if has_kv:
        k_ref, vt_ref = refs[:2]
        del refs[:2]
    carry_ref, act_ref, h_ref = refs

    i = pl.program_id(0)
    tm = x_ref.shape[0]
    d_ff = wdn_ref.shape[0]
    halo = SUBLANES

    @pl.when(i % tiles_per_seq == 0)
    def _():
        carry_ref[...] = jnp.zeros_like(carry_ref)

    x = x_ref[...]
    if has_attn:
        x = x + jnp.dot(a_ref[...], wo_ref[...], preferred_element_type=f32)
    h_ref[...] = _rms(x, fg_ref[...]).astype(bf16)
    row = lax.broadcasted_iota(jnp.int32, (halo, fc), 0)

    def up_proj(col):
        return jnp.dot(h_ref[...], wup_ref[:, col:col + fc], preferred_element_type=f32)

    def conv(col, up):
        prev = carry_ref[:, col:col + fc]
        carry_ref[:, col:col + fc] = up[tm - halo:]
        c = cb_ref[:, col:col + fc] + cw_ref[2:3, col:col + fc] * up
        for back in range(1, CONV_WIDTH):
            r = pltpu.roll(up, back, 0)
            head = jnp.where(row < back, pltpu.roll(prev, back, 0), r[:halo])
            c = c + cw_ref[2 - back:3 - back, col:col + fc] * jnp.concatenate([head, r[halo:]], axis=0)
        return c

    cols = list(range(0, d_ff, fc))
    bounds = [round(s * len(cols) / DOWN_SPLITS) * fc for s in range(DOWN_SPLITS + 1)]
    pending = None
    ups = (up_proj(cols[0]), up_proj(d_ff + cols[0]))
    for idx, col in enumerate(cols):
        nxt = (up_proj(cols[idx + 1]), up_proj(d_ff + cols[idx + 1])) if idx + 1 < len(cols) else None
        if pending is not None:
            x = x + jnp.dot(act_ref[:, pending[0]:pending[1]], wdn_ref[pending[0]:pending[1], :],
                            preferred_element_type=f32)
            pending = None
        act_ref[:, col:col + fc] = (_gelu_tanh(conv(col, ups[0])) * conv(d_ff + col, ups[1])).astype(bf16)
        if col + fc in bounds:
            pending = (bounds[bounds.index(col + fc) - 1], col + fc)
        ups = nxt
    lo, hi = pending
    halves = [slice(0, tm // 2), slice(tm // 2, tm)]
    xs = [x[r] + jnp.dot(act_ref[r, lo:hi], wdn_ref[lo:hi, :], preferred_element_type=f32) for r in halves]
    projs = [jnp.dot(p_ref[r, :].astype(bf16), wproj_ref[...], preferred_element_type=f32) for r in halves]
    gates = [jax.nn.sigmoid(jnp.dot(_rms(xh, pg_ref[...]).astype(bf16), wgate_ref[...],
                                    preferred_element_type=f32)) for xh in xs]
    xs = [xh + g * pr for xh, g, pr in zip(xs, gates, projs)]
    for r, xh in zip(halves, xs):
        o_ref[r, :] = xh
    if has_q or has_kv:
        xn = jnp.concatenate([xh * lax.rsqrt(jnp.mean(xh * xh, axis=-1, keepdims=True) + EPS) for xh in xs],
                             axis=0)
    if has_q:
        q = jnp.dot((xn * qg_ref[...]).astype(bf16), wq_ref[...], preferred_element_type=f32)
        q_ref[...] = _seg_norm(q, seg_ref, qgain_ref).astype(q_ref.dtype)
    if has_kv:
        hk = (xn * kvg_ref[...]).astype(bf16)
        k = jnp.dot(hk, wk_ref[...], preferred_element_type=f32)
        k_ref[...] = _seg_norm(k, seg_ref, kgain_ref).astype(k_ref.dtype)
        _store_vt(vt_ref, lax.dot_general(wvt_ref[...], hk, (((1,), (1,)), ((), ())),
                                          preferred_element_type=f32))


def _ffn_ple(x, p, layer, ffn, ple, seqlen, attn=None, next_q=None, next_kv=None, tm=512, fc=256):
    n, d = x.shape
    ffn_g, w_up, conv_w, conv_b, w_down = ffn
    ple_g, w_gate, w_proj = ple
    d_ff = w_down.shape[-2]
    bsz = n // seqlen
    tps = seqlen // tm
    row = lambda i: (i, 0)

    def full(a):
        if a.ndim == 3:
            return pl.BlockSpec((None,) + a.shape[1:], lambda i: (layer, 0, 0), pipeline_mode=pl.Buffered(1))
        return pl.BlockSpec(a.shape, lambda i: (0, 0), pipeline_mode=pl.Buffered(1))

    vec = lambda a: a.astype(f32)[None, :]
    args = [x, p]
    in_specs = [pl.BlockSpec((tm, d), row), pl.BlockSpec((None, tm, p.shape[2]), lambda i: (layer, i, 0))]
    if attn is not None:
        args += [attn[0], attn[1]]
        in_specs += [pl.BlockSpec((tm, attn[0].shape[1]), row), full(attn[1])]
    weights = [vec(ffn_g), w_up, conv_w.astype(f32), vec(conv_b), w_down, vec(ple_g), w_gate, w_proj]
    out_specs = [pl.BlockSpec((tm, d), row)]
    out_shape = [jax.ShapeDtypeStruct((n, d), f32)]
    if next_q is not None or next_kv is not None:
        weights.append(_segment_ones())
    if next_q is not None:
        weights += [vec(next_q[0]), next_q[1], _segment_gain(next_q[2])]
        out_specs.append(pl.BlockSpec((tm, next_q[1].shape[1]), row))
        out_shape.append(jax.ShapeDtypeStruct((n, next_q[1].shape[1]), bf16))
    if next_kv is not None:
        weights += [vec(next_kv[0]), next_kv[1], next_kv[2], _segment_gain(next_kv[3])]
        out_specs += [pl.BlockSpec((tm, next_kv[1].shape[1]), row),
                      pl.BlockSpec((1, N_HEADS, V_ROWS, tm), lambda i: (i // tps, 0, 0, i % tps))]
        out_shape += [jax.ShapeDtypeStruct((n, next_kv[1].shape[1]), bf16),
                      jax.ShapeDtypeStruct((bsz, N_HEADS, V_ROWS, seqlen), bf16)]
    args += weights
    in_specs += [full(w) for w in weights]
    return pl.pallas_call(
        functools.partial(_ffn_ple_kernel, tps, fc, attn is not None, next_q is not None, next_kv is not None),
        grid=(n // tm,),
        in_specs=in_specs,
        out_specs=out_specs,
        out_shape=out_shape,
        scratch_shapes=[pltpu.VMEM((SUBLANES, 2 * d_ff), f32),
                        pltpu.VMEM((tm, d_ff), bf16),
                        pltpu.VMEM((tm, d), bf16)],
        compiler_params=_cparams("arbitrary"),
        name="ffn_ple",
    )(*args)


def kernel(x, p, ssm_norm, ssm_w_in, ssm_log_dt, ssm_lambda_re, ssm_lambda_im, ssm_b_re, ssm_b_im, ssm_c_re, ssm_c_im, ssm_d, ssm_w_glu, kv_norm, kv_w, k_norm, attn_norm, attn_w_q, q_norm, lambda_q1, lambda_k1, lambda_q2, lambda_k2, subln, attn_w_o, rel_bias, ffn_norm, ffn_w_up, ffn_conv_w, ffn_conv_b, ffn_w_down, ple_norm, ple_w_gate, ple_w_proj):
    bsz, seqlen, d = x.shape
    depth = p.shape[0]
    n_a = ssm_norm.shape[0]
    n = bsz * seqlen
    x = x.reshape(n, d).astype(f32)
    p = p.reshape(depth, n, p.shape[-1])
    nk = N_HEADS * 2 * HEAD_DIM
    assert 1 <= n_a < depth
    diag, corner, far = _bias_tables(rel_bias, ATTN_BLOCK, ATTN_BLOCK)
    tables = jax.vmap(functools.partial(_ssm_tables, n_chunks=seqlen // SSM_T))(
        ssm_log_dt, ssm_lambda_re, ssm_lambda_im, ssm_b_re, ssm_b_im, ssm_c_re, ssm_c_im)
    w_in_t = jnp.swapaxes(ssm_w_in, 1, 2).astype(bf16)
    w_glu, w_q, w_o = ssm_w_glu.astype(bf16), attn_w_q.astype(bf16), attn_w_o.astype(bf16)
    w_up, w_down = ffn_w_up.astype(bf16), ffn_w_down.astype(bf16)
    w_gate, w_proj = ple_w_gate.astype(bf16), ple_w_proj.astype(bf16)
    lam_all = (jnp.exp(jnp.sum(lambda_q1.astype(f32) * lambda_k1.astype(f32), axis=-1))
               - jnp.exp(jnp.sum(lambda_q2.astype(f32) * lambda_k2.astype(f32), axis=-1)))
    q = k = vt = None
    for i in range(depth):
        attn = None
        if i < n_a:
            x3 = x.reshape(bsz, seqlen, d)
            ut = _win_t(x3, ssm_norm[i], w_in_t[i])
            yt = _ssm(ut, tables, i)
            x = _glu_res(x3, yt, ut, ssm_d[i], w_glu[i]).reshape(n, d)
        else:
            j = i - n_a
            lam_init = 0.8 - 0.6 * math.exp(-0.3 * i)
            o = _diff_attention(q, k, vt, diag, corner, far, lam_all[j] + lam_init, subln[j], 1.0 - lam_init,
                                bsz, seqlen, ATTN_BLOCK, ATTN_BLOCK)
            attn = (o, w_o[j])
        j_next = i + 1 - n_a
        next_q = next_kv = None
        if 0 <= j_next < depth - n_a:
            next_q = (attn_norm[j_next], w_q[j_next], q_norm[j_next])
            if j_next == 0:
                next_kv = (kv_norm, kv_w[:, :nk].astype(bf16), kv_w[:, nk:].T.astype(bf16), k_norm)
        outs = _ffn_ple(x, p, i,
                        (ffn_norm[i], w_up, ffn_conv_w[i], ffn_conv_b[i], w_down),
                        (ple_norm[i], w_gate, w_proj),
                        seqlen, attn=attn, next_q=next_q, next_kv=next_kv)
        x = outs[0]
        if next_q is not None:
            q = outs[1]
        if next_kv is not None:
            k, vt = outs[2], outs[3]
    return x.reshape(bsz, seqlen, d)
```

```python
import functools
import math

import jax
import jax.numpy as jnp
from jax import lax
from jax.experimental import pallas as pl
from jax.experimental.pallas import tpu as pltpu

EPS = 1e-6
NEG_INF = -1e30

SSM_GROUP_DIM = 16
SSM_STATE = 64
N_HEADS = 8
HEAD_DIM = 64
REL_BUCKETS = 32
REL_MAX_EXACT = REL_BUCKETS // 2
REL_MAX_DIST = 128
CONV_WIDTH = 3

LANES = 128
SUBLANES = 8
MXU_DIM = 256
VMEM_LIMIT_BYTES = 56 * 1024 * 1024

SSM_T = MXU_DIM // SSM_GROUP_DIM
SEG_CHUNK = MXU_DIM
ATTN_BLOCK = 256
ATTN_HEADS_PER_STEP = 8
CORNER = LANES
V_ROWS = 2 * HEAD_DIM + 16
LOG2E = math.log2(math.e)
GLU_SPLIT = 4
DOWN_SPLITS = 1

bf16 = jnp.bfloat16
f32 = jnp.float32


def _cparams(*sem):
    return pltpu.CompilerParams(dimension_semantics=sem, vmem_limit_bytes=VMEM_LIMIT_BYTES)


def _rms(xf, g):
    return xf * lax.rsqrt(jnp.mean(xf * xf, axis=-1, keepdims=True) + EPS) * g


def _seg_norm(y, seg_ref, gain_ref):
    cols = []
    for c in range(y.shape[1] // SEG_CHUNK):
        yc = y[:, c * SEG_CHUNK:(c + 1) * SEG_CHUNK]
        ss = jnp.dot((yc * yc).astype(bf16), seg_ref[...], preferred_element_type=f32)
        cols.append(yc * lax.rsqrt(ss * (1.0 / HEAD_DIM) + EPS) * gain_ref[...])
    return jnp.concatenate(cols, axis=1)


def _segment_ones():
    idx = jnp.arange(SEG_CHUNK) // HEAD_DIM
    return (idx[:, None] == idx[None, :]).astype(bf16)


def _segment_gain(g):
    return jnp.tile(g.astype(f32), SEG_CHUNK // HEAD_DIM)[None, :]


def _store_vt(vt_ref, vt):
    hd2 = 2 * HEAD_DIM
    for hh in range(vt_ref.shape[1]):
        vt_ref[0, hh, :hd2, :] = vt[hh * hd2:(hh + 1) * hd2].astype(vt_ref.dtype)
        vt_ref[0, hh, hd2:, :] = jnp.ones((V_ROWS - hd2, vt.shape[1]), vt_ref.dtype)


def _win_kernel(x_ref, g_ref, wt_ref, ut_ref):
    n_chunks, nt, d = x_ref.shape
    g_n = ut_ref.shape[0]
    xt = pltpu.einshape("ktd->tkd", x_ref[...]).reshape(nt * n_chunks, d)
    h = _rms(xt, g_ref[...]).astype(bf16)
    ut = lax.dot_general(wt_ref[...], h, (((1,), (1,)), ((), ())), preferred_element_type=f32)
    for tt in range(nt):
        blk = ut[:, tt * n_chunks:(tt + 1) * n_chunks].reshape(g_n, SSM_GROUP_DIM, n_chunks)
        ut_ref[:, tt * SSM_GROUP_DIM:(tt + 1) * SSM_GROUP_DIM, :] = blk.astype(ut_ref.dtype)


def _by_chunk(x3):
    bsz, seqlen, d = x3.shape
    return x3.reshape(bsz, seqlen // SSM_T, SSM_T // SUBLANES, SUBLANES, d)


def _chunk_spec(x5):
    _, n_chunks, _, nt, d = x5.shape
    return pl.BlockSpec((None, n_chunks, None, nt, d), lambda b, t: (b, 0, t, 0, 0))


def _win_t(x3, g, w_t):
    bsz, seqlen, d = x3.shape
    n_chunks = seqlen // SSM_T
    assert n_chunks == LANES
    g_n = d // SSM_GROUP_DIM
    x5 = _by_chunk(x3)
    return pl.pallas_call(
        _win_kernel,
        grid=(bsz, SSM_T // SUBLANES),
        in_specs=[_chunk_spec(x5),
                  pl.BlockSpec((1, d), lambda b, t: (0, 0)),
                  pl.BlockSpec(w_t.shape, lambda b, t: (0, 0))],
        out_specs=pl.BlockSpec((g_n, SUBLANES * SSM_GROUP_DIM, n_chunks), lambda b, t: (0, t, b)),
        out_shape=jax.ShapeDtypeStruct((g_n, SSM_T * SSM_GROUP_DIM, bsz * n_chunks), bf16),
        compiler_params=_cparams("parallel", "parallel"),
        name="win_t",
    )(x5, g.astype(f32)[None, :], w_t)


def _ssm_kernel(n_rounds, u_ref, mt_ref, w_ref, v_ref, pw_ref, y_ref, xin_ref):
    u0 = u_ref[0]
    u1 = u_ref[1]
    half = LANES
    seg = LANES
    st = (jnp.dot(w_ref[0], u0, preferred_element_type=f32)
          + jnp.dot(w_ref[1], u1, preferred_element_type=f32))
    n_seq = st.shape[1] // seg
    xr = jnp.stack([st[:half, b * seg:(b + 1) * seg] for b in range(n_seq)])
    xi = jnp.stack([st[half:, b * seg:(b + 1) * seg] for b in range(n_seq)])
    chunk = lax.broadcasted_iota(jnp.int32, xr.shape, 2)
    for j in range(n_rounds):
        d = 1 << j
        ar = pw_ref[j, 0]
        ai = pw_ref[j, 1]
        sr = pltpu.roll(xr, d, 2)
        si = pltpu.roll(xi, d, 2)
        xr, xi = xr + (ar * sr - ai * si), xi + (ar * si + ai * sr)
    xr = jnp.where(chunk >= 1, pltpu.roll(xr, 1, 2), 0.0).astype(bf16)
    xi = jnp.where(chunk >= 1, pltpu.roll(xi, 1, 2), 0.0).astype(bf16)
    for b in range(n_seq):
        xin_ref[:half, b * seg:(b + 1) * seg] = xr[b]
        xin_ref[half:, b * seg:(b + 1) * seg] = xi[b]
    ys = jnp.dot(v_ref[...], xin_ref[...], preferred_element_type=f32)
    y_ref[0] = (jnp.dot(mt_ref[0], u0, preferred_element_type=f32) + ys[:MXU_DIM]).astype(y_ref.dtype)
    y_ref[1] = (jnp.dot(mt_ref[1], u1, preferred_element_type=f32) + ys[MXU_DIM:]).astype(y_ref.dtype)


def _ssm_tables(log_dt, lam_re, lam_im, b_re, b_im, c_re, c_im, n_chunks):
    g_n, p_n = lam_re.shape
    t_n, c_n = SSM_T, SSM_GROUP_DIM
    dt = jnp.exp(log_dt.astype(f32))[:, None]
    lr = lam_re.astype(f32)
    li = lam_im.astype(f32)

    def power(n):
        nn = jnp.asarray(n, f32).reshape((-1, 1, 1))
        mag = jnp.exp(lr * dt * nn)
        return mag * jnp.cos(li * dt * nn), mag * jnp.sin(li * dt * nn)

    ab_r, ab_i = power([1])
    ab_r, ab_i = ab_r[0], ab_i[0]
    den = lr * lr + li * li
    f_r = ((ab_r - 1.0) * lr + ab_i * li) / den
    f_i = (ab_i * lr - (ab_r - 1.0) * li) / den
    br = b_re.astype(f32)
    bi = b_im.astype(f32)
    bb_r = f_r[..., None] * br - f_i[..., None] * bi
    bb_i = f_r[..., None] * bi + f_i[..., None] * br
    cr = c_re.astype(f32)
    ci = c_im.astype(f32)

    qr, qi = power(t_n - 1 - jnp.arange(t_n))
    t_r = qr[..., None] * bb_r[None] - qi[..., None] * bb_i[None]
    t_i = qr[..., None] * bb_i[None] + qi[..., None] * bb_r[None]
    kern = (jnp.einsum("gcp,dgpk->gcdk", cr, t_r, precision=lax.Precision.HIGHEST)
            - jnp.einsum("gcp,dgpk->gcdk", ci, t_i, precision=lax.Precision.HIGHEST))
    kern = kern.reshape(g_n, c_n, t_n * c_n)
    kern = jnp.concatenate([kern, jnp.zeros_like(kern)], axis=-1)
    mt = jnp.concatenate([kern[:, :, (t_n - 1 - t) * c_n:(2 * t_n - 1 - t) * c_n] for t in range(t_n)], axis=1)
    w_r = t_r.transpose(1, 2, 0, 3).reshape(g_n, p_n, t_n * c_n)
    w_i = t_i.transpose(1, 2, 0, 3).reshape(g_n, p_n, t_n * c_n)
    pr1, pi1 = power(1 + jnp.arange(t_n))
    pr1 = pr1.transpose(1, 0, 2)
    pi1 = pi1.transpose(1, 0, 2)
    v_r = (cr[:, None] * pr1[:, :, None, :] - ci[:, None] * pi1[:, :, None, :]).reshape(g_n, t_n * c_n, p_n)
    v_i = -(cr[:, None] * pi1[:, :, None, :] + ci[:, None] * pr1[:, :, None, :]).reshape(g_n, t_n * c_n, p_n)

    gp = g_n // 2
    z = jnp.zeros((gp, p_n, t_n * c_n), f32)
    w_r = w_r.reshape(gp, 2, p_n, t_n * c_n)
    w_i = w_i.reshape(gp, 2, p_n, t_n * c_n)
    w0 = jnp.concatenate([w_r[:, 0], z, w_i[:, 0], z], axis=1)
    w1 = jnp.concatenate([z, w_r[:, 1], z, w_i[:, 1]], axis=1)
    w_pack = jnp.stack([w0, w1], axis=1)
    zv = jnp.zeros((gp, t_n * c_n, p_n), f32)
    v_r = v_r.reshape(gp, 2, t_n * c_n, p_n)
    v_i = v_i.reshape(gp, 2, t_n * c_n, p_n)
    v_pack = jnp.concatenate([
        jnp.concatenate([v_r[:, 0], zv, v_i[:, 0], zv], axis=-1),
        jnp.concatenate([zv, v_r[:, 1], zv, v_i[:, 1]], axis=-1)], axis=1)
    n_rounds = (n_chunks - 1).bit_length()
    qr, qi = power(t_n * (2 ** jnp.arange(max(n_rounds, 1))))
    pw = jnp.stack([qr.reshape(-1, gp, 2 * p_n), qi.reshape(-1, gp, 2 * p_n)], axis=2)
    reach = (jnp.arange(LANES)[None, :] >= (2 ** jnp.arange(pw.shape[0]))[:, None]).astype(f32)
    pw = pw.transpose(1, 0, 2, 3)[..., None] * reach[None, :, None, None, :]
    return mt.reshape(gp, 2, t_n * c_n, t_n * c_n).astype(bf16), w_pack.astype(bf16), v_pack.astype(bf16), pw


def _ssm(ut, tables, layer):
    mt, w_pack, v_pack, pw = tables
    n_rounds = pw.shape[2]
    g_n, k, cols = ut.shape
    gp = g_n // 2
    return pl.pallas_call(
        functools.partial(_ssm_kernel, n_rounds),
        grid=(gp,),
        in_specs=[pl.BlockSpec((2, k, cols), lambda g: (g, 0, 0)),
                  pl.BlockSpec((None, None, 2, k, k), lambda g: (layer, g, 0, 0, 0)),
                  pl.BlockSpec((None, None, 2, k, k), lambda g: (layer, g, 0, 0, 0)),
                  pl.BlockSpec((None, None, 2 * k, k), lambda g: (layer, g, 0, 0)),
                  pl.BlockSpec((None, None) + pw.shape[2:], lambda g: (layer, g, 0, 0, 0, 0))],
        out_specs=pl.BlockSpec((2, k, cols), lambda g: (g, 0, 0)),
        out_shape=jax.ShapeDtypeStruct((g_n, k, cols), bf16),
        scratch_shapes=[pltpu.VMEM((k, cols), bf16)],
        compiler_params=_cparams("parallel"),
        name="ssm",
    )(ut, mt, w_pack, v_pack, pw)


def _glu_kernel(x_ref, y_ref, u_ref, d_ref, w_ref, o_ref):
    n_chunks, nt, dm = x_ref.shape
    g_n = y_ref.shape[0]
    per = nt // GLU_SPLIT
    upd = []
    for s in range(GLU_SPLIT):
        cols = []
        for tt in range(s * per, (s + 1) * per):
            sl = slice(tt * SSM_GROUP_DIM, (tt + 1) * SSM_GROUP_DIM)
            y = y_ref[:, sl, :].astype(f32) + d_ref[...] * u_ref[:, sl, :].astype(f32)
            cols.append(_gelu_tanh(y).reshape(g_n * SSM_GROUP_DIM, n_chunks))
        gt = jnp.concatenate(cols, axis=1)
        r = jnp.dot(gt.T.astype(bf16), w_ref[...], preferred_element_type=f32)
        upd.append(r[:, :dm] * jax.nn.sigmoid(r[:, dm:]))
    upd = jnp.concatenate(upd, axis=0).reshape(nt, n_chunks, dm)
    o_ref[...] = x_ref[...] + pltpu.einshape("tkd->ktd", upd)


def _glu_res(x3, yt, ut, d_skip, w_glu):
    bsz, seqlen, d = x3.shape
    g_n, _, cols = yt.shape
    n_chunks = cols // bsz
    d_tab = jnp.broadcast_to(d_skip.astype(f32).reshape(g_n, SSM_GROUP_DIM, 1), (g_n, SSM_GROUP_DIM, n_chunks))
    feat = pl.BlockSpec((g_n, SUBLANES * SSM_GROUP_DIM, n_chunks), lambda b, t: (0, t, b))
    x5 = _by_chunk(x3)
    out = pl.pallas_call(
        _glu_kernel,
        grid=(bsz, SSM_T // SUBLANES),
        in_specs=[_chunk_spec(x5), feat, feat,
                  pl.BlockSpec(d_tab.shape, lambda b, t: (0, 0, 0)),
                  pl.BlockSpec(w_glu.shape, lambda b, t: (0, 0))],
        out_specs=_chunk_spec(x5),
        out_shape=jax.ShapeDtypeStruct(x5.shape, f32),
        compiler_params=_cparams("parallel", "parallel"),
        name="glu_res",
    )(x5, yt, ut, d_tab, w_glu)
    return out.reshape(bsz, seqlen, d)


def _attn_kernel(tq, tk, out_scale, lam_ref, far_ref, q_ref, k_ref, vt_ref, diag_ref, corner_ref, sub_ref,
                 o_ref, qm_ref, m_ref, acc_ref, s_ref):
    hd2 = 2 * HEAD_DIM
    heads = range(vt_ref.shape[1])
    lane = lax.broadcasted_iota(jnp.int32, (tq, hd2), 1)
    far = [far_ref[pl.program_id(1) * len(heads) + hh] for hh in heads]

    def q_block(qi, carry):
        rows = pl.ds(pl.multiple_of(qi * tq, tq), tq)
        for hh in heads:
            q = q_ref[0, rows, hh * hd2:(hh + 1) * hd2].astype(f32) * (HEAD_DIM ** -0.5 * LOG2E)
            qm_ref[hh, :tq] = jnp.where(lane < HEAD_DIM, q, 0.0).astype(bf16)
            qm_ref[hh, tq:] = jnp.where(lane >= HEAD_DIM, q, 0.0).astype(bf16)
        m_ref[...] = jnp.full_like(m_ref, NEG_INF)
        acc_ref[...] = jnp.zeros_like(acc_ref)

        def fill(slot, ki):
            for hh in heads:
                k = k_ref[0, pl.ds(pl.multiple_of(ki * tk, tk), tk), hh * hd2:(hh + 1) * hd2]
                s_ref[slot, hh] = lax.dot_general(k, qm_ref[hh], (((1,), (1,)), ((), ())),
                                                  preferred_element_type=f32)

        def accumulate(hh, ki, s, shift):
            vt = vt_ref[0, hh, :, pl.ds(pl.multiple_of(ki * tk, tk), tk)]
            m_prev = m_ref[hh]
            m_new = jnp.maximum(m_prev, jnp.max(s, axis=0, keepdims=True) + shift)
            p = jnp.exp2(s - (m_new - shift))
            acc_ref[hh] = (jnp.exp2(m_prev - m_new) * acc_ref[hh]
                           + jnp.dot(vt, p.astype(bf16), preferred_element_type=f32))
            m_ref[hh] = m_new

        def earlier(slot, ki):
            near = jnp.where(ki == qi - 1, 1.0, 0.0)
            for hh in heads:
                s = s_ref[slot, hh]
                c = near * corner_ref[hh]
                lo = s[tk - CORNER:]
                lo = jnp.concatenate([lo[:, :CORNER] + c, lo[:, CORNER:tq], lo[:, tq:tq + CORNER] + c,
                                      lo[:, tq + CORNER:]], axis=1)
                accumulate(hh, ki, jnp.concatenate([s[:tk - CORNER], lo], axis=0), far[hh])

        def pair(i, carry):
            fill(1, 2 * i + 1)
            earlier(0, 2 * i)
            fill(0, 2 * i + 2)
            earlier(1, 2 * i + 1)
            return carry

        fill(0, 0)
        lax.fori_loop(0, qi // 2, pair, 0)

        def diagonal(slot):
            for hh in heads:
                b = diag_ref[hh]
                accumulate(hh, qi, s_ref[slot, hh] + jnp.concatenate([b, b], axis=1), 0.0)

        @pl.when(qi % 2 == 1)
        def _():
            fill(1, qi)
            earlier(0, qi - 1)
            diagonal(1)

        @pl.when(qi % 2 == 0)
        def _():
            diagonal(0)

        for hh in heads:
            acc = acc_ref[hh]
            ot = acc[:hd2] / acc[hd2:hd2 + 1]
            ot = ot[:, :tq] - lam_ref[0] * ot[:, tq:]
            ot = ot * lax.rsqrt(jnp.mean(ot * ot, axis=0, keepdims=True) + EPS)
            o_ref[0, rows, hh * hd2:(hh + 1) * hd2] = (ot.T * sub_ref[...] * out_scale).astype(o_ref.dtype)
        return carry

    lax.fori_loop(0, k_ref.shape[1] // tq, q_block, 0)


def _bias_tables(rel_bias, tq, tk):
    n = jnp.arange(2 * tq)
    nf = jnp.maximum(n, REL_MAX_EXACT).astype(f32)
    large = REL_MAX_EXACT + (jnp.log(nf / REL_MAX_EXACT) / math.log(REL_MAX_DIST / REL_MAX_EXACT)
                             * (REL_BUCKETS - REL_MAX_EXACT)).astype(jnp.int32)
    large = jnp.minimum(large, REL_BUCKETS - 1)
    bucket = jnp.where(n < REL_MAX_EXACT, n, large)
    onehot = (bucket[:, None] == jnp.arange(REL_BUCKETS)[None, :]).astype(f32)
    by_dist = jnp.dot(onehot, rel_bias.astype(f32), precision=lax.Precision.HIGHEST).T * LOG2E
    n_heads = by_dist.shape[0]
    neg = jnp.full((n_heads, tk), NEG_INF, f32)
    period = tq + tk
    tiles = []
    for d, tail in ((0, neg), (1, by_dist[:, tq - tk:tq])):
        r = jnp.concatenate([by_dist[:, d * tq:(d + 1) * tq], tail], axis=1)
        t = jnp.tile(r, (1, tk))[:, :tk * (period - 1)].reshape(n_heads, tk, period - 1)
        tiles.append(t[:, :, :tq])
    far = rel_bias.astype(f32)[REL_BUCKETS - 1] * LOG2E
    corner = tiles[1][:, tk - CORNER:, :CORNER] - far[:, None, None]
    return tiles[0], corner, far


def _far_bucket_from():
    n = REL_MAX_EXACT
    while REL_MAX_EXACT + int(math.log(n / REL_MAX_EXACT) / math.log(REL_MAX_DIST / REL_MAX_EXACT)
                              * (REL_BUCKETS - REL_MAX_EXACT)) < REL_BUCKETS - 1:
        n += 1
    return n


def _diff_attention(q, k, vt, diag, corner, far, lam, subln, out_scale, bsz, seqlen, tq, tk):
    assert tq == tk and tq - CORNER + 1 >= _far_bucket_from()
    hd2 = 2 * HEAD_DIM
    q3 = q.reshape(bsz, seqlen, N_HEADS * hd2)
    k3 = k.reshape(bsz, seqlen, N_HEADS * hd2)
    smem = pl.BlockSpec(memory_space=pltpu.SMEM)
    hps = ATTN_HEADS_PER_STEP
    o = pl.pallas_call(
        functools.partial(_attn_kernel, tq, tk, out_scale),
        grid=(bsz, N_HEADS // hps),
        in_specs=[smem, smem,
                  pl.BlockSpec((1, seqlen, hps * hd2), lambda b, h: (b, 0, h)),
                  pl.BlockSpec((1, seqlen, hps * hd2), lambda b, h: (b, 0, h)),
                  pl.BlockSpec((1, hps, V_ROWS, seqlen), lambda b, h: (b, h, 0, 0)),
                  pl.BlockSpec((hps, tk, tq), lambda b, h: (h, 0, 0)),
                  pl.BlockSpec((hps, CORNER, CORNER), lambda b, h: (h, 0, 0)),
                  pl.BlockSpec((1, hd2), lambda b, h: (0, 0))],
        out_specs=pl.BlockSpec((1, seqlen, hps * hd2), lambda b, h: (b, 0, h)),
        out_shape=jax.ShapeDtypeStruct((bsz, seqlen, N_HEADS * hd2), bf16),
        scratch_shapes=[pltpu.VMEM((hps, 2 * tq, hd2), bf16), pltpu.VMEM((hps, 1, 2 * tq), f32),
                        pltpu.VMEM((hps, V_ROWS, 2 * tq), f32), pltpu.VMEM((2, hps, tk, 2 * tq), f32)],
        compiler_params=_cparams("parallel", "parallel"),
        name="diff_attn",
    )(lam.reshape(1).astype(f32), far, q3, k3, vt, diag, corner, subln.astype(f32)[None, :])
    return o.reshape(bsz * seqlen, N_HEADS * hd2)


def _gelu_tanh(v):
    c = 2.0 * math.sqrt(2.0 / math.pi)
    return v * jax.nn.sigmoid(v * (c + (c * 0.044715) * (v * v)))


def _ffn_ple_kernel(tiles_per_seq, fc, has_attn, has_q, has_kv, *refs):
    refs = list(refs)
    x_ref, p_ref = refs[:2]
    del refs[:2]
    if has_attn:
        a_ref, wo_ref = refs[:2]
        del refs[:2]
    fg_ref, wup_ref, cw_ref, cb_ref, wdn_ref, pg_ref, wgate_ref, wproj_ref = refs[:8]
    del refs[:8]
    if has_q or has_kv:
        seg_ref = refs.pop(0)
    if has_q:
        qg_ref, wq_ref, qgain_ref = refs[:3]
        del refs[:3]
    if has_kv:
        kvg_ref, wk_ref, wvt_ref, kgain_ref = refs[:4]
        del refs[:4]
    o_ref = refs.pop(0)
    if has_q:
        q_ref = refs.pop(0)
    if has_kv:
        k_ref, vt_ref = refs[:2]
        del refs[:2]
    carry_ref, act_ref, h_ref = refs

    i = pl.program_id(0)
    tm = x_ref.shape[0]
    d_ff = wdn_ref.shape[0]
    halo = SUBLANES

    @pl.when(i % tiles_per_seq == 0)
    def _():
        carry_ref[...] = jnp.zeros_like(carry_ref)

    x = x_ref[...]
    if has_attn:
        x = x + jnp.dot(a_ref[...], wo_ref[...], preferred_element_type=f32)
    h_ref[...] = _rms(x, fg_ref[...]).astype(bf16)
    row = lax.broadcasted_iota(jnp.int32, (halo, fc), 0)

    def up_proj(col):
        return jnp.dot(h_ref[...], wup_ref[:, col:col + fc], preferred_element_type=f32)

    def conv(col, up):
        prev = carry_ref[:, col:col + fc]
        carry_ref[:, col:col + fc] = up[tm - halo:]
        c = cb_ref[:, col:col + fc] + cw_ref[2:3, col:col + fc] * up
        for back in range(1, CONV_WIDTH):
            r = pltpu.roll(up, back, 0)
            head = jnp.where(row < back, pltpu.roll(prev, back, 0), r[:halo])
            c = c + cw_ref[2 - back:3 - back, col:col + fc] * jnp.concatenate([head, r[halo:]], axis=0)
        return c

    cols = list(range(0, d_ff, fc))
    bounds = [round(s * len(cols) / DOWN_SPLITS) * fc for s in range(DOWN_SPLITS + 1)]
    pending = None
    ups = (up_proj(cols[0]), up_proj(d_ff + cols[0]))
    for idx, col in enumerate(cols):
        nxt = (up_proj(cols[idx + 1]), up_proj(d_ff + cols[idx + 1])) if idx + 1 < len(cols) else None
        if pending is not None:
            x = x + jnp.dot(act_ref[:, pending[0]:pending[1]], wdn_ref[pending[0]:pending[1], :],
                            preferred_element_type=f32)
            pending = None
        act_ref[:, col:col + fc] = (_gelu_tanh(conv(col, ups[0])) * conv(d_ff + col, ups[1])).astype(bf16)
        if col + fc in bounds:
            pending = (bounds[bounds.index(col + fc) - 1], col + fc)
        ups = nxt
    lo, hi = pending
    halves = [slice(0, tm // 2), slice(tm // 2, tm)]
    xs = [x[r] + jnp.dot(act_ref[r, lo:hi], wdn_ref[lo:hi, :], preferred_element_type=f32) for r in halves]
    projs = [jnp.dot(p_ref[r, :].astype(bf16), wproj_ref[...], preferred_element_type=f32) for r in halves]
    gates = [jax.nn.sigmoid(jnp.dot(_rms(xh, pg_ref[...]).astype(bf16), wgate_ref[...],
                                    preferred_element_type=f32)) for xh in xs]
    xs = [xh + g * pr for xh, g, pr in zip(xs, gates, projs)]
    for r, xh in zip(halves, xs):
        o_ref[r, :] = xh
    if has_q or has_kv:
        xn = jnp.concatenate([xh * lax.rsqrt(jnp.mean(xh * xh, axis=-1, keepdims=True) + EPS) for xh in xs],
                             axis=0)
    if has_q:
        q = jnp.dot((xn * qg_ref[...]).astype(bf16), wq_ref[...], preferred_element_type=f32)
        q_ref[...] = _seg_norm(q, seg_ref, qgain_ref).astype(q_ref.dtype)
    if has_kv:
        hk = (xn * kvg_ref[...]).astype(bf16)
        k = jnp.dot(hk, wk_ref[...], preferred_element_type=f32)
        k_ref[...] = _seg_norm(k, seg_ref, kgain_ref).astype(k_ref.dtype)
        _store_vt(vt_ref, lax.dot_general(wvt_ref[...], hk, (((1,), (1,)), ((), ())),
                                          preferred_element_type=f32))


def _ffn_ple(x, p, layer, ffn, ple, seqlen, attn=None, next_q=None, next_kv=None, tm=512, fc=256):
    n, d = x.shape
    ffn_g, w_up, conv_w, conv_b, w_down = ffn
    ple_g, w_gate, w_proj = ple
    d_ff = w_down.shape[-2]
    bsz = n // seqlen
    tps = seqlen // tm
    row = lambda i: (i, 0)

    def full(a):
        if a.ndim == 3:
            return pl.BlockSpec((None,) + a.shape[1:], lambda i: (layer, 0, 0), pipeline_mode=pl.Buffered(1))
        return pl.BlockSpec(a.shape, lambda i: (0, 0), pipeline_mode=pl.Buffered(1))

    vec = lambda a: a.astype(f32)[None, :]
    args = [x, p]
    in_specs = [pl.BlockSpec((tm, d), row), pl.BlockSpec((None, tm, p.shape[2]), lambda i: (layer, i, 0))]
    if attn is not None:
        args += [attn[0], attn[1]]
        in_specs += [pl.BlockSpec((tm, attn[0].shape[1]), row), full(attn[1])]
    weights = [vec(ffn_g), w_up, conv_w.astype(f32), vec(conv_b), w_down, vec(ple_g), w_gate, w_proj]
    out_specs = [pl.BlockSpec((tm, d), row)]
    out_shape = [jax.ShapeDtypeStruct((n, d), f32)]
    if next_q is not None or next_kv is not None:
        weights.append(_segment_ones())
    if next_q is not None:
        weights += [vec(next_q[0]), next_q[1], _segment_gain(next_q[2])]
        out_specs.append(pl.BlockSpec((tm, next_q[1].shape[1]), row))
        out_shape.append(jax.ShapeDtypeStruct((n, next_q[1].shape[1]), bf16))
    if next_kv is not None:
        weights += [vec(next_kv[0]), next_kv[1], next_kv[2], _segment_gain(next_kv[3])]
        out_specs += [pl.BlockSpec((tm, next_kv[1].shape[1]), row),
                      pl.BlockSpec((1, N_HEADS, V_ROWS, tm), lambda i: (i // tps, 0, 0, i % tps))]
        out_shape += [jax.ShapeDtypeStruct((n, next_kv[1].shape[1]), bf16),
                      jax.ShapeDtypeStruct((bsz, N_HEADS, V_ROWS, seqlen), bf16)]
    args += weights
    in_specs += [full(w) for w in weights]
    return pl.pallas_call(
        functools.partial(_ffn_ple_kernel, tps, fc, attn is not None, next_q is not None, next_kv is not None),
        grid=(n // tm,),
        in_specs=in_specs,
        out_specs=out_specs,
        out_shape=out_shape,
        scratch_shapes=[pltpu.VMEM((SUBLANES, 2 * d_ff), f32),
                        pltpu.VMEM((tm, d_ff), bf16),
                        pltpu.VMEM((tm, d), bf16)],
        compiler_params=_cparams("arbitrary"),
        name="ffn_ple",
    )(*args)


def kernel(x, p, ssm_norm, ssm_w_in, ssm_log_dt, ssm_lambda_re, ssm_lambda_im, ssm_b_re, ssm_b_im, ssm_c_re, ssm_c_im, ssm_d, ssm_w_glu, kv_norm, kv_w, k_norm, attn_norm, attn_w_q, q_norm, lambda_q1, lambda_k1, lambda_q2, lambda_k2, subln, attn_w_o, rel_bias, ffn_norm, ffn_w_up, ffn_conv_w, ffn_conv_b, ffn_w_down, ple_norm, ple_w_gate, ple_w_proj):
    bsz, seqlen, d = x.shape
    depth = p.shape[0]
    n_a = ssm_norm.shape[0]
    n = bsz * seqlen
    x = x.reshape(n, d).astype(f32)
    p = p.reshape(depth, n, p.shape[-1])
    nk = N_HEADS * 2 * HEAD_DIM
    assert 1 <= n_a < depth
    diag, corner, far = _bias_tables(rel_bias, ATTN_BLOCK, ATTN_BLOCK)
    tables = jax.vmap(functools.partial(_ssm_tables, n_chunks=seqlen // SSM_T))(
        ssm_log_dt, ssm_lambda_re, ssm_lambda_im, ssm_b_re, ssm_b_im, ssm_c_re, ssm_c_im)
    w_in_t = jnp.swapaxes(ssm_w_in, 1, 2).astype(bf16)
    w_glu, w_q, w_o = ssm_w_glu.astype(bf16), attn_w_q.astype(bf16), attn_w_o.astype(bf16)
    w_up, w_down = ffn_w_up.astype(bf16), ffn_w_down.astype(bf16)
    w_gate, w_proj = ple_w_gate.astype(bf16), ple_w_proj.astype(bf16)
    lam_all = (jnp.exp(jnp.sum(lambda_q1.astype(f32) * lambda_k1.astype(f32), axis=-1))
               - jnp.exp(jnp.sum(lambda_q2.astype(f32) * lambda_k2.astype(f32), axis=-1)))
    q = k = vt = None
    for i in range(depth):
        attn = None
        if i < n_a:
            x3 = x.reshape(bsz, seqlen, d)
            ut = _win_t(x3, ssm_norm[i], w_in_t[i])
            yt = _ssm(ut, tables, i)
            x = _glu_res(x3, yt, ut, ssm_d[i], w_glu[i]).reshape(n, d)
        else:
            j = i - n_a
            lam_init = 0.8 - 0.6 * math.exp(-0.3 * i)
            o = _diff_attention(q, k, vt, diag, corner, far, lam_all[j] + lam_init, subln[j], 1.0 - lam_init,
                                bsz, seqlen, ATTN_BLOCK, ATTN_BLOCK)
            attn = (o, w_o[j])
        j_next = i + 1 - n_a
        next_q = next_kv = None
        if 0 <= j_next < depth - n_a:
            next_q = (attn_norm[j_next], w_q[j_next], q_norm[j_next])
            if j_next == 0:
                next_kv = (kv_norm, kv_w[:, :nk].astype(bf16), kv_w[:, nk:].T.astype(bf16), k_norm)
        outs = _ffn_ple(x, p, i,
                        (ffn_norm[i], w_up, ffn_conv_w[i], ffn_conv_b[i], w_down),
                        (ple_norm[i], w_gate, w_proj),
                        seqlen, attn=attn, next_q=next_q, next_kv=next_kv)
        x = outs[0]
        if next_q is not None:
            q = outs[1]
        if next_kv is not None:
            k, vt = outs[2], outs[3]
    return x.reshape(bsz, seqlen, d)
```

```python
import functools
import math

import jax
import jax.numpy as jnp
from jax import lax
from jax.experimental import pallas as pl
from jax.experimental.pallas import tpu as pltpu

EPS = 1e-6
NEG_INF = -1e30

SSM_GROUP_DIM = 16
SSM_STATE = 64
N_HEADS = 8
HEAD_DIM = 64
REL_BUCKETS = 32
REL_MAX_EXACT = REL_BUCKETS // 2
REL_MAX_DIST = 128
CONV_WIDTH = 3

LANES = 128
SUBLANES = 8
MXU_DIM = 256
VMEM_LIMIT_BYTES = 56 * 1024 * 1024

SSM_T = MXU_DIM // SSM_GROUP_DIM
SEG_CHUNK = MXU_DIM
ATTN_BLOCK = 256
ATTN_HEADS_PER_STEP = 8
CORNER = LANES
V_ROWS = 2 * HEAD_DIM + 16
LOG2E = math.log2(math.e)
SSM_PAIRS_PER_STEP = 2
GLU_SPLIT = 4
DOWN_SPLITS = 1

bf16 = jnp.bfloat16
f32 = jnp.float32


def _cparams(*sem):
    return pltpu.CompilerParams(dimension_semantics=sem, vmem_limit_bytes=VMEM_LIMIT_BYTES)


def _rms(xf, g):
    return xf * lax.rsqrt(jnp.mean(xf * xf, axis=-1, keepdims=True) + EPS) * g


def _seg_norm(y, seg_ref, gain_ref):
    cols = []
    for c in range(y.shape[1] // SEG_CHUNK):
        yc = y[:, c * SEG_CHUNK:(c + 1) * SEG_CHUNK]
        ss = jnp.dot((yc * yc).astype(bf16), seg_ref[...], preferred_element_type=f32)
        cols.append(yc * lax.rsqrt(ss * (1.0 / HEAD_DIM) + EPS) * gain_ref[...])
    return jnp.concatenate(cols, axis=1)


def _segment_ones():
    idx = jnp.arange(SEG_CHUNK) // HEAD_DIM
    return (idx[:, None] == idx[None, :]).astype(bf16)


def _segment_gain(g):
    return jnp.tile(g.astype(f32), SEG_CHUNK // HEAD_DIM)[None, :]


def _store_vt(vt_ref, vt):
    hd2 = 2 * HEAD_DIM
    for hh in range(vt_ref.shape[1]):
        vt_ref[0, hh, :hd2, :] = vt[hh * hd2:(hh + 1) * hd2].astype(vt_ref.dtype)
        vt_ref[0, hh, hd2:, :] = jnp.ones((V_ROWS - hd2, vt.shape[1]), vt_ref.dtype)


def _win_kernel(x_ref, g_ref, wt_ref, ut_ref):
    n_chunks, nt, d = x_ref.shape
    g_n = ut_ref.shape[0]
    xt = pltpu.einshape("ktd->tkd", x_ref[...]).reshape(nt * n_chunks, d)
    h = _rms(xt, g_ref[...]).astype(bf16)
    ut = lax.dot_general(wt_ref[...], h, (((1,), (1,)), ((), ())), preferred_element_type=f32)
    for tt in range(nt):
        blk = ut[:, tt * n_chunks:(tt + 1) * n_chunks].reshape(g_n, SSM_GROUP_DIM, n_chunks)
        ut_ref[:, tt * SSM_GROUP_DIM:(tt + 1) * SSM_GROUP_DIM, :] = blk.astype(ut_ref.dtype)


def _by_chunk(x3):
    bsz, seqlen, d = x3.shape
    return x3.reshape(bsz, seqlen // SSM_T, SSM_T // SUBLANES, SUBLANES, d)


def _chunk_spec(x5):
    _, n_chunks, _, nt, d = x5.shape
    return pl.BlockSpec((None, n_chunks, None, nt, d), lambda b, t: (b, 0, t, 0, 0))


def _win_t(x3, g, w_t):
    bsz, seqlen, d = x3.shape
    n_chunks = seqlen // SSM_T
    assert n_chunks == LANES
    g_n = d // SSM_GROUP_DIM
    x5 = _by_chunk(x3)
    return pl.pallas_call(
        _win_kernel,
        grid=(bsz, SSM_T // SUBLANES),
        in_specs=[_chunk_spec(x5),
                  pl.BlockSpec((1, d), lambda b, t: (0, 0)),
                  pl.BlockSpec(w_t.shape, lambda b, t: (0, 0))],
        out_specs=pl.BlockSpec((g_n, SUBLANES * SSM_GROUP_DIM, n_chunks), lambda b, t: (0, t, b)),
        out_shape=jax.ShapeDtypeStruct((g_n, SSM_T * SSM_GROUP_DIM, bsz * n_chunks), bf16),
        compiler_params=_cparams("parallel", "parallel"),
        name="win_t",
    )(x5, g.astype(f32)[None, :], w_t)


def _ssm_kernel(n_rounds, u_ref, mt_ref, w_ref, v_ref, pw_ref, y_ref, xin_ref):
    for pp in range(mt_ref.shape[0]):
        _ssm_pair(n_rounds, u_ref.at[2 * pp:2 * pp + 2], mt_ref.at[pp], w_ref.at[pp], v_ref.at[pp],
                  pw_ref.at[pp], y_ref.at[2 * pp:2 * pp + 2], xin_ref.at[pp])


def _ssm_pair(n_rounds, u_ref, mt_ref, w_ref, v_ref, pw_ref, y_ref, xin_ref):
    u0 = u_ref[0]
    u1 = u_ref[1]
    half = LANES
    seg = LANES
    st = (jnp.dot(w_ref[0], u0, preferred_element_type=f32)
          + jnp.dot(w_ref[1], u1, preferred_element_type=f32))
    n_seq = st.shape[1] // seg
    xr = jnp.stack([st[:half, b * seg:(b + 1) * seg] for b in range(n_seq)])
    xi = jnp.stack([st[half:, b * seg:(b + 1) * seg] for b in range(n_seq)])
    chunk = lax.broadcasted_iota(jnp.int32, xr.shape, 2)
    for j in range(n_rounds):
        d = 1 << j
        ar = pw_ref[j, 0]
        ai = pw_ref[j, 1]
        sr = pltpu.roll(xr, d, 2)
        si = pltpu.roll(xi, d, 2)
        xr, xi = xr + (ar * sr - ai * si), xi + (ar * si + ai * sr)
    xr = jnp.where(chunk >= 1, pltpu.roll(xr, 1, 2), 0.0).astype(bf16)
    xi = jnp.where(chunk >= 1, pltpu.roll(xi, 1, 2), 0.0).astype(bf16)
    for b in range(n_seq):
        xin_ref[:half, b * seg:(b + 1) * seg] = xr[b]
        xin_ref[half:, b * seg:(b + 1) * seg] = xi[b]
    ys = jnp.dot(v_ref[...], xin_ref[...], preferred_element_type=f32)
    y_ref[0] = (jnp.dot(mt_ref[0], u0, preferred_element_type=f32) + ys[:MXU_DIM]).astype(y_ref.dtype)
    y_ref[1] = (jnp.dot(mt_ref[1], u1, preferred_element_type=f32) + ys[MXU_DIM:]).astype(y_ref.dtype)


def _ssm_tables(log_dt, lam_re, lam_im, b_re, b_im, c_re, c_im, n_chunks):
    g_n, p_n = lam_re.shape
    t_n, c_n = SSM_T, SSM_GROUP_DIM
    dt = jnp.exp(log_dt.astype(f32))[:, None]
    lr = lam_re.astype(f32)
    li = lam_im.astype(f32)

    def power(n):
        nn = jnp.asarray(n, f32).reshape((-1, 1, 1))
        mag = jnp.exp(lr * dt * nn)
        return mag * jnp.cos(li * dt * nn), mag * jnp.sin(li * dt * nn)

    ab_r, ab_i = power([1])
    ab_r, ab_i = ab_r[0], ab_i[0]
    den = lr * lr + li * li
    f_r = ((ab_r - 1.0) * lr + ab_i * li) / den
    f_i = (ab_i * lr - (ab_r - 1.0) * li) / den
    br = b_re.astype(f32)
    bi = b_im.astype(f32)
    bb_r = f_r[..., None] * br - f_i[..., None] * bi
    bb_i = f_r[..., None] * bi + f_i[..., None] * br
    cr = c_re.astype(f32)
    ci = c_im.astype(f32)

    qr, qi = power(t_n - 1 - jnp.arange(t_n))
    t_r = qr[..., None] * bb_r[None] - qi[..., None] * bb_i[None]
    t_i = qr[..., None] * bb_i[None] + qi[..., None] * bb_r[None]
    kern = (jnp.einsum("gcp,dgpk->gcdk", cr, t_r, precision=lax.Precision.HIGHEST)
            - jnp.einsum("gcp,dgpk->gcdk", ci, t_i, precision=lax.Precision.HIGHEST))
    kern = kern.reshape(g_n, c_n, t_n * c_n)
    kern = jnp.concatenate([kern, jnp.zeros_like(kern)], axis=-1)
    mt = jnp.concatenate([kern[:, :, (t_n - 1 - t) * c_n:(2 * t_n - 1 - t) * c_n] for t in range(t_n)], axis=1)
    w_r = t_r.transpose(1, 2, 0, 3).reshape(g_n, p_n, t_n * c_n)
    w_i = t_i.transpose(1, 2, 0, 3).reshape(g_n, p_n, t_n * c_n)
    pr1, pi1 = power(1 + jnp.arange(t_n))
    pr1 = pr1.transpose(1, 0, 2)
    pi1 = pi1.transpose(1, 0, 2)
    v_r = (cr[:, None] * pr1[:, :, None, :] - ci[:, None] * pi1[:, :, None, :]).reshape(g_n, t_n * c_n, p_n)
    v_i = -(cr[:, None] * pi1[:, :, None, :] + ci[:, None] * pr1[:, :, None, :]).reshape(g_n, t_n * c_n, p_n)

    gp = g_n // 2
    z = jnp.zeros((gp, p_n, t_n * c_n), f32)
    w_r = w_r.reshape(gp, 2, p_n, t_n * c_n)
    w_i = w_i.reshape(gp, 2, p_n, t_n * c_n)
    w0 = jnp.concatenate([w_r[:, 0], z, w_i[:, 0], z], axis=1)
    w1 = jnp.concatenate([z, w_r[:, 1], z, w_i[:, 1]], axis=1)
    w_pack = jnp.stack([w0, w1], axis=1)
    zv = jnp.zeros((gp, t_n * c_n, p_n), f32)
    v_r = v_r.reshape(gp, 2, t_n * c_n, p_n)
    v_i = v_i.reshape(gp, 2, t_n * c_n, p_n)
    v_pack = jnp.concatenate([
        jnp.concatenate([v_r[:, 0], zv, v_i[:, 0], zv], axis=-1),
        jnp.concatenate([zv, v_r[:, 1], zv, v_i[:, 1]], axis=-1)], axis=1)
    n_rounds = (n_chunks - 1).bit_length()
    qr, qi = power(t_n * (2 ** jnp.arange(max(n_rounds, 1))))
    pw = jnp.stack([qr.reshape(-1, gp, 2 * p_n), qi.reshape(-1, gp, 2 * p_n)], axis=2)
    reach = (jnp.arange(LANES)[None, :] >= (2 ** jnp.arange(pw.shape[0]))[:, None]).astype(f32)
    pw = pw.transpose(1, 0, 2, 3)[..., None] * reach[None, :, None, None, :]
    return mt.reshape(gp, 2, t_n * c_n, t_n * c_n).astype(bf16), w_pack.astype(bf16), v_pack.astype(bf16), pw


def _ssm(ut, tables, layer):
    mt, w_pack, v_pack, pw = tables
    n_rounds = pw.shape[2]
    g_n, k, cols = ut.shape
    gp = g_n // 2
    pp = SSM_PAIRS_PER_STEP
    return pl.pallas_call(
        functools.partial(_ssm_kernel, n_rounds),
        grid=(gp // pp,),
        in_specs=[pl.BlockSpec((2 * pp, k, cols), lambda g: (g, 0, 0)),
                  pl.BlockSpec((None, pp, 2, k, k), lambda g: (layer, g, 0, 0, 0)),
                  pl.BlockSpec((None, pp, 2, k, k), lambda g: (layer, g, 0, 0, 0)),
                  pl.BlockSpec((None, pp, 2 * k, k), lambda g: (layer, g, 0, 0)),
                  pl.BlockSpec((None, pp) + pw.shape[2:], lambda g: (layer, g, 0, 0, 0, 0))],
        out_specs=pl.BlockSpec((2 * pp, k, cols), lambda g: (g, 0, 0)),
        out_shape=jax.ShapeDtypeStruct((g_n, k, cols), bf16),
        scratch_shapes=[pltpu.VMEM((pp, k, cols), bf16)],
        compiler_params=_cparams("parallel"),
        name="ssm",
    )(ut, mt, w_pack, v_pack, pw)


def _glu_kernel(x_ref, y_ref, u_ref, d_ref, w_ref, o_ref):
    n_chunks, nt, dm = x_ref.shape
    g_n = y_ref.shape[0]
    per = nt // GLU_SPLIT
    upd = []
    for s in range(GLU_SPLIT):
        cols = []
        for tt in range(s * per, (s + 1) * per):
            sl = slice(tt * SSM_GROUP_DIM, (tt + 1) * SSM_GROUP_DIM)
            y = y_ref[:, sl, :].astype(f32) + d_ref[...] * u_ref[:, sl, :].astype(f32)
            cols.append(_gelu_tanh(y).reshape(g_n * SSM_GROUP_DIM, n_chunks))
        gt = jnp.concatenate(cols, axis=1)
        r = jnp.dot(gt.T.astype(bf16), w_ref[...], preferred_element_type=f32)
        upd.append(r[:, :dm] * jax.nn.sigmoid(r[:, dm:]))
    upd = jnp.concatenate(upd, axis=0).reshape(nt, n_chunks, dm)
    o_ref[...] = x_ref[...] + pltpu.einshape("tkd->ktd", upd)


def _glu_res(x3, yt, ut, d_skip, w_glu):
    bsz, seqlen, d = x3.shape
    g_n, _, cols = yt.shape
    n_chunks = cols // bsz
    d_tab = jnp.broadcast_to(d_skip.astype(f32).reshape(g_n, SSM_GROUP_DIM, 1), (g_n, SSM_GROUP_DIM, n_chunks))
    feat = pl.BlockSpec((g_n, SUBLANES * SSM_GROUP_DIM, n_chunks), lambda b, t: (0, t, b))
    x5 = _by_chunk(x3)
    out = pl.pallas_call(
        _glu_kernel,
        grid=(bsz, SSM_T // SUBLANES),
        in_specs=[_chunk_spec(x5), feat, feat,
                  pl.BlockSpec(d_tab.shape, lambda b, t: (0, 0, 0)),
                  pl.BlockSpec(w_glu.shape, lambda b, t: (0, 0))],
        out_specs=_chunk_spec(x5),
        out_shape=jax.ShapeDtypeStruct(x5.shape, f32),
        compiler_params=_cparams("parallel", "parallel"),
        name="glu_res",
    )(x5, yt, ut, d_tab, w_glu)
    return out.reshape(bsz, seqlen, d)


def _attn_kernel(tq, tk, out_scale, lam_ref, far_ref, q_ref, k_ref, vt_ref, diag_ref, corner_ref, sub_ref,
                 o_ref, qm_ref, m_ref, acc_ref, s_ref):
    hd2 = 2 * HEAD_DIM
    heads = range(vt_ref.shape[1])
    lane = lax.broadcasted_iota(jnp.int32, (tq, hd2), 1)
    far = [far_ref[pl.program_id(1) * len(heads) + hh] for hh in heads]

    def q_block(qi, carry):
        rows = pl.ds(pl.multiple_of(qi * tq, tq), tq)
        for hh in heads:
            q = q_ref[0, rows, hh * hd2:(hh + 1) * hd2].astype(f32) * (HEAD_DIM ** -0.5 * LOG2E)
            qm_ref[hh, :tq] = jnp.where(lane < HEAD_DIM, q, 0.0).astype(bf16)
            qm_ref[hh, tq:] = jnp.where(lane >= HEAD_DIM, q, 0.0).astype(bf16)
        m_ref[...] = jnp.full_like(m_ref, NEG_INF)
        acc_ref[...] = jnp.zeros_like(acc_ref)

        def fill(slot, ki):
            for hh in heads:
                k = k_ref[0, pl.ds(pl.multiple_of(ki * tk, tk), tk), hh * hd2:(hh + 1) * hd2]
                s_ref[slot, hh] = lax.dot_general(k, qm_ref[hh], (((1,), (1,)), ((), ())),
                                                  preferred_element_type=f32)

        def accumulate(hh, ki, s, shift):
            vt = vt_ref[0, hh, :, pl.ds(pl.multiple_of(ki * tk, tk), tk)]
            m_prev = m_ref[hh]
            m_new = jnp.maximum(m_prev, jnp.max(s, axis=0, keepdims=True) + shift)
            p = jnp.exp2(s - (m_new - shift))
            acc_ref[hh] = (jnp.exp2(m_prev - m_new) * acc_ref[hh]
                           + jnp.dot(vt, p.astype(bf16), preferred_element_type=f32))
            m_ref[hh] = m_new

        def earlier(slot, ki):
            near = jnp.where(ki == qi - 1, 1.0, 0.0)
            for hh in heads:
                s = s_ref[slot, hh]
                c = near * corner_ref[hh]
                lo = s[tk - CORNER:]
                lo = jnp.concatenate([lo[:, :CORNER] + c, lo[:, CORNER:tq], lo[:, tq:tq + CORNER] + c,
                                      lo[:, tq + CORNER:]], axis=1)
                accumulate(hh, ki, jnp.concatenate([s[:tk - CORNER], lo], axis=0), far[hh])

        def pair(i, carry):
            fill(1, 2 * i + 1)
            earlier(0, 2 * i)
            fill(0, 2 * i + 2)
            earlier(1, 2 * i + 1)
            return carry

        fill(0, 0)
        lax.fori_loop(0, qi // 2, pair, 0)

        def diagonal(slot):
            for hh in heads:
                b = diag_ref[hh]
                accumulate(hh, qi, s_ref[slot, hh] + jnp.concatenate([b, b], axis=1), 0.0)

        @pl.when(qi % 2 == 1)
        def _():
            fill(1, qi)
            earlier(0, qi - 1)
            diagonal(1)

        @pl.when(qi % 2 == 0)
        def _():
            diagonal(0)

        for hh in heads:
            acc = acc_ref[hh]
            ot = acc[:hd2] / acc[hd2:hd2 + 1]
            ot = ot[:, :tq] - lam_ref[0] * ot[:, tq:]
            ot = ot * lax.rsqrt(jnp.mean(ot * ot, axis=0, keepdims=True) + EPS)
            o_ref[0, rows, hh * hd2:(hh + 1) * hd2] = (ot.T * sub_ref[...] * out_scale).astype(o_ref.dtype)
        return carry

    lax.fori_loop(0, k_ref.shape[1] // tq, q_block, 0)


def _bias_tables(rel_bias, tq, tk):
    n = jnp.arange(2 * tq)
    nf = jnp.maximum(n, REL_MAX_EXACT).astype(f32)
    large = REL_MAX_EXACT + (jnp.log(nf / REL_MAX_EXACT) / math.log(REL_MAX_DIST / REL_MAX_EXACT)
                             * (REL_BUCKETS - REL_MAX_EXACT)).astype(jnp.int32)
    large = jnp.minimum(large, REL_BUCKETS - 1)
    bucket = jnp.where(n < REL_MAX_EXACT, n, large)
    onehot = (bucket[:, None] == jnp.arange(REL_BUCKETS)[None, :]).astype(f32)
    by_dist = jnp.dot(onehot, rel_bias.astype(f32), precision=lax.Precision.HIGHEST).T * LOG2E
    n_heads = by_dist.shape[0]
    neg = jnp.full((n_heads, tk), NEG_INF, f32)
    period = tq + tk
    tiles = []
    for d, tail in ((0, neg), (1, by_dist[:, tq - tk:tq])):
        r = jnp.concatenate([by_dist[:, d * tq:(d + 1) * tq], tail], axis=1)
        t = jnp.tile(r, (1, tk))[:, :tk * (period - 1)].reshape(n_heads, tk, period - 1)
        tiles.append(t[:, :, :tq])
    far = rel_bias.astype(f32)[REL_BUCKETS - 1] * LOG2E
    corner = tiles[1][:, tk - CORNER:, :CORNER] - far[:, None, None]
    return tiles[0], corner, far


def _far_bucket_from():
    n = REL_MAX_EXACT
    while REL_MAX_EXACT + int(math.log(n / REL_MAX_EXACT) / math.log(REL_MAX_DIST / REL_MAX_EXACT)
                              * (REL_BUCKETS - REL_MAX_EXACT)) < REL_BUCKETS - 1:
        n += 1
    return n


def _diff_attention(q, k, vt, diag, corner, far, lam, subln, out_scale, bsz, seqlen, tq, tk):
    assert tq == tk and tq - CORNER + 1 >= _far_bucket_from()
    hd2 = 2 * HEAD_DIM
    q3 = q.reshape(bsz, seqlen, N_HEADS * hd2)
    k3 = k.reshape(bsz, seqlen, N_HEADS * hd2)
    smem = pl.BlockSpec(memory_space=pltpu.SMEM)
    hps = ATTN_HEADS_PER_STEP
    o = pl.pallas_call(
        functools.partial(_attn_kernel, tq, tk, out_scale),
        grid=(bsz, N_HEADS // hps),
        in_specs=[smem, smem,
                  pl.BlockSpec((1, seqlen, hps * hd2), lambda b, h: (b, 0, h)),
                  pl.BlockSpec((1, seqlen, hps * hd2), lambda b, h: (b, 0, h)),
                  pl.BlockSpec((1, hps, V_ROWS, seqlen), lambda b, h: (b, h, 0, 0)),
                  pl.BlockSpec((hps, tk, tq), lambda b, h: (h, 0, 0)),
                  pl.BlockSpec((hps, CORNER, CORNER), lambda b, h: (h, 0, 0)),
                  pl.BlockSpec((1, hd2), lambda b, h: (0, 0))],
        out_specs=pl.BlockSpec((1, seqlen, hps * hd2), lambda b, h: (b, 0, h)),
        out_shape=jax.ShapeDtypeStruct((bsz, seqlen, N_HEADS * hd2), bf16),
        scratch_shapes=[pltpu.VMEM((hps, 2 * tq, hd2), bf16), pltpu.VMEM((hps, 1, 2 * tq), f32),
                        pltpu.VMEM((hps, V_ROWS, 2 * tq), f32), pltpu.VMEM((2, hps, tk, 2 * tq), f32)],
        compiler_params=_cparams("parallel", "parallel"),
        name="diff_attn",
    )(lam.reshape(1).astype(f32), far, q3, k3, vt, diag, corner, subln.astype(f32)[None, :])
    return o.reshape(bsz * seqlen, N_HEADS * hd2)


def _gelu_tanh(v):
    c = 2.0 * math.sqrt(2.0 / math.pi)
    return v * jax.nn.sigmoid(v * (c + (c * 0.044715) * (v * v)))


def _ffn_ple_kernel(tiles_per_seq, fc, has_attn, has_q, has_kv, *refs):
    refs = list(refs)
    x_ref, p_ref = refs[:2]
    del refs[:2]
    if has_attn:
        a_ref, wo_ref = refs[:2]
        del refs[:2]
    fg_ref, wup_ref, cw_ref, cb_ref, wdn_ref, pg_ref, wgate_ref, wproj_ref = refs[:8]
    del refs[:8]
    if has_q or has_kv:
        seg_ref = refs.pop(0)
    if has_q:
        qg_ref, wq_ref, qgain_ref = refs[:3]
        del refs[:3]
    if has_kv:
        kvg_ref, wk_ref, wvt_ref, kgain_ref = refs[:4]
        del refs[:4]
    o_ref = refs.pop(0)
    if has_q:
        q_ref = refs.pop(0)
    if has_kv:
        k_ref, vt_ref = refs[:2]
        del refs[:2]
    carry_ref, act_ref, h_ref = refs

    i = pl.program_id(0)
    tm = x_ref.shape[0]
    d_ff = wdn_ref.shape[0]
    halo = SUBLANES

    @pl.when(i % tiles_per_seq == 0)
    def _():
        carry_ref[...] = jnp.zeros_like(carry_ref)

    x = x_ref[...]
    if has_attn:
        x = x + jnp.dot(a_ref[...], wo_ref[...], preferred_element_type=f32)
    h_ref[...] = _rms(x, fg_ref[...]).astype(bf16)
    row = lax.broadcasted_iota(jnp.int32, (halo, fc), 0)

    def up_proj(col):
        return jnp.dot(h_ref[...], wup_ref[:, col:col + fc], preferred_element_type=f32)

    def conv(col, up):
        prev = carry_ref[:, col:col + fc]
        carry_ref[:, col:col + fc] = up[tm - halo:]
        c = cb_ref[:, col:col + fc] + cw_ref[2:3, col:col + fc] * up
        for back in range(1, CONV_WIDTH):
            r = pltpu.roll(up, back, 0)
            head = jnp.where(row < back, pltpu.roll(prev, back, 0), r[:halo])
            c = c + cw_ref[2 - back:3 - back, col:col + fc] * jnp.concatenate([head, r[halo:]], axis=0)
        return c

    cols = list(range(0, d_ff, fc))
    bounds = [round(s * len(cols) / DOWN_SPLITS) * fc for s in range(DOWN_SPLITS + 1)]
    pending = None
    ups = (up_proj(cols[0]), up_proj(d_ff + cols[0]))
    for idx, col in enumerate(cols):
        nxt = (up_proj(cols[idx + 1]), up_proj(d_ff + cols[idx + 1])) if idx + 1 < len(cols) else None
        if pending is not None:
            x = x + jnp.dot(act_ref[:, pending[0]:pending[1]], wdn_ref[pending[0]:pending[1], :],
                            preferred_element_type=f32)
            pending = None
        act_ref[:, col:col + fc] = (_gelu_tanh(conv(col, ups[0])) * conv(d_ff + col, ups[1])).astype(bf16)
        if col + fc in bounds:
            pending = (bounds[bounds.index(col + fc) - 1], col + fc)
        ups = nxt
    lo, hi = pending
    halves = [slice(0, tm // 2), slice(tm // 2, tm)]
    xs = [x[r] + jnp.dot(act_ref[r, lo:hi], wdn_ref[lo:hi, :], preferred_element_type=f32) for r in halves]
    projs = [jnp.dot(p_ref[r, :].astype(bf16), wproj_ref[...], preferred_element_type=f32) for r in halves]
    gates = [jax.nn.sigmoid(jnp.dot(_rms(xh, pg_ref[...]).astype(bf16), wgate_ref[...],
                                    preferred_element_type=f32)) for xh in xs]
    xs = [xh + g * pr for xh, g, pr in zip(xs, gates, projs)]
    for r, xh in zip(halves, xs):
        o_ref[r, :] = xh
    if has_q or has_kv:
        xn = jnp.concatenate([xh * lax.rsqrt(jnp.mean(xh * xh, axis=-1, keepdims=True) + EPS) for xh in xs],
                             axis=0)
    if has_q:
        q = jnp.dot((xn * qg_ref[...]).astype(bf16), wq_ref[...], preferred_element_type=f32)
        q_ref[...] = _seg_norm(q, seg_ref, qgain_ref).astype(q_ref.dtype)
    if has_kv:
        hk = (xn * kvg_ref[...]).astype(bf16)
        k = jnp.dot(hk, wk_ref[...], preferred_element_type=f32)
        k_ref[...] = _seg_norm(k, seg_ref, kgain_ref).astype(k_ref.dtype)
        _store_vt(vt_ref, lax.dot_general(wvt_ref[...], hk, (((1,), (1,)), ((), ())),
                                          preferred_element_type=f32))


def _ffn_ple(x, p, layer, ffn, ple, seqlen, attn=None, next_q=None, next_kv=None, tm=512, fc=256):
    n, d = x.shape
    ffn_g, w_up, conv_w, conv_b, w_down = ffn
    ple_g, w_gate, w_proj = ple
    d_ff = w_down.shape[-2]
    bsz = n // seqlen
    tps = seqlen // tm
    row = lambda i: (i, 0)

    def full(a):
        if a.ndim == 3:
            return pl.BlockSpec((None,) + a.shape[1:], lambda i: (layer, 0, 0), pipeline_mode=pl.Buffered(1))
        return pl.BlockSpec(a.shape, lambda i: (0, 0), pipeline_mode=pl.Buffered(1))

    vec = lambda a: a.astype(f32)[None, :]
    args = [x, p]
    in_specs = [pl.BlockSpec((tm, d), row), pl.BlockSpec((None, tm, p.shape[2]), lambda i: (layer, i, 0))]
    if attn is not None:
        args += [attn[0], attn[1]]
        in_specs += [pl.BlockSpec((tm, attn[0].shape[1]), row), full(attn[1])]
    weights = [vec(ffn_g), w_up, conv_w.astype(f32), vec(conv_b), w_down, vec(ple_g), w_gate, w_proj]
    out_specs = [pl.BlockSpec((tm, d), row)]
    out_shape = [jax.ShapeDtypeStruct((n, d), f32)]
    if next_q is not None or next_kv is not None:
        weights.append(_segment_ones())
    if next_q is not None:
        weights += [vec(next_q[0]), next_q[1], _segment_gain(next_q[2])]
        out_specs.append(pl.BlockSpec((tm, next_q[1].shape[1]), row))
        out_shape.append(jax.ShapeDtypeStruct((n, next_q[1].shape[1]), bf16))
    if next_kv is not None:
        weights += [vec(next_kv[0]), next_kv[1], next_kv[2], _segment_gain(next_kv[3])]
        out_specs += [pl.BlockSpec((tm, next_kv[1].shape[1]), row),
                      pl.BlockSpec((1, N_HEADS, V_ROWS, tm), lambda i: (i // tps, 0, 0, i % tps))]
        out_shape += [jax.ShapeDtypeStruct((n, next_kv[1].shape[1]), bf16),
                      jax.ShapeDtypeStruct((bsz, N_HEADS, V_ROWS, seqlen), bf16)]
    args += weights
    in_specs += [full(w) for w in weights]
    return pl.pallas_call(
        functools.partial(_ffn_ple_kernel, tps, fc, attn is not None, next_q is not None, next_kv is not None),
        grid=(n // tm,),
        in_specs=in_specs,
        out_specs=out_specs,
        out_shape=out_shape,
        scratch_shapes=[pltpu.VMEM((SUBLANES, 2 * d_ff), f32),
                        pltpu.VMEM((tm, d_ff), bf16),
                        pltpu.VMEM((tm, d), bf16)],
        compiler_params=_cparams("arbitrary"),
        name="ffn_ple",
    )(*args)


def kernel(x, p, ssm_norm, ssm_w_in, ssm_log_dt, ssm_lambda_re, ssm_lambda_im, ssm_b_re, ssm_b_im, ssm_c_re, ssm_c_im, ssm_d, ssm_w_glu, kv_norm, kv_w, k_norm, attn_norm, attn_w_q, q_norm, lambda_q1, lambda_k1, lambda_q2, lambda_k2, subln, attn_w_o, rel_bias, ffn_norm, ffn_w_up, ffn_conv_w, ffn_conv_b, ffn_w_down, ple_norm, ple_w_gate, ple_w_proj):
    bsz, seqlen, d = x.shape
    depth = p.shape[0]
    n_a = ssm_norm.shape[0]
    n = bsz * seqlen
    x = x.reshape(n, d).astype(f32)
    p = p.reshape(depth, n, p.shape[-1])
    nk = N_HEADS * 2 * HEAD_DIM
    assert 1 <= n_a < depth
    diag, corner, far = _bias_tables(rel_bias, ATTN_BLOCK, ATTN_BLOCK)
    tables = jax.vmap(functools.partial(_ssm_tables, n_chunks=seqlen // SSM_T))(
        ssm_log_dt, ssm_lambda_re, ssm_lambda_im, ssm_b_re, ssm_b_im, ssm_c_re, ssm_c_im)
    w_in_t = jnp.swapaxes(ssm_w_in, 1, 2).astype(bf16)
    w_glu, w_q, w_o = ssm_w_glu.astype(bf16), attn_w_q.astype(bf16), attn_w_o.astype(bf16)
    w_up, w_down = ffn_w_up.astype(bf16), ffn_w_down.astype(bf16)
    w_gate, w_proj = ple_w_gate.astype(bf16), ple_w_proj.astype(bf16)
    lam_all = (jnp.exp(jnp.sum(lambda_q1.astype(f32) * lambda_k1.astype(f32), axis=-1))
               - jnp.exp(jnp.sum(lambda_q2.astype(f32) * lambda_k2.astype(f32), axis=-1)))
    q = k = vt = None
    for i in range(depth):
        attn = None
        if i < n_a:
            x3 = x.reshape(bsz, seqlen, d)
            ut = _win_t(x3, ssm_norm[i], w_in_t[i])
            yt = _ssm(ut, tables, i)
            x = _glu_res(x3, yt, ut, ssm_d[i], w_glu[i]).reshape(n, d)
        else:
            j = i - n_a
            lam_init = 0.8 - 0.6 * math.exp(-0.3 * i)
            o = _diff_attention(q, k, vt, diag, corner, far, lam_all[j] + lam_init, subln[j], 1.0 - lam_init,
                                bsz, seqlen, ATTN_BLOCK, ATTN_BLOCK)
            attn = (o, w_o[j])
        j_next = i + 1 - n_a
        next_q = next_kv = None
        if 0 <= j_next < depth - n_a:
            next_q = (attn_norm[j_next], w_q[j_next], q_norm[j_next])
            if j_next == 0:
                next_kv = (kv_norm, kv_w[:, :nk].astype(bf16), kv_w[:, nk:].T.astype(bf16), k_norm)
        outs = _ffn_ple(x, p, i,
                        (ffn_norm[i], w_up, ffn_conv_w[i], ffn_conv_b[i], w_down),
                        (ple_norm[i], w_gate, w_proj),
                        seqlen, attn=attn, next_q=next_q, next_kv=next_kv)
        x = outs[0]
        if next_q is not None:
            q = outs[1]
        if next_kv is not None:
            k, vt = outs[2], outs[3]
    return x.reshape(bsz, seqlen, d)
```

```python
import functools
import math

import jax
import jax.numpy as jnp
from jax import lax
from jax.experimental import pallas as pl
from jax.experimental.pallas import tpu as pltpu

EPS = 1e-6
NEG_INF = -1e30

SSM_GROUP_DIM = 16
SSM_STATE = 64
N_HEADS = 8
HEAD_DIM = 64
REL_BUCKETS = 32
REL_MAX_EXACT = REL_BUCKETS // 2
REL_MAX_DIST = 128
CONV_WIDTH = 3

LANES = 128
SUBLANES = 8
MXU_DIM = 256
VMEM_LIMIT_BYTES = 56 * 1024 * 1024

SSM_T = MXU_DIM // SSM_GROUP_DIM
SEG_CHUNK = MXU_DIM
ATTN_BLOCK = 256
ATTN_HEADS_PER_STEP = 8
CORNER = LANES
V_ROWS = 2 * HEAD_DIM + 16
LOG2E = math.log2(math.e)
SSM_PAIRS_PER_STEP = 1
WIN_SPLIT = 2
GLU_SPLIT = 4
DOWN_SPLITS = 1

bf16 = jnp.bfloat16
f32 = jnp.float32


def _cparams(*sem):
    return pltpu.CompilerParams(dimension_semantics=sem, vmem_limit_bytes=VMEM_LIMIT_BYTES)


def _rms(xf, g):
    return xf * lax.rsqrt(jnp.mean(xf * xf, axis=-1, keepdims=True) + EPS) * g


def _seg_norm(y, seg_ref, gain_ref):
    cols = []
    for c in range(y.shape[1] // SEG_CHUNK):
        yc = y[:, c * SEG_CHUNK:(c + 1) * SEG_CHUNK]
        ss = jnp.dot((yc * yc).astype(bf16), seg_ref[...], preferred_element_type=f32)
        cols.append(yc * lax.rsqrt(ss * (1.0 / HEAD_DIM) + EPS) * gain_ref[...])
    return jnp.concatenate(cols, axis=1)


def _segment_ones():
    idx = jnp.arange(SEG_CHUNK) // HEAD_DIM
    return (idx[:, None] == idx[None, :]).astype(bf16)


def _segment_gain(g):
    return jnp.tile(g.astype(f32), SEG_CHUNK // HEAD_DIM)[None, :]


def _store_vt(vt_ref, vt):
    hd2 = 2 * HEAD_DIM
    for hh in range(vt_ref.shape[1]):
        vt_ref[0, hh, :hd2, :] = vt[hh * hd2:(hh + 1) * hd2].astype(vt_ref.dtype)
        vt_ref[0, hh, hd2:, :] = jnp.ones((V_ROWS - hd2, vt.shape[1]), vt_ref.dtype)


def _win_kernel(x_ref, g_ref, wt_ref, ut_ref):
    n_chunks, nt, d = x_ref.shape
    g_n = ut_ref.shape[0]
    xt = pltpu.einshape("ktd->tkd", x_ref[...])
    per = nt // WIN_SPLIT

    def normed(s):
        rows = xt[s * per:(s + 1) * per].reshape(per * n_chunks, d)
        return _rms(rows, g_ref[...]).astype(bf16)

    h = normed(0)
    for s in range(WIN_SPLIT):
        ut = lax.dot_general(wt_ref[...], h, (((1,), (1,)), ((), ())), preferred_element_type=f32)
        if s + 1 < WIN_SPLIT:
            h = normed(s + 1)
        for tt in range(per):
            blk = ut[:, tt * n_chunks:(tt + 1) * n_chunks].reshape(g_n, SSM_GROUP_DIM, n_chunks)
            r0 = (s * per + tt) * SSM_GROUP_DIM
            ut_ref[:, r0:r0 + SSM_GROUP_DIM, :] = blk.astype(ut_ref.dtype)


def _by_chunk(x3):
    bsz, seqlen, d = x3.shape
    return x3.reshape(bsz, seqlen // SSM_T, SSM_T // SUBLANES, SUBLANES, d)


def _chunk_spec(x5):
    _, n_chunks, _, nt, d = x5.shape
    return pl.BlockSpec((None, n_chunks, None, nt, d), lambda b, t: (b, 0, t, 0, 0))


def _win_t(x3, g, w_t):
    bsz, seqlen, d = x3.shape
    n_chunks = seqlen // SSM_T
    assert n_chunks == LANES
    g_n = d // SSM_GROUP_DIM
    x5 = _by_chunk(x3)
    return pl.pallas_call(
        _win_kernel,
        grid=(bsz, SSM_T // SUBLANES),
        in_specs=[_chunk_spec(x5),
                  pl.BlockSpec((1, d), lambda b, t: (0, 0)),
                  pl.BlockSpec(w_t.shape, lambda b, t: (0, 0))],
        out_specs=pl.BlockSpec((g_n, SUBLANES * SSM_GROUP_DIM, n_chunks), lambda b, t: (0, t, b)),
        out_shape=jax.ShapeDtypeStruct((g_n, SSM_T * SSM_GROUP_DIM, bsz * n_chunks), bf16),
        compiler_params=_cparams("parallel", "parallel"),
        name="win_t",
    )(x5, g.astype(f32)[None, :], w_t)


def _ssm_kernel(n_rounds, u_ref, mt_ref, w_ref, v_ref, pw_ref, y_ref, xin_ref):
    for pp in range(mt_ref.shape[0]):
        _ssm_pair(n_rounds, u_ref.at[2 * pp:2 * pp + 2], mt_ref.at[pp], w_ref.at[pp], v_ref.at[pp],
                  pw_ref.at[pp], y_ref.at[2 * pp:2 * pp + 2], xin_ref.at[pp])


def _ssm_pair(n_rounds, u_ref, mt_ref, w_ref, v_ref, pw_ref, y_ref, xin_ref):
    u0 = u_ref[0]
    u1 = u_ref[1]
    half = LANES
    seg = LANES
    st = (jnp.dot(w_ref[0], u0, preferred_element_type=f32)
          + jnp.dot(w_ref[1], u1, preferred_element_type=f32))
    n_seq = st.shape[1] // seg
    xr = jnp.stack([st[:half, b * seg:(b + 1) * seg] for b in range(n_seq)])
    xi = jnp.stack([st[half:, b * seg:(b + 1) * seg] for b in range(n_seq)])
    chunk = lax.broadcasted_iota(jnp.int32, xr.shape, 2)
    for j in range(n_rounds):
        d = 1 << j
        ar = pw_ref[j, 0]
        ai = pw_ref[j, 1]
        sr = pltpu.roll(xr, d, 2)
        si = pltpu.roll(xi, d, 2)
        xr, xi = xr + (ar * sr - ai * si), xi + (ar * si + ai * sr)
    xr = jnp.where(chunk >= 1, pltpu.roll(xr, 1, 2), 0.0).astype(bf16)
    xi = jnp.where(chunk >= 1, pltpu.roll(xi, 1, 2), 0.0).astype(bf16)
    for b in range(n_seq):
        xin_ref[:half, b * seg:(b + 1) * seg] = xr[b]
        xin_ref[half:, b * seg:(b + 1) * seg] = xi[b]
    ys = jnp.dot(v_ref[...], xin_ref[...], preferred_element_type=f32)
    y_ref[0] = (jnp.dot(mt_ref[0], u0, preferred_element_type=f32) + ys[:MXU_DIM]).astype(y_ref.dtype)
    y_ref[1] = (jnp.dot(mt_ref[1], u1, preferred_element_type=f32) + ys[MXU_DIM:]).astype(y_ref.dtype)


def _ssm_tables(log_dt, lam_re, lam_im, b_re, b_im, c_re, c_im, n_chunks):
    g_n, p_n = lam_re.shape
    t_n, c_n = SSM_T, SSM_GROUP_DIM
    dt = jnp.exp(log_dt.astype(f32))[:, None]
    lr = lam_re.astype(f32)
    li = lam_im.astype(f32)

    def power(n):
        nn = jnp.asarray(n, f32).reshape((-1, 1, 1))
        mag = jnp.exp(lr * dt * nn)
        return mag * jnp.cos(li * dt * nn), mag * jnp.sin(li * dt * nn)

    ab_r, ab_i = power([1])
    ab_r, ab_i = ab_r[0], ab_i[0]
    den = lr * lr + li * li
    f_r = ((ab_r - 1.0) * lr + ab_i * li) / den
    f_i = (ab_i * lr - (ab_r - 1.0) * li) / den
    br = b_re.astype(f32)
    bi = b_im.astype(f32)
    bb_r = f_r[..., None] * br - f_i[..., None] * bi
    bb_i = f_r[..., None] * bi + f_i[..., None] * br
    cr = c_re.astype(f32)
    ci = c_im.astype(f32)

    qr, qi = power(t_n - 1 - jnp.arange(t_n))
    t_r = qr[..., None] * bb_r[None] - qi[..., None] * bb_i[None]
    t_i = qr[..., None] * bb_i[None] + qi[..., None] * bb_r[None]
    kern = (jnp.einsum("gcp,dgpk->gcdk", cr, t_r, precision=lax.Precision.HIGHEST)
            - jnp.einsum("gcp,dgpk->gcdk", ci, t_i, precision=lax.Precision.HIGHEST))
    kern = kern.reshape(g_n, c_n, t_n * c_n)
    kern = jnp.concatenate([kern, jnp.zeros_like(kern)], axis=-1)
    mt = jnp.concatenate([kern[:, :, (t_n - 1 - t) * c_n:(2 * t_n - 1 - t) * c_n] for t in range(t_n)], axis=1)
    w_r = t_r.transpose(1, 2, 0, 3).reshape(g_n, p_n, t_n * c_n)
    w_i = t_i.transpose(1, 2, 0, 3).reshape(g_n, p_n, t_n * c_n)
    pr1, pi1 = power(1 + jnp.arange(t_n))
    pr1 = pr1.transpose(1, 0, 2)
    pi1 = pi1.transpose(1, 0, 2)
    v_r = (cr[:, None] * pr1[:, :, None, :] - ci[:, None] * pi1[:, :, None, :]).reshape(g_n, t_n * c_n, p_n)
    v_i = -(cr[:, None] * pi1[:, :, None, :] + ci[:, None] * pr1[:, :, None, :]).reshape(g_n, t_n * c_n, p_n)

    gp = g_n // 2
    z = jnp.zeros((gp, p_n, t_n * c_n), f32)
    w_r = w_r.reshape(gp, 2, p_n, t_n * c_n)
    w_i = w_i.reshape(gp, 2, p_n, t_n * c_n)
    w0 = jnp.concatenate([w_r[:, 0], z, w_i[:, 0], z], axis=1)
    w1 = jnp.concatenate([z, w_r[:, 1], z, w_i[:, 1]], axis=1)
    w_pack = jnp.stack([w0, w1], axis=1)
    zv = jnp.zeros((gp, t_n * c_n, p_n), f32)
    v_r = v_r.reshape(gp, 2, t_n * c_n, p_n)
    v_i = v_i.reshape(gp, 2, t_n * c_n, p_n)
    v_pack = jnp.concatenate([
        jnp.concatenate([v_r[:, 0], zv, v_i[:, 0], zv], axis=-1),
        jnp.concatenate([zv, v_r[:, 1], zv, v_i[:, 1]], axis=-1)], axis=1)
    n_rounds = (n_chunks - 1).bit_length()
    qr, qi = power(t_n * (2 ** jnp.arange(max(n_rounds, 1))))
    pw = jnp.stack([qr.reshape(-1, gp, 2 * p_n), qi.reshape(-1, gp, 2 * p_n)], axis=2)
    reach = (jnp.arange(LANES)[None, :] >= (2 ** jnp.arange(pw.shape[0]))[:, None]).astype(f32)
    pw = pw.transpose(1, 0, 2, 3)[..., None] * reach[None, :, None, None, :]
    return mt.reshape(gp, 2, t_n * c_n, t_n * c_n).astype(bf16), w_pack.astype(bf16), v_pack.astype(bf16), pw


def _ssm(ut, tables, layer):
    mt, w_pack, v_pack, pw = tables
    n_rounds = pw.shape[2]
    g_n, k, cols = ut.shape
    gp = g_n // 2
    pp = SSM_PAIRS_PER_STEP
    return pl.pallas_call(
        functools.partial(_ssm_kernel, n_rounds),
        grid=(gp // pp,),
        in_specs=[pl.BlockSpec((2 * pp, k, cols), lambda g: (g, 0, 0)),
                  pl.BlockSpec((None, pp, 2, k, k), lambda g: (layer, g, 0, 0, 0)),
                  pl.BlockSpec((None, pp, 2, k, k), lambda g: (layer, g, 0, 0, 0)),
                  pl.BlockSpec((None, pp, 2 * k, k), lambda g: (layer, g, 0, 0)),
                  pl.BlockSpec((None, pp) + pw.shape[2:], lambda g: (layer, g, 0, 0, 0, 0))],
        out_specs=pl.BlockSpec((2 * pp, k, cols), lambda g: (g, 0, 0)),
        out_shape=jax.ShapeDtypeStruct((g_n, k, cols), bf16),
        scratch_shapes=[pltpu.VMEM((pp, k, cols), bf16)],
        compiler_params=_cparams("parallel"),
        name="ssm",
    )(ut, mt, w_pack, v_pack, pw)


def _glu_kernel(x_ref, y_ref, u_ref, d_ref, w_ref, o_ref):
    n_chunks, nt, dm = x_ref.shape
    g_n = y_ref.shape[0]
    per = nt // GLU_SPLIT

    def activations(s):
        cols = []
        for tt in range(s * per, (s + 1) * per):
            sl = slice(tt * SSM_GROUP_DIM, (tt + 1) * SSM_GROUP_DIM)
            y = y_ref[:, sl, :].astype(f32) + d_ref[...] * u_ref[:, sl, :].astype(f32)
            cols.append(_gelu_tanh(y).reshape(g_n * SSM_GROUP_DIM, n_chunks))
        return jnp.concatenate(cols, axis=1).T.astype(bf16)

    upd = []
    g = activations(0)
    for s in range(GLU_SPLIT):
        r = jnp.dot(g, w_ref[...], preferred_element_type=f32)
        if s + 1 < GLU_SPLIT:
            g = activations(s + 1)
        upd.append(r[:, :dm] * jax.nn.sigmoid(r[:, dm:]))
    upd = jnp.concatenate(upd, axis=0).reshape(nt, n_chunks, dm)
    o_ref[...] = x_ref[...] + pltpu.einshape("tkd->ktd", upd)


def _glu_res(x3, yt, ut, d_skip, w_glu):
    bsz, seqlen, d = x3.shape
    g_n, _, cols = yt.shape
    n_chunks = cols // bsz
    d_tab = jnp.broadcast_to(d_skip.astype(f32).reshape(g_n, SSM_GROUP_DIM, 1), (g_n, SSM_GROUP_DIM, n_chunks))
    feat = pl.BlockSpec((g_n, SUBLANES * SSM_GROUP_DIM, n_chunks), lambda b, t: (0, t, b))
    x5 = _by_chunk(x3)
    out = pl.pallas_call(
        _glu_kernel,
        grid=(bsz, SSM_T // SUBLANES),
        in_specs=[_chunk_spec(x5), feat, feat,
                  pl.BlockSpec(d_tab.shape, lambda b, t: (0, 0, 0)),
                  pl.BlockSpec(w_glu.shape, lambda b, t: (0, 0))],
        out_specs=_chunk_spec(x5),
        out_shape=jax.ShapeDtypeStruct(x5.shape, f32),
        compiler_params=_cparams("parallel", "parallel"),
        name="glu_res",
    )(x5, yt, ut, d_tab, w_glu)
    return out.reshape(bsz, seqlen, d)


def _attn_kernel(tq, tk, out_scale, lam_ref, far_ref, q_ref, k_ref, vt_ref, diag_ref, corner_ref, sub_ref,
                 o_ref, qm_ref, m_ref, acc_ref, s_ref):
    hd2 = 2 * HEAD_DIM
    heads = range(vt_ref.shape[1])
    lane = lax.broadcasted_iota(jnp.int32, (tq, hd2), 1)
    far = [far_ref[pl.program_id(1) * len(heads) + hh] for hh in heads]

    def q_block(qi, carry):
        rows = pl.ds(pl.multiple_of(qi * tq, tq), tq)
        for hh in heads:
            q = q_ref[0, rows, hh * hd2:(hh + 1) * hd2].astype(f32) * (HEAD_DIM ** -0.5 * LOG2E)
            qm_ref[hh, :, :tq] = jnp.where(lane < HEAD_DIM, q, 0.0).T.astype(bf16)
            qm_ref[hh, :, tq:] = jnp.where(lane >= HEAD_DIM, q, 0.0).T.astype(bf16)
        m_ref[...] = jnp.full_like(m_ref, NEG_INF)
        acc_ref[...] = jnp.zeros_like(acc_ref)

        def fill(slot, ki):
            for hh in heads:
                k = k_ref[0, pl.ds(pl.multiple_of(ki * tk, tk), tk), hh * hd2:(hh + 1) * hd2]
                s_ref[slot, hh] = jnp.dot(k, qm_ref[hh], preferred_element_type=f32)

        def accumulate(hh, ki, s, shift):
            vt = vt_ref[0, hh, :, pl.ds(pl.multiple_of(ki * tk, tk), tk)]
            m_prev = m_ref[hh]
            m_new = jnp.maximum(m_prev, jnp.max(s, axis=0, keepdims=True) + shift)
            p = jnp.exp2(s - (m_new - shift))
            acc_ref[hh] = (jnp.exp2(m_prev - m_new) * acc_ref[hh]
                           + jnp.dot(vt, p.astype(bf16), preferred_element_type=f32))
            m_ref[hh] = m_new

        def earlier(slot, ki):
            near = jnp.where(ki == qi - 1, 1.0, 0.0)
            for hh in heads:
                s = s_ref[slot, hh]
                c = near * corner_ref[hh]
                lo = s[tk - CORNER:]
                lo = jnp.concatenate([lo[:, :CORNER] + c, lo[:, CORNER:tq], lo[:, tq:tq + CORNER] + c,
                                      lo[:, tq + CORNER:]], axis=1)
                accumulate(hh, ki, jnp.concatenate([s[:tk - CORNER], lo], axis=0), far[hh])

        def pair(i, carry):
            fill(1, 2 * i + 1)
            earlier(0, 2 * i)
            fill(0, 2 * i + 2)
            earlier(1, 2 * i + 1)
            return carry

        fill(0, 0)
        lax.fori_loop(0, qi // 2, pair, 0)

        def diagonal(slot):
            for hh in heads:
                b = diag_ref[hh]
                accumulate(hh, qi, s_ref[slot, hh] + jnp.concatenate([b, b], axis=1), 0.0)

        @pl.when(qi % 2 == 1)
        def _():
            fill(1, qi)
            earlier(0, qi - 1)
            diagonal(1)

        @pl.when(qi % 2 == 0)
        def _():
            diagonal(0)

        for hh in heads:
            acc = acc_ref[hh]
            ot = acc[:hd2] / acc[hd2:hd2 + 1]
            ot = ot[:, :tq] - lam_ref[0] * ot[:, tq:]
            ot = ot * lax.rsqrt(jnp.mean(ot * ot, axis=0, keepdims=True) + EPS)
            o_ref[0, rows, hh * hd2:(hh + 1) * hd2] = (ot.T * sub_ref[...] * out_scale).astype(o_ref.dtype)
        return carry

    lax.fori_loop(0, k_ref.shape[1] // tq, q_block, 0)


def _bias_tables(rel_bias, tq, tk):
    n = jnp.arange(2 * tq)
    nf = jnp.maximum(n, REL_MAX_EXACT).astype(f32)
    large = REL_MAX_EXACT + (jnp.log(nf / REL_MAX_EXACT) / math.log(REL_MAX_DIST / REL_MAX_EXACT)
                             * (REL_BUCKETS - REL_MAX_EXACT)).astype(jnp.int32)
    large = jnp.minimum(large, REL_BUCKETS - 1)
    bucket = jnp.where(n < REL_MAX_EXACT, n, large)
    onehot = (bucket[:, None] == jnp.arange(REL_BUCKETS)[None, :]).astype(f32)
    by_dist = jnp.dot(onehot, rel_bias.astype(f32), precision=lax.Precision.HIGHEST).T * LOG2E
    n_heads = by_dist.shape[0]
    neg = jnp.full((n_heads, tk), NEG_INF, f32)
    period = tq + tk
    tiles = []
    for d, tail in ((0, neg), (1, by_dist[:, tq - tk:tq])):
        r = jnp.concatenate([by_dist[:, d * tq:(d + 1) * tq], tail], axis=1)
        t = jnp.tile(r, (1, tk))[:, :tk * (period - 1)].reshape(n_heads, tk, period - 1)
        tiles.append(t[:, :, :tq])
    far = rel_bias.astype(f32)[REL_BUCKETS - 1] * LOG2E
    corner = tiles[1][:, tk - CORNER:, :CORNER] - far[:, None, None]
    return tiles[0], corner, far


def _far_bucket_from():
    n = REL_MAX_EXACT
    while REL_MAX_EXACT + int(math.log(n / REL_MAX_EXACT) / math.log(REL_MAX_DIST / REL_MAX_EXACT)
                              * (REL_BUCKETS - REL_MAX_EXACT)) < REL_BUCKETS - 1:
        n += 1
    return n


def _diff_attention(q, k, vt, diag, corner, far, lam, subln, out_scale, bsz, seqlen, tq, tk):
    assert tq == tk and tq - CORNER + 1 >= _far_bucket_from()
    hd2 = 2 * HEAD_DIM
    q3 = q.reshape(bsz, seqlen, N_HEADS * hd2)
    k3 = k.reshape(bsz, seqlen, N_HEADS * hd2)
    smem = pl.BlockSpec(memory_space=pltpu.SMEM)
    hps = ATTN_HEADS_PER_STEP
    o = pl.pallas_call(
        functools.partial(_attn_kernel, tq, tk, out_scale),
        grid=(bsz, N_HEADS // hps),
        in_specs=[smem, smem,
                  pl.BlockSpec((1, seqlen, hps * hd2), lambda b, h: (b, 0, h)),
                  pl.BlockSpec((1, seqlen, hps * hd2), lambda b, h: (b, 0, h)),
                  pl.BlockSpec((1, hps, V_ROWS, seqlen), lambda b, h: (b, h, 0, 0)),
                  pl.BlockSpec((hps, tk, tq), lambda b, h: (h, 0, 0)),
                  pl.BlockSpec((hps, CORNER, CORNER), lambda b, h: (h, 0, 0)),
                  pl.BlockSpec((1, hd2), lambda b, h: (0, 0))],
        out_specs=pl.BlockSpec((1, seqlen, hps * hd2), lambda b, h: (b, 0, h)),
        out_shape=jax.ShapeDtypeStruct((bsz, seqlen, N_HEADS * hd2), bf16),
        scratch_shapes=[pltpu.VMEM((hps, hd2, 2 * tq), bf16), pltpu.VMEM((hps, 1, 2 * tq), f32),
                        pltpu.VMEM((hps, V_ROWS, 2 * tq), f32), pltpu.VMEM((2, hps, tk, 2 * tq), f32)],
        compiler_params=_cparams("parallel", "parallel"),
        name="diff_attn",
    )(lam.reshape(1).astype(f32), far, q3, k3, vt, diag, corner, subln.astype(f32)[None, :])
    return o.reshape(bsz * seqlen, N_HEADS * hd2)


def _gelu_tanh(v):
    c = 2.0 * math.sqrt(2.0 / math.pi)
    return v * jax.nn.sigmoid(v * (c + (c * 0.044715) * (v * v)))


def _ffn_ple_kernel(tiles_per_seq, fc, has_attn, has_q, has_kv, *refs):
    refs = list(refs)
    x_ref, p_ref = refs[:2]
    del refs[:2]
    if has_attn:
        a_ref, wo_ref = refs[:2]
        del refs[:2]
    fg_ref, wup_ref, cw_ref, cb_ref, wdn_ref, pg_ref, wgate_ref, wproj_ref = refs[:8]
    del refs[:8]
    if has_q or has_kv:
        seg_ref = refs.pop(0)
    if has_q:
        qg_ref, wq_ref, qgain_ref = refs[:3]
        del refs[:3]
    if has_kv:
        kvg_ref, wk_ref, wvt_ref, kgain_ref = refs[:4]
        del refs[:4]
    o_ref = refs.pop(0)
    if has_q:
        q_ref = refs.pop(0)
    if has_kv:
        k_ref, vt_ref = refs[:2]
        del refs[:2]
    carry_ref, act_ref, h_ref = refs

    i = pl.program_id(0)
    tm = x_ref.shape[0]
    d_ff = wdn_ref.shape[0]
    halo = SUBLANES

    @pl.when(i % tiles_per_seq == 0)
    def _():
        carry_ref[...] = jnp.zeros_like(carry_ref)

    x = x_ref[...]
    if has_attn:
        x = x + jnp.dot(a_ref[...], wo_ref[...], preferred_element_type=f32)
    h_ref[...] = _rms(x, fg_ref[...]).astype(bf16)
    row = lax.broadcasted_iota(jnp.int32, (halo, fc), 0)

    def up_proj(col):
        return jnp.dot(h_ref[...], wup_ref[:, col:col + fc], preferred_element_type=f32)

    def conv(col, up):
        prev = carry_ref[:, col:col + fc]
        carry_ref[:, col:col + fc] = up[tm - halo:]
        c = cb_ref[:, col:col + fc] + cw_ref[2:3, col:col + fc] * up
        for back in range(1, CONV_WIDTH):
            r = pltpu.roll(up, back, 0)
            head = jnp.where(row < back, pltpu.roll(prev, back, 0), r[:halo])
            c = c + cw_ref[2 - back:3 - back, col:col + fc] * jnp.concatenate([head, r[halo:]], axis=0)
        return c

    cols = list(range(0, d_ff, fc))
    bounds = [round(s * len(cols) / DOWN_SPLITS) * fc for s in range(DOWN_SPLITS + 1)]
    pending = None
    ups = (up_proj(cols[0]), up_proj(d_ff + cols[0]))
    for idx, col in enumerate(cols):
        nxt = (up_proj(cols[idx + 1]), up_proj(d_ff + cols[idx + 1])) if idx + 1 < len(cols) else None
        if pending is not None:
            x = x + jnp.dot(act_ref[:, pending[0]:pending[1]], wdn_ref[pending[0]:pending[1], :],
                            preferred_element_type=f32)
            pending = None
        act_ref[:, col:col + fc] = (_gelu_tanh(conv(col, ups[0])) * conv(d_ff + col, ups[1])).astype(bf16)
        if col + fc in bounds:
            pending = (bounds[bounds.index(col + fc) - 1], col + fc)
        ups = nxt
    lo, hi = pending
    halves = [slice(0, tm // 2), slice(tm // 2, tm)]
    xs = [x[r] + jnp.dot(act_ref[r, lo:hi], wdn_ref[lo:hi, :], preferred_element_type=f32) for r in halves]
    projs = [jnp.dot(p_ref[r, :].astype(bf16), wproj_ref[...], preferred_element_type=f32) for r in halves]
    gates = [jax.nn.sigmoid(jnp.dot(_rms(xh, pg_ref[...]).astype(bf16), wgate_ref[...],
                                    preferred_element_type=f32)) for xh in xs]
    xs = [xh + g * pr for xh, g, pr in zip(xs, gates, projs)]
    for r, xh in zip(halves, xs):
        o_ref[r, :] = xh
    if has_q or has_kv:
        xn = jnp.concatenate([xh * lax.rsqrt(jnp.mean(xh * xh, axis=-1, keepdims=True) + EPS) for xh in xs],
                             axis=0)
    if has_q:
        q = jnp.dot((xn * qg_ref[...]).astype(bf16), wq_ref[...], preferred_element_type=f32)
        q_ref[...] = _seg_norm(q, seg_ref, qgain_ref).astype(q_ref.dtype)
    if has_kv:
        hk = (xn * kvg_ref[...]).astype(bf16)
        k = jnp.dot(hk, wk_ref[...], preferred_element_type=f32)
        k_ref[...] = _seg_norm(k, seg_ref, kgain_ref).astype(k_ref.dtype)
        _store_vt(vt_ref, lax.dot_general(wvt_ref[...], hk, (((1,), (1,)), ((), ())),
                                          preferred_element_type=f32))


def _ffn_ple(x, p, layer, ffn, ple, seqlen, attn=None, next_q=None, next_kv=None, tm=512, fc=256):
    n, d = x.shape
    ffn_g, w_up, conv_w, conv_b, w_down = ffn
    ple_g, w_gate, w_proj = ple
    d_ff = w_down.shape[-2]
    bsz = n // seqlen
    tps = seqlen // tm
    row = lambda i: (i, 0)

    def full(a):
        if a.ndim == 3:
            return pl.BlockSpec((None,) + a.shape[1:], lambda i: (layer, 0, 0), pipeline_mode=pl.Buffered(1))
        return pl.BlockSpec(a.shape, lambda i: (0, 0), pipeline_mode=pl.Buffered(1))

    vec = lambda a: a.astype(f32)[None, :]
    args = [x, p]
    in_specs = [pl.BlockSpec((tm, d), row), pl.BlockSpec((None, tm, p.shape[2]), lambda i: (layer, i, 0))]
    if attn is not None:
        args += [attn[0], attn[1]]
        in_specs += [pl.BlockSpec((tm, attn[0].shape[1]), row), full(attn[1])]
    weights = [vec(ffn_g), w_up, conv_w.astype(f32), vec(conv_b), w_down, vec(ple_g), w_gate, w_proj]
    out_specs = [pl.BlockSpec((tm, d), row)]
    out_shape = [jax.ShapeDtypeStruct((n, d), f32)]
    if next_q is not None or next_kv is not None:
        weights.append(_segment_ones())
    if next_q is not None:
        weights += [vec(next_q[0]), next_q[1], _segment_gain(next_q[2])]
        out_specs.append(pl.BlockSpec((tm, next_q[1].shape[1]), row))
        out_shape.append(jax.ShapeDtypeStruct((n, next_q[1].shape[1]), bf16))
    if next_kv is not None:
        weights += [vec(next_kv[0]), next_kv[1], next_kv[2], _segment_gain(next_kv[3])]
        out_specs += [pl.BlockSpec((tm, next_kv[1].shape[1]), row),
                      pl.BlockSpec((1, N_HEADS, V_ROWS, tm), lambda i: (i // tps, 0, 0, i % tps))]
        out_shape += [jax.ShapeDtypeStruct((n, next_kv[1].shape[1]), bf16),
                      jax.ShapeDtypeStruct((bsz, N_HEADS, V_ROWS, seqlen), bf16)]
    args += weights
    in_specs += [full(w) for w in weights]
    return pl.pallas_call(
        functools.partial(_ffn_ple_kernel, tps, fc, attn is not None, next_q is not None, next_kv is not None),
        grid=(n // tm,),
        in_specs=in_specs,
        out_specs=out_specs,
        out_shape=out_shape,
        scratch_shapes=[pltpu.VMEM((SUBLANES, 2 * d_ff), f32),
                        pltpu.VMEM((tm, d_ff), bf16),
                        pltpu.VMEM((tm, d), bf16)],
        compiler_params=_cparams("arbitrary"),
        name="ffn_ple",
    )(*args)


def kernel(x, p, ssm_norm, ssm_w_in, ssm_log_dt, ssm_lambda_re, ssm_lambda_im, ssm_b_re, ssm_b_im, ssm_c_re, ssm_c_im, ssm_d, ssm_w_glu, kv_norm, kv_w, k_norm, attn_norm, attn_w_q, q_norm, lambda_q1, lambda_k1, lambda_q2, lambda_k2, subln, attn_w_o, rel_bias, ffn_norm, ffn_w_up, ffn_conv_w, ffn_conv_b, ffn_w_down, ple_norm, ple_w_gate, ple_w_proj):
    bsz, seqlen, d = x.shape
    depth = p.shape[0]
    n_a = ssm_norm.shape[0]
    n = bsz * seqlen
    x = x.reshape(n, d).astype(f32)
    p = p.reshape(depth, n, p.shape[-1])
    nk = N_HEADS * 2 * HEAD_DIM
    assert 1 <= n_a < depth
    diag, corner, far = _bias_tables(rel_bias, ATTN_BLOCK, ATTN_BLOCK)
    tables = jax.vmap(functools.partial(_ssm_tables, n_chunks=seqlen // SSM_T))(
        ssm_log_dt, ssm_lambda_re, ssm_lambda_im, ssm_b_re, ssm_b_im, ssm_c_re, ssm_c_im)
    w_in_t = jnp.swapaxes(ssm_w_in, 1, 2).astype(bf16)
    w_glu, w_q, w_o = ssm_w_glu.astype(bf16), attn_w_q.astype(bf16), attn_w_o.astype(bf16)
    w_up, w_down = ffn_w_up.astype(bf16), ffn_w_down.astype(bf16)
    w_gate, w_proj = ple_w_gate.astype(bf16), ple_w_proj.astype(bf16)
    lam_all = (jnp.exp(jnp.sum(lambda_q1.astype(f32) * lambda_k1.astype(f32), axis=-1))
               - jnp.exp(jnp.sum(lambda_q2.astype(f32) * lambda_k2.astype(f32), axis=-1)))
    q = k = vt = None
    for i in range(depth):
        attn = None
        if i < n_a:
            x3 = x.reshape(bsz, seqlen, d)
            ut = _win_t(x3, ssm_norm[i], w_in_t[i])
            yt = _ssm(ut, tables, i)
            x = _glu_res(x3, yt, ut, ssm_d[i], w_glu[i]).reshape(n, d)
        else:
            j = i - n_a
            lam_init = 0.8 - 0.6 * math.exp(-0.3 * i)
            o = _diff_attention(q, k, vt, diag, corner, far, lam_all[j] + lam_init, subln[j], 1.0 - lam_init,
                                bsz, seqlen, ATTN_BLOCK, ATTN_BLOCK)
            attn = (o, w_o[j])
        j_next = i + 1 - n_a
        next_q = next_kv = None
        if 0 <= j_next < depth - n_a:
            next_q = (attn_norm[j_next], w_q[j_next], q_norm[j_next])
            if j_next == 0:
                next_kv = (kv_norm, kv_w[:, :nk].astype(bf16), kv_w[:, nk:].T.astype(bf16), k_norm)
        outs = _ffn_ple(x, p, i,
                        (ffn_norm[i], w_up, ffn_conv_w[i], ffn_conv_b[i], w_down),
                        (ple_norm[i], w_gate, w_proj),
                        seqlen, attn=attn, next_q=next_q, next_kv=next_kv)
        x = outs[0]
        if next_q is not None:
            q = outs[1]
        if next_kv is not None:
            k, vt = outs[2], outs[3]
    return x.reshape(bsz, seqlen, d)
```

```python
import functools
import math

import jax
import jax.numpy as jnp
from jax import lax
from jax.experimental import pallas as pl
from jax.experimental.pallas import tpu as pltpu

EPS = 1e-6
NEG_INF = -1e30

SSM_GROUP_DIM = 16
N_HEADS = 8
HEAD_DIM = 64
REL_BUCKETS = 32
REL_MAX_EXACT = REL_BUCKETS // 2
REL_MAX_DIST = 128
CONV_WIDTH = 3

LANES = 128
SUBLANES = 8
MXU_DIM = 256
VMEM_LIMIT_BYTES = 56 * 1024 * 1024

SSM_T = MXU_DIM // SSM_GROUP_DIM
SEG_CHUNK = MXU_DIM
ATTN_BLOCK = 256
ATTN_HEADS_PER_STEP = N_HEADS
CORNER = LANES
V_ROWS = 2 * HEAD_DIM + 16
LOG2E = math.log2(math.e)
GLU_SPLIT = 4
FFN_ROWS = 512
FFN_COLS = MXU_DIM

bf16 = jnp.bfloat16
f32 = jnp.float32


def _cparams(*sem):
    return pltpu.CompilerParams(dimension_semantics=sem, vmem_limit_bytes=VMEM_LIMIT_BYTES)


def _rms(xf, g):
    return xf * lax.rsqrt(jnp.mean(xf * xf, axis=-1, keepdims=True) + EPS) * g


def _gelu_tanh(v):
    c = 2.0 * math.sqrt(2.0 / math.pi)
    return v * jax.nn.sigmoid(v * (c + (c * 0.044715) * (v * v)))


def _seg_norm(y, seg_ref, gain_ref):
    cols = []
    for c in range(y.shape[1] // SEG_CHUNK):
        yc = y[:, c * SEG_CHUNK:(c + 1) * SEG_CHUNK]
        ss = jnp.dot((yc * yc).astype(bf16), seg_ref[...], preferred_element_type=f32)
        cols.append(yc * lax.rsqrt(ss * (1.0 / HEAD_DIM) + EPS) * gain_ref[...])
    return jnp.concatenate(cols, axis=1)


def _segment_ones():
    idx = jnp.arange(SEG_CHUNK) // HEAD_DIM
    return (idx[:, None] == idx[None, :]).astype(bf16)


def _segment_gain(g):
    return jnp.tile(g.astype(f32), SEG_CHUNK // HEAD_DIM)[None, :]


def _store_vt(vt_ref, vt):
    hd2 = 2 * HEAD_DIM
    for hh in range(vt_ref.shape[1]):
        vt_ref[0, hh, :hd2, :] = vt[hh * hd2:(hh + 1) * hd2].astype(vt_ref.dtype)
        vt_ref[0, hh, hd2:, :] = jnp.ones((V_ROWS - hd2, vt.shape[1]), vt_ref.dtype)


def _win_kernel(x_ref, g_ref, wt_ref, ut_ref):
    n_chunks, nt, d = x_ref.shape
    g_n = ut_ref.shape[0]
    xt = pltpu.einshape("ktd->tkd", x_ref[...]).reshape(nt * n_chunks, d)
    h = _rms(xt, g_ref[...]).astype(bf16)
    ut = lax.dot_general(wt_ref[...], h, (((1,), (1,)), ((), ())), preferred_element_type=f32)
    for tt in range(nt):
        blk = ut[:, tt * n_chunks:(tt + 1) * n_chunks].reshape(g_n, SSM_GROUP_DIM, n_chunks)
        ut_ref[:, tt * SSM_GROUP_DIM:(tt + 1) * SSM_GROUP_DIM, :] = blk.astype(ut_ref.dtype)


def _by_chunk(x3):
    bsz, seqlen, d = x3.shape
    return x3.reshape(bsz, seqlen // SSM_T, SSM_T // SUBLANES, SUBLANES, d)


def _chunk_spec(x5):
    _, n_chunks, _, nt, d = x5.shape
    return pl.BlockSpec((None, n_chunks, None, nt, d), lambda b, t: (b, 0, t, 0, 0))


def _win_t(x3, g, w_t):
    bsz, seqlen, d = x3.shape
    n_chunks = seqlen // SSM_T
    assert n_chunks == LANES
    g_n = d // SSM_GROUP_DIM
    x5 = _by_chunk(x3)
    return pl.pallas_call(
        _win_kernel,
        grid=(bsz, SSM_T // SUBLANES),
        in_specs=[_chunk_spec(x5),
                  pl.BlockSpec((1, d), lambda b, t: (0, 0)),
                  pl.BlockSpec(w_t.shape, lambda b, t: (0, 0))],
        out_specs=pl.BlockSpec((g_n, SUBLANES * SSM_GROUP_DIM, n_chunks), lambda b, t: (0, t, b)),
        out_shape=jax.ShapeDtypeStruct((g_n, SSM_T * SSM_GROUP_DIM, bsz * n_chunks), bf16),
        compiler_params=_cparams("parallel", "parallel"),
        name="win_t",
    )(x5, g.astype(f32)[None, :], w_t)


def _ssm_kernel(n_rounds, u_ref, mt_ref, w_ref, v_ref, pw_ref, y_ref, xin_ref):
    u0 = u_ref[0]
    u1 = u_ref[1]
    half = LANES
    seg = LANES
    st = (jnp.dot(w_ref[0], u0, preferred_element_type=f32)
          + jnp.dot(w_ref[1], u1, preferred_element_type=f32))
    n_seq = st.shape[1] // seg
    xr = jnp.stack([st[:half, b * seg:(b + 1) * seg] for b in range(n_seq)])
    xi = jnp.stack([st[half:, b * seg:(b + 1) * seg] for b in range(n_seq)])
    chunk = lax.broadcasted_iota(jnp.int32, xr.shape, 2)
    for j in range(n_rounds):
        d = 1 << j
        ar = pw_ref[j, 0]
        ai = pw_ref[j, 1]
        sr = pltpu.roll(xr, d, 2)
        si = pltpu.roll(xi, d, 2)
        xr, xi = xr + (ar * sr - ai * si), xi + (ar * si + ai * sr)
    xr = jnp.where(chunk >= 1, pltpu.roll(xr, 1, 2), 0.0).astype(bf16)
    xi = jnp.where(chunk >= 1, pltpu.roll(xi, 1, 2), 0.0).astype(bf16)
    for b in range(n_seq):
        xin_ref[:half, b * seg:(b + 1) * seg] = xr[b]
        xin_ref[half:, b * seg:(b + 1) * seg] = xi[b]
    ys = jnp.dot(v_ref[...], xin_ref[...], preferred_element_type=f32)
    y_ref[0] = (jnp.dot(mt_ref[0], u0, preferred_element_type=f32) + ys[:MXU_DIM]).astype(y_ref.dtype)
    y_ref[1] = (jnp.dot(mt_ref[1], u1, preferred_element_type=f32) + ys[MXU_DIM:]).astype(y_ref.dtype)


def _ssm_tables(log_dt, lam_re, lam_im, b_re, b_im, c_re, c_im, n_chunks):
    g_n, p_n = lam_re.shape
    t_n, c_n = SSM_T, SSM_GROUP_DIM
    dt = jnp.exp(log_dt.astype(f32))[:, None]
    lr = lam_re.astype(f32)
    li = lam_im.astype(f32)

    def power(n):
        nn = jnp.asarray(n, f32).reshape((-1, 1, 1))
        mag = jnp.exp(lr * dt * nn)
        return mag * jnp.cos(li * dt * nn), mag * jnp.sin(li * dt * nn)

    ab_r, ab_i = power([1])
    ab_r, ab_i = ab_r[0], ab_i[0]
    den = lr * lr + li * li
    f_r = ((ab_r - 1.0) * lr + ab_i * li) / den
    f_i = (ab_i * lr - (ab_r - 1.0) * li) / den
    br = b_re.astype(f32)
    bi = b_im.astype(f32)
    bb_r = f_r[..., None] * br - f_i[..., None] * bi
    bb_i = f_r[..., None] * bi + f_i[..., None] * br
    cr = c_re.astype(f32)
    ci = c_im.astype(f32)

    qr, qi = power(t_n - 1 - jnp.arange(t_n))
    t_r = qr[..., None] * bb_r[None] - qi[..., None] * bb_i[None]
    t_i = qr[..., None] * bb_i[None] + qi[..., None] * bb_r[None]
    kern = (jnp.einsum("gcp,dgpk->gcdk", cr, t_r, precision=lax.Precision.HIGHEST)
            - jnp.einsum("gcp,dgpk->gcdk", ci, t_i, precision=lax.Precision.HIGHEST))
    kern = kern.reshape(g_n, c_n, t_n * c_n)
    kern = jnp.concatenate([kern, jnp.zeros_like(kern)], axis=-1)
    mt = jnp.concatenate([kern[:, :, (t_n - 1 - t) * c_n:(2 * t_n - 1 - t) * c_n] for t in range(t_n)], axis=1)
    w_r = t_r.transpose(1, 2, 0, 3).reshape(g_n, p_n, t_n * c_n)
    w_i = t_i.transpose(1, 2, 0, 3).reshape(g_n, p_n, t_n * c_n)
    pr1, pi1 = power(1 + jnp.arange(t_n))
    pr1 = pr1.transpose(1, 0, 2)
    pi1 = pi1.transpose(1, 0, 2)
    v_r = (cr[:, None] * pr1[:, :, None, :] - ci[:, None] * pi1[:, :, None, :]).reshape(g_n, t_n * c_n, p_n)
    v_i = -(cr[:, None] * pi1[:, :, None, :] + ci[:, None] * pr1[:, :, None, :]).reshape(g_n, t_n * c_n, p_n)

    gp = g_n // 2
    z = jnp.zeros((gp, p_n, t_n * c_n), f32)
    w_r = w_r.reshape(gp, 2, p_n, t_n * c_n)
    w_i = w_i.reshape(gp, 2, p_n, t_n * c_n)
    w0 = jnp.concatenate([w_r[:, 0], z, w_i[:, 0], z], axis=1)
    w1 = jnp.concatenate([z, w_r[:, 1], z, w_i[:, 1]], axis=1)
    w_pack = jnp.stack([w0, w1], axis=1)
    zv = jnp.zeros((gp, t_n * c_n, p_n), f32)
    v_r = v_r.reshape(gp, 2, t_n * c_n, p_n)
    v_i = v_i.reshape(gp, 2, t_n * c_n, p_n)
    v_pack = jnp.concatenate([
        jnp.concatenate([v_r[:, 0], zv, v_i[:, 0], zv], axis=-1),
        jnp.concatenate([zv, v_r[:, 1], zv, v_i[:, 1]], axis=-1)], axis=1)
    n_rounds = (n_chunks - 1).bit_length()
    sr, si = power(t_n * (2 ** jnp.arange(max(n_rounds, 1))))
    pw = jnp.stack([sr.reshape(-1, gp, 2 * p_n), si.reshape(-1, gp, 2 * p_n)], axis=2)
    reach = (jnp.arange(LANES)[None, :] >= (2 ** jnp.arange(pw.shape[0]))[:, None]).astype(f32)
    pw = pw.transpose(1, 0, 2, 3)[..., None] * reach[None, :, None, None, :]
    return mt.reshape(gp, 2, t_n * c_n, t_n * c_n).astype(bf16), w_pack.astype(bf16), v_pack.astype(bf16), pw


def _ssm(ut, tables, layer):
    mt, w_pack, v_pack, pw = tables
    n_rounds = pw.shape[2]
    g_n, k, cols = ut.shape
    gp = g_n // 2
    return pl.pallas_call(
        functools.partial(_ssm_kernel, n_rounds),
        grid=(gp,),
        in_specs=[pl.BlockSpec((2, k, cols), lambda g: (g, 0, 0)),
                  pl.BlockSpec((None, None, 2, k, k), lambda g: (layer, g, 0, 0, 0)),
                  pl.BlockSpec((None, None, 2, k, k), lambda g: (layer, g, 0, 0, 0)),
                  pl.BlockSpec((None, None, 2 * k, k), lambda g: (layer, g, 0, 0)),
                  pl.BlockSpec((None, None) + pw.shape[2:], lambda g: (layer, g, 0, 0, 0, 0))],
        out_specs=pl.BlockSpec((2, k, cols), lambda g: (g, 0, 0)),
        out_shape=jax.ShapeDtypeStruct((g_n, k, cols), bf16),
        scratch_shapes=[pltpu.VMEM((k, cols), bf16)],
        compiler_params=_cparams("parallel"),
        name="ssm",
    )(ut, mt, w_pack, v_pack, pw)


def _glu_kernel(x_ref, y_ref, u_ref, d_ref, w_ref, o_ref):
    n_chunks, nt, dm = x_ref.shape
    g_n = y_ref.shape[0]
    per = nt // GLU_SPLIT
    upd = []
    for s in range(GLU_SPLIT):
        cols = []
        for tt in range(s * per, (s + 1) * per):
            sl = slice(tt * SSM_GROUP_DIM, (tt + 1) * SSM_GROUP_DIM)
            y = y_ref[:, sl, :].astype(f32) + d_ref[...] * u_ref[:, sl, :].astype(f32)
            cols.append(_gelu_tanh(y).reshape(g_n * SSM_GROUP_DIM, n_chunks))
        gt = jnp.concatenate(cols, axis=1)
        r = jnp.dot(gt.T.astype(bf16), w_ref[...], preferred_element_type=f32)
        upd.append(r[:, :dm] * jax.nn.sigmoid(r[:, dm:]))
    upd = jnp.concatenate(upd, axis=0).reshape(nt, n_chunks, dm)
    o_ref[...] = x_ref[...] + pltpu.einshape("tkd->ktd", upd)


def _glu_res(x3, yt, ut, d_skip, w_glu):
    bsz, seqlen, d = x3.shape
    g_n, _, cols = yt.shape
    n_chunks = cols // bsz
    d_tab = jnp.broadcast_to(d_skip.astype(f32).reshape(g_n, SSM_GROUP_DIM, 1), (g_n, SSM_GROUP_DIM, n_chunks))
    feat = pl.BlockSpec((g_n, SUBLANES * SSM_GROUP_DIM, n_chunks), lambda b, t: (0, t, b))
    x5 = _by_chunk(x3)
    out = pl.pallas_call(
        _glu_kernel,
        grid=(bsz, SSM_T // SUBLANES),
        in_specs=[_chunk_spec(x5), feat, feat,
                  pl.BlockSpec(d_tab.shape, lambda b, t: (0, 0, 0)),
                  pl.BlockSpec(w_glu.shape, lambda b, t: (0, 0))],
        out_specs=_chunk_spec(x5),
        out_shape=jax.ShapeDtypeStruct(x5.shape, f32),
        compiler_params=_cparams("parallel", "parallel"),
        name="glu_res",
    )(x5, yt, ut, d_tab, w_glu)
    return out.reshape(bsz, seqlen, d)


def _attn_kernel(tq, tk, out_scale, lam_ref, far_ref, q_ref, k_ref, vt_ref, diag_ref, corner_ref, sub_ref,
                 o_ref, qm_ref, m_ref, acc_ref, s_ref):
    hd2 = 2 * HEAD_DIM
    heads = range(vt_ref.shape[1])
    lane = lax.broadcasted_iota(jnp.int32, (tq, hd2), 1)
    far = [far_ref[pl.program_id(1) * len(heads) + hh] for hh in heads]

    def q_block(qi, carry):
        rows = pl.ds(pl.multiple_of(qi * tq, tq), tq)
        for hh in heads:
            q = q_ref[0, rows, hh * hd2:(hh + 1) * hd2].astype(f32) * (HEAD_DIM ** -0.5 * LOG2E)
            qm_ref[hh, :tq] = jnp.where(lane < HEAD_DIM, q, 0.0).astype(bf16)
            qm_ref[hh, tq:] = jnp.where(lane >= HEAD_DIM, q, 0.0).astype(bf16)
        m_ref[...] = jnp.full_like(m_ref, NEG_INF)
        acc_ref[...] = jnp.zeros_like(acc_ref)

        def fill(slot, ki):
            for hh in heads:
                k = k_ref[0, pl.ds(pl.multiple_of(ki * tk, tk), tk), hh * hd2:(hh + 1) * hd2]
                s_ref[slot, hh] = lax.dot_general(k, qm_ref[hh], (((1,), (1,)), ((), ())),
                                                  preferred_element_type=f32)

        def accumulate(hh, ki, s, shift):
            vt = vt_ref[0, hh, :, pl.ds(pl.multiple_of(ki * tk, tk), tk)]
            m_prev = m_ref[hh]
            m_new = jnp.maximum(m_prev, jnp.max(s, axis=0, keepdims=True) + shift)
            p = jnp.exp2(s - (m_new - shift))
            acc_ref[hh] = (jnp.exp2(m_prev - m_new) * acc_ref[hh]
                           + jnp.dot(vt, p.astype(bf16), preferred_element_type=f32))
            m_ref[hh] = m_new

        def earlier(slot, ki):
            near = jnp.where(ki == qi - 1, 1.0, 0.0)
            for hh in heads:
                s = s_ref[slot, hh]
                c = near * corner_ref[hh]
                lo = s[tk - CORNER:]
                lo = jnp.concatenate([lo[:, :CORNER] + c, lo[:, CORNER:tq], lo[:, tq:tq + CORNER] + c,
                                      lo[:, tq + CORNER:]], axis=1)
                accumulate(hh, ki, jnp.concatenate([s[:tk - CORNER], lo], axis=0), far[hh])

        def pair(i, carry):
            fill(1, 2 * i + 1)
            earlier(0, 2 * i)
            fill(0, 2 * i + 2)
            earlier(1, 2 * i + 1)
            return carry

        fill(0, 0)
        lax.fori_loop(0, qi // 2, pair, 0)

        def diagonal(slot):
            for hh in heads:
                b = diag_ref[hh]
                accumulate(hh, qi, s_ref[slot, hh] + jnp.concatenate([b, b], axis=1), 0.0)

        @pl.when(qi % 2 == 1)
        def _():
            fill(1, qi)
            earlier(0, qi - 1)
            diagonal(1)

        @pl.when(qi % 2 == 0)
        def _():
            diagonal(0)

        for hh in heads:
            acc = acc_ref[hh]
            ot = acc[:hd2] / acc[hd2:hd2 + 1]
            ot = ot[:, :tq] - lam_ref[0] * ot[:, tq:]
            ot = ot * lax.rsqrt(jnp.mean(ot * ot, axis=0, keepdims=True) + EPS)
            o_ref[0, rows, hh * hd2:(hh + 1) * hd2] = (ot.T * sub_ref[...] * out_scale).astype(o_ref.dtype)
        return carry

    lax.fori_loop(0, k_ref.shape[1] // tq, q_block, 0)


def _bias_tables(rel_bias, tq, tk):
    n = jnp.arange(2 * tq)
    nf = jnp.maximum(n, REL_MAX_EXACT).astype(f32)
    large = REL_MAX_EXACT + (jnp.log(nf / REL_MAX_EXACT) / math.log(REL_MAX_DIST / REL_MAX_EXACT)
                             * (REL_BUCKETS - REL_MAX_EXACT)).astype(jnp.int32)
    large = jnp.minimum(large, REL_BUCKETS - 1)
    bucket = jnp.where(n < REL_MAX_EXACT, n, large)
    onehot = (bucket[:, None] == jnp.arange(REL_BUCKETS)[None, :]).astype(f32)
    by_dist = jnp.dot(onehot, rel_bias.astype(f32), precision=lax.Precision.HIGHEST).T * LOG2E
    n_heads = by_dist.shape[0]
    neg = jnp.full((n_heads, tk), NEG_INF, f32)
    period = tq + tk
    tiles = []
    for d, tail in ((0, neg), (1, by_dist[:, tq - tk:tq])):
        r = jnp.concatenate([by_dist[:, d * tq:(d + 1) * tq], tail], axis=1)
        t = jnp.tile(r, (1, tk))[:, :tk * (period - 1)].reshape(n_heads, tk, period - 1)
        tiles.append(t[:, :, :tq])
    far = rel_bias.astype(f32)[REL_BUCKETS - 1] * LOG2E
    corner = tiles[1][:, tk - CORNER:, :CORNER] - far[:, None, None]
    return tiles[0], corner, far


def _far_bucket_from():
    n = REL_MAX_EXACT
    while REL_MAX_EXACT + int(math.log(n / REL_MAX_EXACT) / math.log(REL_MAX_DIST / REL_MAX_EXACT)
                              * (REL_BUCKETS - REL_MAX_EXACT)) < REL_BUCKETS - 1:
        n += 1
    return n


def _diff_attention(q, k, vt, diag, corner, far, lam, subln, out_scale, bsz, seqlen, tq, tk):
    assert tq == tk and tq - CORNER + 1 >= _far_bucket_from()
    hd2 = 2 * HEAD_DIM
    q3 = q.reshape(bsz, seqlen, N_HEADS * hd2)
    k3 = k.reshape(bsz, seqlen, N_HEADS * hd2)
    smem = pl.BlockSpec(memory_space=pltpu.SMEM)
    hps = ATTN_HEADS_PER_STEP
    o = pl.pallas_call(
        functools.partial(_attn_kernel, tq, tk, out_scale),
        grid=(bsz, N_HEADS // hps),
        in_specs=[smem, smem,
                  pl.BlockSpec((1, seqlen, hps * hd2), lambda b, h: (b, 0, h)),
                  pl.BlockSpec((1, seqlen, hps * hd2), lambda b, h: (b, 0, h)),
                  pl.BlockSpec((1, hps, V_ROWS, seqlen), lambda b, h: (b, h, 0, 0)),
                  pl.BlockSpec((hps, tk, tq), lambda b, h: (h, 0, 0)),
                  pl.BlockSpec((hps, CORNER, CORNER), lambda b, h: (h, 0, 0)),
                  pl.BlockSpec((1, hd2), lambda b, h: (0, 0))],
        out_specs=pl.BlockSpec((1, seqlen, hps * hd2), lambda b, h: (b, 0, h)),
        out_shape=jax.ShapeDtypeStruct((bsz, seqlen, N_HEADS * hd2), bf16),
        scratch_shapes=[pltpu.VMEM((hps, 2 * tq, hd2), bf16), pltpu.VMEM((hps, 1, 2 * tq), f32),
                        pltpu.VMEM((hps, V_ROWS, 2 * tq), f32), pltpu.VMEM((2, hps, tk, 2 * tq), f32)],
        compiler_params=_cparams("parallel", "parallel"),
        name="diff_attn",
    )(lam.reshape(1).astype(f32), far, q3, k3, vt, diag, corner, subln.astype(f32)[None, :])
    return o.reshape(bsz * seqlen, N_HEADS * hd2)


def _ffn_ple_kernel(tiles_per_seq, has_attn, has_q, has_kv, *refs):
    refs = list(refs)
    x_ref, p_ref = refs[:2]
    del refs[:2]
    if has_attn:
        a_ref, wo_ref = refs[:2]
        del refs[:2]
    fg_ref, wup_ref, cw_ref, cb_ref, wdn_ref, pg_ref, wgate_ref, wproj_ref = refs[:8]
    del refs[:8]
    if has_q or has_kv:
        seg_ref = refs.pop(0)
    if has_q:
        qg_ref, wq_ref, qgain_ref = refs[:3]
        del refs[:3]
    if has_kv:
        kvg_ref, wk_ref, wvt_ref, kgain_ref = refs[:4]
        del refs[:4]
    o_ref = refs.pop(0)
    if has_q:
        q_ref = refs.pop(0)
    if has_kv:
        k_ref, vt_ref = refs[:2]
        del refs[:2]
    carry_ref, act_ref, h_ref = refs

    i = pl.program_id(0)
    tm = x_ref.shape[0]
    d_ff = wdn_ref.shape[0]
    fc = FFN_COLS
    halo = SUBLANES

    @pl.when(i % tiles_per_seq == 0)
    def _():
        carry_ref[...] = jnp.zeros_like(carry_ref)

    x = x_ref[...]
    if has_attn:
        x = x + jnp.dot(a_ref[...], wo_ref[...], preferred_element_type=f32)
    h_ref[...] = _rms(x, fg_ref[...]).astype(bf16)
    row = lax.broadcasted_iota(jnp.int32, (halo, fc), 0)

    def up_proj(col):
        return jnp.dot(h_ref[...], wup_ref[:, col:col + fc], preferred_element_type=f32)

    def conv(col, up):
        prev = carry_ref[:, col:col + fc]
        carry_ref[:, col:col + fc] = up[tm - halo:]
        c = cb_ref[:, col:col + fc] + cw_ref[2:3, col:col + fc] * up
        for back in range(1, CONV_WIDTH):
            r = pltpu.roll(up, back, 0)
            head = jnp.where(row < back, pltpu.roll(prev, back, 0), r[:halo])
            c = c + cw_ref[2 - back:3 - back, col:col + fc] * jnp.concatenate([head, r[halo:]], axis=0)
        return c

    cols = list(range(0, d_ff, fc))
    ups = (up_proj(cols[0]), up_proj(d_ff + cols[0]))
    for idx, col in enumerate(cols):
        nxt = (up_proj(cols[idx + 1]), up_proj(d_ff + cols[idx + 1])) if idx + 1 < len(cols) else None
        act_ref[:, col:col + fc] = (_gelu_tanh(conv(col, ups[0])) * conv(d_ff + col, ups[1])).astype(bf16)
        ups = nxt
    halves = [slice(0, tm // 2), slice(tm // 2, tm)]
    xs = [x[r] + jnp.dot(act_ref[r, :], wdn_ref[...], preferred_element_type=f32) for r in halves]
    projs = [jnp.dot(p_ref[r, :].astype(bf16), wproj_ref[...], preferred_element_type=f32) for r in halves]
    gates = [jax.nn.sigmoid(jnp.dot(_rms(xh, pg_ref[...]).astype(bf16), wgate_ref[...],
                                    preferred_element_type=f32)) for xh in xs]
    xs = [xh + g * pr for xh, g, pr in zip(xs, gates, projs)]
    for r, xh in zip(halves, xs):
        o_ref[r, :] = xh
    if has_q or has_kv:
        xn = jnp.concatenate([xh * lax.rsqrt(jnp.mean(xh * xh, axis=-1, keepdims=True) + EPS) for xh in xs],
                             axis=0)
    if has_q:
        q = jnp.dot((xn * qg_ref[...]).astype(bf16), wq_ref[...], preferred_element_type=f32)
        q_ref[...] = _seg_norm(q, seg_ref, qgain_ref).astype(q_ref.dtype)
    if has_kv:
        hk = (xn * kvg_ref[...]).astype(bf16)
        k = jnp.dot(hk, wk_ref[...], preferred_element_type=f32)
        k_ref[...] = _seg_norm(k, seg_ref, kgain_ref).astype(k_ref.dtype)
        _store_vt(vt_ref, lax.dot_general(wvt_ref[...], hk, (((1,), (1,)), ((), ())),
                                          preferred_element_type=f32))


def _ffn_ple(x, p, layer, ffn, ple, seqlen, attn=None, next_q=None, next_kv=None):
    n, d = x.shape
    ffn_g, w_up, conv_w, conv_b, w_down = ffn
    ple_g, w_gate, w_proj = ple
    d_ff = w_down.shape[-2]
    tm = FFN_ROWS
    bsz = n // seqlen
    tps = seqlen // tm
    row = lambda i: (i, 0)

    def full(a):
        if a.ndim == 3:
            return pl.BlockSpec((None,) + a.shape[1:], lambda i: (layer, 0, 0), pipeline_mode=pl.Buffered(1))
        return pl.BlockSpec(a.shape, lambda i: (0, 0), pipeline_mode=pl.Buffered(1))

    vec = lambda a: a.astype(f32)[None, :]
    args = [x, p]
    in_specs = [pl.BlockSpec((tm, d), row), pl.BlockSpec((None, tm, p.shape[2]), lambda i: (layer, i, 0))]
    if attn is not None:
        args += [attn[0], attn[1]]
        in_specs += [pl.BlockSpec((tm, attn[0].shape[1]), row), full(attn[1])]
    weights = [vec(ffn_g), w_up, conv_w.astype(f32), vec(conv_b), w_down, vec(ple_g), w_gate, w_proj]
    out_specs = [pl.BlockSpec((tm, d), row)]
    out_shape = [jax.ShapeDtypeStruct((n, d), f32)]
    if next_q is not None or next_kv is not None:
        weights.append(_segment_ones())
    if next_q is not None:
        weights += [vec(next_q[0]), next_q[1], _segment_gain(next_q[2])]
        out_specs.append(pl.BlockSpec((tm, next_q[1].shape[1]), row))
        out_shape.append(jax.ShapeDtypeStruct((n, next_q[1].shape[1]), bf16))
    if next_kv is not None:
        weights += [vec(next_kv[0]), next_kv[1], next_kv[2], _segment_gain(next_kv[3])]
        out_specs += [pl.BlockSpec((tm, next_kv[1].shape[1]), row),
                      pl.BlockSpec((1, N_HEADS, V_ROWS, tm), lambda i: (i // tps, 0, 0, i % tps))]
        out_shape += [jax.ShapeDtypeStruct((n, next_kv[1].shape[1]), bf16),
                      jax.ShapeDtypeStruct((bsz, N_HEADS, V_ROWS, seqlen), bf16)]
    args += weights
    in_specs += [full(w) for w in weights]
    return pl.pallas_call(
        functools.partial(_ffn_ple_kernel, tps, attn is not None, next_q is not None, next_kv is not None),
        grid=(n // tm,),
        in_specs=in_specs,
        out_specs=out_specs,
        out_shape=out_shape,
        scratch_shapes=[pltpu.VMEM((SUBLANES, 2 * d_ff), f32),
                        pltpu.VMEM((tm, d_ff), bf16),
                        pltpu.VMEM((tm, d), bf16)],
        compiler_params=_cparams("arbitrary"),
        name="ffn_ple",
    )(*args)


def kernel(x, p, ssm_norm, ssm_w_in, ssm_log_dt, ssm_lambda_re, ssm_lambda_im, ssm_b_re, ssm_b_im, ssm_c_re, ssm_c_im, ssm_d, ssm_w_glu, kv_norm, kv_w, k_norm, attn_norm, attn_w_q, q_norm, lambda_q1, lambda_k1, lambda_q2, lambda_k2, subln, attn_w_o, rel_bias, ffn_norm, ffn_w_up, ffn_conv_w, ffn_conv_b, ffn_w_down, ple_norm, ple_w_gate, ple_w_proj):
    bsz, seqlen, d = x.shape
    depth = p.shape[0]
    n_a = ssm_norm.shape[0]
    n = bsz * seqlen
    x = x.reshape(n, d).astype(f32)
    p = p.reshape(depth, n, p.shape[-1])
    nk = N_HEADS * 2 * HEAD_DIM
    assert 1 <= n_a < depth
    diag, corner, far = _bias_tables(rel_bias, ATTN_BLOCK, ATTN_BLOCK)
    tables = jax.vmap(functools.partial(_ssm_tables, n_chunks=seqlen // SSM_T))(
        ssm_log_dt, ssm_lambda_re, ssm_lambda_im, ssm_b_re, ssm_b_im, ssm_c_re, ssm_c_im)
    w_in_t = jnp.swapaxes(ssm_w_in, 1, 2).astype(bf16)
    w_glu, w_q, w_o = ssm_w_glu.astype(bf16), attn_w_q.astype(bf16), attn_w_o.astype(bf16)
    w_up, w_down = ffn_w_up.astype(bf16), ffn_w_down.astype(bf16)
    w_gate, w_proj = ple_w_gate.astype(bf16), ple_w_proj.astype(bf16)
    lam_all = (jnp.exp(jnp.sum(lambda_q1.astype(f32) * lambda_k1.astype(f32), axis=-1))
               - jnp.exp(jnp.sum(lambda_q2.astype(f32) * lambda_k2.astype(f32), axis=-1)))
    q = k = vt = None
    for i in range(depth):
        attn = None
        if i < n_a:
            x3 = x.reshape(bsz, seqlen, d)
            ut = _win_t(x3, ssm_norm[i], w_in_t[i])
            yt = _ssm(ut, tables, i)
            x = _glu_res(x3, yt, ut, ssm_d[i], w_glu[i]).reshape(n, d)
        else:
            j = i - n_a
            lam_init = 0.8 - 0.6 * math.exp(-0.3 * i)
            o = _diff_attention(q, k, vt, diag, corner, far, lam_all[j] + lam_init, subln[j], 1.0 - lam_init,
                                bsz, seqlen, ATTN_BLOCK, ATTN_BLOCK)
            attn = (o, w_o[j])
        j_next = i + 1 - n_a
        next_q = next_kv = None
        if 0 <= j_next < depth - n_a:
            next_q = (attn_norm[j_next], w_q[j_next], q_norm[j_next])
            if j_next == 0:
                next_kv = (kv_norm, kv_w[:, :nk].astype(bf16), kv_w[:, nk:].T.astype(bf16), k_norm)
        outs = _ffn_ple(x, p, i,
                        (ffn_norm[i], w_up, ffn_conv_w[i], ffn_conv_b[i], w_down),
                        (ple_norm[i], w_gate, w_proj),
                        seqlen, attn=attn, next_q=next_q, next_kv=next_kv)
        x = outs[0]
        if next_q is not None:
            q = outs[1]
        if next_kv is not None:
            k, vt = outs[2], outs[3]
    return x.reshape(bsz, seqlen, d)
```

```python
import functools
import math

import jax
import jax.numpy as jnp
from jax import lax
from jax.experimental import pallas as pl
from jax.experimental.pallas import tpu as pltpu

EPS = 1e-6
NEG_INF = -1e30

SSM_GROUP_DIM = 16
N_HEADS = 8
HEAD_DIM = 64
REL_BUCKETS = 32
REL_MAX_EXACT = REL_BUCKETS // 2
REL_MAX_DIST = 128
CONV_WIDTH = 3

LANES = 128
SUBLANES = 8
MXU_DIM = 256
VMEM_LIMIT_BYTES = 56 * 1024 * 1024

SSM_T = MXU_DIM // SSM_GROUP_DIM
SEG_CHUNK = MXU_DIM
ATTN_BLOCK = 256
ATTN_HEADS_PER_STEP = N_HEADS
CORNER = LANES
V_ROWS = 2 * HEAD_DIM + 16
LOG2E = math.log2(math.e)
GLU_SPLIT = 4
FFN_ROWS = 512
FFN_COLS = MXU_DIM

bf16 = jnp.bfloat16
f32 = jnp.float32


def _cparams(*sem):
    return pltpu.CompilerParams(dimension_semantics=sem, vmem_limit_bytes=VMEM_LIMIT_BYTES)


def _rms(xf, g):
    return xf * lax.rsqrt(jnp.mean(xf * xf, axis=-1, keepdims=True) + EPS) * g


def _gelu_tanh(v):
    c = 2.0 * math.sqrt(2.0 / math.pi)
    return v * jax.nn.sigmoid(v * (c + (c * 0.044715) * (v * v)))


def _seg_norm(y, seg_ref, gain_ref):
    cols = []
    for c in range(y.shape[1] // SEG_CHUNK):
        yc = y[:, c * SEG_CHUNK:(c + 1) * SEG_CHUNK]
        ss = jnp.dot((yc * yc).astype(bf16), seg_ref[...], preferred_element_type=f32)
        cols.append(yc * lax.rsqrt(ss * (1.0 / HEAD_DIM) + EPS) * gain_ref[...])
    return jnp.concatenate(cols, axis=1)


def _segment_ones():
    idx = jnp.arange(SEG_CHUNK) // HEAD_DIM
    return (idx[:, None] == idx[None, :]).astype(bf16)


def _segment_gain(g):
    return jnp.tile(g.astype(f32), SEG_CHUNK // HEAD_DIM)[None, :]


def _store_vt(vt_ref, vt):
    hd2 = 2 * HEAD_DIM
    for hh in range(vt_ref.shape[1]):
        vt_ref[0, hh, :hd2, :] = vt[hh * hd2:(hh + 1) * hd2].astype(vt_ref.dtype)
        vt_ref[0, hh, hd2:, :] = jnp.ones((V_ROWS - hd2, vt.shape[1]), vt_ref.dtype)


def _win_kernel(x_ref, g_ref, wt_ref, ut_ref):
    n_chunks, nt, d = x_ref.shape
    g_n = ut_ref.shape[0]
    xt = pltpu.einshape("ktd->tkd", x_ref[...]).reshape(nt * n_chunks, d)
    h = _rms(xt, g_ref[...]).astype(bf16)
    ut = lax.dot_general(wt_ref[...], h, (((1,), (1,)), ((), ())), preferred_element_type=f32)
    for tt in range(nt):
        blk = ut[:, tt * n_chunks:(tt + 1) * n_chunks].reshape(g_n, SSM_GROUP_DIM, n_chunks)
        ut_ref[:, tt * SSM_GROUP_DIM:(tt + 1) * SSM_GROUP_DIM, :] = blk.astype(ut_ref.dtype)


def _by_chunk(x3):
    bsz, seqlen, d = x3.shape
    return x3.reshape(bsz, seqlen // SSM_T, SSM_T // SUBLANES, SUBLANES, d)


def _chunk_spec(x5):
    _, n_chunks, _, nt, d = x5.shape
    return pl.BlockSpec((None, n_chunks, None, nt, d), lambda b, t: (b, 0, t, 0, 0))


def _win_t(x3, g, w_t):
    bsz, seqlen, d = x3.shape
    n_chunks = seqlen // SSM_T
    assert n_chunks == LANES
    g_n = d // SSM_GROUP_DIM
    x5 = _by_chunk(x3)
    return pl.pallas_call(
        _win_kernel,
        grid=(bsz, SSM_T // SUBLANES),
        in_specs=[_chunk_spec(x5),
                  pl.BlockSpec((1, d), lambda b, t: (0, 0)),
                  pl.BlockSpec(w_t.shape, lambda b, t: (0, 0))],
        out_specs=pl.BlockSpec((g_n, SUBLANES * SSM_GROUP_DIM, n_chunks), lambda b, t: (0, t, b)),
        out_shape=jax.ShapeDtypeStruct((g_n, SSM_T * SSM_GROUP_DIM, bsz * n_chunks), bf16),
        compiler_params=_cparams("parallel", "parallel"),
        name="win_t",
    )(x5, g.astype(f32)[None, :], w_t)


def _ssm_kernel(n_rounds, u_ref, mt_ref, w_ref, v_ref, pw_ref, y_ref, xin_ref):
    u0 = u_ref[0]
    u1 = u_ref[1]
    half = LANES
    seg = LANES
    st = (jnp.dot(w_ref[0], u0, preferred_element_type=f32)
          + jnp.dot(w_ref[1], u1, preferred_element_type=f32))
    n_seq = st.shape[1] // seg
    xr = jnp.stack([st[:half, b * seg:(b + 1) * seg] for b in range(n_seq)])
    xi = jnp.stack([st[half:, b * seg:(b + 1) * seg] for b in range(n_seq)])
    chunk = lax.broadcasted_iota(jnp.int32, xr.shape, 2)
    for j in range(n_rounds):
        d = 1 << j
        ar = pw_ref[j, 0]
        ai = pw_ref[j, 1]
        sr = pltpu.roll(xr, d, 2)
        si = pltpu.roll(xi, d, 2)
        xr, xi = xr + (ar * sr - ai * si), xi + (ar * si + ai * sr)
    xr = jnp.where(chunk >= 1, pltpu.roll(xr, 1, 2), 0.0).astype(bf16)
    xi = jnp.where(chunk >= 1, pltpu.roll(xi, 1, 2), 0.0).astype(bf16)
    for b in range(n_seq):
        xin_ref[:half, b * seg:(b + 1) * seg] = xr[b]
        xin_ref[half:, b * seg:(b + 1) * seg] = xi[b]
    ys = jnp.dot(v_ref[...], xin_ref[...], preferred_element_type=f32)
    y_ref[0] = (jnp.dot(mt_ref[0], u0, preferred_element_type=f32) + ys[:MXU_DIM]).astype(y_ref.dtype)
    y_ref[1] = (jnp.dot(mt_ref[1], u1, preferred_element_type=f32) + ys[MXU_DIM:]).astype(y_ref.dtype)


def _ssm_tables(log_dt, lam_re, lam_im, b_re, b_im, c_re, c_im, n_chunks):
    g_n, p_n = lam_re.shape
    t_n, c_n = SSM_T, SSM_GROUP_DIM
    dt = jnp.exp(log_dt.astype(f32))[:, None]
    lr = lam_re.astype(f32)
    li = lam_im.astype(f32)

    def power(n):
        nn = jnp.asarray(n, f32).reshape((-1, 1, 1))
        mag = jnp.exp(lr * dt * nn)
        return mag * jnp.cos(li * dt * nn), mag * jnp.sin(li * dt * nn)

    ab_r, ab_i = power([1])
    ab_r, ab_i = ab_r[0], ab_i[0]
    den = lr * lr + li * li
    f_r = ((ab_r - 1.0) * lr + ab_i * li) / den
    f_i = (ab_i * lr - (ab_r - 1.0) * li) / den
    br = b_re.astype(f32)
    bi = b_im.astype(f32)
    bb_r = f_r[..., None] * br - f_i[..., None] * bi
    bb_i = f_r[..., None] * bi + f_i[..., None] * br
    cr = c_re.astype(f32)
    ci = c_im.astype(f32)

    qr, qi = power(t_n - 1 - jnp.arange(t_n))
    t_r = qr[..., None] * bb_r[None] - qi[..., None] * bb_i[None]
    t_i = qr[..., None] * bb_i[None] + qi[..., None] * bb_r[None]
    kern = (jnp.einsum("gcp,dgpk->gcdk", cr, t_r, precision=lax.Precision.HIGHEST)
            - jnp.einsum("gcp,dgpk->gcdk", ci, t_i, precision=lax.Precision.HIGHEST))
    kern = kern.reshape(g_n, c_n, t_n * c_n)
    kern = jnp.concatenate([kern, jnp.zeros_like(kern)], axis=-1)
    mt = jnp.concatenate([kern[:, :, (t_n - 1 - t) * c_n:(2 * t_n - 1 - t) * c_n] for t in range(t_n)], axis=1)
    w_r = t_r.transpose(1, 2, 0, 3).reshape(g_n, p_n, t_n * c_n)
    w_i = t_i.transpose(1, 2, 0, 3).reshape(g_n, p_n, t_n * c_n)
    pr1, pi1 = power(1 + jnp.arange(t_n))
    pr1 = pr1.transpose(1, 0, 2)
    pi1 = pi1.transpose(1, 0, 2)
    v_r = (cr[:, None] * pr1[:, :, None, :] - ci[:, None] * pi1[:, :, None, :]).reshape(g_n, t_n * c_n, p_n)
    v_i = -(cr[:, None] * pi1[:, :, None, :] + ci[:, None] * pr1[:, :, None, :]).reshape(g_n, t_n * c_n, p_n)

    gp = g_n // 2
    z = jnp.zeros((gp, p_n, t_n * c_n), f32)
    w_r = w_r.reshape(gp, 2, p_n, t_n * c_n)
    w_i = w_i.reshape(gp, 2, p_n, t_n * c_n)
    w0 = jnp.concatenate([w_r[:, 0], z, w_i[:, 0], z], axis=1)
    w1 = jnp.concatenate([z, w_r[:, 1], z, w_i[:, 1]], axis=1)
    w_pack = jnp.stack([w0, w1], axis=1)
    zv = jnp.zeros((gp, t_n * c_n, p_n), f32)
    v_r = v_r.reshape(gp, 2, t_n * c_n, p_n)
    v_i = v_i.reshape(gp, 2, t_n * c_n, p_n)
    v_pack = jnp.concatenate([
        jnp.concatenate([v_r[:, 0], zv, v_i[:, 0], zv], axis=-1),
        jnp.concatenate([zv, v_r[:, 1], zv, v_i[:, 1]], axis=-1)], axis=1)
    n_rounds = (n_chunks - 1).bit_length()
    sr, si = power(t_n * (2 ** jnp.arange(max(n_rounds, 1))))
    pw = jnp.stack([sr.reshape(-1, gp, 2 * p_n), si.reshape(-1, gp, 2 * p_n)], axis=2)
    reach = (jnp.arange(LANES)[None, :] >= (2 ** jnp.arange(pw.shape[0]))[:, None]).astype(f32)
    pw = pw.transpose(1, 0, 2, 3)[..., None] * reach[None, :, None, None, :]
    return mt.reshape(gp, 2, t_n * c_n, t_n * c_n).astype(bf16), w_pack.astype(bf16), v_pack.astype(bf16), pw


def _ssm(ut, tables, layer):
    mt, w_pack, v_pack, pw = tables
    n_rounds = pw.shape[2]
    g_n, k, cols = ut.shape
    gp = g_n // 2
    return pl.pallas_call(
        functools.partial(_ssm_kernel, n_rounds),
        grid=(gp,),
        in_specs=[pl.BlockSpec((2, k, cols), lambda g: (g, 0, 0)),
                  pl.BlockSpec((None, None, 2, k, k), lambda g: (layer, g, 0, 0, 0)),
                  pl.BlockSpec((None, None, 2, k, k), lambda g: (layer, g, 0, 0, 0)),
                  pl.BlockSpec((None, None, 2 * k, k), lambda g: (layer, g, 0, 0)),
                  pl.BlockSpec((None, None) + pw.shape[2:], lambda g: (layer, g, 0, 0, 0, 0))],
        out_specs=pl.BlockSpec((2, k, cols), lambda g: (g, 0, 0)),
        out_shape=jax.ShapeDtypeStruct((g_n, k, cols), bf16),
        scratch_shapes=[pltpu.VMEM((k, cols), bf16)],
        compiler_params=_cparams("parallel"),
        name="ssm",
    )(ut, mt, w_pack, v_pack, pw)


def _glu_kernel(x_ref, y_ref, u_ref, d_ref, w_ref, o_ref):
    n_chunks, nt, dm = x_ref.shape
    g_n = y_ref.shape[0]
    per = nt // GLU_SPLIT
    upd = []
    for s in range(GLU_SPLIT):
        cols = []
        for tt in range(s * per, (s + 1) * per):
            sl = slice(tt * SSM_GROUP_DIM, (tt + 1) * SSM_GROUP_DIM)
            y = y_ref[:, sl, :].astype(f32) + d_ref[...] * u_ref[:, sl, :].astype(f32)
            cols.append(_gelu_tanh(y).reshape(g_n * SSM_GROUP_DIM, n_chunks))
        gt = jnp.concatenate(cols, axis=1)
        r = lax.dot_general(gt.astype(bf16), w_ref[...], (((0,), (0,)), ((), ())), preferred_element_type=f32)
        upd.append(r[:, :dm] * jax.nn.sigmoid(r[:, dm:]))
    upd = jnp.concatenate(upd, axis=0).reshape(nt, n_chunks, dm)
    o_ref[...] = x_ref[...] + pltpu.einshape("tkd->ktd", upd)


def _glu_res(x3, yt, ut, d_skip, w_glu):
    bsz, seqlen, d = x3.shape
    g_n, _, cols = yt.shape
    n_chunks = cols // bsz
    d_tab = jnp.broadcast_to(d_skip.astype(f32).reshape(g_n, SSM_GROUP_DIM, 1), (g_n, SSM_GROUP_DIM, n_chunks))
    feat = pl.BlockSpec((g_n, SUBLANES * SSM_GROUP_DIM, n_chunks), lambda b, t: (0, t, b))
    x5 = _by_chunk(x3)
    out = pl.pallas_call(
        _glu_kernel,
        grid=(bsz, SSM_T // SUBLANES),
        in_specs=[_chunk_spec(x5), feat, feat,
                  pl.BlockSpec(d_tab.shape, lambda b, t: (0, 0, 0)),
                  pl.BlockSpec(w_glu.shape, lambda b, t: (0, 0))],
        out_specs=_chunk_spec(x5),
        out_shape=jax.ShapeDtypeStruct(x5.shape, f32),
        compiler_params=_cparams("parallel", "parallel"),
        name="glu_res",
    )(x5, yt, ut, d_tab, w_glu)
    return out.reshape(bsz, seqlen, d)


def _attn_kernel(tq, tk, out_scale, lam_ref, far_ref, q_ref, k_ref, vt_ref, diag_ref, corner_ref, sub_ref,
                 o_ref, qm_ref, m_ref, acc_ref, s_ref):
    hd2 = 2 * HEAD_DIM
    heads = range(vt_ref.shape[1])
    lane = lax.broadcasted_iota(jnp.int32, (tq, hd2), 1)
    far = [far_ref[pl.program_id(1) * len(heads) + hh] for hh in heads]

    def q_block(qi, carry):
        rows = pl.ds(pl.multiple_of(qi * tq, tq), tq)
        for hh in heads:
            q = q_ref[0, rows, hh * hd2:(hh + 1) * hd2].astype(f32) * (HEAD_DIM ** -0.5 * LOG2E)
            qm_ref[hh, :tq] = jnp.where(lane < HEAD_DIM, q, 0.0).astype(bf16)
            qm_ref[hh, tq:] = jnp.where(lane >= HEAD_DIM, q, 0.0).astype(bf16)
        m_ref[...] = jnp.full_like(m_ref, NEG_INF)
        acc_ref[...] = jnp.zeros_like(acc_ref)

        def fill(slot, ki):
            for hh in heads:
                k = k_ref[0, pl.ds(pl.multiple_of(ki * tk, tk), tk), hh * hd2:(hh + 1) * hd2]
                s_ref[slot, hh] = lax.dot_general(k, qm_ref[hh], (((1,), (1,)), ((), ())),
                                                  preferred_element_type=f32)

        def accumulate(hh, ki, s, shift):
            vt = vt_ref[0, hh, :, pl.ds(pl.multiple_of(ki * tk, tk), tk)]
            m_prev = m_ref[hh]
            m_new = jnp.maximum(m_prev, jnp.max(s, axis=0, keepdims=True) + shift)
            p = jnp.exp2(s - (m_new - shift))
            acc_ref[hh] = (jnp.exp2(m_prev - m_new) * acc_ref[hh]
                           + jnp.dot(vt, p.astype(bf16), preferred_element_type=f32))
            m_ref[hh] = m_new

        def earlier(slot, ki):
            near = jnp.where(ki == qi - 1, 1.0, 0.0)
            for hh in heads:
                s = s_ref[slot, hh]
                c = near * corner_ref[hh]
                lo = s[tk - CORNER:]
                lo = jnp.concatenate([lo[:, :CORNER] + c, lo[:, CORNER:tq], lo[:, tq:tq + CORNER] + c,
                                      lo[:, tq + CORNER:]], axis=1)
                accumulate(hh, ki, jnp.concatenate([s[:tk - CORNER], lo], axis=0), far[hh])

        def pair(i, carry):
            fill(1, 2 * i + 1)
            earlier(0, 2 * i)
            fill(0, 2 * i + 2)
            earlier(1, 2 * i + 1)
            return carry

        fill(0, 0)
        lax.fori_loop(0, qi // 2, pair, 0)

        def diagonal(slot):
            for hh in heads:
                b = diag_ref[hh]
                accumulate(hh, qi, s_ref[slot, hh] + jnp.concatenate([b, b], axis=1), 0.0)

        @pl.when(qi % 2 == 1)
        def _():
            fill(1, qi)
            earlier(0, qi - 1)
            diagonal(1)

        @pl.when(qi % 2 == 0)
        def _():
            diagonal(0)

        for hh in heads:
            acc = acc_ref[hh]
            ot = acc[:hd2] / acc[hd2:hd2 + 1]
            ot = ot[:, :tq] - lam_ref[0] * ot[:, tq:]
            ot = ot * lax.rsqrt(jnp.mean(ot * ot, axis=0, keepdims=True) + EPS)
            o_ref[0, rows, hh * hd2:(hh + 1) * hd2] = (ot.T * sub_ref[...] * out_scale).astype(o_ref.dtype)
        return carry

    lax.fori_loop(0, k_ref.shape[1] // tq, q_block, 0)


def _bias_tables(rel_bias, tq, tk):
    n = jnp.arange(2 * tq)
    nf = jnp.maximum(n, REL_MAX_EXACT).astype(f32)
    large = REL_MAX_EXACT + (jnp.log(nf / REL_MAX_EXACT) / math.log(REL_MAX_DIST / REL_MAX_EXACT)
                             * (REL_BUCKETS - REL_MAX_EXACT)).astype(jnp.int32)
    large = jnp.minimum(large, REL_BUCKETS - 1)
    bucket = jnp.where(n < REL_MAX_EXACT, n, large)
    onehot = (bucket[:, None] == jnp.arange(REL_BUCKETS)[None, :]).astype(f32)
    by_dist = jnp.dot(onehot, rel_bias.astype(f32), precision=lax.Precision.HIGHEST).T * LOG2E
    n_heads = by_dist.shape[0]
    neg = jnp.full((n_heads, tk), NEG_INF, f32)
    period = tq + tk
    tiles = []
    for d, tail in ((0, neg), (1, by_dist[:, tq - tk:tq])):
        r = jnp.concatenate([by_dist[:, d * tq:(d + 1) * tq], tail], axis=1)
        t = jnp.tile(r, (1, tk))[:, :tk * (period - 1)].reshape(n_heads, tk, period - 1)
        tiles.append(t[:, :, :tq])
    far = rel_bias.astype(f32)[REL_BUCKETS - 1] * LOG2E
    corner = tiles[1][:, tk - CORNER:, :CORNER] - far[:, None, None]
    return tiles[0], corner, far


def _far_bucket_from():
    n = REL_MAX_EXACT
    while REL_MAX_EXACT + int(math.log(n / REL_MAX_EXACT) / math.log(REL_MAX_DIST / REL_MAX_EXACT)
                              * (REL_BUCKETS - REL_MAX_EXACT)) < REL_BUCKETS - 1:
        n += 1
    return n


def _diff_attention(q, k, vt, diag, corner, far, lam, subln, out_scale, bsz, seqlen, tq, tk):
    assert tq == tk and tq - CORNER + 1 >= _far_bucket_from()
    hd2 = 2 * HEAD_DIM
    q3 = q.reshape(bsz, seqlen, N_HEADS * hd2)
    k3 = k.reshape(bsz, seqlen, N_HEADS * hd2)
    smem = pl.BlockSpec(memory_space=pltpu.SMEM)
    hps = ATTN_HEADS_PER_STEP
    o = pl.pallas_call(
        functools.partial(_attn_kernel, tq, tk, out_scale),
        grid=(bsz, N_HEADS // hps),
        in_specs=[smem, smem,
                  pl.BlockSpec((1, seqlen, hps * hd2), lambda b, h: (b, 0, h)),
                  pl.BlockSpec((1, seqlen, hps * hd2), lambda b, h: (b, 0, h)),
                  pl.BlockSpec((1, hps, V_ROWS, seqlen), lambda b, h: (b, h, 0, 0)),
                  pl.BlockSpec((hps, tk, tq), lambda b, h: (h, 0, 0)),
                  pl.BlockSpec((hps, CORNER, CORNER), lambda b, h: (h, 0, 0)),
                  pl.BlockSpec((1, hd2), lambda b, h: (0, 0))],
        out_specs=pl.BlockSpec((1, seqlen, hps * hd2), lambda b, h: (b, 0, h)),
        out_shape=jax.ShapeDtypeStruct((bsz, seqlen, N_HEADS * hd2), bf16),
        scratch_shapes=[pltpu.VMEM((hps, 2 * tq, hd2), bf16), pltpu.VMEM((hps, 1, 2 * tq), f32),
                        pltpu.VMEM((hps, V_ROWS, 2 * tq), f32), pltpu.VMEM((2, hps, tk, 2 * tq), f32)],
        compiler_params=_cparams("parallel", "parallel"),
        name="diff_attn",
    )(lam.reshape(1).astype(f32), far, q3, k3, vt, diag, corner, subln.astype(f32)[None, :])
    return o.reshape(bsz * seqlen, N_HEADS * hd2)


def _ffn_ple_kernel(tiles_per_seq, has_attn, has_q, has_kv, *refs):
    refs = list(refs)
    x_ref, p_ref = refs[:2]
    del refs[:2]
    if has_attn:
        a_ref, wo_ref = refs[:2]
        del refs[:2]
    fg_ref, wup_ref, cw_ref, cb_ref, wdn_ref, pg_ref, wgate_ref, wproj_ref = refs[:8]
    del refs[:8]
    if has_q or has_kv:
        seg_ref = refs.pop(0)
    if has_q:
        qg_ref, wq_ref, qgain_ref = refs[:3]
        del refs[:3]
    if has_kv:
        kvg_ref, wk_ref, wvt_ref, kgain_ref = refs[:4]
        del refs[:4]
    o_ref = refs.pop(0)
    if has_q:
        q_ref = refs.pop(0)
    if has_kv:
        k_ref, vt_ref = refs[:2]
        del refs[:2]
    carry_ref, act_ref, h_ref = refs

    i = pl.program_id(0)
    tm = x_ref.shape[0]
    d_ff = wdn_ref.shape[0]
    fc = FFN_COLS
    halo = SUBLANES

    @pl.when(i % tiles_per_seq == 0)
    def _():
        carry_ref[...] = jnp.zeros_like(carry_ref)

    x = x_ref[...]
    if has_attn:
        x = x + jnp.dot(a_ref[...], wo_ref[...], preferred_element_type=f32)
    h_ref[...] = _rms(x, fg_ref[...]).astype(bf16)
    row = lax.broadcasted_iota(jnp.int32, (halo, fc), 0)

    def up_proj(col):
        return jnp.dot(h_ref[...], wup_ref[:, col:col + fc], preferred_element_type=f32)

    def conv(col, up):
        prev = carry_ref[:, col:col + fc]
        carry_ref[:, col:col + fc] = up[tm - halo:]
        c = cb_ref[:, col:col + fc] + cw_ref[2:3, col:col + fc] * up
        for back in range(1, CONV_WIDTH):
            r = pltpu.roll(up, back, 0)
            head = jnp.where(row < back, pltpu.roll(prev, back, 0), r[:halo])
            c = c + cw_ref[2 - back:3 - back, col:col + fc] * jnp.concatenate([head, r[halo:]], axis=0)
        return c

    cols = list(range(0, d_ff, fc))
    ups = (up_proj(cols[0]), up_proj(d_ff + cols[0]))
    for idx, col in enumerate(cols):
        nxt = (up_proj(cols[idx + 1]), up_proj(d_ff + cols[idx + 1])) if idx + 1 < len(cols) else None
        act_ref[:, col:col + fc] = (_gelu_tanh(conv(col, ups[0])) * conv(d_ff + col, ups[1])).astype(bf16)
        ups = nxt
    halves = [slice(0, tm // 2), slice(tm // 2, tm)]
    xs = [x[r] + jnp.dot(act_ref[r, :], wdn_ref[...], preferred_element_type=f32) for r in halves]
    projs = [jnp.dot(p_ref[r, :].astype(bf16), wproj_ref[...], preferred_element_type=f32) for r in halves]
    gates = [jax.nn.sigmoid(jnp.dot(_rms(xh, pg_ref[...]).astype(bf16), wgate_ref[...],
                                    preferred_element_type=f32)) for xh in xs]
    xs = [xh + g * pr for xh, g, pr in zip(xs, gates, projs)]
    for r, xh in zip(halves, xs):
        o_ref[r, :] = xh
    if has_q or has_kv:
        xn = jnp.concatenate([xh * lax.rsqrt(jnp.mean(xh * xh, axis=-1, keepdims=True) + EPS) for xh in xs],
                             axis=0)
    if has_q:
        q = jnp.dot((xn * qg_ref[...]).astype(bf16), wq_ref[...], preferred_element_type=f32)
        q_ref[...] = _seg_norm(q, seg_ref, qgain_ref).astype(q_ref.dtype)
    if has_kv:
        hk = (xn * kvg_ref[...]).astype(bf16)
        k = jnp.dot(hk, wk_ref[...], preferred_element_type=f32)
        k_ref[...] = _seg_norm(k, seg_ref, kgain_ref).astype(k_ref.dtype)
        _store_vt(vt_ref, lax.dot_general(wvt_ref[...], hk, (((1,), (1,)), ((), ())),
                                          preferred_element_type=f32))


def _ffn_ple(x, p, layer, ffn, ple, seqlen, attn=None, next_q=None, next_kv=None):
    n, d = x.shape
    ffn_g, w_up, conv_w, conv_b, w_down = ffn
    ple_g, w_gate, w_proj = ple
    d_ff = w_down.shape[-2]
    tm = FFN_ROWS
    bsz = n // seqlen
    tps = seqlen // tm
    row = lambda i: (i, 0)

    def full(a):
        if a.ndim == 3:
            return pl.BlockSpec((None,) + a.shape[1:], lambda i: (layer, 0, 0), pipeline_mode=pl.Buffered(1))
        return pl.BlockSpec(a.shape, lambda i: (0, 0), pipeline_mode=pl.Buffered(1))

    vec = lambda a: a.astype(f32)[None, :]
    args = [x, p]
    in_specs = [pl.BlockSpec((tm, d), row), pl.BlockSpec((None, tm, p.shape[2]), lambda i: (layer, i, 0))]
    if attn is not None:
        args += [attn[0], attn[1]]
        in_specs += [pl.BlockSpec((tm, attn[0].shape[1]), row), full(attn[1])]
    weights = [vec(ffn_g), w_up, conv_w.astype(f32), vec(conv_b), w_down, vec(ple_g), w_gate, w_proj]
    out_specs = [pl.BlockSpec((tm, d), row)]
    out_shape = [jax.ShapeDtypeStruct((n, d), f32)]
    if next_q is not None or next_kv is not None:
        weights.append(_segment_ones())
    if next_q is not None:
        weights += [vec(next_q[0]), next_q[1], _segment_gain(next_q[2])]
        out_specs.append(pl.BlockSpec((tm, next_q[1].shape[1]), row))
        out_shape.append(jax.ShapeDtypeStruct((n, next_q[1].shape[1]), bf16))
    if next_kv is not None:
        weights += [vec(next_kv[0]), next_kv[1], next_kv[2], _segment_gain(next_kv[3])]
        out_specs += [pl.BlockSpec((tm, next_kv[1].shape[1]), row),
                      pl.BlockSpec((1, N_HEADS, V_ROWS, tm), lambda i: (i // tps, 0, 0, i % tps))]
        out_shape += [jax.ShapeDtypeStruct((n, next_kv[1].shape[1]), bf16),
                      jax.ShapeDtypeStruct((bsz, N_HEADS, V_ROWS, seqlen), bf16)]
    args += weights
    in_specs += [full(w) for w in weights]
    return pl.pallas_call(
        functools.partial(_ffn_ple_kernel, tps, attn is not None, next_q is not None, next_kv is not None),
        grid=(n // tm,),
        in_specs=in_specs,
        out_specs=out_specs,
        out_shape=out_shape,
        scratch_shapes=[pltpu.VMEM((SUBLANES, 2 * d_ff), f32),
                        pltpu.VMEM((tm, d_ff), bf16),
                        pltpu.VMEM((tm, d), bf16)],
        compiler_params=_cparams("arbitrary"),
        name="ffn_ple",
    )(*args)


def kernel(x, p, ssm_norm, ssm_w_in, ssm_log_dt, ssm_lambda_re, ssm_lambda_im, ssm_b_re, ssm_b_im, ssm_c_re, ssm_c_im, ssm_d, ssm_w_glu, kv_norm, kv_w, k_norm, attn_norm, attn_w_q, q_norm, lambda_q1, lambda_k1, lambda_q2, lambda_k2, subln, attn_w_o, rel_bias, ffn_norm, ffn_w_up, ffn_conv_w, ffn_conv_b, ffn_w_down, ple_norm, ple_w_gate, ple_w_proj):
    bsz, seqlen, d = x.shape
    depth = p.shape[0]
    n_a = ssm_norm.shape[0]
    n = bsz * seqlen
    x = x.reshape(n, d).astype(f32)
    p = p.reshape(depth, n, p.shape[-1])
    nk = N_HEADS * 2 * HEAD_DIM
    assert 1 <= n_a < depth
    diag, corner, far = _bias_tables(rel_bias, ATTN_BLOCK, ATTN_BLOCK)
    tables = jax.vmap(functools.partial(_ssm_tables, n_chunks=seqlen // SSM_T))(
        ssm_log_dt, ssm_lambda_re, ssm_lambda_im, ssm_b_re, ssm_b_im, ssm_c_re, ssm_c_im)
    w_in_t = jnp.swapaxes(ssm_w_in, 1, 2).astype(bf16)
    w_glu, w_q, w_o = ssm_w_glu.astype(bf16), attn_w_q.astype(bf16), attn_w_o.astype(bf16)
    w_up, w_down = ffn_w_up.astype(bf16), ffn_w_down.astype(bf16)
    w_gate, w_proj = ple_w_gate.astype(bf16), ple_w_proj.astype(bf16)
    lam_all = (jnp.exp(jnp.sum(lambda_q1.astype(f32) * lambda_k1.astype(f32), axis=-1))
               - jnp.exp(jnp.sum(lambda_q2.astype(f32) * lambda_k2.astype(f32), axis=-1)))
    q = k = vt = None
    for i in range(depth):
        attn = None
        if i < n_a:
            x3 = x.reshape(bsz, seqlen, d)
            ut = _win_t(x3, ssm_norm[i], w_in_t[i])
            yt = _ssm(ut, tables, i)
            x = _glu_res(x3, yt, ut, ssm_d[i], w_glu[i]).reshape(n, d)
        else:
            j = i - n_a
            lam_init = 0.8 - 0.6 * math.exp(-0.3 * i)
            o = _diff_attention(q, k, vt, diag, corner, far, lam_all[j] + lam_init, subln[j], 1.0 - lam_init,
                                bsz, seqlen, ATTN_BLOCK, ATTN_BLOCK)
            attn = (o, w_o[j])
        j_next = i + 1 - n_a
        next_q = next_kv = None
        if 0 <= j_next < depth - n_a:
            next_q = (attn_norm[j_next], w_q[j_next], q_norm[j_next])
            if j_next == 0:
                next_kv = (kv_norm, kv_w[:, :nk].astype(bf16), kv_w[:, nk:].T.astype(bf16), k_norm)
        outs = _ffn_ple(x, p, i,
                        (ffn_norm[i], w_up, ffn_conv_w[i], ffn_conv_b[i], w_down),
                        (ple_norm[i], w_gate, w_proj),
                        seqlen, attn=attn, next_q=next_q, next_kv=next_kv)
        x = outs[0]
        if next_q is not None:
            q = outs[1]
        if next_kv is not None:
            k, vt = outs[2], outs[3]
    return x.reshape(bsz, seqlen, d)
```

```python
import functools
import math

import jax
import jax.numpy as jnp
from jax import lax
from jax.experimental import pallas as pl
from jax.experimental.pallas import tpu as pltpu

EPS = 1e-6
NEG_INF = -1e30

SSM_GROUP_DIM = 16
N_HEADS = 8
HEAD_DIM = 64
REL_BUCKETS = 32
REL_MAX_EXACT = REL_BUCKETS // 2
REL_MAX_DIST = 128
CONV_WIDTH = 3

LANES = 128
SUBLANES = 8
MXU_DIM = 256
VMEM_LIMIT_BYTES = 56 * 1024 * 1024

SSM_T = MXU_DIM // SSM_GROUP_DIM
SEG_CHUNK = MXU_DIM
ATTN_BLOCK = 256
ATTN_HEADS_PER_STEP = N_HEADS
CORNER = LANES
V_ROWS = 2 * HEAD_DIM + 16
LOG2E = math.log2(math.e)
GLU_SPLIT = 4
FFN_ROWS = 512
FFN_COLS = MXU_DIM

bf16 = jnp.bfloat16
f32 = jnp.float32


def _cparams(*sem):
    return pltpu.CompilerParams(dimension_semantics=sem, vmem_limit_bytes=VMEM_LIMIT_BYTES)


def _rms(xf, g):
    return xf * lax.rsqrt(jnp.mean(xf * xf, axis=-1, keepdims=True) + EPS) * g


def _gelu_tanh(v):
    c = 2.0 * math.sqrt(2.0 / math.pi)
    return v * jax.nn.sigmoid(v * (c + (c * 0.044715) * (v * v)))


def _seg_norm(y, seg_ref, gain_ref):
    cols = []
    for c in range(y.shape[1] // SEG_CHUNK):
        yc = y[:, c * SEG_CHUNK:(c + 1) * SEG_CHUNK]
        ss = jnp.dot((yc * yc).astype(bf16), seg_ref[...], preferred_element_type=f32)
        cols.append(yc * lax.rsqrt(ss * (1.0 / HEAD_DIM) + EPS) * gain_ref[...])
    return jnp.concatenate(cols, axis=1)


def _segment_ones():
    idx = jnp.arange(SEG_CHUNK) // HEAD_DIM
    return (idx[:, None] == idx[None, :]).astype(bf16)


def _segment_gain(g):
    return jnp.tile(g.astype(f32), SEG_CHUNK // HEAD_DIM)[None, :]


def _store_vt(vt_ref, vt):
    hd2 = 2 * HEAD_DIM
    for hh in range(vt_ref.shape[1]):
        vt_ref[0, hh, :hd2, :] = vt[hh * hd2:(hh + 1) * hd2].astype(vt_ref.dtype)
        vt_ref[0, hh, hd2:, :] = jnp.ones((V_ROWS - hd2, vt.shape[1]), vt_ref.dtype)


def _win_kernel(x_ref, g_ref, wt_ref, ut_ref):
    n_chunks, nt, d = x_ref.shape
    g_n = ut_ref.shape[0]
    xt = pltpu.einshape("ktd->tkd", x_ref[...]).reshape(nt * n_chunks, d)
    h = _rms(xt, g_ref[...]).astype(bf16)
    ut = lax.dot_general(wt_ref[...], h, (((1,), (1,)), ((), ())), preferred_element_type=f32)
    for tt in range(nt):
        blk = ut[:, tt * n_chunks:(tt + 1) * n_chunks].reshape(g_n, SSM_GROUP_DIM, n_chunks)
        ut_ref[:, tt * SSM_GROUP_DIM:(tt + 1) * SSM_GROUP_DIM, :] = blk.astype(ut_ref.dtype)


def _by_chunk(x3):
    bsz, seqlen, d = x3.shape
    return x3.reshape(bsz, seqlen // SSM_T, SSM_T // SUBLANES, SUBLANES, d)


def _chunk_spec(x5):
    _, n_chunks, _, nt, d = x5.shape
    return pl.BlockSpec((None, n_chunks, None, nt, d), lambda b, t: (b, 0, t, 0, 0))


def _win_t(x3, g, w_t):
    bsz, seqlen, d = x3.shape
    n_chunks = seqlen // SSM_T
    assert n_chunks == LANES
    g_n = d // SSM_GROUP_DIM
    x5 = _by_chunk(x3)
    return pl.pallas_call(
        _win_kernel,
        grid=(bsz, SSM_T // SUBLANES),
        in_specs=[_chunk_spec(x5),
                  pl.BlockSpec((1, d), lambda b, t: (0, 0)),
                  pl.BlockSpec(w_t.shape, lambda b, t: (0, 0))],
        out_specs=pl.BlockSpec((g_n, SUBLANES * SSM_GROUP_DIM, n_chunks), lambda b, t: (0, t, b)),
        out_shape=jax.ShapeDtypeStruct((g_n, SSM_T * SSM_GROUP_DIM, bsz * n_chunks), bf16),
        compiler_params=_cparams("parallel", "parallel"),
        name="win_t",
    )(x5, g.astype(f32)[None, :], w_t)


def _ssm_kernel(n_rounds, u_ref, mt_ref, w_ref, v_ref, pw_ref, y_ref, xin_ref):
    u0 = u_ref[0]
    u1 = u_ref[1]
    half = LANES
    seg = LANES
    st = (jnp.dot(w_ref[0], u0, preferred_element_type=f32)
          + jnp.dot(w_ref[1], u1, preferred_element_type=f32))
    n_seq = st.shape[1] // seg
    xr = jnp.stack([st[:half, b * seg:(b + 1) * seg] for b in range(n_seq)])
    xi = jnp.stack([st[half:, b * seg:(b + 1) * seg] for b in range(n_seq)])
    chunk = lax.broadcasted_iota(jnp.int32, xr.shape, 2)
    for j in range(n_rounds):
        d = 1 << j
        ar = pw_ref[j, 0]
        ai = pw_ref[j, 1]
        sr = pltpu.roll(xr, d, 2)
        si = pltpu.roll(xi, d, 2)
        xr, xi = xr + (ar * sr - ai * si), xi + (ar * si + ai * sr)
    xr = jnp.where(chunk >= 1, pltpu.roll(xr, 1, 2), 0.0).astype(bf16)
    xi = jnp.where(chunk >= 1, pltpu.roll(xi, 1, 2), 0.0).astype(bf16)
    for b in range(n_seq):
        xin_ref[:half, b * seg:(b + 1) * seg] = xr[b]
        xin_ref[half:, b * seg:(b + 1) * seg] = xi[b]
    ys = jnp.dot(v_ref[...], xin_ref[...], preferred_element_type=f32)
    y_ref[0] = (jnp.dot(mt_ref[0], u0, preferred_element_type=f32) + ys[:MXU_DIM]).astype(y_ref.dtype)
    y_ref[1] = (jnp.dot(mt_ref[1], u1, preferred_element_type=f32) + ys[MXU_DIM:]).astype(y_ref.dtype)


def _ssm_tables(log_dt, lam_re, lam_im, b_re, b_im, c_re, c_im, n_chunks):
    g_n, p_n = lam_re.shape
    t_n, c_n = SSM_T, SSM_GROUP_DIM
    dt = jnp.exp(log_dt.astype(f32))[:, None]
    lr = lam_re.astype(f32)
    li = lam_im.astype(f32)

    def power(n):
        nn = jnp.asarray(n, f32).reshape((-1, 1, 1))
        mag = jnp.exp(lr * dt * nn)
        return mag * jnp.cos(li * dt * nn), mag * jnp.sin(li * dt * nn)

    ab_r, ab_i = power([1])
    ab_r, ab_i = ab_r[0], ab_i[0]
    den = lr * lr + li * li
    f_r = ((ab_r - 1.0) * lr + ab_i * li) / den
    f_i = (ab_i * lr - (ab_r - 1.0) * li) / den
    br = b_re.astype(f32)
    bi = b_im.astype(f32)
    bb_r = f_r[..., None] * br - f_i[..., None] * bi
    bb_i = f_r[..., None] * bi + f_i[..., None] * br
    cr = c_re.astype(f32)
    ci = c_im.astype(f32)

    qr, qi = power(t_n - 1 - jnp.arange(t_n))
    t_r = qr[..., None] * bb_r[None] - qi[..., None] * bb_i[None]
    t_i = qr[..., None] * bb_i[None] + qi[..., None] * bb_r[None]
    kern = (jnp.einsum("gcp,dgpk->gcdk", cr, t_r, precision=lax.Precision.HIGHEST)
            - jnp.einsum("gcp,dgpk->gcdk", ci, t_i, precision=lax.Precision.HIGHEST))
    kern = kern.reshape(g_n, c_n, t_n * c_n)
    kern = jnp.concatenate([kern, jnp.zeros_like(kern)], axis=-1)
    mt = jnp.concatenate([kern[:, :, (t_n - 1 - t) * c_n:(2 * t_n - 1 - t) * c_n] for t in range(t_n)], axis=1)
    w_r = t_r.transpose(1, 2, 0, 3).reshape(g_n, p_n, t_n * c_n)
    w_i = t_i.transpose(1, 2, 0, 3).reshape(g_n, p_n, t_n * c_n)
    pr1, pi1 = power(1 + jnp.arange(t_n))
    pr1 = pr1.transpose(1, 0, 2)
    pi1 = pi1.transpose(1, 0, 2)
    v_r = (cr[:, None] * pr1[:, :, None, :] - ci[:, None] * pi1[:, :, None, :]).reshape(g_n, t_n * c_n, p_n)
    v_i = -(cr[:, None] * pi1[:, :, None, :] + ci[:, None] * pr1[:, :, None, :]).reshape(g_n, t_n * c_n, p_n)

    gp = g_n // 2
    z = jnp.zeros((gp, p_n, t_n * c_n), f32)
    w_r = w_r.reshape(gp, 2, p_n, t_n * c_n)
    w_i = w_i.reshape(gp, 2, p_n, t_n * c_n)
    w0 = jnp.concatenate([w_r[:, 0], z, w_i[:, 0], z], axis=1)
    w1 = jnp.concatenate([z, w_r[:, 1], z, w_i[:, 1]], axis=1)
    w_pack = jnp.stack([w0, w1], axis=1)
    zv = jnp.zeros((gp, t_n * c_n, p_n), f32)
    v_r = v_r.reshape(gp, 2, t_n * c_n, p_n)
    v_i = v_i.reshape(gp, 2, t_n * c_n, p_n)
    v_pack = jnp.concatenate([
        jnp.concatenate([v_r[:, 0], zv, v_i[:, 0], zv], axis=-1),
        jnp.concatenate([zv, v_r[:, 1], zv, v_i[:, 1]], axis=-1)], axis=1)
    n_rounds = (n_chunks - 1).bit_length()
    sr, si = power(t_n * (2 ** jnp.arange(max(n_rounds, 1))))
    pw = jnp.stack([sr.reshape(-1, gp, 2 * p_n), si.reshape(-1, gp, 2 * p_n)], axis=2)
    reach = (jnp.arange(LANES)[None, :] >= (2 ** jnp.arange(pw.shape[0]))[:, None]).astype(f32)
    pw = pw.transpose(1, 0, 2, 3)[..., None] * reach[None, :, None, None, :]
    return mt.reshape(gp, 2, t_n * c_n, t_n * c_n).astype(bf16), w_pack.astype(bf16), v_pack.astype(bf16), pw


def _ssm(ut, tables, layer):
    mt, w_pack, v_pack, pw = tables
    n_rounds = pw.shape[2]
    g_n, k, cols = ut.shape
    gp = g_n // 2
    return pl.pallas_call(
        functools.partial(_ssm_kernel, n_rounds),
        grid=(gp,),
        in_specs=[pl.BlockSpec((2, k, cols), lambda g: (g, 0, 0)),
                  pl.BlockSpec((None, None, 2, k, k), lambda g: (layer, g, 0, 0, 0)),
                  pl.BlockSpec((None, None, 2, k, k), lambda g: (layer, g, 0, 0, 0)),
                  pl.BlockSpec((None, None, 2 * k, k), lambda g: (layer, g, 0, 0)),
                  pl.BlockSpec((None, None) + pw.shape[2:], lambda g: (layer, g, 0, 0, 0, 0))],
        out_specs=pl.BlockSpec((2, k, cols), lambda g: (g, 0, 0)),
        out_shape=jax.ShapeDtypeStruct((g_n, k, cols), bf16),
        scratch_shapes=[pltpu.VMEM((k, cols), bf16)],
        compiler_params=_cparams("parallel"),
        name="ssm",
    )(ut, mt, w_pack, v_pack, pw)


def _glu_kernel(x_ref, y_ref, u_ref, d_ref, w_ref, o_ref):
    n_chunks, nt, dm = x_ref.shape
    g_n = y_ref.shape[0]
    per = nt // GLU_SPLIT
    upd = []
    for s in range(GLU_SPLIT):
        cols = []
        for tt in range(s * per, (s + 1) * per):
            sl = slice(tt * SSM_GROUP_DIM, (tt + 1) * SSM_GROUP_DIM)
            y = y_ref[:, sl, :].astype(f32) + d_ref[...] * u_ref[:, sl, :].astype(f32)
            cols.append(_gelu_tanh(y).reshape(g_n * SSM_GROUP_DIM, n_chunks))
        gt = jnp.concatenate(cols, axis=1)
        r = jnp.dot(gt.T.astype(bf16), w_ref[...], preferred_element_type=f32)
        upd.append(r[:, :dm] * jax.nn.sigmoid(r[:, dm:]))
    upd = jnp.concatenate(upd, axis=0).reshape(nt, n_chunks, dm)
    o_ref[...] = x_ref[...] + pltpu.einshape("tkd->ktd", upd)


def _glu_res(x3, yt, ut, d_skip, w_glu):
    bsz, seqlen, d = x3.shape
    g_n, _, cols = yt.shape
    n_chunks = cols // bsz
    d_tab = jnp.broadcast_to(d_skip.astype(f32).reshape(g_n, SSM_GROUP_DIM, 1), (g_n, SSM_GROUP_DIM, n_chunks))
    feat = pl.BlockSpec((g_n, SUBLANES * SSM_GROUP_DIM, n_chunks), lambda b, t: (0, t, b))
    x5 = _by_chunk(x3)
    out = pl.pallas_call(
        _glu_kernel,
        grid=(bsz, SSM_T // SUBLANES),
        in_specs=[_chunk_spec(x5), feat, feat,
                  pl.BlockSpec(d_tab.shape, lambda b, t: (0, 0, 0)),
                  pl.BlockSpec(w_glu.shape, lambda b, t: (0, 0))],
        out_specs=_chunk_spec(x5),
        out_shape=jax.ShapeDtypeStruct(x5.shape, f32),
        compiler_params=_cparams("parallel", "parallel"),
        name="glu_res",
    )(x5, yt, ut, d_tab, w_glu)
    return out.reshape(bsz, seqlen, d)


def _attn_kernel(tq, tk, out_scale, lam_ref, far_ref, q_ref, k_ref, vt_ref, diag_ref, corner_ref, sub_ref,
                 o_ref, qm_ref, m_ref, acc_ref, s_ref):
    hd2 = 2 * HEAD_DIM
    heads = range(vt_ref.shape[1])
    lane = lax.broadcasted_iota(jnp.int32, (tq, hd2), 1)
    far = [far_ref[pl.program_id(1) * len(heads) + hh] for hh in heads]

    n_q = k_ref.shape[1] // tq

    def fill(slot, ki):
        for hh in heads:
            k = k_ref[0, pl.ds(pl.multiple_of(ki * tk, tk), tk), hh * hd2:(hh + 1) * hd2]
            s_ref[slot, hh] = lax.dot_general(k, qm_ref[hh], (((1,), (1,)), ((), ())),
                                              preferred_element_type=f32)

    def prepare(qn):
        rows = pl.ds(pl.multiple_of(qn * tq, tq), tq)
        for hh in heads:
            q = q_ref[0, rows, hh * hd2:(hh + 1) * hd2].astype(f32) * (HEAD_DIM ** -0.5 * LOG2E)
            qm_ref[hh, :tq] = jnp.where(lane < HEAD_DIM, q, 0.0).astype(bf16)
            qm_ref[hh, tq:] = jnp.where(lane >= HEAD_DIM, q, 0.0).astype(bf16)
        fill(0, 0)

    def reset():
        m_ref[...] = jnp.full_like(m_ref, NEG_INF)
        acc_ref[...] = jnp.zeros_like(acc_ref)

    def q_block(qi, carry):
        rows = pl.ds(pl.multiple_of(qi * tq, tq), tq)

        def accumulate(hh, ki, s, shift):
            vt = vt_ref[0, hh, :, pl.ds(pl.multiple_of(ki * tk, tk), tk)]
            m_prev = m_ref[hh]
            m_new = jnp.maximum(m_prev, jnp.max(s, axis=0, keepdims=True) + shift)
            p = jnp.exp2(s - (m_new - shift))
            acc_ref[hh] = (jnp.exp2(m_prev - m_new) * acc_ref[hh]
                           + jnp.dot(vt, p.astype(bf16), preferred_element_type=f32))
            m_ref[hh] = m_new

        def earlier(slot, ki):
            near = jnp.where(ki == qi - 1, 1.0, 0.0)
            for hh in heads:
                s = s_ref[slot, hh]
                c = near * corner_ref[hh]
                lo = s[tk - CORNER:]
                lo = jnp.concatenate([lo[:, :CORNER] + c, lo[:, CORNER:tq], lo[:, tq:tq + CORNER] + c,
                                      lo[:, tq + CORNER:]], axis=1)
                accumulate(hh, ki, jnp.concatenate([s[:tk - CORNER], lo], axis=0), far[hh])

        def pair(i, carry):
            fill(1, 2 * i + 1)
            earlier(0, 2 * i)
            fill(0, 2 * i + 2)
            earlier(1, 2 * i + 1)
            return carry

        lax.fori_loop(0, qi // 2, pair, 0)

        def diagonal(slot):
            for hh in heads:
                b = diag_ref[hh]
                accumulate(hh, qi, s_ref[slot, hh] + jnp.concatenate([b, b], axis=1), 0.0)

        @pl.when(qi % 2 == 1)
        def _():
            fill(1, qi)
            earlier(0, qi - 1)
            diagonal(1)

        @pl.when(qi % 2 == 0)
        def _():
            diagonal(0)

        prepare(jnp.minimum(qi + 1, n_q - 1))
        for hh in heads:
            acc = acc_ref[hh]
            ot = acc[:hd2] / acc[hd2:hd2 + 1]
            ot = ot[:, :tq] - lam_ref[0] * ot[:, tq:]
            ot = ot * lax.rsqrt(jnp.mean(ot * ot, axis=0, keepdims=True) + EPS)
            o_ref[0, rows, hh * hd2:(hh + 1) * hd2] = (ot.T * sub_ref[...] * out_scale).astype(o_ref.dtype)
        reset()
        return carry

    prepare(0)
    reset()
    lax.fori_loop(0, n_q, q_block, 0)


def _bias_tables(rel_bias, tq, tk):
    n = jnp.arange(2 * tq)
    nf = jnp.maximum(n, REL_MAX_EXACT).astype(f32)
    large = REL_MAX_EXACT + (jnp.log(nf / REL_MAX_EXACT) / math.log(REL_MAX_DIST / REL_MAX_EXACT)
                             * (REL_BUCKETS - REL_MAX_EXACT)).astype(jnp.int32)
    large = jnp.minimum(large, REL_BUCKETS - 1)
    bucket = jnp.where(n < REL_MAX_EXACT, n, large)
    onehot = (bucket[:, None] == jnp.arange(REL_BUCKETS)[None, :]).astype(f32)
    by_dist = jnp.dot(onehot, rel_bias.astype(f32), precision=lax.Precision.HIGHEST).T * LOG2E
    n_heads = by_dist.shape[0]
    neg = jnp.full((n_heads, tk), NEG_INF, f32)
    period = tq + tk
    tiles = []
    for d, tail in ((0, neg), (1, by_dist[:, tq - tk:tq])):
        r = jnp.concatenate([by_dist[:, d * tq:(d + 1) * tq], tail], axis=1)
        t = jnp.tile(r, (1, tk))[:, :tk * (period - 1)].reshape(n_heads, tk, period - 1)
        tiles.append(t[:, :, :tq])
    far = rel_bias.astype(f32)[REL_BUCKETS - 1] * LOG2E
    corner = tiles[1][:, tk - CORNER:, :CORNER] - far[:, None, None]
    return tiles[0], corner, far


def _far_bucket_from():
    n = REL_MAX_EXACT
    while REL_MAX_EXACT + int(math.log(n / REL_MAX_EXACT) / math.log(REL_MAX_DIST / REL_MAX_EXACT)
                              * (REL_BUCKETS - REL_MAX_EXACT)) < REL_BUCKETS - 1:
        n += 1
    return n


def _diff_attention(q, k, vt, diag, corner, far, lam, subln, out_scale, bsz, seqlen, tq, tk):
    assert tq == tk and tq - CORNER + 1 >= _far_bucket_from()
    hd2 = 2 * HEAD_DIM
    q3 = q.reshape(bsz, seqlen, N_HEADS * hd2)
    k3 = k.reshape(bsz, seqlen, N_HEADS * hd2)
    smem = pl.BlockSpec(memory_space=pltpu.SMEM)
    hps = ATTN_HEADS_PER_STEP
    o = pl.pallas_call(
        functools.partial(_attn_kernel, tq, tk, out_scale),
        grid=(bsz, N_HEADS // hps),
        in_specs=[smem, smem,
                  pl.BlockSpec((1, seqlen, hps * hd2), lambda b, h: (b, 0, h)),
                  pl.BlockSpec((1, seqlen, hps * hd2), lambda b, h: (b, 0, h)),
                  pl.BlockSpec((1, hps, V_ROWS, seqlen), lambda b, h: (b, h, 0, 0)),
                  pl.BlockSpec((hps, tk, tq), lambda b, h: (h, 0, 0)),
                  pl.BlockSpec((hps, CORNER, CORNER), lambda b, h: (h, 0, 0)),
                  pl.BlockSpec((1, hd2), lambda b, h: (0, 0))],
        out_specs=pl.BlockSpec((1, seqlen, hps * hd2), lambda b, h: (b, 0, h)),
        out_shape=jax.ShapeDtypeStruct((bsz, seqlen, N_HEADS * hd2), bf16),
        scratch_shapes=[pltpu.VMEM((hps, 2 * tq, hd2), bf16), pltpu.VMEM((hps, 1, 2 * tq), f32),
                        pltpu.VMEM((hps, V_ROWS, 2 * tq), f32), pltpu.VMEM((2, hps, tk, 2 * tq), f32)],
        compiler_params=_cparams("parallel", "parallel"),
        name="diff_attn",
    )(lam.reshape(1).astype(f32), far, q3, k3, vt, diag, corner, subln.astype(f32)[None, :])
    return o.reshape(bsz * seqlen, N_HEADS * hd2)


def _ffn_ple_kernel(tiles_per_seq, has_attn, has_q, has_kv, *refs):
    refs = list(refs)
    x_ref, p_ref = refs[:2]
    del refs[:2]
    if has_attn:
        a_ref, wo_ref = refs[:2]
        del refs[:2]
    fg_ref, wup_ref, cw_ref, cb_ref, wdn_ref, pg_ref, wgate_ref, wproj_ref = refs[:8]
    del refs[:8]
    if has_q or has_kv:
        seg_ref = refs.pop(0)
    if has_q:
        qg_ref, wq_ref, qgain_ref = refs[:3]
        del refs[:3]
    if has_kv:
        kvg_ref, wk_ref, wvt_ref, kgain_ref = refs[:4]
        del refs[:4]
    o_ref = refs.pop(0)
    if has_q:
        q_ref = refs.pop(0)
    if has_kv:
        k_ref, vt_ref = refs[:2]
        del refs[:2]
    carry_ref, act_ref, h_ref = refs

    i = pl.program_id(0)
    tm = x_ref.shape[0]
    d_ff = wdn_ref.shape[0]
    fc = FFN_COLS
    halo = SUBLANES

    @pl.when(i % tiles_per_seq == 0)
    def _():
        carry_ref[...] = jnp.zeros_like(carry_ref)

    x = x_ref[...]
    if has_attn:
        x = x + jnp.dot(a_ref[...], wo_ref[...], preferred_element_type=f32)
    h_ref[...] = _rms(x, fg_ref[...]).astype(bf16)
    row = lax.broadcasted_iota(jnp.int32, (halo, fc), 0)

    def up_proj(col):
        return jnp.dot(h_ref[...], wup_ref[:, col:col + fc], preferred_element_type=f32)

    def conv(col, up):
        prev = carry_ref[:, col:col + fc]
        carry_ref[:, col:col + fc] = up[tm - halo:]
        c = cb_ref[:, col:col + fc] + cw_ref[2:3, col:col + fc] * up
        for back in range(1, CONV_WIDTH):
            r = pltpu.roll(up, back, 0)
            head = jnp.where(row < back, pltpu.roll(prev, back, 0), r[:halo])
            c = c + cw_ref[2 - back:3 - back, col:col + fc] * jnp.concatenate([head, r[halo:]], axis=0)
        return c

    cols = list(range(0, d_ff, fc))
    ups = (up_proj(cols[0]), up_proj(d_ff + cols[0]))
    for idx, col in enumerate(cols):
        nxt = (up_proj(cols[idx + 1]), up_proj(d_ff + cols[idx + 1])) if idx + 1 < len(cols) else None
        act_ref[:, col:col + fc] = (_gelu_tanh(conv(col, ups[0])) * conv(d_ff + col, ups[1])).astype(bf16)
        ups = nxt
    halves = [slice(0, tm // 2), slice(tm // 2, tm)]
    xs = [x[r] + jnp.dot(act_ref[r, :], wdn_ref[...], preferred_element_type=f32) for r in halves]
    projs = [jnp.dot(p_ref[r, :].astype(bf16), wproj_ref[...], preferred_element_type=f32) for r in halves]
    gates = [jax.nn.sigmoid(jnp.dot(_rms(xh, pg_ref[...]).astype(bf16), wgate_ref[...],
                                    preferred_element_type=f32)) for xh in xs]
    xs = [xh + g * pr for xh, g, pr in zip(xs, gates, projs)]
    for r, xh in zip(halves, xs):
        o_ref[r, :] = xh
    if has_q or has_kv:
        xn = jnp.concatenate([xh * lax.rsqrt(jnp.mean(xh * xh, axis=-1, keepdims=True) + EPS) for xh in xs],
                             axis=0)
    if has_q:
        q = jnp.dot((xn * qg_ref[...]).astype(bf16), wq_ref[...], preferred_element_type=f32)
        q_ref[...] = _seg_norm(q, seg_ref, qgain_ref).astype(q_ref.dtype)
    if has_kv:
        hk = (xn * kvg_ref[...]).astype(bf16)
        k = jnp.dot(hk, wk_ref[...], preferred_element_type=f32)
        k_ref[...] = _seg_norm(k, seg_ref, kgain_ref).astype(k_ref.dtype)
        _store_vt(vt_ref, lax.dot_general(wvt_ref[...], hk, (((1,), (1,)), ((), ())),
                                          preferred_element_type=f32))


def _ffn_ple(x, p, layer, ffn, ple, seqlen, attn=None, next_q=None, next_kv=None):
    n, d = x.shape
    ffn_g, w_up, conv_w, conv_b, w_down = ffn
    ple_g, w_gate, w_proj = ple
    d_ff = w_down.shape[-2]
    tm = FFN_ROWS
    bsz = n // seqlen
    tps = seqlen // tm
    row = lambda i: (i, 0)

    def full(a):
        if a.ndim == 3:
            return pl.BlockSpec((None,) + a.shape[1:], lambda i: (layer, 0, 0), pipeline_mode=pl.Buffered(1))
        return pl.BlockSpec(a.shape, lambda i: (0, 0), pipeline_mode=pl.Buffered(1))

    vec = lambda a: a.astype(f32)[None, :]
    args = [x, p]
    in_specs = [pl.BlockSpec((tm, d), row), pl.BlockSpec((None, tm, p.shape[2]), lambda i: (layer, i, 0))]
    if attn is not None:
        args += [attn[0], attn[1]]
        in_specs += [pl.BlockSpec((tm, attn[0].shape[1]), row), full(attn[1])]
    weights = [vec(ffn_g), w_up, conv_w.astype(f32), vec(conv_b), w_down, vec(ple_g), w_gate, w_proj]
    out_specs = [pl.BlockSpec((tm, d), row)]
    out_shape = [jax.ShapeDtypeStruct((n, d), f32)]
    if next_q is not None or next_kv is not None:
        weights.append(_segment_ones())
    if next_q is not None:
        weights += [vec(next_q[0]), next_q[1], _segment_gain(next_q[2])]
        out_specs.append(pl.BlockSpec((tm, next_q[1].shape[1]), row))
        out_shape.append(jax.ShapeDtypeStruct((n, next_q[1].shape[1]), bf16))
    if next_kv is not None:
        weights += [vec(next_kv[0]), next_kv[1], next_kv[2], _segment_gain(next_kv[3])]
        out_specs += [pl.BlockSpec((tm, next_kv[1].shape[1]), row),
                      pl.BlockSpec((1, N_HEADS, V_ROWS, tm), lambda i: (i // tps, 0, 0, i % tps))]
        out_shape += [jax.ShapeDtypeStruct((n, next_kv[1].shape[1]), bf16),
                      jax.ShapeDtypeStruct((bsz, N_HEADS, V_ROWS, seqlen), bf16)]
    args += weights
    in_specs += [full(w) for w in weights]
    return pl.pallas_call(
        functools.partial(_ffn_ple_kernel, tps, attn is not None, next_q is not None, next_kv is not None),
        grid=(n // tm,),
        in_specs=in_specs,
        out_specs=out_specs,
        out_shape=out_shape,
        scratch_shapes=[pltpu.VMEM((SUBLANES, 2 * d_ff), f32),
                        pltpu.VMEM((tm, d_ff), bf16),
                        pltpu.VMEM((tm, d), bf16)],
        compiler_params=_cparams("arbitrary"),
        name="ffn_ple",
    )(*args)


def kernel(x, p, ssm_norm, ssm_w_in, ssm_log_dt, ssm_lambda_re, ssm_lambda_im, ssm_b_re, ssm_b_im, ssm_c_re, ssm_c_im, ssm_d, ssm_w_glu, kv_norm, kv_w, k_norm, attn_norm, attn_w_q, q_norm, lambda_q1, lambda_k1, lambda_q2, lambda_k2, subln, attn_w_o, rel_bias, ffn_norm, ffn_w_up, ffn_conv_w, ffn_conv_b, ffn_w_down, ple_norm, ple_w_gate, ple_w_proj):
    bsz, seqlen, d = x.shape
    depth = p.shape[0]
    n_a = ssm_norm.shape[0]
    n = bsz * seqlen
    x = x.reshape(n, d).astype(f32)
    p = p.reshape(depth, n, p.shape[-1])
    nk = N_HEADS * 2 * HEAD_DIM
    assert 1 <= n_a < depth
    diag, corner, far = _bias_tables(rel_bias, ATTN_BLOCK, ATTN_BLOCK)
    tables = jax.vmap(functools.partial(_ssm_tables, n_chunks=seqlen // SSM_T))(
        ssm_log_dt, ssm_lambda_re, ssm_lambda_im, ssm_b_re, ssm_b_im, ssm_c_re, ssm_c_im)
    w_in_t = jnp.swapaxes(ssm_w_in, 1, 2).astype(bf16)
    w_glu, w_q, w_o = ssm_w_glu.astype(bf16), attn_w_q.astype(bf16), attn_w_o.astype(bf16)
    w_up, w_down = ffn_w_up.astype(bf16), ffn_w_down.astype(bf16)
    w_gate, w_proj = ple_w_gate.astype(bf16), ple_w_proj.astype(bf16)
    lam_all = (jnp.exp(jnp.sum(lambda_q1.astype(f32) * lambda_k1.astype(f32), axis=-1))
               - jnp.exp(jnp.sum(lambda_q2.astype(f32) * lambda_k2.astype(f32), axis=-1)))
    q = k = vt = None
    for i in range(depth):
        attn = None
        if i < n_a:
            x3 = x.reshape(bsz, seqlen, d)
            ut = _win_t(x3, ssm_norm[i], w_in_t[i])
            yt = _ssm(ut, tables, i)
            x = _glu_res(x3, yt, ut, ssm_d[i], w_glu[i]).reshape(n, d)
        else:
            j = i - n_a
            lam_init = 0.8 - 0.6 * math.exp(-0.3 * i)
            o = _diff_attention(q, k, vt, diag, corner, far, lam_all[j] + lam_init, subln[j], 1.0 - lam_init,
                                bsz, seqlen, ATTN_BLOCK, ATTN_BLOCK)
            attn = (o, w_o[j])
        j_next = i + 1 - n_a
        next_q = next_kv = None
        if 0 <= j_next < depth - n_a:
            next_q = (attn_norm[j_next], w_q[j_next], q_norm[j_next])
            if j_next == 0:
                next_kv = (kv_norm, kv_w[:, :nk].astype(bf16), kv_w[:, nk:].T.astype(bf16), k_norm)
        outs = _ffn_ple(x, p, i,
                        (ffn_norm[i], w_up, ffn_conv_w[i], ffn_conv_b[i], w_down),
                        (ple_norm[i], w_gate, w_proj),
                        seqlen, attn=attn, next_q=next_q, next_kv=next_kv)
        x = outs[0]
        if next_q is not None:
            q = outs[1]
        if next_kv is not None:
            k, vt = outs[2], outs[3]
    return x.reshape(bsz, seqlen, d)
```

```python
import functools
import math

import jax
import jax.numpy as jnp
from jax import lax
from jax.experimental import pallas as pl
from jax.experimental.pallas import tpu as pltpu

EPS = 1e-6
NEG_INF = -1e30

SSM_GROUP_DIM = 16
N_HEADS = 8
HEAD_DIM = 64
REL_BUCKETS = 32
REL_MAX_EXACT = REL_BUCKETS // 2
REL_MAX_DIST = 128
CONV_WIDTH = 3

LANES = 128
SUBLANES = 8
MXU_DIM = 256
VMEM_LIMIT_BYTES = 56 * 1024 * 1024

SSM_T = MXU_DIM // SSM_GROUP_DIM
SEG_CHUNK = MXU_DIM
ATTN_BLOCK = 256
ATTN_HEADS_PER_STEP = N_HEADS
CORNER = LANES
V_ROWS = 2 * HEAD_DIM + 16
LOG2E = math.log2(math.e)
GLU_SPLIT = 4
FFN_ROWS = 512
FFN_COLS = MXU_DIM

bf16 = jnp.bfloat16
f32 = jnp.float32


def _cparams(*sem):
    return pltpu.CompilerParams(dimension_semantics=sem, vmem_limit_bytes=VMEM_LIMIT_BYTES)


def _rms(xf, g):
    return xf * lax.rsqrt(jnp.mean(xf * xf, axis=-1, keepdims=True) + EPS) * g


def _gelu_tanh(v):
    c = 2.0 * math.sqrt(2.0 / math.pi)
    return v * jax.nn.sigmoid(v * (c + (c * 0.044715) * (v * v)))


def _seg_norm(y, seg_ref, gain_ref):
    cols = []
    for c in range(y.shape[1] // SEG_CHUNK):
        yc = y[:, c * SEG_CHUNK:(c + 1) * SEG_CHUNK]
        ss = jnp.dot((yc * yc).astype(bf16), seg_ref[...], preferred_element_type=f32)
        cols.append(yc * lax.rsqrt(ss * (1.0 / HEAD_DIM) + EPS) * gain_ref[...])
    return jnp.concatenate(cols, axis=1)


def _segment_ones():
    idx = jnp.arange(SEG_CHUNK) // HEAD_DIM
    return (idx[:, None] == idx[None, :]).astype(bf16)


def _segment_gain(g):
    return jnp.tile(g.astype(f32), SEG_CHUNK // HEAD_DIM)[None, :]


def _store_vt(vt_ref, vt):
    hd2 = 2 * HEAD_DIM
    for hh in range(vt_ref.shape[1]):
        vt_ref[0, hh, :hd2, :] = vt[hh * hd2:(hh + 1) * hd2].astype(vt_ref.dtype)
        vt_ref[0, hh, hd2:, :] = jnp.ones((V_ROWS - hd2, vt.shape[1]), vt_ref.dtype)


def _win_kernel(x_ref, g_ref, wt_ref, ut_ref):
    n_chunks, nt, d = x_ref.shape
    g_n = ut_ref.shape[0]
    xt = pltpu.einshape("ktd->tkd", x_ref[...]).reshape(nt * n_chunks, d)
    h = _rms(xt, g_ref[...]).astype(bf16)
    ut = lax.dot_general(wt_ref[...], h, (((1,), (1,)), ((), ())), preferred_element_type=f32)
    for tt in range(nt):
        blk = ut[:, tt * n_chunks:(tt + 1) * n_chunks].reshape(g_n, SSM_GROUP_DIM, n_chunks)
        ut_ref[:, tt * SSM_GROUP_DIM:(tt + 1) * SSM_GROUP_DIM, :] = blk.astype(ut_ref.dtype)


def _by_chunk(x3):
    bsz, seqlen, d = x3.shape
    return x3.reshape(bsz, seqlen // SSM_T, SSM_T // SUBLANES, SUBLANES, d)


def _chunk_spec(x5):
    _, n_chunks, _, nt, d = x5.shape
    return pl.BlockSpec((None, n_chunks, None, nt, d), lambda b, t: (b, 0, t, 0, 0))


def _win_t(x3, g, w_t):
    bsz, seqlen, d = x3.shape
    n_chunks = seqlen // SSM_T
    assert n_chunks == LANES
    g_n = d // SSM_GROUP_DIM
    x5 = _by_chunk(x3)
    return pl.pallas_call(
        _win_kernel,
        grid=(bsz, SSM_T // SUBLANES),
        in_specs=[_chunk_spec(x5),
                  pl.BlockSpec((1, d), lambda b, t: (0, 0)),
                  pl.BlockSpec(w_t.shape, lambda b, t: (0, 0))],
        out_specs=pl.BlockSpec((g_n, SUBLANES * SSM_GROUP_DIM, n_chunks), lambda b, t: (0, t, b)),
        out_shape=jax.ShapeDtypeStruct((g_n, SSM_T * SSM_GROUP_DIM, bsz * n_chunks), bf16),
        compiler_params=_cparams("parallel", "parallel"),
        name="win_t",
    )(x5, g.astype(f32)[None, :], w_t)


def _ssm_kernel(n_rounds, u_ref, mt_ref, w_ref, v_ref, pw_ref, y_ref, xin_ref):
    u0 = u_ref[0]
    u1 = u_ref[1]
    half = LANES
    seg = LANES
    st = (jnp.dot(w_ref[0], u0, preferred_element_type=f32)
          + jnp.dot(w_ref[1], u1, preferred_element_type=f32))
    n_seq = st.shape[1] // seg
    xr = jnp.stack([st[:half, b * seg:(b + 1) * seg] for b in range(n_seq)])
    xi = jnp.stack([st[half:, b * seg:(b + 1) * seg] for b in range(n_seq)])
    chunk = lax.broadcasted_iota(jnp.int32, xr.shape, 2)
    for j in range(n_rounds):
        d = 1 << j
        ar = pw_ref[j, 0]
        ai = pw_ref[j, 1]
        sr = pltpu.roll(xr, d, 2)
        si = pltpu.roll(xi, d, 2)
        xr, xi = xr + (ar * sr - ai * si), xi + (ar * si + ai * sr)
    xr = jnp.where(chunk >= 1, pltpu.roll(xr, 1, 2), 0.0).astype(bf16)
    xi = jnp.where(chunk >= 1, pltpu.roll(xi, 1, 2), 0.0).astype(bf16)
    for b in range(n_seq):
        xin_ref[:half, b * seg:(b + 1) * seg] = xr[b]
        xin_ref[half:, b * seg:(b + 1) * seg] = xi[b]
    ys = jnp.dot(v_ref[...], xin_ref[...], preferred_element_type=f32)
    y_ref[0] = (jnp.dot(mt_ref[0], u0, preferred_element_type=f32) + ys[:MXU_DIM]).astype(y_ref.dtype)
    y_ref[1] = (jnp.dot(mt_ref[1], u1, preferred_element_type=f32) + ys[MXU_DIM:]).astype(y_ref.dtype)


def _ssm_tables(log_dt, lam_re, lam_im, b_re, b_im, c_re, c_im, n_chunks):
    g_n, p_n = lam_re.shape
    t_n, c_n = SSM_T, SSM_GROUP_DIM
    dt = jnp.exp(log_dt.astype(f32))[:, None]
    lr = lam_re.astype(f32)
    li = lam_im.astype(f32)

    def power(n):
        nn = jnp.asarray(n, f32).reshape((-1, 1, 1))
        mag = jnp.exp(lr * dt * nn)
        return mag * jnp.cos(li * dt * nn), mag * jnp.sin(li * dt * nn)

    ab_r, ab_i = power([1])
    ab_r, ab_i = ab_r[0], ab_i[0]
    den = lr * lr + li * li
    f_r = ((ab_r - 1.0) * lr + ab_i * li) / den
    f_i = (ab_i * lr - (ab_r - 1.0) * li) / den
    br = b_re.astype(f32)
    bi = b_im.astype(f32)
    bb_r = f_r[..., None] * br - f_i[..., None] * bi
    bb_i = f_r[..., None] * bi + f_i[..., None] * br
    cr = c_re.astype(f32)
    ci = c_im.astype(f32)

    qr, qi = power(t_n - 1 - jnp.arange(t_n))
    t_r = qr[..., None] * bb_r[None] - qi[..., None] * bb_i[None]
    t_i = qr[..., None] * bb_i[None] + qi[..., None] * bb_r[None]
    kern = (jnp.einsum("gcp,dgpk->gcdk", cr, t_r, precision=lax.Precision.HIGHEST)
            - jnp.einsum("gcp,dgpk->gcdk", ci, t_i, precision=lax.Precision.HIGHEST))
    kern = kern.reshape(g_n, c_n, t_n * c_n)
    kern = jnp.concatenate([kern, jnp.zeros_like(kern)], axis=-1)
    mt = jnp.concatenate([kern[:, :, (t_n - 1 - t) * c_n:(2 * t_n - 1 - t) * c_n] for t in range(t_n)], axis=1)
    w_r = t_r.transpose(1, 2, 0, 3).reshape(g_n, p_n, t_n * c_n)
    w_i = t_i.transpose(1, 2, 0, 3).reshape(g_n, p_n, t_n * c_n)
    pr1, pi1 = power(1 + jnp.arange(t_n))
    pr1 = pr1.transpose(1, 0, 2)
    pi1 = pi1.transpose(1, 0, 2)
    v_r = (cr[:, None] * pr1[:, :, None, :] - ci[:, None] * pi1[:, :, None, :]).reshape(g_n, t_n * c_n, p_n)
    v_i = -(cr[:, None] * pi1[:, :, None, :] + ci[:, None] * pr1[:, :, None, :]).reshape(g_n, t_n * c_n, p_n)

    gp = g_n // 2
    z = jnp.zeros((gp, p_n, t_n * c_n), f32)
    w_r = w_r.reshape(gp, 2, p_n, t_n * c_n)
    w_i = w_i.reshape(gp, 2, p_n, t_n * c_n)
    w0 = jnp.concatenate([w_r[:, 0], z, w_i[:, 0], z], axis=1)
    w1 = jnp.concatenate([z, w_r[:, 1], z, w_i[:, 1]], axis=1)
    w_pack = jnp.stack([w0, w1], axis=1)
    zv = jnp.zeros((gp, t_n * c_n, p_n), f32)
    v_r = v_r.reshape(gp, 2, t_n * c_n, p_n)
    v_i = v_i.reshape(gp, 2, t_n * c_n, p_n)
    v_pack = jnp.concatenate([
        jnp.concatenate([v_r[:, 0], zv, v_i[:, 0], zv], axis=-1),
        jnp.concatenate([zv, v_r[:, 1], zv, v_i[:, 1]], axis=-1)], axis=1)
    n_rounds = (n_chunks - 1).bit_length()
    sr, si = power(t_n * (2 ** jnp.arange(max(n_rounds, 1))))
    pw = jnp.stack([sr.reshape(-1, gp, 2 * p_n), si.reshape(-1, gp, 2 * p_n)], axis=2)
    reach = (jnp.arange(LANES)[None, :] >= (2 ** jnp.arange(pw.shape[0]))[:, None]).astype(f32)
    pw = pw.transpose(1, 0, 2, 3)[..., None] * reach[None, :, None, None, :]
    return mt.reshape(gp, 2, t_n * c_n, t_n * c_n).astype(bf16), w_pack.astype(bf16), v_pack.astype(bf16), pw


def _ssm(ut, tables, layer):
    mt, w_pack, v_pack, pw = tables
    n_rounds = pw.shape[2]
    g_n, k, cols = ut.shape
    gp = g_n // 2
    return pl.pallas_call(
        functools.partial(_ssm_kernel, n_rounds),
        grid=(gp,),
        in_specs=[pl.BlockSpec((2, k, cols), lambda g: (g, 0, 0)),
                  pl.BlockSpec((None, None, 2, k, k), lambda g: (layer, g, 0, 0, 0)),
                  pl.BlockSpec((None, None, 2, k, k), lambda g: (layer, g, 0, 0, 0)),
                  pl.BlockSpec((None, None, 2 * k, k), lambda g: (layer, g, 0, 0)),
                  pl.BlockSpec((None, None) + pw.shape[2:], lambda g: (layer, g, 0, 0, 0, 0))],
        out_specs=pl.BlockSpec((2, k, cols), lambda g: (g, 0, 0)),
        out_shape=jax.ShapeDtypeStruct((g_n, k, cols), bf16),
        scratch_shapes=[pltpu.VMEM((k, cols), bf16)],
        compiler_params=_cparams("parallel"),
        name="ssm",
    )(ut, mt, w_pack, v_pack, pw)


def _glu_kernel(x_ref, y_ref, u_ref, d_ref, w_ref, o_ref):
    n_chunks, nt, dm = x_ref.shape
    g_n = y_ref.shape[0]
    per = nt // GLU_SPLIT
    upd = []
    for s in range(GLU_SPLIT):
        cols = []
        for tt in range(s * per, (s + 1) * per):
            sl = slice(tt * SSM_GROUP_DIM, (tt + 1) * SSM_GROUP_DIM)
            y = y_ref[:, sl, :].astype(f32) + d_ref[...] * u_ref[:, sl, :].astype(f32)
            cols.append(_gelu_tanh(y).reshape(g_n * SSM_GROUP_DIM, n_chunks))
        gt = jnp.concatenate(cols, axis=1)
        r = jnp.dot(gt.T.astype(bf16), w_ref[...], preferred_element_type=f32)
        upd.append(r[:, :dm] * jax.nn.sigmoid(r[:, dm:]))
    upd = jnp.concatenate(upd, axis=0).reshape(nt, n_chunks, dm)
    o_ref[...] = x_ref[...] + pltpu.einshape("tkd->ktd", upd)


def _glu_res(x3, yt, ut, d_skip, w_glu):
    bsz, seqlen, d = x3.shape
    g_n, _, cols = yt.shape
    n_chunks = cols // bsz
    d_tab = jnp.broadcast_to(d_skip.astype(f32).reshape(g_n, SSM_GROUP_DIM, 1), (g_n, SSM_GROUP_DIM, n_chunks))
    feat = pl.BlockSpec((g_n, SUBLANES * SSM_GROUP_DIM, n_chunks), lambda b, t: (0, t, b))
    x5 = _by_chunk(x3)
    out = pl.pallas_call(
        _glu_kernel,
        grid=(bsz, SSM_T // SUBLANES),
        in_specs=[_chunk_spec(x5), feat, feat,
                  pl.BlockSpec(d_tab.shape, lambda b, t: (0, 0, 0)),
                  pl.BlockSpec(w_glu.shape, lambda b, t: (0, 0))],
        out_specs=_chunk_spec(x5),
        out_shape=jax.ShapeDtypeStruct(x5.shape, f32),
        compiler_params=_cparams("parallel", "parallel"),
        name="glu_res",
    )(x5, yt, ut, d_tab, w_glu)
    return out.reshape(bsz, seqlen, d)


def _attn_kernel(tq, tk, out_scale, lam_ref, far_ref, q_ref, k_ref, vt_ref, diag_ref, corner_ref, sub_ref,
                 o_ref, qm_ref, m_ref, acc_ref, s_ref):
    hd2 = 2 * HEAD_DIM
    heads = range(vt_ref.shape[1])
    lane = lax.broadcasted_iota(jnp.int32, (tq, hd2), 1)
    far = [far_ref[pl.program_id(1) * len(heads) + hh] for hh in heads]

    def q_block(qi, carry):
        rows = pl.ds(pl.multiple_of(qi * tq, tq), tq)
        for hh in heads:
            q = q_ref[0, rows, hh * hd2:(hh + 1) * hd2].astype(f32) * (HEAD_DIM ** -0.5 * LOG2E)
            qm_ref[hh, :tq] = jnp.where(lane < HEAD_DIM, q, 0.0).astype(bf16)
            qm_ref[hh, tq:] = jnp.where(lane >= HEAD_DIM, q, 0.0).astype(bf16)
        m_ref[...] = jnp.full_like(m_ref, NEG_INF)
        acc_ref[...] = jnp.zeros_like(acc_ref)

        def fill(slot, ki):
            for hh in heads:
                k = k_ref[0, pl.ds(pl.multiple_of(ki * tk, tk), tk), hh * hd2:(hh + 1) * hd2]
                s_ref[slot, hh] = lax.dot_general(k, qm_ref[hh], (((1,), (1,)), ((), ())),
                                                  preferred_element_type=f32)

        def accumulate(hh, ki, parts):
            s = parts[0] if len(parts) == 1 else jnp.concatenate(parts, axis=0)
            vt = vt_ref[0, hh, :, pl.ds(pl.multiple_of(ki * tk, tk), len(parts) * tk)]
            m_prev = m_ref[hh]
            m_new = jnp.maximum(m_prev, jnp.max(s, axis=0, keepdims=True) + far[hh])
            p = jnp.exp2(s - (m_new - far[hh]))
            acc_ref[hh] = (jnp.exp2(m_prev - m_new) * acc_ref[hh]
                           + jnp.dot(vt, p.astype(bf16), preferred_element_type=f32))
            m_ref[hh] = m_new

        def neighbour(hh, s, near):
            c = near * corner_ref[hh]
            lo = s[tk - CORNER:]
            lo = jnp.concatenate([lo[:, :CORNER] + c, lo[:, CORNER:tq], lo[:, tq:tq + CORNER] + c,
                                  lo[:, tq + CORNER:]], axis=1)
            return jnp.concatenate([s[:tk - CORNER], lo], axis=0)

        def diagonal(hh, s):
            b = diag_ref[hh]
            return s + jnp.concatenate([b, b], axis=1)

        def pair(i, carry):
            fill(0, 2 * i)
            fill(1, 2 * i + 1)
            near = jnp.where(2 * i + 1 == qi - 1, 1.0, 0.0)
            for hh in heads:
                accumulate(hh, 2 * i, [s_ref[0, hh], neighbour(hh, s_ref[1, hh], near)])
            return carry

        lax.fori_loop(0, qi // 2, pair, 0)

        @pl.when(qi % 2 == 1)
        def _():
            fill(0, qi - 1)
            fill(1, qi)
            for hh in heads:
                accumulate(hh, qi - 1, [neighbour(hh, s_ref[0, hh], 1.0), diagonal(hh, s_ref[1, hh])])

        @pl.when(qi % 2 == 0)
        def _():
            fill(0, qi)
            for hh in heads:
                accumulate(hh, qi, [diagonal(hh, s_ref[0, hh])])

        for hh in heads:
            acc = acc_ref[hh]
            ot = acc[:hd2] / acc[hd2:hd2 + 1]
            ot = ot[:, :tq] - lam_ref[0] * ot[:, tq:]
            ot = ot * lax.rsqrt(jnp.mean(ot * ot, axis=0, keepdims=True) + EPS)
            o_ref[0, rows, hh * hd2:(hh + 1) * hd2] = (ot.T * sub_ref[...] * out_scale).astype(o_ref.dtype)
        return carry

    lax.fori_loop(0, k_ref.shape[1] // tq, q_block, 0)


def _bias_tables(rel_bias, tq, tk):
    n = jnp.arange(2 * tq)
    nf = jnp.maximum(n, REL_MAX_EXACT).astype(f32)
    large = REL_MAX_EXACT + (jnp.log(nf / REL_MAX_EXACT) / math.log(REL_MAX_DIST / REL_MAX_EXACT)
                             * (REL_BUCKETS - REL_MAX_EXACT)).astype(jnp.int32)
    large = jnp.minimum(large, REL_BUCKETS - 1)
    bucket = jnp.where(n < REL_MAX_EXACT, n, large)
    onehot = (bucket[:, None] == jnp.arange(REL_BUCKETS)[None, :]).astype(f32)
    by_dist = jnp.dot(onehot, rel_bias.astype(f32), precision=lax.Precision.HIGHEST).T * LOG2E
    n_heads = by_dist.shape[0]
    neg = jnp.full((n_heads, tk), NEG_INF, f32)
    period = tq + tk
    tiles = []
    for d, tail in ((0, neg), (1, by_dist[:, tq - tk:tq])):
        r = jnp.concatenate([by_dist[:, d * tq:(d + 1) * tq], tail], axis=1)
        t = jnp.tile(r, (1, tk))[:, :tk * (period - 1)].reshape(n_heads, tk, period - 1)
        tiles.append(t[:, :, :tq])
    far = rel_bias.astype(f32)[REL_BUCKETS - 1] * LOG2E
    corner = tiles[1][:, tk - CORNER:, :CORNER] - far[:, None, None]
    return tiles[0] - far[:, None, None], corner, far


def _far_bucket_from():
    n = REL_MAX_EXACT
    while REL_MAX_EXACT + int(math.log(n / REL_MAX_EXACT) / math.log(REL_MAX_DIST / REL_MAX_EXACT)
                              * (REL_BUCKETS - REL_MAX_EXACT)) < REL_BUCKETS - 1:
        n += 1
    return n


def _diff_attention(q, k, vt, diag, corner, far, lam, subln, out_scale, bsz, seqlen, tq, tk):
    assert tq == tk and tq - CORNER + 1 >= _far_bucket_from()
    hd2 = 2 * HEAD_DIM
    q3 = q.reshape(bsz, seqlen, N_HEADS * hd2)
    k3 = k.reshape(bsz, seqlen, N_HEADS * hd2)
    smem = pl.BlockSpec(memory_space=pltpu.SMEM)
    hps = ATTN_HEADS_PER_STEP
    o = pl.pallas_call(
        functools.partial(_attn_kernel, tq, tk, out_scale),
        grid=(bsz, N_HEADS // hps),
        in_specs=[smem, smem,
                  pl.BlockSpec((1, seqlen, hps * hd2), lambda b, h: (b, 0, h)),
                  pl.BlockSpec((1, seqlen, hps * hd2), lambda b, h: (b, 0, h)),
                  pl.BlockSpec((1, hps, V_ROWS, seqlen), lambda b, h: (b, h, 0, 0)),
                  pl.BlockSpec((hps, tk, tq), lambda b, h: (h, 0, 0)),
                  pl.BlockSpec((hps, CORNER, CORNER), lambda b, h: (h, 0, 0)),
                  pl.BlockSpec((1, hd2), lambda b, h: (0, 0))],
        out_specs=pl.BlockSpec((1, seqlen, hps * hd2), lambda b, h: (b, 0, h)),
        out_shape=jax.ShapeDtypeStruct((bsz, seqlen, N_HEADS * hd2), bf16),
        scratch_shapes=[pltpu.VMEM((hps, 2 * tq, hd2), bf16), pltpu.VMEM((hps, 1, 2 * tq), f32),
                        pltpu.VMEM((hps, V_ROWS, 2 * tq), f32), pltpu.VMEM((2, hps, tk, 2 * tq), f32)],
        compiler_params=_cparams("parallel", "parallel"),
        name="diff_attn",
    )(lam.reshape(1).astype(f32), far, q3, k3, vt, diag, corner, subln.astype(f32)[None, :])
    return o.reshape(bsz * seqlen, N_HEADS * hd2)


def _ffn_ple_kernel(tiles_per_seq, has_attn, has_q, has_kv, *refs):
    refs = list(refs)
    x_ref, p_ref = refs[:2]
    del refs[:2]
    if has_attn:
        a_ref, wo_ref = refs[:2]
        del refs[:2]
    fg_ref, wup_ref, cw_ref, cb_ref, wdn_ref, pg_ref, wgate_ref, wproj_ref = refs[:8]
    del refs[:8]
    if has_q or has_kv:
        seg_ref = refs.pop(0)
    if has_q:
        qg_ref, wq_ref, qgain_ref = refs[:3]
        del refs[:3]
    if has_kv:
        kvg_ref, wk_ref, wvt_ref, kgain_ref = refs[:4]
        del refs[:4]
    o_ref = refs.pop(0)
    if has_q:
        q_ref = refs.pop(0)
    if has_kv:
        k_ref, vt_ref = refs[:2]
        del refs[:2]
    carry_ref, act_ref, h_ref = refs

    i = pl.program_id(0)
    tm = x_ref.shape[0]
    d_ff = wdn_ref.shape[0]
    fc = FFN_COLS
    halo = SUBLANES

    @pl.when(i % tiles_per_seq == 0)
    def _():
        carry_ref[...] = jnp.zeros_like(carry_ref)

    x = x_ref[...]
    if has_attn:
        x = x + jnp.dot(a_ref[...], wo_ref[...], preferred_element_type=f32)
    h_ref[...] = _rms(x, fg_ref[...]).astype(bf16)
    row = lax.broadcasted_iota(jnp.int32, (halo, fc), 0)

    def up_proj(col):
        return jnp.dot(h_ref[...], wup_ref[:, col:col + fc], preferred_element_type=f32)

    def conv(col, up):
        prev = carry_ref[:, col:col + fc]
        carry_ref[:, col:col + fc] = up[tm - halo:]
        c = cb_ref[:, col:col + fc] + cw_ref[2:3, col:col + fc] * up
        for back in range(1, CONV_WIDTH):
            r = pltpu.roll(up, back, 0)
            head = jnp.where(row < back, pltpu.roll(prev, back, 0), r[:halo])
            c = c + cw_ref[2 - back:3 - back, col:col + fc] * jnp.concatenate([head, r[halo:]], axis=0)
        return c

    cols = list(range(0, d_ff, fc))
    ups = (up_proj(cols[0]), up_proj(d_ff + cols[0]))
    for idx, col in enumerate(cols):
        nxt = (up_proj(cols[idx + 1]), up_proj(d_ff + cols[idx + 1])) if idx + 1 < len(cols) else None
        act_ref[:, col:col + fc] = (_gelu_tanh(conv(col, ups[0])) * conv(d_ff + col, ups[1])).astype(bf16)
        ups = nxt
    halves = [slice(0, tm // 2), slice(tm // 2, tm)]
    xs = [x[r] + jnp.dot(act_ref[r, :], wdn_ref[...], preferred_element_type=f32) for r in halves]
    projs = [jnp.dot(p_ref[r, :].astype(bf16), wproj_ref[...], preferred_element_type=f32) for r in halves]
    gates = [jax.nn.sigmoid(jnp.dot(_rms(xh, pg_ref[...]).astype(bf16), wgate_ref[...],
                                    preferred_element_type=f32)) for xh in xs]
    xs = [xh + g * pr for xh, g, pr in zip(xs, gates, projs)]
    for r, xh in zip(halves, xs):
        o_ref[r, :] = xh
    if has_q or has_kv:
        xn = jnp.concatenate([xh * lax.rsqrt(jnp.mean(xh * xh, axis=-1, keepdims=True) + EPS) for xh in xs],
                             axis=0)
    if has_q:
        q = jnp.dot((xn * qg_ref[...]).astype(bf16), wq_ref[...], preferred_element_type=f32)
        q_ref[...] = _seg_norm(q, seg_ref, qgain_ref).astype(q_ref.dtype)
    if has_kv:
        hk = (xn * kvg_ref[...]).astype(bf16)
        k = jnp.dot(hk, wk_ref[...], preferred_element_type=f32)
        k_ref[...] = _seg_norm(k, seg_ref, kgain_ref).astype(k_ref.dtype)
        _store_vt(vt_ref, lax.dot_general(wvt_ref[...], hk, (((1,), (1,)), ((), ())),
                                          preferred_element_type=f32))


def _ffn_ple(x, p, layer, ffn, ple, seqlen, attn=None, next_q=None, next_kv=None):
    n, d = x.shape
    ffn_g, w_up, conv_w, conv_b, w_down = ffn
    ple_g, w_gate, w_proj = ple
    d_ff = w_down.shape[-2]
    tm = FFN_ROWS
    bsz = n // seqlen
    tps = seqlen // tm
    row = lambda i: (i, 0)

    def full(a):
        if a.ndim == 3:
            return pl.BlockSpec((None,) + a.shape[1:], lambda i: (layer, 0, 0), pipeline_mode=pl.Buffered(1))
        return pl.BlockSpec(a.shape, lambda i: (0, 0), pipeline_mode=pl.Buffered(1))

    vec = lambda a: a.astype(f32)[None, :]
    args = [x, p]
    in_specs = [pl.BlockSpec((tm, d), row), pl.BlockSpec((None, tm, p.shape[2]), lambda i: (layer, i, 0))]
    if attn is not None:
        args += [attn[0], attn[1]]
        in_specs += [pl.BlockSpec((tm, attn[0].shape[1]), row), full(attn[1])]
    weights = [vec(ffn_g), w_up, conv_w.astype(f32), vec(conv_b), w_down, vec(ple_g), w_gate, w_proj]
    out_specs = [pl.BlockSpec((tm, d), row)]
    out_shape = [jax.ShapeDtypeStruct((n, d), f32)]
    if next_q is not None or next_kv is not None:
        weights.append(_segment_ones())
    if next_q is not None:
        weights += [vec(next_q[0]), next_q[1], _segment_gain(next_q[2])]
        out_specs.append(pl.BlockSpec((tm, next_q[1].shape[1]), row))
        out_shape.append(jax.ShapeDtypeStruct((n, next_q[1].shape[1]), bf16))
    if next_kv is not None:
        weights += [vec(next_kv[0]), next_kv[1], next_kv[2], _segment_gain(next_kv[3])]
        out_specs += [pl.BlockSpec((tm, next_kv[1].shape[1]), row),
                      pl.BlockSpec((1, N_HEADS, V_ROWS, tm), lambda i: (i // tps, 0, 0, i % tps))]
        out_shape += [jax.ShapeDtypeStruct((n, next_kv[1].shape[1]), bf16),
                      jax.ShapeDtypeStruct((bsz, N_HEADS, V_ROWS, seqlen), bf16)]
    args += weights
    in_specs += [full(w) for w in weights]
    return pl.pallas_call(
        functools.partial(_ffn_ple_kernel, tps, attn is not None, next_q is not None, next_kv is not None),
        grid=(n // tm,),
        in_specs=in_specs,
        out_specs=out_specs,
        out_shape=out_shape,
        scratch_shapes=[pltpu.VMEM((SUBLANES, 2 * d_ff), f32),
                        pltpu.VMEM((tm, d_ff), bf16),
                        pltpu.VMEM((tm, d), bf16)],
        compiler_params=_cparams("arbitrary"),
        name="ffn_ple",
    )(*args)


def kernel(x, p, ssm_norm, ssm_w_in, ssm_log_dt, ssm_lambda_re, ssm_lambda_im, ssm_b_re, ssm_b_im, ssm_c_re, ssm_c_im, ssm_d, ssm_w_glu, kv_norm, kv_w, k_norm, attn_norm, attn_w_q, q_norm, lambda_q1, lambda_k1, lambda_q2, lambda_k2, subln, attn_w_o, rel_bias, ffn_norm, ffn_w_up, ffn_conv_w, ffn_conv_b, ffn_w_down, ple_norm, ple_w_gate, ple_w_proj):
    bsz, seqlen, d = x.shape
    depth = p.shape[0]
    n_a = ssm_norm.shape[0]
    n = bsz * seqlen
    x = x.reshape(n, d).astype(f32)
    p = p.reshape(depth, n, p.shape[-1])
    nk = N_HEADS * 2 * HEAD_DIM
    assert 1 <= n_a < depth
    diag, corner, far = _bias_tables(rel_bias, ATTN_BLOCK, ATTN_BLOCK)
    tables = jax.vmap(functools.partial(_ssm_tables, n_chunks=seqlen // SSM_T))(
        ssm_log_dt, ssm_lambda_re, ssm_lambda_im, ssm_b_re, ssm_b_im, ssm_c_re, ssm_c_im)
    w_in_t = jnp.swapaxes(ssm_w_in, 1, 2).astype(bf16)
    w_glu, w_q, w_o = ssm_w_glu.astype(bf16), attn_w_q.astype(bf16), attn_w_o.astype(bf16)
    w_up, w_down = ffn_w_up.astype(bf16), ffn_w_down.astype(bf16)
    w_gate, w_proj = ple_w_gate.astype(bf16), ple_w_proj.astype(bf16)
    lam_all = (jnp.exp(jnp.sum(lambda_q1.astype(f32) * lambda_k1.astype(f32), axis=-1))
               - jnp.exp(jnp.sum(lambda_q2.astype(f32) * lambda_k2.astype(f32), axis=-1)))
    q = k = vt = None
    for i in range(depth):
        attn = None
        if i < n_a:
            x3 = x.reshape(bsz, seqlen, d)
            ut = _win_t(x3, ssm_norm[i], w_in_t[i])
            yt = _ssm(ut, tables, i)
            x = _glu_res(x3, yt, ut, ssm_d[i], w_glu[i]).reshape(n, d)
        else:
            j = i - n_a
            lam_init = 0.8 - 0.6 * math.exp(-0.3 * i)
            o = _diff_attention(q, k, vt, diag, corner, far, lam_all[j] + lam_init, subln[j], 1.0 - lam_init,
                                bsz, seqlen, ATTN_BLOCK, ATTN_BLOCK)
            attn = (o, w_o[j])
        j_next = i + 1 - n_a
        next_q = next_kv = None
        if 0 <= j_next < depth - n_a:
            next_q = (attn_norm[j_next], w_q[j_next], q_norm[j_next])
            if j_next == 0:
                next_kv = (kv_norm, kv_w[:, :nk].astype(bf16), kv_w[:, nk:].T.astype(bf16), k_norm)
        outs = _ffn_ple(x, p, i,
                        (ffn_norm[i], w_up, ffn_conv_w[i], ffn_conv_b[i], w_down),
                        (ple_norm[i], w_gate, w_proj),
                        seqlen, attn=attn, next_q=next_q, next_kv=next_kv)
        x = outs[0]
        if next_q is not None:
            q = outs[1]
        if next_kv is not None:
            k, vt = outs[2], outs[3]
    return x.reshape(bsz, seqlen, d)
```

```python
import functools
import math

import jax
import jax.numpy as jnp
from jax import lax
from jax.experimental import pallas as pl
from jax.experimental.pallas import tpu as pltpu

EPS = 1e-6
NEG_INF = -1e30

SSM_GROUP_DIM = 16
N_HEADS = 8
HEAD_DIM = 64
REL_BUCKETS = 32
REL_MAX_EXACT = REL_BUCKETS // 2
REL_MAX_DIST = 128
CONV_WIDTH = 3

LANES = 128
SUBLANES = 8
MXU_DIM = 256
VMEM_LIMIT_BYTES = 56 * 1024 * 1024

SSM_T = MXU_DIM // SSM_GROUP_DIM
SEG_CHUNK = MXU_DIM
ATTN_BLOCK = 256
ATTN_HEADS_PER_STEP = N_HEADS
CORNER = LANES
V_ROWS = 2 * HEAD_DIM + 16
LOG2E = math.log2(math.e)
GLU_SPLIT = 4
FFN_ROWS = 512
FFN_COLS = MXU_DIM

bf16 = jnp.bfloat16
f32 = jnp.float32


def _cparams(*sem):
    return pltpu.CompilerParams(dimension_semantics=sem, vmem_limit_bytes=VMEM_LIMIT_BYTES)


def _rms(xf, g):
    return xf * lax.rsqrt(jnp.mean(xf * xf, axis=-1, keepdims=True) + EPS) * g


def _gelu_tanh(v):
    c = 2.0 * math.sqrt(2.0 / math.pi)
    return v * jax.nn.sigmoid(v * (c + (c * 0.044715) * (v * v)))


def _seg_norm(y, seg_ref, gain_ref):
    cols = []
    for c in range(y.shape[1] // SEG_CHUNK):
        yc = y[:, c * SEG_CHUNK:(c + 1) * SEG_CHUNK]
        ss = jnp.dot((yc * yc).astype(bf16), seg_ref[...], preferred_element_type=f32)
        cols.append(yc * lax.rsqrt(ss * (1.0 / HEAD_DIM) + EPS) * gain_ref[...])
    return jnp.concatenate(cols, axis=1)


def _segment_ones():
    idx = jnp.arange(SEG_CHUNK) // HEAD_DIM
    return (idx[:, None] == idx[None, :]).astype(bf16)


def _segment_gain(g):
    return jnp.tile(g.astype(f32), SEG_CHUNK // HEAD_DIM)[None, :]


def _store_vt(vt_ref, vt):
    hd2 = 2 * HEAD_DIM
    for hh in range(vt_ref.shape[1]):
        vt_ref[0, hh, :hd2, :] = vt[hh * hd2:(hh + 1) * hd2].astype(vt_ref.dtype)
        vt_ref[0, hh, hd2:, :] = jnp.ones((V_ROWS - hd2, vt.shape[1]), vt_ref.dtype)


def _win_kernel(x_ref, g_ref, wt_ref, ut_ref):
    n_chunks, nt, d = x_ref.shape
    g_n = ut_ref.shape[0]
    xt = pltpu.einshape("ktd->tkd", x_ref[...]).reshape(nt * n_chunks, d)
    h = _rms(xt, g_ref[...]).astype(bf16)
    ut = lax.dot_general(wt_ref[...], h, (((1,), (1,)), ((), ())), preferred_element_type=f32)
    for tt in range(nt):
        blk = ut[:, tt * n_chunks:(tt + 1) * n_chunks].reshape(g_n, SSM_GROUP_DIM, n_chunks)
        ut_ref[:, tt * SSM_GROUP_DIM:(tt + 1) * SSM_GROUP_DIM, :] = blk.astype(ut_ref.dtype)


def _by_chunk(x3):
    bsz, seqlen, d = x3.shape
    return x3.reshape(bsz, seqlen // SSM_T, SSM_T // SUBLANES, SUBLANES, d)


def _chunk_spec(x5):
    _, n_chunks, _, nt, d = x5.shape
    return pl.BlockSpec((None, n_chunks, None, nt, d), lambda b, t: (b, 0, t, 0, 0))


def _win_t(x3, g, w_t):
    bsz, seqlen, d = x3.shape
    n_chunks = seqlen // SSM_T
    assert n_chunks == LANES
    g_n = d // SSM_GROUP_DIM
    x5 = _by_chunk(x3)
    return pl.pallas_call(
        _win_kernel,
        grid=(bsz, SSM_T // SUBLANES),
        in_specs=[_chunk_spec(x5),
                  pl.BlockSpec((1, d), lambda b, t: (0, 0)),
                  pl.BlockSpec(w_t.shape, lambda b, t: (0, 0))],
        out_specs=pl.BlockSpec((g_n, SUBLANES * SSM_GROUP_DIM, n_chunks), lambda b, t: (0, t, b)),
        out_shape=jax.ShapeDtypeStruct((g_n, SSM_T * SSM_GROUP_DIM, bsz * n_chunks), bf16),
        compiler_params=_cparams("parallel", "parallel"),
        name="win_t",
    )(x5, g.astype(f32)[None, :], w_t)


def _ssm_kernel(n_rounds, u_ref, mt_ref, w_ref, v_ref, pw_ref, y_ref, xin_ref):
    u0 = u_ref[0]
    u1 = u_ref[1]
    half = LANES
    seg = LANES
    st = (jnp.dot(w_ref[0], u0, preferred_element_type=f32)
          + jnp.dot(w_ref[1], u1, preferred_element_type=f32))
    n_seq = st.shape[1] // seg
    xr = jnp.stack([st[:half, b * seg:(b + 1) * seg] for b in range(n_seq)])
    xi = jnp.stack([st[half:, b * seg:(b + 1) * seg] for b in range(n_seq)])
    chunk = lax.broadcasted_iota(jnp.int32, xr.shape, 2)
    for j in range(n_rounds):
        d = 1 << j
        ar = pw_ref[j, 0]
        ai = pw_ref[j, 1]
        sr = pltpu.roll(xr, d, 2)
        si = pltpu.roll(xi, d, 2)
        xr, xi = xr + (ar * sr - ai * si), xi + (ar * si + ai * sr)
    xr = jnp.where(chunk >= 1, pltpu.roll(xr, 1, 2), 0.0).astype(bf16)
    xi = jnp.where(chunk >= 1, pltpu.roll(xi, 1, 2), 0.0).astype(bf16)
    for b in range(n_seq):
        xin_ref[:half, b * seg:(b + 1) * seg] = xr[b]
        xin_ref[half:, b * seg:(b + 1) * seg] = xi[b]
    ys = jnp.dot(v_ref[...], xin_ref[...], preferred_element_type=f32)
    y_ref[0] = (jnp.dot(mt_ref[0], u0, preferred_element_type=f32) + ys[:MXU_DIM]).astype(y_ref.dtype)
    y_ref[1] = (jnp.dot(mt_ref[1], u1, preferred_element_type=f32) + ys[MXU_DIM:]).astype(y_ref.dtype)


def _ssm_tables(log_dt, lam_re, lam_im, b_re, b_im, c_re, c_im, n_chunks):
    g_n, p_n = lam_re.shape
    t_n, c_n = SSM_T, SSM_GROUP_DIM
    dt = jnp.exp(log_dt.astype(f32))[:, None]
    lr = lam_re.astype(f32)
    li = lam_im.astype(f32)

    def power(n):
        nn = jnp.asarray(n, f32).reshape((-1, 1, 1))
        mag = jnp.exp(lr * dt * nn)
        return mag * jnp.cos(li * dt * nn), mag * jnp.sin(li * dt * nn)

    ab_r, ab_i = power([1])
    ab_r, ab_i = ab_r[0], ab_i[0]
    den = lr * lr + li * li
    f_r = ((ab_r - 1.0) * lr + ab_i * li) / den
    f_i = (ab_i * lr - (ab_r - 1.0) * li) / den
    br = b_re.astype(f32)
    bi = b_im.astype(f32)
    bb_r = f_r[..., None] * br - f_i[..., None] * bi
    bb_i = f_r[..., None] * bi + f_i[..., None] * br
    cr = c_re.astype(f32)
    ci = c_im.astype(f32)

    qr, qi = power(t_n - 1 - jnp.arange(t_n))
    t_r = qr[..., None] * bb_r[None] - qi[..., None] * bb_i[None]
    t_i = qr[..., None] * bb_i[None] + qi[..., None] * bb_r[None]
    kern = (jnp.einsum("gcp,dgpk->gcdk", cr, t_r, precision=lax.Precision.HIGHEST)
            - jnp.einsum("gcp,dgpk->gcdk", ci, t_i, precision=lax.Precision.HIGHEST))
    kern = kern.reshape(g_n, c_n, t_n * c_n)
    kern = jnp.concatenate([kern, jnp.zeros_like(kern)], axis=-1)
    mt = jnp.concatenate([kern[:, :, (t_n - 1 - t) * c_n:(2 * t_n - 1 - t) * c_n] for t in range(t_n)], axis=1)
    w_r = t_r.transpose(1, 2, 0, 3).reshape(g_n, p_n, t_n * c_n)
    w_i = t_i.transpose(1, 2, 0, 3).reshape(g_n, p_n, t_n * c_n)
    pr1, pi1 = power(1 + jnp.arange(t_n))
    pr1 = pr1.transpose(1, 0, 2)
    pi1 = pi1.transpose(1, 0, 2)
    v_r = (cr[:, None] * pr1[:, :, None, :] - ci[:, None] * pi1[:, :, None, :]).reshape(g_n, t_n * c_n, p_n)
    v_i = -(cr[:, None] * pi1[:, :, None, :] + ci[:, None] * pr1[:, :, None, :]).reshape(g_n, t_n * c_n, p_n)

    gp = g_n // 2
    z = jnp.zeros((gp, p_n, t_n * c_n), f32)
    w_r = w_r.reshape(gp, 2, p_n, t_n * c_n)
    w_i = w_i.reshape(gp, 2, p_n, t_n * c_n)
    w0 = jnp.concatenate([w_r[:, 0], z, w_i[:, 0], z], axis=1)
    w1 = jnp.concatenate([z, w_r[:, 1], z, w_i[:, 1]], axis=1)
    w_pack = jnp.stack([w0, w1], axis=1)
    zv = jnp.zeros((gp, t_n * c_n, p_n), f32)
    v_r = v_r.reshape(gp, 2, t_n * c_n, p_n)
    v_i = v_i.reshape(gp, 2, t_n * c_n, p_n)
    v_pack = jnp.concatenate([
        jnp.concatenate([v_r[:, 0], zv, v_i[:, 0], zv], axis=-1),
        jnp.concatenate([zv, v_r[:, 1], zv, v_i[:, 1]], axis=-1)], axis=1)
    n_rounds = (n_chunks - 1).bit_length()
    sr, si = power(t_n * (2 ** jnp.arange(max(n_rounds, 1))))
    pw = jnp.stack([sr.reshape(-1, gp, 2 * p_n), si.reshape(-1, gp, 2 * p_n)], axis=2)
    reach = (jnp.arange(LANES)[None, :] >= (2 ** jnp.arange(pw.shape[0]))[:, None]).astype(f32)
    pw = pw.transpose(1, 0, 2, 3)[..., None] * reach[None, :, None, None, :]
    return mt.reshape(gp, 2, t_n * c_n, t_n * c_n).astype(bf16), w_pack.astype(bf16), v_pack.astype(bf16), pw


def _ssm(ut, tables, layer):
    mt, w_pack, v_pack, pw = tables
    n_rounds = pw.shape[2]
    g_n, k, cols = ut.shape
    gp = g_n // 2
    return pl.pallas_call(
        functools.partial(_ssm_kernel, n_rounds),
        grid=(gp,),
        in_specs=[pl.BlockSpec((2, k, cols), lambda g: (g, 0, 0)),
                  pl.BlockSpec((None, None, 2, k, k), lambda g: (layer, g, 0, 0, 0)),
                  pl.BlockSpec((None, None, 2, k, k), lambda g: (layer, g, 0, 0, 0)),
                  pl.BlockSpec((None, None, 2 * k, k), lambda g: (layer, g, 0, 0)),
                  pl.BlockSpec((None, None) + pw.shape[2:], lambda g: (layer, g, 0, 0, 0, 0))],
        out_specs=pl.BlockSpec((2, k, cols), lambda g: (g, 0, 0)),
        out_shape=jax.ShapeDtypeStruct((g_n, k, cols), bf16),
        scratch_shapes=[pltpu.VMEM((k, cols), bf16)],
        compiler_params=_cparams("parallel"),
        name="ssm",
    )(ut, mt, w_pack, v_pack, pw)


def _glu_kernel(x_ref, y_ref, u_ref, d_ref, w_ref, o_ref):
    n_chunks, nt, dm = x_ref.shape
    g_n = y_ref.shape[0]
    per = nt // GLU_SPLIT
    upd = []
    for s in range(GLU_SPLIT):
        cols = []
        for tt in range(s * per, (s + 1) * per):
            sl = slice(tt * SSM_GROUP_DIM, (tt + 1) * SSM_GROUP_DIM)
            y = y_ref[:, sl, :].astype(f32) + d_ref[...] * u_ref[:, sl, :].astype(f32)
            cols.append(_gelu_tanh(y).reshape(g_n * SSM_GROUP_DIM, n_chunks))
        gt = jnp.concatenate(cols, axis=1)
        r = jnp.dot(gt.T.astype(bf16), w_ref[...], preferred_element_type=f32)
        upd.append(r[:, :dm] * jax.nn.sigmoid(r[:, dm:]))
    upd = jnp.concatenate(upd, axis=0).reshape(nt, n_chunks, dm)
    o_ref[...] = x_ref[...] + pltpu.einshape("tkd->ktd", upd)


def _glu_res(x3, yt, ut, d_skip, w_glu):
    bsz, seqlen, d = x3.shape
    g_n, _, cols = yt.shape
    n_chunks = cols // bsz
    d_tab = jnp.broadcast_to(d_skip.astype(f32).reshape(g_n, SSM_GROUP_DIM, 1), (g_n, SSM_GROUP_DIM, n_chunks))
    feat = pl.BlockSpec((g_n, SUBLANES * SSM_GROUP_DIM, n_chunks), lambda b, t: (0, t, b))
    x5 = _by_chunk(x3)
    out = pl.pallas_call(
        _glu_kernel,
        grid=(bsz, SSM_T // SUBLANES),
        in_specs=[_chunk_spec(x5), feat, feat,
                  pl.BlockSpec(d_tab.shape, lambda b, t: (0, 0, 0)),
                  pl.BlockSpec(w_glu.shape, lambda b, t: (0, 0))],
        out_specs=_chunk_spec(x5),
        out_shape=jax.ShapeDtypeStruct(x5.shape, f32),
        compiler_params=_cparams("parallel", "parallel"),
        name="glu_res",
    )(x5, yt, ut, d_tab, w_glu)
    return out.reshape(bsz, seqlen, d)


def _attn_kernel(tq, tk, out_scale, lam_ref, far_ref, q_ref, k_ref, vt_ref, diag_ref, corner_ref, sub_ref,
                 o_ref, qm_ref, m_ref, acc_ref, s_ref):
    hd2 = 2 * HEAD_DIM
    heads = range(vt_ref.shape[1])
    lane = lax.broadcasted_iota(jnp.int32, (tq, hd2), 1)
    far = [far_ref[pl.program_id(1) * len(heads) + hh] for hh in heads]

    def q_block(qi, carry):
        rows = pl.ds(pl.multiple_of(qi * tq, tq), tq)
        for hh in heads:
            q = q_ref[0, rows, hh * hd2:(hh + 1) * hd2].astype(f32) * (HEAD_DIM ** -0.5 * LOG2E)
            qm_ref[hh, :tq] = jnp.where(lane < HEAD_DIM, q, 0.0).astype(bf16)
            qm_ref[hh, tq:] = jnp.where(lane >= HEAD_DIM, q, 0.0).astype(bf16)
        m_ref[...] = jnp.full_like(m_ref, NEG_INF)
        acc_ref[...] = jnp.zeros_like(acc_ref)

        def fill(ki, n_blocks):
            for hh in heads:
                k = k_ref[0, pl.ds(pl.multiple_of(ki * tk, tk), n_blocks * tk), hh * hd2:(hh + 1) * hd2]
                s_ref[hh, :n_blocks * tk] = lax.dot_general(k, qm_ref[hh], (((1,), (1,)), ((), ())),
                                                            preferred_element_type=f32)

        def accumulate(hh, ki, parts):
            s = parts[0] if len(parts) == 1 else jnp.concatenate(parts, axis=0)
            vt = vt_ref[0, hh, :, pl.ds(pl.multiple_of(ki * tk, tk), len(parts) * tk)]
            m_prev = m_ref[hh]
            m_new = jnp.maximum(m_prev, jnp.max(s, axis=0, keepdims=True) + far[hh])
            p = jnp.exp2(s - (m_new - far[hh]))
            acc_ref[hh] = (jnp.exp2(m_prev - m_new) * acc_ref[hh]
                           + jnp.dot(vt, p.astype(bf16), preferred_element_type=f32))
            m_ref[hh] = m_new

        def neighbour(hh, s, near):
            c = near * corner_ref[hh]
            lo = s[tk - CORNER:]
            lo = jnp.concatenate([lo[:, :CORNER] + c, lo[:, CORNER:tq], lo[:, tq:tq + CORNER] + c,
                                  lo[:, tq + CORNER:]], axis=1)
            return jnp.concatenate([s[:tk - CORNER], lo], axis=0)

        def diagonal(hh, s):
            b = diag_ref[hh]
            return s + jnp.concatenate([b, b], axis=1)

        def pair(i, carry):
            fill(2 * i, 2)
            near = jnp.where(2 * i + 1 == qi - 1, 1.0, 0.0)
            for hh in heads:
                accumulate(hh, 2 * i, [s_ref[hh, :tk], neighbour(hh, s_ref[hh, tk:], near)])
            return carry

        lax.fori_loop(0, qi // 2, pair, 0)

        @pl.when(qi % 2 == 1)
        def _():
            fill(qi - 1, 2)
            for hh in heads:
                accumulate(hh, qi - 1, [neighbour(hh, s_ref[hh, :tk], 1.0), diagonal(hh, s_ref[hh, tk:])])

        @pl.when(qi % 2 == 0)
        def _():
            fill(qi, 1)
            for hh in heads:
                accumulate(hh, qi, [diagonal(hh, s_ref[hh, :tk])])

        for hh in heads:
            acc = acc_ref[hh]
            ot = acc[:hd2] / acc[hd2:hd2 + 1]
            ot = ot[:, :tq] - lam_ref[0] * ot[:, tq:]
            ot = ot * lax.rsqrt(jnp.mean(ot * ot, axis=0, keepdims=True) + EPS)
            o_ref[0, rows, hh * hd2:(hh + 1) * hd2] = (ot.T * sub_ref[...] * out_scale).astype(o_ref.dtype)
        return carry

    lax.fori_loop(0, k_ref.shape[1] // tq, q_block, 0)


def _bias_tables(rel_bias, tq, tk):
    n = jnp.arange(2 * tq)
    nf = jnp.maximum(n, REL_MAX_EXACT).astype(f32)
    large = REL_MAX_EXACT + (jnp.log(nf / REL_MAX_EXACT) / math.log(REL_MAX_DIST / REL_MAX_EXACT)
                             * (REL_BUCKETS - REL_MAX_EXACT)).astype(jnp.int32)
    large = jnp.minimum(large, REL_BUCKETS - 1)
    bucket = jnp.where(n < REL_MAX_EXACT, n, large)
    onehot = (bucket[:, None] == jnp.arange(REL_BUCKETS)[None, :]).astype(f32)
    by_dist = jnp.dot(onehot, rel_bias.astype(f32), precision=lax.Precision.HIGHEST).T * LOG2E
    n_heads = by_dist.shape[0]
    neg = jnp.full((n_heads, tk), NEG_INF, f32)
    period = tq + tk
    tiles = []
    for d, tail in ((0, neg), (1, by_dist[:, tq - tk:tq])):
        r = jnp.concatenate([by_dist[:, d * tq:(d + 1) * tq], tail], axis=1)
        t = jnp.tile(r, (1, tk))[:, :tk * (period - 1)].reshape(n_heads, tk, period - 1)
        tiles.append(t[:, :, :tq])
    far = rel_bias.astype(f32)[REL_BUCKETS - 1] * LOG2E
    corner = tiles[1][:, tk - CORNER:, :CORNER] - far[:, None, None]
    return tiles[0] - far[:, None, None], corner, far


def _far_bucket_from():
    n = REL_MAX_EXACT
    while REL_MAX_EXACT + int(math.log(n / REL_MAX_EXACT) / math.log(REL_MAX_DIST / REL_MAX_EXACT)
                              * (REL_BUCKETS - REL_MAX_EXACT)) < REL_BUCKETS - 1:
        n += 1
    return n


def _diff_attention(q, k, vt, diag, corner, far, lam, subln, out_scale, bsz, seqlen, tq, tk):
    assert tq == tk and tq - CORNER + 1 >= _far_bucket_from()
    hd2 = 2 * HEAD_DIM
    q3 = q.reshape(bsz, seqlen, N_HEADS * hd2)
    k3 = k.reshape(bsz, seqlen, N_HEADS * hd2)
    smem = pl.BlockSpec(memory_space=pltpu.SMEM)
    hps = ATTN_HEADS_PER_STEP
    o = pl.pallas_call(
        functools.partial(_attn_kernel, tq, tk, out_scale),
        grid=(bsz, N_HEADS // hps),
        in_specs=[smem, smem,
                  pl.BlockSpec((1, seqlen, hps * hd2), lambda b, h: (b, 0, h)),
                  pl.BlockSpec((1, seqlen, hps * hd2), lambda b, h: (b, 0, h)),
                  pl.BlockSpec((1, hps, V_ROWS, seqlen), lambda b, h: (b, h, 0, 0)),
                  pl.BlockSpec((hps, tk, tq), lambda b, h: (h, 0, 0)),
                  pl.BlockSpec((hps, CORNER, CORNER), lambda b, h: (h, 0, 0)),
                  pl.BlockSpec((1, hd2), lambda b, h: (0, 0))],
        out_specs=pl.BlockSpec((1, seqlen, hps * hd2), lambda b, h: (b, 0, h)),
        out_shape=jax.ShapeDtypeStruct((bsz, seqlen, N_HEADS * hd2), bf16),
        scratch_shapes=[pltpu.VMEM((hps, 2 * tq, hd2), bf16), pltpu.VMEM((hps, 1, 2 * tq), f32),
                        pltpu.VMEM((hps, V_ROWS, 2 * tq), f32), pltpu.VMEM((hps, 2 * tk, 2 * tq), f32)],
        compiler_params=_cparams("parallel", "parallel"),
        name="diff_attn",
    )(lam.reshape(1).astype(f32), far, q3, k3, vt, diag, corner, subln.astype(f32)[None, :])
    return o.reshape(bsz * seqlen, N_HEADS * hd2)


def _ffn_ple_kernel(tiles_per_seq, has_attn, has_q, has_kv, *refs):
    refs = list(refs)
    x_ref, p_ref = refs[:2]
    del refs[:2]
    if has_attn:
        a_ref, wo_ref = refs[:2]
        del refs[:2]
    fg_ref, wup_ref, cw_ref, cb_ref, wdn_ref, pg_ref, wgate_ref, wproj_ref = refs[:8]
    del refs[:8]
    if has_q or has_kv:
        seg_ref = refs.pop(0)
    if has_q:
        qg_ref, wq_ref, qgain_ref = refs[:3]
        del refs[:3]
    if has_kv:
        kvg_ref, wk_ref, wvt_ref, kgain_ref = refs[:4]
        del refs[:4]
    o_ref = refs.pop(0)
    if has_q:
        q_ref = refs.pop(0)
    if has_kv:
        k_ref, vt_ref = refs[:2]
        del refs[:2]
    carry_ref, act_ref, h_ref = refs

    i = pl.program_id(0)
    tm = x_ref.shape[0]
    d_ff = wdn_ref.shape[0]
    fc = FFN_COLS
    halo = SUBLANES

    @pl.when(i % tiles_per_seq == 0)
    def _():
        carry_ref[...] = jnp.zeros_like(carry_ref)

    x = x_ref[...]
    if has_attn:
        x = x + jnp.dot(a_ref[...], wo_ref[...], preferred_element_type=f32)
    h_ref[...] = _rms(x, fg_ref[...]).astype(bf16)
    row = lax.broadcasted_iota(jnp.int32, (halo, fc), 0)

    def up_proj(col):
        return jnp.dot(h_ref[...], wup_ref[:, col:col + fc], preferred_element_type=f32)

    def conv(col, up):
        prev = carry_ref[:, col:col + fc]
        carry_ref[:, col:col + fc] = up[tm - halo:]
        c = cb_ref[:, col:col + fc] + cw_ref[2:3, col:col + fc] * up
        for back in range(1, CONV_WIDTH):
            r = pltpu.roll(up, back, 0)
            head = jnp.where(row < back, pltpu.roll(prev, back, 0), r[:halo])
            c = c + cw_ref[2 - back:3 - back, col:col + fc] * jnp.concatenate([head, r[halo:]], axis=0)
        return c

    cols = list(range(0, d_ff, fc))
    ups = (up_proj(cols[0]), up_proj(d_ff + cols[0]))
    for idx, col in enumerate(cols):
        nxt = (up_proj(cols[idx + 1]), up_proj(d_ff + cols[idx + 1])) if idx + 1 < len(cols) else None
        act_ref[:, col:col + fc] = (_gelu_tanh(conv(col, ups[0])) * conv(d_ff + col, ups[1])).astype(bf16)
        ups = nxt
    halves = [slice(0, tm // 2), slice(tm // 2, tm)]
    xs = [x[r] + jnp.dot(act_ref[r, :], wdn_ref[...], preferred_element_type=f32) for r in halves]
    projs = [jnp.dot(p_ref[r, :].astype(bf16), wproj_ref[...], preferred_element_type=f32) for r in halves]
    gates = [jax.nn.sigmoid(jnp.dot(_rms(xh, pg_ref[...]).astype(bf16), wgate_ref[...],
                                    preferred_element_type=f32)) for xh in xs]
    xs = [xh + g * pr for xh, g, pr in zip(xs, gates, projs)]
    for r, xh in zip(halves, xs):
        o_ref[r, :] = xh
    if has_q or has_kv:
        xn = jnp.concatenate([xh * lax.rsqrt(jnp.mean(xh * xh, axis=-1, keepdims=True) + EPS) for xh in xs],
                             axis=0)
    if has_q:
        q = jnp.dot((xn * qg_ref[...]).astype(bf16), wq_ref[...], preferred_element_type=f32)
        q_ref[...] = _seg_norm(q, seg_ref, qgain_ref).astype(q_ref.dtype)
    if has_kv:
        hk = (xn * kvg_ref[...]).astype(bf16)
        k = jnp.dot(hk, wk_ref[...], preferred_element_type=f32)
        k_ref[...] = _seg_norm(k, seg_ref, kgain_ref).astype(k_ref.dtype)
        _store_vt(vt_ref, lax.dot_general(wvt_ref[...], hk, (((1,), (1,)), ((), ())),
                                          preferred_element_type=f32))


def _ffn_ple(x, p, layer, ffn, ple, seqlen, attn=None, next_q=None, next_kv=None):
    n, d = x.shape
    ffn_g, w_up, conv_w, conv_b, w_down = ffn
    ple_g, w_gate, w_proj = ple
    d_ff = w_down.shape[-2]
    tm = FFN_ROWS
    bsz = n // seqlen
    tps = seqlen // tm
    row = lambda i: (i, 0)

    def full(a):
        if a.ndim == 3:
            return pl.BlockSpec((None,) + a.shape[1:], lambda i: (layer, 0, 0), pipeline_mode=pl.Buffered(1))
        return pl.BlockSpec(a.shape, lambda i: (0, 0), pipeline_mode=pl.Buffered(1))

    vec = lambda a: a.astype(f32)[None, :]
    args = [x, p]
    in_specs = [pl.BlockSpec((tm, d), row), pl.BlockSpec((None, tm, p.shape[2]), lambda i: (layer, i, 0))]
    if attn is not None:
        args += [attn[0], attn[1]]
        in_specs += [pl.BlockSpec((tm, attn[0].shape[1]), row), full(attn[1])]
    weights = [vec(ffn_g), w_up, conv_w.astype(f32), vec(conv_b), w_down, vec(ple_g), w_gate, w_proj]
    out_specs = [pl.BlockSpec((tm, d), row)]
    out_shape = [jax.ShapeDtypeStruct((n, d), f32)]
    if next_q is not None or next_kv is not None:
        weights.append(_segment_ones())
    if next_q is not None:
        weights += [vec(next_q[0]), next_q[1], _segment_gain(next_q[2])]
        out_specs.append(pl.BlockSpec((tm, next_q[1].shape[1]), row))
        out_shape.append(jax.ShapeDtypeStruct((n, next_q[1].shape[1]), bf16))
    if next_kv is not None:
        weights += [vec(next_kv[0]), next_kv[1], next_kv[2], _segment_gain(next_kv[3])]
        out_specs += [pl.BlockSpec((tm, next_kv[1].shape[1]), row),
                      pl.BlockSpec((1, N_HEADS, V_ROWS, tm), lambda i: (i // tps, 0, 0, i % tps))]
        out_shape += [jax.ShapeDtypeStruct((n, next_kv[1].shape[1]), bf16),
                      jax.ShapeDtypeStruct((bsz, N_HEADS, V_ROWS, seqlen), bf16)]
    args += weights
    in_specs += [full(w) for w in weights]
    return pl.pallas_call(
        functools.partial(_ffn_ple_kernel, tps, attn is not None, next_q is not None, next_kv is not None),
        grid=(n // tm,),
        in_specs=in_specs,
        out_specs=out_specs,
        out_shape=out_shape,
        scratch_shapes=[pltpu.VMEM((SUBLANES, 2 * d_ff), f32),
                        pltpu.VMEM((tm, d_ff), bf16),
                        pltpu.VMEM((tm, d), bf16)],
        compiler_params=_cparams("arbitrary"),
        name="ffn_ple",
    )(*args)


def kernel(x, p, ssm_norm, ssm_w_in, ssm_log_dt, ssm_lambda_re, ssm_lambda_im, ssm_b_re, ssm_b_im, ssm_c_re, ssm_c_im, ssm_d, ssm_w_glu, kv_norm, kv_w, k_norm, attn_norm, attn_w_q, q_norm, lambda_q1, lambda_k1, lambda_q2, lambda_k2, subln, attn_w_o, rel_bias, ffn_norm, ffn_w_up, ffn_conv_w, ffn_conv_b, ffn_w_down, ple_norm, ple_w_gate, ple_w_proj):
    bsz, seqlen, d = x.shape
    depth = p.shape[0]
    n_a = ssm_norm.shape[0]
    n = bsz * seqlen
    x = x.reshape(n, d).astype(f32)
    p = p.reshape(depth, n, p.shape[-1])
    nk = N_HEADS * 2 * HEAD_DIM
    assert 1 <= n_a < depth
    diag, corner, far = _bias_tables(rel_bias, ATTN_BLOCK, ATTN_BLOCK)
    tables = jax.vmap(functools.partial(_ssm_tables, n_chunks=seqlen // SSM_T))(
        ssm_log_dt, ssm_lambda_re, ssm_lambda_im, ssm_b_re, ssm_b_im, ssm_c_re, ssm_c_im)
    w_in_t = jnp.swapaxes(ssm_w_in, 1, 2).astype(bf16)
    w_glu, w_q, w_o = ssm_w_glu.astype(bf16), attn_w_q.astype(bf16), attn_w_o.astype(bf16)
    w_up, w_down = ffn_w_up.astype(bf16), ffn_w_down.astype(bf16)
    w_gate, w_proj = ple_w_gate.astype(bf16), ple_w_proj.astype(bf16)
    lam_all = (jnp.exp(jnp.sum(lambda_q1.astype(f32) * lambda_k1.astype(f32), axis=-1))
               - jnp.exp(jnp.sum(lambda_q2.astype(f32) * lambda_k2.astype(f32), axis=-1)))
    q = k = vt = None
    for i in range(depth):
        attn = None
        if i < n_a:
            x3 = x.reshape(bsz, seqlen, d)
            ut = _win_t(x3, ssm_norm[i], w_in_t[i])
            yt = _ssm(ut, tables, i)
            x = _glu_res(x3, yt, ut, ssm_d[i], w_glu[i]).reshape(n, d)
        else:
            j = i - n_a
            lam_init = 0.8 - 0.6 * math.exp(-0.3 * i)
            o = _diff_attention(q, k, vt, diag, corner, far, lam_all[j] + lam_init, subln[j], 1.0 - lam_init,
                                bsz, seqlen, ATTN_BLOCK, ATTN_BLOCK)
            attn = (o, w_o[j])
        j_next = i + 1 - n_a
        next_q = next_kv = None
        if 0 <= j_next < depth - n_a:
            next_q = (attn_norm[j_next], w_q[j_next], q_norm[j_next])
            if j_next == 0:
                next_kv = (kv_norm, kv_w[:, :nk].astype(bf16), kv_w[:, nk:].T.astype(bf16), k_norm)
        outs = _ffn_ple(x, p, i,
                        (ffn_norm[i], w_up, ffn_conv_w[i], ffn_conv_b[i], w_down),
                        (ple_norm[i], w_gate, w_proj),
                        seqlen, attn=attn, next_q=next_q, next_kv=next_kv)
        x = outs[0]
        if next_q is not None:
            q = outs[1]
        if next_kv is not None:
            k, vt = outs[2], outs[3]
    return x.reshape(bsz, seqlen, d)
```

```python
import functools
import math

import jax
import jax.numpy as jnp
from jax import lax
from jax.experimental import pallas as pl
from jax.experimental.pallas import tpu as pltpu

EPS = 1e-6
NEG_INF = -1e30

SSM_GROUP_DIM = 16
N_HEADS = 8
HEAD_DIM = 64
REL_BUCKETS = 32
REL_MAX_EXACT = REL_BUCKETS // 2
REL_MAX_DIST = 128
CONV_WIDTH = 3

LANES = 128
SUBLANES = 8
MXU_DIM = 256
VMEM_LIMIT_BYTES = 56 * 1024 * 1024

SSM_T = MXU_DIM // SSM_GROUP_DIM
SEG_CHUNK = MXU_DIM
ATTN_BLOCK = 256
ATTN_HEADS_PER_STEP = N_HEADS
CORNER = LANES
V_ROWS = 2 * HEAD_DIM + 16
LOG2E = math.log2(math.e)
GLU_SPLIT = 4
FFN_ROWS = 512
FFN_COLS = MXU_DIM

bf16 = jnp.bfloat16
f32 = jnp.float32


def _cparams(*sem):
    return pltpu.CompilerParams(dimension_semantics=sem, vmem_limit_bytes=VMEM_LIMIT_BYTES)


def _rms(xf, g):
    return xf * lax.rsqrt(jnp.mean(xf * xf, axis=-1, keepdims=True) + EPS) * g


def _gelu_tanh(v):
    c = 2.0 * math.sqrt(2.0 / math.pi)
    return v * jax.nn.sigmoid(v * (c + (c * 0.044715) * (v * v)))


def _seg_norm(y, seg_ref, gain_ref):
    cols = []
    for c in range(y.shape[1] // SEG_CHUNK):
        yc = y[:, c * SEG_CHUNK:(c + 1) * SEG_CHUNK]
        ss = jnp.dot((yc * yc).astype(bf16), seg_ref[...], preferred_element_type=f32)
        cols.append(yc * lax.rsqrt(ss * (1.0 / HEAD_DIM) + EPS) * gain_ref[...])
    return jnp.concatenate(cols, axis=1)


def _segment_ones():
    idx = jnp.arange(SEG_CHUNK) // HEAD_DIM
    return (idx[:, None] == idx[None, :]).astype(bf16)


def _segment_gain(g):
    return jnp.tile(g.astype(f32), SEG_CHUNK // HEAD_DIM)[None, :]


def _store_vt(vt_ref, vt):
    hd2 = 2 * HEAD_DIM
    for hh in range(vt_ref.shape[1]):
        vt_ref[0, hh, :hd2, :] = vt[hh * hd2:(hh + 1) * hd2].astype(vt_ref.dtype)
        vt_ref[0, hh, hd2:, :] = jnp.ones((V_ROWS - hd2, vt.shape[1]), vt_ref.dtype)


def _win_kernel(x_ref, g_ref, wt_ref, ut_ref):
    n_chunks, nt, d = x_ref.shape
    g_n = ut_ref.shape[0]
    xt = pltpu.einshape("ktd->tkd", x_ref[...]).reshape(nt * n_chunks, d)
    h = _rms(xt, g_ref[...]).astype(bf16)
    ut = lax.dot_general(wt_ref[...], h, (((1,), (1,)), ((), ())), preferred_element_type=f32)
    for tt in range(nt):
        blk = ut[:, tt * n_chunks:(tt + 1) * n_chunks].reshape(g_n, SSM_GROUP_DIM, n_chunks)
        ut_ref[:, tt * SSM_GROUP_DIM:(tt + 1) * SSM_GROUP_DIM, :] = blk.astype(ut_ref.dtype)


def _by_chunk(x3):
    bsz, seqlen, d = x3.shape
    return x3.reshape(bsz, seqlen // SSM_T, SSM_T // SUBLANES, SUBLANES, d)


def _chunk_spec(x5):
    _, n_chunks, _, nt, d = x5.shape
    return pl.BlockSpec((None, n_chunks, None, nt, d), lambda b, t: (b, 0, t, 0, 0))


def _win_t(x3, g, w_t):
    bsz, seqlen, d = x3.shape
    n_chunks = seqlen // SSM_T
    assert n_chunks == LANES
    g_n = d // SSM_GROUP_DIM
    x5 = _by_chunk(x3)
    return pl.pallas_call(
        _win_kernel,
        grid=(bsz, SSM_T // SUBLANES),
        in_specs=[_chunk_spec(x5),
                  pl.BlockSpec((1, d), lambda b, t: (0, 0)),
                  pl.BlockSpec(w_t.shape, lambda b, t: (0, 0))],
        out_specs=pl.BlockSpec((g_n, SUBLANES * SSM_GROUP_DIM, n_chunks), lambda b, t: (0, t, b)),
        out_shape=jax.ShapeDtypeStruct((g_n, SSM_T * SSM_GROUP_DIM, bsz * n_chunks), bf16),
        compiler_params=_cparams("parallel", "parallel"),
        name="win_t",
    )(x5, g.astype(f32)[None, :], w_t)


def _ssm_kernel(n_rounds, u_ref, mt_ref, w_ref, v_ref, pw_ref, y_ref, xin_ref):
    u0 = u_ref[0]
    u1 = u_ref[1]
    half = LANES
    seg = LANES
    st = (jnp.dot(w_ref[0], u0, preferred_element_type=f32)
          + jnp.dot(w_ref[1], u1, preferred_element_type=f32))
    n_seq = st.shape[1] // seg
    xr = jnp.stack([st[:half, b * seg:(b + 1) * seg] for b in range(n_seq)])
    xi = jnp.stack([st[half:, b * seg:(b + 1) * seg] for b in range(n_seq)])
    chunk = lax.broadcasted_iota(jnp.int32, xr.shape, 2)
    for j in range(n_rounds):
        d = 1 << j
        ar = pw_ref[j, 0]
        ai = pw_ref[j, 1]
        sr = pltpu.roll(xr, d, 2)
        si = pltpu.roll(xi, d, 2)
        xr, xi = xr + (ar * sr - ai * si), xi + (ar * si + ai * sr)
    xr = jnp.where(chunk >= 1, pltpu.roll(xr, 1, 2), 0.0).astype(bf16)
    xi = jnp.where(chunk >= 1, pltpu.roll(xi, 1, 2), 0.0).astype(bf16)
    for b in range(n_seq):
        xin_ref[:half, b * seg:(b + 1) * seg] = xr[b]
        xin_ref[half:, b * seg:(b + 1) * seg] = xi[b]
    ys = jnp.dot(v_ref[...], xin_ref[...], preferred_element_type=f32)
    y_ref[0] = (jnp.dot(mt_ref[0], u0, preferred_element_type=f32) + ys[:MXU_DIM]).astype(y_ref.dtype)
    y_ref[1] = (jnp.dot(mt_ref[1], u1, preferred_element_type=f32) + ys[MXU_DIM:]).astype(y_ref.dtype)


def _ssm_tables(log_dt, lam_re, lam_im, b_re, b_im, c_re, c_im, n_chunks):
    g_n, p_n = lam_re.shape
    t_n, c_n = SSM_T, SSM_GROUP_DIM
    dt = jnp.exp(log_dt.astype(f32))[:, None]
    lr = lam_re.astype(f32)
    li = lam_im.astype(f32)

    def power(n):
        nn = jnp.asarray(n, f32).reshape((-1, 1, 1))
        mag = jnp.exp(lr * dt * nn)
        return mag * jnp.cos(li * dt * nn), mag * jnp.sin(li * dt * nn)

    ab_r, ab_i = power([1])
    ab_r, ab_i = ab_r[0], ab_i[0]
    den = lr * lr + li * li
    f_r = ((ab_r - 1.0) * lr + ab_i * li) / den
    f_i = (ab_i * lr - (ab_r - 1.0) * li) / den
    br = b_re.astype(f32)
    bi = b_im.astype(f32)
    bb_r = f_r[..., None] * br - f_i[..., None] * bi
    bb_i = f_r[..., None] * bi + f_i[..., None] * br
    cr = c_re.astype(f32)
    ci = c_im.astype(f32)

    qr, qi = power(t_n - 1 - jnp.arange(t_n))
    t_r = qr[..., None] * bb_r[None] - qi[..., None] * bb_i[None]
    t_i = qr[..., None] * bb_i[None] + qi[..., None] * bb_r[None]
    kern = (jnp.einsum("gcp,dgpk->gcdk", cr, t_r, precision=lax.Precision.HIGHEST)
            - jnp.einsum("gcp,dgpk->gcdk", ci, t_i, precision=lax.Precision.HIGHEST))
    kern = kern.reshape(g_n, c_n, t_n * c_n)
    kern = jnp.concatenate([kern, jnp.zeros_like(kern)], axis=-1)
    mt = jnp.concatenate([kern[:, :, (t_n - 1 - t) * c_n:(2 * t_n - 1 - t) * c_n] for t in range(t_n)], axis=1)
    w_r = t_r.transpose(1, 2, 0, 3).reshape(g_n, p_n, t_n * c_n)
    w_i = t_i.transpose(1, 2, 0, 3).reshape(g_n, p_n, t_n * c_n)
    pr1, pi1 = power(1 + jnp.arange(t_n))
    pr1 = pr1.transpose(1, 0, 2)
    pi1 = pi1.transpose(1, 0, 2)
    v_r = (cr[:, None] * pr1[:, :, None, :] - ci[:, None] * pi1[:, :, None, :]).reshape(g_n, t_n * c_n, p_n)
    v_i = -(cr[:, None] * pi1[:, :, None, :] + ci[:, None] * pr1[:, :, None, :]).reshape(g_n, t_n * c_n, p_n)

    gp = g_n // 2
    z = jnp.zeros((gp, p_n, t_n * c_n), f32)
    w_r = w_r.reshape(gp, 2, p_n, t_n * c_n)
    w_i = w_i.reshape(gp, 2, p_n, t_n * c_n)
    w0 = jnp.concatenate([w_r[:, 0], z, w_i[:, 0], z], axis=1)
    w1 = jnp.concatenate([z, w_r[:, 1], z, w_i[:, 1]], axis=1)
    w_pack = jnp.stack([w0, w1], axis=1)
    zv = jnp.zeros((gp, t_n * c_n, p_n), f32)
    v_r = v_r.reshape(gp, 2, t_n * c_n, p_n)
    v_i = v_i.reshape(gp, 2, t_n * c_n, p_n)
    v_pack = jnp.concatenate([
        jnp.concatenate([v_r[:, 0], zv, v_i[:, 0], zv], axis=-1),
        jnp.concatenate([zv, v_r[:, 1], zv, v_i[:, 1]], axis=-1)], axis=1)
    n_rounds = (n_chunks - 1).bit_length()
    sr, si = power(t_n * (2 ** jnp.arange(max(n_rounds, 1))))
    pw = jnp.stack([sr.reshape(-1, gp, 2 * p_n), si.reshape(-1, gp, 2 * p_n)], axis=2)
    reach = (jnp.arange(LANES)[None, :] >= (2 ** jnp.arange(pw.shape[0]))[:, None]).astype(f32)
    pw = pw.transpose(1, 0, 2, 3)[..., None] * reach[None, :, None, None, :]
    return mt.reshape(gp, 2, t_n * c_n, t_n * c_n).astype(bf16), w_pack.astype(bf16), v_pack.astype(bf16), pw


def _ssm(ut, tables, layer):
    mt, w_pack, v_pack, pw = tables
    n_rounds = pw.shape[2]
    g_n, k, cols = ut.shape
    gp = g_n // 2
    return pl.pallas_call(
        functools.partial(_ssm_kernel, n_rounds),
        grid=(gp,),
        in_specs=[pl.BlockSpec((2, k, cols), lambda g: (g, 0, 0)),
                  pl.BlockSpec((None, None, 2, k, k), lambda g: (layer, g, 0, 0, 0)),
                  pl.BlockSpec((None, None, 2, k, k), lambda g: (layer, g, 0, 0, 0)),
                  pl.BlockSpec((None, None, 2 * k, k), lambda g: (layer, g, 0, 0)),
                  pl.BlockSpec((None, None) + pw.shape[2:], lambda g: (layer, g, 0, 0, 0, 0))],
        out_specs=pl.BlockSpec((2, k, cols), lambda g: (g, 0, 0)),
        out_shape=jax.ShapeDtypeStruct((g_n, k, cols), bf16),
        scratch_shapes=[pltpu.VMEM((k, cols), bf16)],
        compiler_params=_cparams("parallel"),
        name="ssm",
    )(ut, mt, w_pack, v_pack, pw)


def _glu_kernel(x_ref, y_ref, u_ref, d_ref, w_ref, o_ref):
    n_chunks, nt, dm = x_ref.shape
    g_n = y_ref.shape[0]
    per = nt // GLU_SPLIT
    upd = []
    for s in range(GLU_SPLIT):
        cols = []
        for tt in range(s * per, (s + 1) * per):
            sl = slice(tt * SSM_GROUP_DIM, (tt + 1) * SSM_GROUP_DIM)
            y = y_ref[:, sl, :].astype(f32) + d_ref[...] * u_ref[:, sl, :].astype(f32)
            cols.append(_gelu_tanh(y).reshape(g_n * SSM_GROUP_DIM, n_chunks))
        gt = jnp.concatenate(cols, axis=1)
        r = jnp.dot(gt.T.astype(bf16), w_ref[...], preferred_element_type=f32)
        upd.append(r[:, :dm] * jax.nn.sigmoid(r[:, dm:]))
    upd = jnp.concatenate(upd, axis=0).reshape(nt, n_chunks, dm)
    o_ref[...] = x_ref[...] + pltpu.einshape("tkd->ktd", upd)


def _glu_res(x3, yt, ut, d_skip, w_glu):
    bsz, seqlen, d = x3.shape
    g_n, _, cols = yt.shape
    n_chunks = cols // bsz
    d_tab = jnp.broadcast_to(d_skip.astype(f32).reshape(g_n, SSM_GROUP_DIM, 1), (g_n, SSM_GROUP_DIM, n_chunks))
    feat = pl.BlockSpec((g_n, SUBLANES * SSM_GROUP_DIM, n_chunks), lambda b, t: (0, t, b))
    x5 = _by_chunk(x3)
    out = pl.pallas_call(
        _glu_kernel,
        grid=(bsz, SSM_T // SUBLANES),
        in_specs=[_chunk_spec(x5), feat, feat,
                  pl.BlockSpec(d_tab.shape, lambda b, t: (0, 0, 0)),
                  pl.BlockSpec(w_glu.shape, lambda b, t: (0, 0))],
        out_specs=_chunk_spec(x5),
        out_shape=jax.ShapeDtypeStruct(x5.shape, f32),
        compiler_params=_cparams("parallel", "parallel"),
        name="glu_res",
    )(x5, yt, ut, d_tab, w_glu)
    return out.reshape(bsz, seqlen, d)


def _attn_kernel(tq, tk, out_scale, lam_ref, far_ref, q_ref, k_ref, vt_ref, diag_ref, corner_ref, sub_ref,
                 o_ref, qm_ref, m_ref, acc_ref, s_ref):
    hd2 = 2 * HEAD_DIM
    heads = range(vt_ref.shape[1])
    lane = lax.broadcasted_iota(jnp.int32, (tq, hd2), 1)
    far = [far_ref[pl.program_id(1) * len(heads) + hh] for hh in heads]

    def q_block(qi, carry):
        rows = pl.ds(pl.multiple_of(qi * tq, tq), tq)
        for hh in heads:
            q = q_ref[0, rows, hh * hd2:(hh + 1) * hd2].astype(f32) * (HEAD_DIM ** -0.5 * LOG2E)
            qm_ref[hh, :tq] = jnp.where(lane < HEAD_DIM, q, 0.0).astype(bf16)
            qm_ref[hh, tq:] = jnp.where(lane >= HEAD_DIM, q, 0.0).astype(bf16)
        m_ref[...] = jnp.full_like(m_ref, NEG_INF)
        acc_ref[...] = jnp.zeros_like(acc_ref)

        def fill(ki, n_blocks):
            for hh in heads:
                k = k_ref[0, pl.ds(pl.multiple_of(ki * tk, tk), n_blocks * tk), hh * hd2:(hh + 1) * hd2]
                s_ref[hh, :n_blocks * tk] = lax.dot_general(k, qm_ref[hh], (((1,), (1,)), ((), ())),
                                                            preferred_element_type=f32)

        def accumulate(hh, ki, parts):
            s = parts[0] if len(parts) == 1 else jnp.concatenate(parts, axis=0)
            vt = vt_ref[0, hh, :, pl.ds(pl.multiple_of(ki * tk, tk), len(parts) * tk)]
            m_prev = m_ref[hh]
            m_new = jnp.maximum(m_prev, jnp.max(s, axis=0, keepdims=True) + far[hh])
            p = jnp.exp2(s - (m_new - far[hh]))
            acc_ref[hh] = (jnp.exp2(m_prev - m_new) * acc_ref[hh]
                           + jnp.dot(vt, p.astype(bf16), preferred_element_type=f32))
            m_ref[hh] = m_new

        def neighbour(hh, s, near):
            c = near * corner_ref[hh]
            lo = s[tk - CORNER:]
            lo = jnp.concatenate([lo[:, :CORNER] + c, lo[:, CORNER:tq], lo[:, tq:tq + CORNER] + c,
                                  lo[:, tq + CORNER:]], axis=1)
            return jnp.concatenate([s[:tk - CORNER], lo], axis=0)

        def diagonal(hh, s):
            b = diag_ref[hh]
            return s + jnp.concatenate([b, b], axis=1)

        def pair(i, carry):
            fill(2 * i, 2)
            near = jnp.where(2 * i + 1 == qi - 1, 1.0, 0.0)
            for hh in heads:
                accumulate(hh, 2 * i, [s_ref[hh, :tk], neighbour(hh, s_ref[hh, tk:], near)])
            return carry

        lax.fori_loop(0, qi // 2, pair, 0)

        @pl.when(qi % 2 == 1)
        def _():
            fill(qi - 1, 2)
            for hh in heads:
                accumulate(hh, qi - 1, [neighbour(hh, s_ref[hh, :tk], 1.0), diagonal(hh, s_ref[hh, tk:])])

        @pl.when(qi % 2 == 0)
        def _():
            fill(qi, 1)
            for hh in heads:
                accumulate(hh, qi, [diagonal(hh, s_ref[hh, :tk])])

        for hh in heads:
            acc = acc_ref[hh]
            ot = acc[:hd2] / acc[hd2:hd2 + 1]
            ot = ot[:, :tq] - lam_ref[0] * ot[:, tq:]
            ot = ot * lax.rsqrt(jnp.mean(ot * ot, axis=0, keepdims=True) + EPS)
            o_ref[0, rows, hh * hd2:(hh + 1) * hd2] = (ot.T * sub_ref[...] * out_scale).astype(o_ref.dtype)
        return carry

    lax.fori_loop(0, k_ref.shape[1] // tq, q_block, 0)


def _bias_tables(rel_bias, tq, tk):
    n = jnp.arange(2 * tq)
    nf = jnp.maximum(n, REL_MAX_EXACT).astype(f32)
    large = REL_MAX_EXACT + (jnp.log(nf / REL_MAX_EXACT) / math.log(REL_MAX_DIST / REL_MAX_EXACT)
                             * (REL_BUCKETS - REL_MAX_EXACT)).astype(jnp.int32)
    large = jnp.minimum(large, REL_BUCKETS - 1)
    bucket = jnp.where(n < REL_MAX_EXACT, n, large)
    onehot = (bucket[:, None] == jnp.arange(REL_BUCKETS)[None, :]).astype(f32)
    by_dist = jnp.dot(onehot, rel_bias.astype(f32), precision=lax.Precision.HIGHEST).T * LOG2E
    n_heads = by_dist.shape[0]
    neg = jnp.full((n_heads, tk), NEG_INF, f32)
    period = tq + tk
    tiles = []
    for d, tail in ((0, neg), (1, by_dist[:, tq - tk:tq])):
        r = jnp.concatenate([by_dist[:, d * tq:(d + 1) * tq], tail], axis=1)
        t = jnp.tile(r, (1, tk))[:, :tk * (period - 1)].reshape(n_heads, tk, period - 1)
        tiles.append(t[:, :, :tq])
    far = rel_bias.astype(f32)[REL_BUCKETS - 1] * LOG2E
    corner = tiles[1][:, tk - CORNER:, :CORNER] - far[:, None, None]
    return tiles[0] - far[:, None, None], corner, far


def _far_bucket_from():
    n = REL_MAX_EXACT
    while REL_MAX_EXACT + int(math.log(n / REL_MAX_EXACT) / math.log(REL_MAX_DIST / REL_MAX_EXACT)
                              * (REL_BUCKETS - REL_MAX_EXACT)) < REL_BUCKETS - 1:
        n += 1
    return n


def _diff_attention(q, k, vt, diag, corner, far, lam, subln, out_scale, bsz, seqlen, tq, tk):
    assert tq == tk and tq - CORNER + 1 >= _far_bucket_from()
    hd2 = 2 * HEAD_DIM
    q3 = q.reshape(bsz, seqlen, N_HEADS * hd2)
    k3 = k.reshape(bsz, seqlen, N_HEADS * hd2)
    smem = pl.BlockSpec(memory_space=pltpu.SMEM)
    hps = ATTN_HEADS_PER_STEP
    o = pl.pallas_call(
        functools.partial(_attn_kernel, tq, tk, out_scale),
        grid=(bsz, N_HEADS // hps),
        in_specs=[smem, smem,
                  pl.BlockSpec((1, seqlen, hps * hd2), lambda b, h: (b, 0, h)),
                  pl.BlockSpec((1, seqlen, hps * hd2), lambda b, h: (b, 0, h)),
                  pl.BlockSpec((1, hps, V_ROWS, seqlen), lambda b, h: (b, h, 0, 0)),
                  pl.BlockSpec((hps, tk, tq), lambda b, h: (h, 0, 0)),
                  pl.BlockSpec((hps, CORNER, CORNER), lambda b, h: (h, 0, 0)),
                  pl.BlockSpec((1, hd2), lambda b, h: (0, 0))],
        out_specs=pl.BlockSpec((1, seqlen, hps * hd2), lambda b, h: (b, 0, h)),
        out_shape=jax.ShapeDtypeStruct((bsz, seqlen, N_HEADS * hd2), bf16),
        scratch_shapes=[pltpu.VMEM((hps, 2 * tq, hd2), bf16), pltpu.VMEM((hps, 1, 2 * tq), f32),
                        pltpu.VMEM((hps, V_ROWS, 2 * tq), f32), pltpu.VMEM((hps, 2 * tk, 2 * tq), f32)],
        compiler_params=_cparams("parallel", "parallel"),
        name="diff_attn",
    )(lam.reshape(1).astype(f32), far, q3, k3, vt, diag, corner, subln.astype(f32)[None, :])
    return o.reshape(bsz * seqlen, N_HEADS * hd2)


def _ffn_ple_kernel(tiles_per_seq, has_attn, has_q, has_kv, *refs):
    refs = list(refs)
    x_ref, p_ref = refs[:2]
    del refs[:2]
    if has_attn:
        a_ref, wo_ref = refs[:2]
        del refs[:2]
    fg_ref, wup_ref, cw_ref, cb_ref, wdn_ref, pg_ref, wgate_ref, wproj_ref = refs[:8]
    del refs[:8]
    if has_q or has_kv:
        seg_ref = refs.pop(0)
    if has_q:
        qg_ref, wq_ref, qgain_ref = refs[:3]
        del refs[:3]
    if has_kv:
        kvg_ref, wk_ref, wvt_ref, kgain_ref = refs[:4]
        del refs[:4]
    o_ref = refs.pop(0)
    if has_q:
        q_ref = refs.pop(0)
    if has_kv:
        k_ref, vt_ref = refs[:2]
        del refs[:2]
    carry_ref, act_ref, h_ref = refs

    i = pl.program_id(0)
    tm = x_ref.shape[0]
    d_ff = wdn_ref.shape[0]
    fc = FFN_COLS
    halo = SUBLANES

    @pl.when(i % tiles_per_seq == 0)
    def _():
        carry_ref[...] = jnp.zeros_like(carry_ref)

    x = x_ref[...]
    if has_attn:
        x = x + jnp.dot(a_ref[...], wo_ref[...], preferred_element_type=f32)
    h_ref[...] = _rms(x, fg_ref[...]).astype(bf16)
    row = lax.broadcasted_iota(jnp.int32, (halo, fc), 0)

    def up_proj(col):
        return jnp.dot(h_ref[...], wup_ref[:, col:col + 2 * fc], preferred_element_type=f32)

    def conv(col, up):
        prev = carry_ref[:, col:col + fc]
        carry_ref[:, col:col + fc] = up[tm - halo:]
        c = cb_ref[:, col:col + fc] + cw_ref[2:3, col:col + fc] * up
        for back in range(1, CONV_WIDTH):
            r = pltpu.roll(up, back, 0)
            head = jnp.where(row < back, pltpu.roll(prev, back, 0), r[:halo])
            c = c + cw_ref[2 - back:3 - back, col:col + fc] * jnp.concatenate([head, r[halo:]], axis=0)
        return c

    cols = list(range(0, d_ff, fc))
    up = up_proj(0)
    for idx, col in enumerate(cols):
        nxt = up_proj(2 * cols[idx + 1]) if idx + 1 < len(cols) else None
        gate = conv(2 * col, up[:, :fc])
        val = conv(2 * col + fc, up[:, fc:])
        act_ref[:, col:col + fc] = (_gelu_tanh(gate) * val).astype(bf16)
        up = nxt
    halves = [slice(0, tm // 2), slice(tm // 2, tm)]
    xs = [x[r] + jnp.dot(act_ref[r, :], wdn_ref[...], preferred_element_type=f32) for r in halves]
    projs = [jnp.dot(p_ref[r, :].astype(bf16), wproj_ref[...], preferred_element_type=f32) for r in halves]
    gates = [jax.nn.sigmoid(jnp.dot(_rms(xh, pg_ref[...]).astype(bf16), wgate_ref[...],
                                    preferred_element_type=f32)) for xh in xs]
    xs = [xh + g * pr for xh, g, pr in zip(xs, gates, projs)]
    for r, xh in zip(halves, xs):
        o_ref[r, :] = xh
    if has_q or has_kv:
        xn = jnp.concatenate([xh * lax.rsqrt(jnp.mean(xh * xh, axis=-1, keepdims=True) + EPS) for xh in xs],
                             axis=0)
    if has_q:
        q = jnp.dot((xn * qg_ref[...]).astype(bf16), wq_ref[...], preferred_element_type=f32)
        q_ref[...] = _seg_norm(q, seg_ref, qgain_ref).astype(q_ref.dtype)
    if has_kv:
        hk = (xn * kvg_ref[...]).astype(bf16)
        k = jnp.dot(hk, wk_ref[...], preferred_element_type=f32)
        k_ref[...] = _seg_norm(k, seg_ref, kgain_ref).astype(k_ref.dtype)
        _store_vt(vt_ref, lax.dot_general(wvt_ref[...], hk, (((1,), (1,)), ((), ())),
                                          preferred_element_type=f32))


def _interleave_halves(a):
    half = a.shape[-1] // 2
    parts = a.reshape(a.shape[:-1] + (2, half // FFN_COLS, FFN_COLS))
    return jnp.swapaxes(parts, -3, -2).reshape(a.shape)


def _ffn_ple(x, p, layer, ffn, ple, seqlen, attn=None, next_q=None, next_kv=None):
    n, d = x.shape
    ffn_g, w_up, conv_w, conv_b, w_down = ffn
    ple_g, w_gate, w_proj = ple
    d_ff = w_down.shape[-2]
    tm = FFN_ROWS
    bsz = n // seqlen
    tps = seqlen // tm
    row = lambda i: (i, 0)

    def full(a):
        if a.ndim == 3:
            return pl.BlockSpec((None,) + a.shape[1:], lambda i: (layer, 0, 0), pipeline_mode=pl.Buffered(1))
        return pl.BlockSpec(a.shape, lambda i: (0, 0), pipeline_mode=pl.Buffered(1))

    vec = lambda a: a.astype(f32)[None, :]
    args = [x, p]
    in_specs = [pl.BlockSpec((tm, d), row), pl.BlockSpec((None, tm, p.shape[2]), lambda i: (layer, i, 0))]
    if attn is not None:
        args += [attn[0], attn[1]]
        in_specs += [pl.BlockSpec((tm, attn[0].shape[1]), row), full(attn[1])]
    weights = [vec(ffn_g), w_up, conv_w.astype(f32), vec(conv_b), w_down, vec(ple_g), w_gate, w_proj]
    out_specs = [pl.BlockSpec((tm, d), row)]
    out_shape = [jax.ShapeDtypeStruct((n, d), f32)]
    if next_q is not None or next_kv is not None:
        weights.append(_segment_ones())
    if next_q is not None:
        weights += [vec(next_q[0]), next_q[1], _segment_gain(next_q[2])]
        out_specs.append(pl.BlockSpec((tm, next_q[1].shape[1]), row))
        out_shape.append(jax.ShapeDtypeStruct((n, next_q[1].shape[1]), bf16))
    if next_kv is not None:
        weights += [vec(next_kv[0]), next_kv[1], next_kv[2], _segment_gain(next_kv[3])]
        out_specs += [pl.BlockSpec((tm, next_kv[1].shape[1]), row),
                      pl.BlockSpec((1, N_HEADS, V_ROWS, tm), lambda i: (i // tps, 0, 0, i % tps))]
        out_shape += [jax.ShapeDtypeStruct((n, next_kv[1].shape[1]), bf16),
                      jax.ShapeDtypeStruct((bsz, N_HEADS, V_ROWS, seqlen), bf16)]
    args += weights
    in_specs += [full(w) for w in weights]
    return pl.pallas_call(
        functools.partial(_ffn_ple_kernel, tps, attn is not None, next_q is not None, next_kv is not None),
        grid=(n // tm,),
        in_specs=in_specs,
        out_specs=out_specs,
        out_shape=out_shape,
        scratch_shapes=[pltpu.VMEM((SUBLANES, 2 * d_ff), f32),
                        pltpu.VMEM((tm, d_ff), bf16),
                        pltpu.VMEM((tm, d), bf16)],
        compiler_params=_cparams("arbitrary"),
        name="ffn_ple",
    )(*args)


def kernel(x, p, ssm_norm, ssm_w_in, ssm_log_dt, ssm_lambda_re, ssm_lambda_im, ssm_b_re, ssm_b_im, ssm_c_re, ssm_c_im, ssm_d, ssm_w_glu, kv_norm, kv_w, k_norm, attn_norm, attn_w_q, q_norm, lambda_q1, lambda_k1, lambda_q2, lambda_k2, subln, attn_w_o, rel_bias, ffn_norm, ffn_w_up, ffn_conv_w, ffn_conv_b, ffn_w_down, ple_norm, ple_w_gate, ple_w_proj):
    bsz, seqlen, d = x.shape
    depth = p.shape[0]
    n_a = ssm_norm.shape[0]
    n = bsz * seqlen
    x = x.reshape(n, d).astype(f32)
    p = p.reshape(depth, n, p.shape[-1])
    nk = N_HEADS * 2 * HEAD_DIM
    assert 1 <= n_a < depth
    diag, corner, far = _bias_tables(rel_bias, ATTN_BLOCK, ATTN_BLOCK)
    tables = jax.vmap(functools.partial(_ssm_tables, n_chunks=seqlen // SSM_T))(
        ssm_log_dt, ssm_lambda_re, ssm_lambda_im, ssm_b_re, ssm_b_im, ssm_c_re, ssm_c_im)
    w_in_t = jnp.swapaxes(ssm_w_in, 1, 2).astype(bf16)
    w_glu, w_q, w_o = ssm_w_glu.astype(bf16), attn_w_q.astype(bf16), attn_w_o.astype(bf16)
    w_up, w_down = _interleave_halves(ffn_w_up).astype(bf16), ffn_w_down.astype(bf16)
    conv_w, conv_b = _interleave_halves(ffn_conv_w), _interleave_halves(ffn_conv_b)
    w_gate, w_proj = ple_w_gate.astype(bf16), ple_w_proj.astype(bf16)
    lam_all = (jnp.exp(jnp.sum(lambda_q1.astype(f32) * lambda_k1.astype(f32), axis=-1))
               - jnp.exp(jnp.sum(lambda_q2.astype(f32) * lambda_k2.astype(f32), axis=-1)))
    q = k = vt = None
    for i in range(depth):
        attn = None
        if i < n_a:
            x3 = x.reshape(bsz, seqlen, d)
            ut = _win_t(x3, ssm_norm[i], w_in_t[i])
            yt = _ssm(ut, tables, i)
            x = _glu_res(x3, yt, ut, ssm_d[i], w_glu[i]).reshape(n, d)
        else:
            j = i - n_a
            lam_init = 0.8 - 0.6 * math.exp(-0.3 * i)
            o = _diff_attention(q, k, vt, diag, corner, far, lam_all[j] + lam_init, subln[j], 1.0 - lam_init,
                                bsz, seqlen, ATTN_BLOCK, ATTN_BLOCK)
            attn = (o, w_o[j])
        j_next = i + 1 - n_a
        next_q = next_kv = None
        if 0 <= j_next < depth - n_a:
            next_q = (attn_norm[j_next], w_q[j_next], q_norm[j_next])
            if j_next == 0:
                next_kv = (kv_norm, kv_w[:, :nk].astype(bf16), kv_w[:, nk:].T.astype(bf16), k_norm)
        outs = _ffn_ple(x, p, i,
                        (ffn_norm[i], w_up, conv_w[i], conv_b[i], w_down),
                        (ple_norm[i], w_gate, w_proj),
                        seqlen, attn=attn, next_q=next_q, next_kv=next_kv)
        x = outs[0]
        if next_q is not None:
            q = outs[1]
        if next_kv is not None:
            k, vt = outs[2], outs[3]
    return x.reshape(bsz, seqlen, d)
```

```python
import functools
import math

import jax
import jax.numpy as jnp
from jax import lax
from jax.experimental import pallas as pl
from jax.experimental.pallas import tpu as pltpu

EPS = 1e-6
NEG_INF = -1e30

SSM_GROUP_DIM = 16
N_HEADS = 8
HEAD_DIM = 64
REL_BUCKETS = 32
REL_MAX_EXACT = REL_BUCKETS // 2
REL_MAX_DIST = 128
CONV_WIDTH = 3

LANES = 128
SUBLANES = 8
MXU_DIM = 256
VMEM_LIMIT_BYTES = 56 * 1024 * 1024

SSM_T = MXU_DIM // SSM_GROUP_DIM
SEG_CHUNK = MXU_DIM
ATTN_BLOCK = 256
ATTN_GROUP = 2
ATTN_HEADS_PER_STEP = N_HEADS
CORNER = LANES
V_ROWS = 2 * HEAD_DIM + 16
LOG2E = math.log2(math.e)
GLU_SPLIT = 4
FFN_ROWS = 512
FFN_COLS = MXU_DIM

bf16 = jnp.bfloat16
f32 = jnp.float32


def _cparams(*sem):
    return pltpu.CompilerParams(dimension_semantics=sem, vmem_limit_bytes=VMEM_LIMIT_BYTES)


def _rms(xf, g):
    return xf * lax.rsqrt(jnp.mean(xf * xf, axis=-1, keepdims=True) + EPS) * g


def _gelu_tanh(v):
    c = 2.0 * math.sqrt(2.0 / math.pi)
    return v * jax.nn.sigmoid(v * (c + (c * 0.044715) * (v * v)))


def _seg_norm(y, seg_ref, gain_ref):
    cols = []
    for c in range(y.shape[1] // SEG_CHUNK):
        yc = y[:, c * SEG_CHUNK:(c + 1) * SEG_CHUNK]
        ss = jnp.dot((yc * yc).astype(bf16), seg_ref[...], preferred_element_type=f32)
        cols.append(yc * lax.rsqrt(ss * (1.0 / HEAD_DIM) + EPS) * gain_ref[...])
    return jnp.concatenate(cols, axis=1)


def _segment_ones():
    idx = jnp.arange(SEG_CHUNK) // HEAD_DIM
    return (idx[:, None] == idx[None, :]).astype(bf16)


def _segment_gain(g):
    return jnp.tile(g.astype(f32), SEG_CHUNK // HEAD_DIM)[None, :]


def _store_vt(vt_ref, vt):
    hd2 = 2 * HEAD_DIM
    for hh in range(vt_ref.shape[1]):
        vt_ref[0, hh, :hd2, :] = vt[hh * hd2:(hh + 1) * hd2].astype(vt_ref.dtype)
        vt_ref[0, hh, hd2:, :] = jnp.ones((V_ROWS - hd2, vt.shape[1]), vt_ref.dtype)


def _win_kernel(x_ref, g_ref, wt_ref, ut_ref):
    n_chunks, nt, d = x_ref.shape
    g_n = ut_ref.shape[0]
    xt = pltpu.einshape("ktd->tkd", x_ref[...]).reshape(nt * n_chunks, d)
    h = _rms(xt, g_ref[...]).astype(bf16)
    ut = lax.dot_general(wt_ref[...], h, (((1,), (1,)), ((), ())), preferred_element_type=f32)
    for tt in range(nt):
        blk = ut[:, tt * n_chunks:(tt + 1) * n_chunks].reshape(g_n, SSM_GROUP_DIM, n_chunks)
        ut_ref[:, tt * SSM_GROUP_DIM:(tt + 1) * SSM_GROUP_DIM, :] = blk.astype(ut_ref.dtype)


def _by_chunk(x3):
    bsz, seqlen, d = x3.shape
    return x3.reshape(bsz, seqlen // SSM_T, SSM_T // SUBLANES, SUBLANES, d)


def _chunk_spec(x5):
    _, n_chunks, _, nt, d = x5.shape
    return pl.BlockSpec((None, n_chunks, None, nt, d), lambda b, t: (b, 0, t, 0, 0))


def _win_t(x3, g, w_t):
    bsz, seqlen, d = x3.shape
    n_chunks = seqlen // SSM_T
    assert n_chunks == LANES
    g_n = d // SSM_GROUP_DIM
    x5 = _by_chunk(x3)
    return pl.pallas_call(
        _win_kernel,
        grid=(bsz, SSM_T // SUBLANES),
        in_specs=[_chunk_spec(x5),
                  pl.BlockSpec((1, d), lambda b, t: (0, 0)),
                  pl.BlockSpec(w_t.shape, lambda b, t: (0, 0))],
        out_specs=pl.BlockSpec((g_n, SUBLANES * SSM_GROUP_DIM, n_chunks), lambda b, t: (0, t, b)),
        out_shape=jax.ShapeDtypeStruct((g_n, SSM_T * SSM_GROUP_DIM, bsz * n_chunks), bf16),
        compiler_params=_cparams("parallel", "parallel"),
        name="win_t",
    )(x5, g.astype(f32)[None, :], w_t)


def _ssm_kernel(n_rounds, u_ref, mt_ref, w_ref, v_ref, pw_ref, y_ref, xin_ref):
    u0 = u_ref[0]
    u1 = u_ref[1]
    half = LANES
    seg = LANES
    st = (jnp.dot(w_ref[0], u0, preferred_element_type=f32)
          + jnp.dot(w_ref[1], u1, preferred_element_type=f32))
    n_seq = st.shape[1] // seg
    xr = jnp.stack([st[:half, b * seg:(b + 1) * seg] for b in range(n_seq)])
    xi = jnp.stack([st[half:, b * seg:(b + 1) * seg] for b in range(n_seq)])
    chunk = lax.broadcasted_iota(jnp.int32, xr.shape, 2)
    for j in range(n_rounds):
        d = 1 << j
        ar = pw_ref[j, 0]
        ai = pw_ref[j, 1]
        sr = pltpu.roll(xr, d, 2)
        si = pltpu.roll(xi, d, 2)
        xr, xi = xr + (ar * sr - ai * si), xi + (ar * si + ai * sr)
    xr = jnp.where(chunk >= 1, pltpu.roll(xr, 1, 2), 0.0).astype(bf16)
    xi = jnp.where(chunk >= 1, pltpu.roll(xi, 1, 2), 0.0).astype(bf16)
    for b in range(n_seq):
        xin_ref[:half, b * seg:(b + 1) * seg] = xr[b]
        xin_ref[half:, b * seg:(b + 1) * seg] = xi[b]
    ys = jnp.dot(v_ref[...], xin_ref[...], preferred_element_type=f32)
    y_ref[0] = (jnp.dot(mt_ref[0], u0, preferred_element_type=f32) + ys[:MXU_DIM]).astype(y_ref.dtype)
    y_ref[1] = (jnp.dot(mt_ref[1], u1, preferred_element_type=f32) + ys[MXU_DIM:]).astype(y_ref.dtype)


def _ssm_tables(log_dt, lam_re, lam_im, b_re, b_im, c_re, c_im, n_chunks):
    g_n, p_n = lam_re.shape
    t_n, c_n = SSM_T, SSM_GROUP_DIM
    dt = jnp.exp(log_dt.astype(f32))[:, None]
    lr = lam_re.astype(f32)
    li = lam_im.astype(f32)

    def power(n):
        nn = jnp.asarray(n, f32).reshape((-1, 1, 1))
        mag = jnp.exp(lr * dt * nn)
        return mag * jnp.cos(li * dt * nn), mag * jnp.sin(li * dt * nn)

    ab_r, ab_i = power([1])
    ab_r, ab_i = ab_r[0], ab_i[0]
    den = lr * lr + li * li
    f_r = ((ab_r - 1.0) * lr + ab_i * li) / den
    f_i = (ab_i * lr - (ab_r - 1.0) * li) / den
    br = b_re.astype(f32)
    bi = b_im.astype(f32)
    bb_r = f_r[..., None] * br - f_i[..., None] * bi
    bb_i = f_r[..., None] * bi + f_i[..., None] * br
    cr = c_re.astype(f32)
    ci = c_im.astype(f32)

    qr, qi = power(t_n - 1 - jnp.arange(t_n))
    t_r = qr[..., None] * bb_r[None] - qi[..., None] * bb_i[None]
    t_i = qr[..., None] * bb_i[None] + qi[..., None] * bb_r[None]
    kern = (jnp.einsum("gcp,dgpk->gcdk", cr, t_r, precision=lax.Precision.HIGHEST)
            - jnp.einsum("gcp,dgpk->gcdk", ci, t_i, precision=lax.Precision.HIGHEST))
    kern = kern.reshape(g_n, c_n, t_n * c_n)
    kern = jnp.concatenate([kern, jnp.zeros_like(kern)], axis=-1)
    mt = jnp.concatenate([kern[:, :, (t_n - 1 - t) * c_n:(2 * t_n - 1 - t) * c_n] for t in range(t_n)], axis=1)
    w_r = t_r.transpose(1, 2, 0, 3).reshape(g_n, p_n, t_n * c_n)
    w_i = t_i.transpose(1, 2, 0, 3).reshape(g_n, p_n, t_n * c_n)
    pr1, pi1 = power(1 + jnp.arange(t_n))
    pr1 = pr1.transpose(1, 0, 2)
    pi1 = pi1.transpose(1, 0, 2)
    v_r = (cr[:, None] * pr1[:, :, None, :] - ci[:, None] * pi1[:, :, None, :]).reshape(g_n, t_n * c_n, p_n)
    v_i = -(cr[:, None] * pi1[:, :, None, :] + ci[:, None] * pr1[:, :, None, :]).reshape(g_n, t_n * c_n, p_n)

    gp = g_n // 2
    z = jnp.zeros((gp, p_n, t_n * c_n), f32)
    w_r = w_r.reshape(gp, 2, p_n, t_n * c_n)
    w_i = w_i.reshape(gp, 2, p_n, t_n * c_n)
    w0 = jnp.concatenate([w_r[:, 0], z, w_i[:, 0], z], axis=1)
    w1 = jnp.concatenate([z, w_r[:, 1], z, w_i[:, 1]], axis=1)
    w_pack = jnp.stack([w0, w1], axis=1)
    zv = jnp.zeros((gp, t_n * c_n, p_n), f32)
    v_r = v_r.reshape(gp, 2, t_n * c_n, p_n)
    v_i = v_i.reshape(gp, 2, t_n * c_n, p_n)
    v_pack = jnp.concatenate([
        jnp.concatenate([v_r[:, 0], zv, v_i[:, 0], zv], axis=-1),
        jnp.concatenate([zv, v_r[:, 1], zv, v_i[:, 1]], axis=-1)], axis=1)
    n_rounds = (n_chunks - 1).bit_length()
    sr, si = power(t_n * (2 ** jnp.arange(max(n_rounds, 1))))
    pw = jnp.stack([sr.reshape(-1, gp, 2 * p_n), si.reshape(-1, gp, 2 * p_n)], axis=2)
    reach = (jnp.arange(LANES)[None, :] >= (2 ** jnp.arange(pw.shape[0]))[:, None]).astype(f32)
    pw = pw.transpose(1, 0, 2, 3)[..., None] * reach[None, :, None, None, :]
    return mt.reshape(gp, 2, t_n * c_n, t_n * c_n).astype(bf16), w_pack.astype(bf16), v_pack.astype(bf16), pw


def _ssm(ut, tables, layer):
    mt, w_pack, v_pack, pw = tables
    n_rounds = pw.shape[2]
    g_n, k, cols = ut.shape
    gp = g_n // 2
    return pl.pallas_call(
        functools.partial(_ssm_kernel, n_rounds),
        grid=(gp,),
        in_specs=[pl.BlockSpec((2, k, cols), lambda g: (g, 0, 0)),
                  pl.BlockSpec((None, None, 2, k, k), lambda g: (layer, g, 0, 0, 0)),
                  pl.BlockSpec((None, None, 2, k, k), lambda g: (layer, g, 0, 0, 0)),
                  pl.BlockSpec((None, None, 2 * k, k), lambda g: (layer, g, 0, 0)),
                  pl.BlockSpec((None, None) + pw.shape[2:], lambda g: (layer, g, 0, 0, 0, 0))],
        out_specs=pl.BlockSpec((2, k, cols), lambda g: (g, 0, 0)),
        out_shape=jax.ShapeDtypeStruct((g_n, k, cols), bf16),
        scratch_shapes=[pltpu.VMEM((k, cols), bf16)],
        compiler_params=_cparams("parallel"),
        name="ssm",
    )(ut, mt, w_pack, v_pack, pw)


def _glu_kernel(x_ref, y_ref, u_ref, d_ref, w_ref, o_ref):
    n_chunks, nt, dm = x_ref.shape
    g_n = y_ref.shape[0]
    per = nt // GLU_SPLIT
    upd = []
    for s in range(GLU_SPLIT):
        cols = []
        for tt in range(s * per, (s + 1) * per):
            sl = slice(tt * SSM_GROUP_DIM, (tt + 1) * SSM_GROUP_DIM)
            y = y_ref[:, sl, :].astype(f32) + d_ref[...] * u_ref[:, sl, :].astype(f32)
            cols.append(_gelu_tanh(y).reshape(g_n * SSM_GROUP_DIM, n_chunks))
        gt = jnp.concatenate(cols, axis=1)
        r = jnp.dot(gt.T.astype(bf16), w_ref[...], preferred_element_type=f32)
        upd.append(r[:, :dm] * jax.nn.sigmoid(r[:, dm:]))
    upd = jnp.concatenate(upd, axis=0).reshape(nt, n_chunks, dm)
    o_ref[...] = x_ref[...] + pltpu.einshape("tkd->ktd", upd)


def _glu_res(x3, yt, ut, d_skip, w_glu):
    bsz, seqlen, d = x3.shape
    g_n, _, cols = yt.shape
    n_chunks = cols // bsz
    d_tab = jnp.broadcast_to(d_skip.astype(f32).reshape(g_n, SSM_GROUP_DIM, 1), (g_n, SSM_GROUP_DIM, n_chunks))
    feat = pl.BlockSpec((g_n, SUBLANES * SSM_GROUP_DIM, n_chunks), lambda b, t: (0, t, b))
    x5 = _by_chunk(x3)
    out = pl.pallas_call(
        _glu_kernel,
        grid=(bsz, SSM_T // SUBLANES),
        in_specs=[_chunk_spec(x5), feat, feat,
                  pl.BlockSpec(d_tab.shape, lambda b, t: (0, 0, 0)),
                  pl.BlockSpec(w_glu.shape, lambda b, t: (0, 0))],
        out_specs=_chunk_spec(x5),
        out_shape=jax.ShapeDtypeStruct(x5.shape, f32),
        compiler_params=_cparams("parallel", "parallel"),
        name="glu_res",
    )(x5, yt, ut, d_tab, w_glu)
    return out.reshape(bsz, seqlen, d)


def _attn_kernel(tq, tk, out_scale, lam_ref, far_ref, q_ref, k_ref, vt_ref, diag_ref, corner_ref, sub_ref,
                 o_ref, qm_ref, m_ref, acc_ref, s_ref):
    hd2 = 2 * HEAD_DIM
    heads = range(vt_ref.shape[1])
    lane = lax.broadcasted_iota(jnp.int32, (tq, hd2), 1)
    far = [far_ref[pl.program_id(1) * len(heads) + hh] for hh in heads]

    def q_block(qi, carry):
        rows = pl.ds(pl.multiple_of(qi * tq, tq), tq)
        for hh in heads:
            q = q_ref[0, rows, hh * hd2:(hh + 1) * hd2].astype(f32) * (HEAD_DIM ** -0.5 * LOG2E)
            qm_ref[hh, :tq] = jnp.where(lane < HEAD_DIM, q, 0.0).astype(bf16)
            qm_ref[hh, tq:] = jnp.where(lane >= HEAD_DIM, q, 0.0).astype(bf16)
        m_ref[...] = jnp.full_like(m_ref, NEG_INF)
        acc_ref[...] = jnp.zeros_like(acc_ref)

        def fill(ki, n_blocks):
            for hh in heads:
                k = k_ref[0, pl.ds(pl.multiple_of(ki * tk, tk), n_blocks * tk), hh * hd2:(hh + 1) * hd2]
                s_ref[hh, :n_blocks * tk] = lax.dot_general(k, qm_ref[hh], (((1,), (1,)), ((), ())),
                                                            preferred_element_type=f32)

        def accumulate(hh, ki, parts):
            s = parts[0] if len(parts) == 1 else jnp.concatenate(parts, axis=0)
            vt = vt_ref[0, hh, :, pl.ds(pl.multiple_of(ki * tk, tk), len(parts) * tk)]
            m_prev = m_ref[hh]
            m_new = jnp.maximum(m_prev, jnp.max(s, axis=0, keepdims=True) + far[hh])
            p = jnp.exp2(s - (m_new - far[hh]))
            acc_ref[hh] = (jnp.exp2(m_prev - m_new) * acc_ref[hh]
                           + jnp.dot(vt, p.astype(bf16), preferred_element_type=f32))
            m_ref[hh] = m_new

        def neighbour(hh, s, near):
            c = near * corner_ref[hh]
            lo = s[tk - CORNER:]
            lo = jnp.concatenate([lo[:, :CORNER] + c, lo[:, CORNER:tq], lo[:, tq:tq + CORNER] + c,
                                  lo[:, tq + CORNER:]], axis=1)
            return jnp.concatenate([s[:tk - CORNER], lo], axis=0)

        def diagonal(hh, s):
            b = diag_ref[hh]
            return s + jnp.concatenate([b, b], axis=1)

        grp = ATTN_GROUP

        def block(hh, j):
            return s_ref[hh, j * tk:(j + 1) * tk]

        def group(i, carry):
            fill(grp * i, grp)
            near = jnp.where(grp * i + grp - 1 == qi - 1, 1.0, 0.0)
            for hh in heads:
                accumulate(hh, grp * i, [block(hh, j) for j in range(grp - 1)]
                           + [neighbour(hh, block(hh, grp - 1), near)])
            return carry

        lax.fori_loop(0, qi // grp, group, 0)

        for r in range(grp):
            @pl.when(qi % grp == r)
            def _(r=r):
                fill(qi - r, r + 1)
                for hh in heads:
                    parts = [block(hh, j) for j in range(r)]
                    if r:
                        parts[-1] = neighbour(hh, parts[-1], 1.0)
                    accumulate(hh, qi - r, parts + [diagonal(hh, block(hh, r))])

        for hh in heads:
            acc = acc_ref[hh]
            ot = acc[:hd2] / acc[hd2:hd2 + 1]
            ot = ot[:, :tq] - lam_ref[0] * ot[:, tq:]
            ot = ot * lax.rsqrt(jnp.mean(ot * ot, axis=0, keepdims=True) + EPS)
            o_ref[0, rows, hh * hd2:(hh + 1) * hd2] = (ot.T * sub_ref[...] * out_scale).astype(o_ref.dtype)
        return carry

    lax.fori_loop(0, k_ref.shape[1] // tq, q_block, 0)


def _bias_tables(rel_bias, tq, tk):
    n = jnp.arange(2 * tq)
    nf = jnp.maximum(n, REL_MAX_EXACT).astype(f32)
    large = REL_MAX_EXACT + (jnp.log(nf / REL_MAX_EXACT) / math.log(REL_MAX_DIST / REL_MAX_EXACT)
                             * (REL_BUCKETS - REL_MAX_EXACT)).astype(jnp.int32)
    large = jnp.minimum(large, REL_BUCKETS - 1)
    bucket = jnp.where(n < REL_MAX_EXACT, n, large)
    onehot = (bucket[:, None] == jnp.arange(REL_BUCKETS)[None, :]).astype(f32)
    by_dist = jnp.dot(onehot, rel_bias.astype(f32), precision=lax.Precision.HIGHEST).T * LOG2E
    n_heads = by_dist.shape[0]
    neg = jnp.full((n_heads, tk), NEG_INF, f32)
    period = tq + tk
    tiles = []
    for d, tail in ((0, neg), (1, by_dist[:, tq - tk:tq])):
        r = jnp.concatenate([by_dist[:, d * tq:(d + 1) * tq], tail], axis=1)
        t = jnp.tile(r, (1, tk))[:, :tk * (period - 1)].reshape(n_heads, tk, period - 1)
        tiles.append(t[:, :, :tq])
    far = rel_bias.astype(f32)[REL_BUCKETS - 1] * LOG2E
    corner = tiles[1][:, tk - CORNER:, :CORNER] - far[:, None, None]
    return tiles[0] - far[:, None, None], corner, far


def _far_bucket_from():
    n = REL_MAX_EXACT
    while REL_MAX_EXACT + int(math.log(n / REL_MAX_EXACT) / math.log(REL_MAX_DIST / REL_MAX_EXACT)
                              * (REL_BUCKETS - REL_MAX_EXACT)) < REL_BUCKETS - 1:
        n += 1
    return n


def _diff_attention(q, k, vt, diag, corner, far, lam, subln, out_scale, bsz, seqlen, tq, tk):
    assert tq == tk and tq - CORNER + 1 >= _far_bucket_from()
    hd2 = 2 * HEAD_DIM
    q3 = q.reshape(bsz, seqlen, N_HEADS * hd2)
    k3 = k.reshape(bsz, seqlen, N_HEADS * hd2)
    smem = pl.BlockSpec(memory_space=pltpu.SMEM)
    hps = ATTN_HEADS_PER_STEP
    o = pl.pallas_call(
        functools.partial(_attn_kernel, tq, tk, out_scale),
        grid=(bsz, N_HEADS // hps),
        in_specs=[smem, smem,
                  pl.BlockSpec((1, seqlen, hps * hd2), lambda b, h: (b, 0, h)),
                  pl.BlockSpec((1, seqlen, hps * hd2), lambda b, h: (b, 0, h)),
                  pl.BlockSpec((1, hps, V_ROWS, seqlen), lambda b, h: (b, h, 0, 0)),
                  pl.BlockSpec((hps, tk, tq), lambda b, h: (h, 0, 0)),
                  pl.BlockSpec((hps, CORNER, CORNER), lambda b, h: (h, 0, 0)),
                  pl.BlockSpec((1, hd2), lambda b, h: (0, 0))],
        out_specs=pl.BlockSpec((1, seqlen, hps * hd2), lambda b, h: (b, 0, h)),
        out_shape=jax.ShapeDtypeStruct((bsz, seqlen, N_HEADS * hd2), bf16),
        scratch_shapes=[pltpu.VMEM((hps, 2 * tq, hd2), bf16), pltpu.VMEM((hps, 1, 2 * tq), f32),
                        pltpu.VMEM((hps, V_ROWS, 2 * tq), f32), pltpu.VMEM((hps, ATTN_GROUP * tk, 2 * tq), f32)],
        compiler_params=_cparams("parallel", "parallel"),
        name="diff_attn",
    )(lam.reshape(1).astype(f32), far, q3, k3, vt, diag, corner, subln.astype(f32)[None, :])
    return o.reshape(bsz * seqlen, N_HEADS * hd2)


def _ffn_ple_kernel(tiles_per_seq, has_attn, has_q, has_kv, *refs):
    refs = list(refs)
    x_ref, p_ref = refs[:2]
    del refs[:2]
    if has_attn:
        a_ref, wo_ref = refs[:2]
        del refs[:2]
    fg_ref, wup_ref, cw_ref, cb_ref, wdn_ref, pg_ref, wgate_ref, wproj_ref = refs[:8]
    del refs[:8]
    if has_q or has_kv:
        seg_ref = refs.pop(0)
    if has_q:
        qg_ref, wq_ref, qgain_ref = refs[:3]
        del refs[:3]
    if has_kv:
        kvg_ref, wk_ref, wvt_ref, kgain_ref = refs[:4]
        del refs[:4]
    o_ref = refs.pop(0)
    if has_q:
        q_ref = refs.pop(0)
    if has_kv:
        k_ref, vt_ref = refs[:2]
        del refs[:2]
    carry_ref, act_ref, h_ref = refs

    i = pl.program_id(0)
    tm = x_ref.shape[0]
    d_ff = wdn_ref.shape[0]
    fc = FFN_COLS
    halo = SUBLANES

    @pl.when(i % tiles_per_seq == 0)
    def _():
        carry_ref[...] = jnp.zeros_like(carry_ref)

    x = x_ref[...]
    if has_attn:
        x = x + jnp.dot(a_ref[...], wo_ref[...], preferred_element_type=f32)
    h_ref[...] = _rms(x, fg_ref[...]).astype(bf16)
    row = lax.broadcasted_iota(jnp.int32, (halo, fc), 0)

    def up_proj(col):
        return jnp.dot(h_ref[...], wup_ref[:, col:col + fc], preferred_element_type=f32)

    def conv(col, up):
        prev = carry_ref[:, col:col + fc]
        carry_ref[:, col:col + fc] = up[tm - halo:]
        c = cb_ref[:, col:col + fc] + cw_ref[2:3, col:col + fc] * up
        for back in range(1, CONV_WIDTH):
            r = pltpu.roll(up, back, 0)
            head = jnp.where(row < back, pltpu.roll(prev, back, 0), r[:halo])
            c = c + cw_ref[2 - back:3 - back, col:col + fc] * jnp.concatenate([head, r[halo:]], axis=0)
        return c

    cols = list(range(0, d_ff, fc))
    ups = (up_proj(cols[0]), up_proj(d_ff + cols[0]))
    for idx, col in enumerate(cols):
        nxt = (up_proj(cols[idx + 1]), up_proj(d_ff + cols[idx + 1])) if idx + 1 < len(cols) else None
        act_ref[:, col:col + fc] = (_gelu_tanh(conv(col, ups[0])) * conv(d_ff + col, ups[1])).astype(bf16)
        ups = nxt
    halves = [slice(0, tm // 2), slice(tm // 2, tm)]
    xs = [x[r] + jnp.dot(act_ref[r, :], wdn_ref[...], preferred_element_type=f32) for r in halves]
    projs = [jnp.dot(p_ref[r, :].astype(bf16), wproj_ref[...], preferred_element_type=f32) for r in halves]
    gates = [jax.nn.sigmoid(jnp.dot(_rms(xh, pg_ref[...]).astype(bf16), wgate_ref[...],
                                    preferred_element_type=f32)) for xh in xs]
    xs = [xh + g * pr for xh, g, pr in zip(xs, gates, projs)]
    for r, xh in zip(halves, xs):
        o_ref[r, :] = xh
    if has_q or has_kv:
        xn = jnp.concatenate([xh * lax.rsqrt(jnp.mean(xh * xh, axis=-1, keepdims=True) + EPS) for xh in xs],
                             axis=0)
    if has_q:
        q = jnp.dot((xn * qg_ref[...]).astype(bf16), wq_ref[...], preferred_element_type=f32)
        q_ref[...] = _seg_norm(q, seg_ref, qgain_ref).astype(q_ref.dtype)
    if has_kv:
        hk = (xn * kvg_ref[...]).astype(bf16)
        k = jnp.dot(hk, wk_ref[...], preferred_element_type=f32)
        k_ref[...] = _seg_norm(k, seg_ref, kgain_ref).astype(k_ref.dtype)
        _store_vt(vt_ref, lax.dot_general(wvt_ref[...], hk, (((1,), (1,)), ((), ())),
                                          preferred_element_type=f32))


def _ffn_ple(x, p, layer, ffn, ple, seqlen, attn=None, next_q=None, next_kv=None):
    n, d = x.shape
    ffn_g, w_up, conv_w, conv_b, w_down = ffn
    ple_g, w_gate, w_proj = ple
    d_ff = w_down.shape[-2]
    tm = FFN_ROWS
    bsz = n // seqlen
    tps = seqlen // tm
    row = lambda i: (i, 0)

    def full(a):
        if a.ndim == 3:
            return pl.BlockSpec((None,) + a.shape[1:], lambda i: (layer, 0, 0), pipeline_mode=pl.Buffered(1))
        return pl.BlockSpec(a.shape, lambda i: (0, 0), pipeline_mode=pl.Buffered(1))

    vec = lambda a: a.astype(f32)[None, :]
    args = [x, p]
    in_specs = [pl.BlockSpec((tm, d), row), pl.BlockSpec((None, tm, p.shape[2]), lambda i: (layer, i, 0))]
    if attn is not None:
        args += [attn[0], attn[1]]
        in_specs += [pl.BlockSpec((tm, attn[0].shape[1]), row), full(attn[1])]
    weights = [vec(ffn_g), w_up, conv_w.astype(f32), vec(conv_b), w_down, vec(ple_g), w_gate, w_proj]
    out_specs = [pl.BlockSpec((tm, d), row)]
    out_shape = [jax.ShapeDtypeStruct((n, d), f32)]
    if next_q is not None or next_kv is not None:
        weights.append(_segment_ones())
    if next_q is not None:
        weights += [vec(next_q[0]), next_q[1], _segment_gain(next_q[2])]
        out_specs.append(pl.BlockSpec((tm, next_q[1].shape[1]), row))
        out_shape.append(jax.ShapeDtypeStruct((n, next_q[1].shape[1]), bf16))
    if next_kv is not None:
        weights += [vec(next_kv[0]), next_kv[1], next_kv[2], _segment_gain(next_kv[3])]
        out_specs += [pl.BlockSpec((tm, next_kv[1].shape[1]), row),
                      pl.BlockSpec((1, N_HEADS, V_ROWS, tm), lambda i: (i // tps, 0, 0, i % tps))]
        out_shape += [jax.ShapeDtypeStruct((n, next_kv[1].shape[1]), bf16),
                      jax.ShapeDtypeStruct((bsz, N_HEADS, V_ROWS, seqlen), bf16)]
    args += weights
    in_specs += [full(w) for w in weights]
    return pl.pallas_call(
        functools.partial(_ffn_ple_kernel, tps, attn is not None, next_q is not None, next_kv is not None),
        grid=(n // tm,),
        in_specs=in_specs,
        out_specs=out_specs,
        out_shape=out_shape,
        scratch_shapes=[pltpu.VMEM((SUBLANES, 2 * d_ff), f32),
                        pltpu.VMEM((tm, d_ff), bf16),
                        pltpu.VMEM((tm, d), bf16)],
        compiler_params=_cparams("arbitrary"),
        name="ffn_ple",
    )(*args)


def kernel(x, p, ssm_norm, ssm_w_in, ssm_log_dt, ssm_lambda_re, ssm_lambda_im, ssm_b_re, ssm_b_im, ssm_c_re, ssm_c_im, ssm_d, ssm_w_glu, kv_norm, kv_w, k_norm, attn_norm, attn_w_q, q_norm, lambda_q1, lambda_k1, lambda_q2, lambda_k2, subln, attn_w_o, rel_bias, ffn_norm, ffn_w_up, ffn_conv_w, ffn_conv_b, ffn_w_down, ple_norm, ple_w_gate, ple_w_proj):
    bsz, seqlen, d = x.shape
    depth = p.shape[0]
    n_a = ssm_norm.shape[0]
    n = bsz * seqlen
    x = x.reshape(n, d).astype(f32)
    p = p.reshape(depth, n, p.shape[-1])
    nk = N_HEADS * 2 * HEAD_DIM
    assert 1 <= n_a < depth
    diag, corner, far = _bias_tables(rel_bias, ATTN_BLOCK, ATTN_BLOCK)
    tables = jax.vmap(functools.partial(_ssm_tables, n_chunks=seqlen // SSM_T))(
        ssm_log_dt, ssm_lambda_re, ssm_lambda_im, ssm_b_re, ssm_b_im, ssm_c_re, ssm_c_im)
    w_in_t = jnp.swapaxes(ssm_w_in, 1, 2).astype(bf16)
    w_glu, w_q, w_o = ssm_w_glu.astype(bf16), attn_w_q.astype(bf16), attn_w_o.astype(bf16)
    w_up, w_down = ffn_w_up.astype(bf16), ffn_w_down.astype(bf16)
    w_gate, w_proj = ple_w_gate.astype(bf16), ple_w_proj.astype(bf16)
    lam_all = (jnp.exp(jnp.sum(lambda_q1.astype(f32) * lambda_k1.astype(f32), axis=-1))
               - jnp.exp(jnp.sum(lambda_q2.astype(f32) * lambda_k2.astype(f32), axis=-1)))
    q = k = vt = None
    for i in range(depth):
        attn = None
        if i < n_a:
            x3 = x.reshape(bsz, seqlen, d)
            ut = _win_t(x3, ssm_norm[i], w_in_t[i])
            yt = _ssm(ut, tables, i)
            x = _glu_res(x3, yt, ut, ssm_d[i], w_glu[i]).reshape(n, d)
        else:
            j = i - n_a
            lam_init = 0.8 - 0.6 * math.exp(-0.3 * i)
            o = _diff_attention(q, k, vt, diag, corner, far, lam_all[j] + lam_init, subln[j], 1.0 - lam_init,
                                bsz, seqlen, ATTN_BLOCK, ATTN_BLOCK)
            attn = (o, w_o[j])
        j_next = i + 1 - n_a
        next_q = next_kv = None
        if 0 <= j_next < depth - n_a:
            next_q = (attn_norm[j_next], w_q[j_next], q_norm[j_next])
            if j_next == 0:
                next_kv = (kv_norm, kv_w[:, :nk].astype(bf16), kv_w[:, nk:].T.astype(bf16), k_norm)
        outs = _ffn_ple(x, p, i,
                        (ffn_norm[i], w_up, ffn_conv_w[i], ffn_conv_b[i], w_down),
                        (ple_norm[i], w_gate, w_proj),
                        seqlen, attn=attn, next_q=next_q, next_kv=next_kv)
        x = outs[0]
        if next_q is not None:
            q = outs[1]
        if next_kv is not None:
            k, vt = outs[2], outs[3]
    return x.reshape(bsz, seqlen, d)
```

```python
import functools
import math

import jax
import jax.numpy as jnp
from jax import lax
from jax.experimental import pallas as pl
from jax.experimental.pallas import tpu as pltpu

EPS = 1e-6
NEG_INF = -1e30

SSM_GROUP_DIM = 16
N_HEADS = 8
HEAD_DIM = 64
REL_BUCKETS = 32
REL_MAX_EXACT = REL_BUCKETS // 2
REL_MAX_DIST = 128
CONV_WIDTH = 3

LANES = 128
SUBLANES = 8
MXU_DIM = 256
VMEM_LIMIT_BYTES = 56 * 1024 * 1024

SSM_T = MXU_DIM // SSM_GROUP_DIM
SEG_CHUNK = MXU_DIM
ATTN_BLOCK = 256
ATTN_GROUP = 2
ATTN_HEADS_PER_STEP = N_HEADS
CORNER = LANES
V_ROWS = 2 * HEAD_DIM + 16
LOG2E = math.log2(math.e)
GLU_SPLIT = 2
FFN_ROWS = 512
FFN_COLS = MXU_DIM

bf16 = jnp.bfloat16
f32 = jnp.float32


def _cparams(*sem):
    return pltpu.CompilerParams(dimension_semantics=sem, vmem_limit_bytes=VMEM_LIMIT_BYTES)


def _rms(xf, g):
    return xf * lax.rsqrt(jnp.mean(xf * xf, axis=-1, keepdims=True) + EPS) * g


def _gelu_tanh(v):
    c = 2.0 * math.sqrt(2.0 / math.pi)
    return v * jax.nn.sigmoid(v * (c + (c * 0.044715) * (v * v)))


def _seg_norm(y, seg_ref, gain_ref):
    cols = []
    for c in range(y.shape[1] // SEG_CHUNK):
        yc = y[:, c * SEG_CHUNK:(c + 1) * SEG_CHUNK]
        ss = jnp.dot((yc * yc).astype(bf16), seg_ref[...], preferred_element_type=f32)
        cols.append(yc * lax.rsqrt(ss * (1.0 / HEAD_DIM) + EPS) * gain_ref[...])
    return jnp.concatenate(cols, axis=1)


def _segment_ones():
    idx = jnp.arange(SEG_CHUNK) // HEAD_DIM
    return (idx[:, None] == idx[None, :]).astype(bf16)


def _segment_gain(g):
    return jnp.tile(g.astype(f32), SEG_CHUNK // HEAD_DIM)[None, :]


def _store_vt(vt_ref, vt):
    hd2 = 2 * HEAD_DIM
    for hh in range(vt_ref.shape[1]):
        vt_ref[0, hh, :hd2, :] = vt[hh * hd2:(hh + 1) * hd2].astype(vt_ref.dtype)
        vt_ref[0, hh, hd2:, :] = jnp.ones((V_ROWS - hd2, vt.shape[1]), vt_ref.dtype)


def _win_kernel(x_ref, g_ref, wt_ref, ut_ref):
    n_chunks, nt, d = x_ref.shape
    g_n = ut_ref.shape[0]
    xt = pltpu.einshape("ktd->tkd", x_ref[...]).reshape(nt * n_chunks, d)
    h = _rms(xt, g_ref[...]).astype(bf16)
    ut = lax.dot_general(wt_ref[...], h, (((1,), (1,)), ((), ())), preferred_element_type=f32)
    for tt in range(nt):
        blk = ut[:, tt * n_chunks:(tt + 1) * n_chunks].reshape(g_n, SSM_GROUP_DIM, n_chunks)
        ut_ref[:, tt * SSM_GROUP_DIM:(tt + 1) * SSM_GROUP_DIM, :] = blk.astype(ut_ref.dtype)


def _by_chunk(x3):
    bsz, seqlen, d = x3.shape
    return x3.reshape(bsz, seqlen // SSM_T, SSM_T // SUBLANES, SUBLANES, d)


def _chunk_spec(x5):
    _, n_chunks, _, nt, d = x5.shape
    return pl.BlockSpec((None, n_chunks, None, nt, d), lambda b, t: (b, 0, t, 0, 0))


def _win_t(x3, g, w_t):
    bsz, seqlen, d = x3.shape
    n_chunks = seqlen // SSM_T
    assert n_chunks == LANES
    g_n = d // SSM_GROUP_DIM
    x5 = _by_chunk(x3)
    return pl.pallas_call(
        _win_kernel,
        grid=(bsz, SSM_T // SUBLANES),
        in_specs=[_chunk_spec(x5),
                  pl.BlockSpec((1, d), lambda b, t: (0, 0)),
                  pl.BlockSpec(w_t.shape, lambda b, t: (0, 0))],
        out_specs=pl.BlockSpec((g_n, SUBLANES * SSM_GROUP_DIM, n_chunks), lambda b, t: (0, t, b)),
        out_shape=jax.ShapeDtypeStruct((g_n, SSM_T * SSM_GROUP_DIM, bsz * n_chunks), bf16),
        compiler_params=_cparams("parallel", "parallel"),
        name="win_t",
    )(x5, g.astype(f32)[None, :], w_t)


def _ssm_kernel(n_rounds, u_ref, mt_ref, w_ref, v_ref, pw_ref, y_ref, xin_ref):
    u0 = u_ref[0]
    u1 = u_ref[1]
    half = LANES
    seg = LANES
    st = (jnp.dot(w_ref[0], u0, preferred_element_type=f32)
          + jnp.dot(w_ref[1], u1, preferred_element_type=f32))
    n_seq = st.shape[1] // seg
    xr = jnp.stack([st[:half, b * seg:(b + 1) * seg] for b in range(n_seq)])
    xi = jnp.stack([st[half:, b * seg:(b + 1) * seg] for b in range(n_seq)])
    chunk = lax.broadcasted_iota(jnp.int32, xr.shape, 2)
    for j in range(n_rounds):
        d = 1 << j
        ar = pw_ref[j, 0]
        ai = pw_ref[j, 1]
        sr = pltpu.roll(xr, d, 2)
        si = pltpu.roll(xi, d, 2)
        xr, xi = xr + (ar * sr - ai * si), xi + (ar * si + ai * sr)
    xr = jnp.where(chunk >= 1, pltpu.roll(xr, 1, 2), 0.0).astype(bf16)
    xi = jnp.where(chunk >= 1, pltpu.roll(xi, 1, 2), 0.0).astype(bf16)
    for b in range(n_seq):
        xin_ref[:half, b * seg:(b + 1) * seg] = xr[b]
        xin_ref[half:, b * seg:(b + 1) * seg] = xi[b]
    ys = jnp.dot(v_ref[...], xin_ref[...], preferred_element_type=f32)
    y_ref[0] = (jnp.dot(mt_ref[0], u0, preferred_element_type=f32) + ys[:MXU_DIM]).astype(y_ref.dtype)
    y_ref[1] = (jnp.dot(mt_ref[1], u1, preferred_element_type=f32) + ys[MXU_DIM:]).astype(y_ref.dtype)


def _ssm_tables(log_dt, lam_re, lam_im, b_re, b_im, c_re, c_im, n_chunks):
    g_n, p_n = lam_re.shape
    t_n, c_n = SSM_T, SSM_GROUP_DIM
    dt = jnp.exp(log_dt.astype(f32))[:, None]
    lr = lam_re.astype(f32)
    li = lam_im.astype(f32)

    def power(n):
        nn = jnp.asarray(n, f32).reshape((-1, 1, 1))
        mag = jnp.exp(lr * dt * nn)
        return mag * jnp.cos(li * dt * nn), mag * jnp.sin(li * dt * nn)

    ab_r, ab_i = power([1])
    ab_r, ab_i = ab_r[0], ab_i[0]
    den = lr * lr + li * li
    f_r = ((ab_r - 1.0) * lr + ab_i * li) / den
    f_i = (ab_i * lr - (ab_r - 1.0) * li) / den
    br = b_re.astype(f32)
    bi = b_im.astype(f32)
    bb_r = f_r[..., None] * br - f_i[..., None] * bi
    bb_i = f_r[..., None] * bi + f_i[..., None] * br
    cr = c_re.astype(f32)
    ci = c_im.astype(f32)

    qr, qi = power(t_n - 1 - jnp.arange(t_n))
    t_r = qr[..., None] * bb_r[None] - qi[..., None] * bb_i[None]
    t_i = qr[..., None] * bb_i[None] + qi[..., None] * bb_r[None]
    kern = (jnp.einsum("gcp,dgpk->gcdk", cr, t_r, precision=lax.Precision.HIGHEST)
            - jnp.einsum("gcp,dgpk->gcdk", ci, t_i, precision=lax.Precision.HIGHEST))
    kern = kern.reshape(g_n, c_n, t_n * c_n)
    kern = jnp.concatenate([kern, jnp.zeros_like(kern)], axis=-1)
    mt = jnp.concatenate([kern[:, :, (t_n - 1 - t) * c_n:(2 * t_n - 1 - t) * c_n] for t in range(t_n)], axis=1)
    w_r = t_r.transpose(1, 2, 0, 3).reshape(g_n, p_n, t_n * c_n)
    w_i = t_i.transpose(1, 2, 0, 3).reshape(g_n, p_n, t_n * c_n)
    pr1, pi1 = power(1 + jnp.arange(t_n))
    pr1 = pr1.transpose(1, 0, 2)
    pi1 = pi1.transpose(1, 0, 2)
    v_r = (cr[:, None] * pr1[:, :, None, :] - ci[:, None] * pi1[:, :, None, :]).reshape(g_n, t_n * c_n, p_n)
    v_i = -(cr[:, None] * pi1[:, :, None, :] + ci[:, None] * pr1[:, :, None, :]).reshape(g_n, t_n * c_n, p_n)

    gp = g_n // 2
    z = jnp.zeros((gp, p_n, t_n * c_n), f32)
    w_r = w_r.reshape(gp, 2, p_n, t_n * c_n)
    w_i = w_i.reshape(gp, 2, p_n, t_n * c_n)
    w0 = jnp.concatenate([w_r[:, 0], z, w_i[:, 0], z], axis=1)
    w1 = jnp.concatenate([z, w_r[:, 1], z, w_i[:, 1]], axis=1)
    w_pack = jnp.stack([w0, w1], axis=1)
    zv = jnp.zeros((gp, t_n * c_n, p_n), f32)
    v_r = v_r.reshape(gp, 2, t_n * c_n, p_n)
    v_i = v_i.reshape(gp, 2, t_n * c_n, p_n)
    v_pack = jnp.concatenate([
        jnp.concatenate([v_r[:, 0], zv, v_i[:, 0], zv], axis=-1),
        jnp.concatenate([zv, v_r[:, 1], zv, v_i[:, 1]], axis=-1)], axis=1)
    n_rounds = (n_chunks - 1).bit_length()
    sr, si = power(t_n * (2 ** jnp.arange(max(n_rounds, 1))))
    pw = jnp.stack([sr.reshape(-1, gp, 2 * p_n), si.reshape(-1, gp, 2 * p_n)], axis=2)
    reach = (jnp.arange(LANES)[None, :] >= (2 ** jnp.arange(pw.shape[0]))[:, None]).astype(f32)
    pw = pw.transpose(1, 0, 2, 3)[..., None] * reach[None, :, None, None, :]
    return mt.reshape(gp, 2, t_n * c_n, t_n * c_n).astype(bf16), w_pack.astype(bf16), v_pack.astype(bf16), pw


def _ssm(ut, tables, layer):
    mt, w_pack, v_pack, pw = tables
    n_rounds = pw.shape[2]
    g_n, k, cols = ut.shape
    gp = g_n // 2
    return pl.pallas_call(
        functools.partial(_ssm_kernel, n_rounds),
        grid=(gp,),
        in_specs=[pl.BlockSpec((2, k, cols), lambda g: (g, 0, 0)),
                  pl.BlockSpec((None, None, 2, k, k), lambda g: (layer, g, 0, 0, 0)),
                  pl.BlockSpec((None, None, 2, k, k), lambda g: (layer, g, 0, 0, 0)),
                  pl.BlockSpec((None, None, 2 * k, k), lambda g: (layer, g, 0, 0)),
                  pl.BlockSpec((None, None) + pw.shape[2:], lambda g: (layer, g, 0, 0, 0, 0))],
        out_specs=pl.BlockSpec((2, k, cols), lambda g: (g, 0, 0)),
        out_shape=jax.ShapeDtypeStruct((g_n, k, cols), bf16),
        scratch_shapes=[pltpu.VMEM((k, cols), bf16)],
        compiler_params=_cparams("parallel"),
        name="ssm",
    )(ut, mt, w_pack, v_pack, pw)


def _glu_kernel(x_ref, y_ref, u_ref, d_ref, w_ref, o_ref):
    n_chunks, nt, dm = x_ref.shape
    g_n = y_ref.shape[0]
    per = nt // GLU_SPLIT
    upd = []
    for s in range(GLU_SPLIT):
        cols = []
        for tt in range(s * per, (s + 1) * per):
            sl = slice(tt * SSM_GROUP_DIM, (tt + 1) * SSM_GROUP_DIM)
            y = y_ref[:, sl, :].astype(f32) + d_ref[...] * u_ref[:, sl, :].astype(f32)
            cols.append(_gelu_tanh(y).reshape(g_n * SSM_GROUP_DIM, n_chunks))
        gt = jnp.concatenate(cols, axis=1)
        r = jnp.dot(gt.T.astype(bf16), w_ref[...], preferred_element_type=f32)
        upd.append(r[:, :dm] * jax.nn.sigmoid(r[:, dm:]))
    upd = jnp.concatenate(upd, axis=0).reshape(nt, n_chunks, dm)
    o_ref[...] = x_ref[...] + pltpu.einshape("tkd->ktd", upd)


def _glu_res(x3, yt, ut, d_skip, w_glu):
    bsz, seqlen, d = x3.shape
    g_n, _, cols = yt.shape
    n_chunks = cols // bsz
    d_tab = jnp.broadcast_to(d_skip.astype(f32).reshape(g_n, SSM_GROUP_DIM, 1), (g_n, SSM_GROUP_DIM, n_chunks))
    feat = pl.BlockSpec((g_n, SUBLANES * SSM_GROUP_DIM, n_chunks), lambda b, t: (0, t, b))
    x5 = _by_chunk(x3)
    out = pl.pallas_call(
        _glu_kernel,
        grid=(bsz, SSM_T // SUBLANES),
        in_specs=[_chunk_spec(x5), feat, feat,
                  pl.BlockSpec(d_tab.shape, lambda b, t: (0, 0, 0)),
                  pl.BlockSpec(w_glu.shape, lambda b, t: (0, 0))],
        out_specs=_chunk_spec(x5),
        out_shape=jax.ShapeDtypeStruct(x5.shape, f32),
        compiler_params=_cparams("parallel", "parallel"),
        name="glu_res",
    )(x5, yt, ut, d_tab, w_glu)
    return out.reshape(bsz, seqlen, d)


def _attn_kernel(tq, tk, out_scale, lam_ref, far_ref, q_ref, k_ref, vt_ref, diag_ref, corner_ref, sub_ref,
                 o_ref, qm_ref, m_ref, acc_ref, s_ref):
    hd2 = 2 * HEAD_DIM
    heads = range(vt_ref.shape[1])
    lane = lax.broadcasted_iota(jnp.int32, (tq, hd2), 1)
    far = [far_ref[pl.program_id(1) * len(heads) + hh] for hh in heads]

    def q_block(qi, carry):
        rows = pl.ds(pl.multiple_of(qi * tq, tq), tq)
        for hh in heads:
            q = q_ref[0, rows, hh * hd2:(hh + 1) * hd2].astype(f32) * (HEAD_DIM ** -0.5 * LOG2E)
            qm_ref[hh, :tq] = jnp.where(lane < HEAD_DIM, q, 0.0).astype(bf16)
            qm_ref[hh, tq:] = jnp.where(lane >= HEAD_DIM, q, 0.0).astype(bf16)
        m_ref[...] = jnp.full_like(m_ref, NEG_INF)
        acc_ref[...] = jnp.zeros_like(acc_ref)

        def fill(ki, n_blocks):
            for hh in heads:
                k = k_ref[0, pl.ds(pl.multiple_of(ki * tk, tk), n_blocks * tk), hh * hd2:(hh + 1) * hd2]
                s_ref[hh, :n_blocks * tk] = lax.dot_general(k, qm_ref[hh], (((1,), (1,)), ((), ())),
                                                            preferred_element_type=f32)

        def accumulate(hh, ki, parts):
            s = parts[0] if len(parts) == 1 else jnp.concatenate(parts, axis=0)
            vt = vt_ref[0, hh, :, pl.ds(pl.multiple_of(ki * tk, tk), len(parts) * tk)]
            m_prev = m_ref[hh]
            m_new = jnp.maximum(m_prev, jnp.max(s, axis=0, keepdims=True) + far[hh])
            p = jnp.exp2(s - (m_new - far[hh]))
            acc_ref[hh] = (jnp.exp2(m_prev - m_new) * acc_ref[hh]
                           + jnp.dot(vt, p.astype(bf16), preferred_element_type=f32))
            m_ref[hh] = m_new

        def neighbour(hh, s, near):
            c = near * corner_ref[hh]
            lo = s[tk - CORNER:]
            lo = jnp.concatenate([lo[:, :CORNER] + c, lo[:, CORNER:tq], lo[:, tq:tq + CORNER] + c,
                                  lo[:, tq + CORNER:]], axis=1)
            return jnp.concatenate([s[:tk - CORNER], lo], axis=0)

        def diagonal(hh, s):
            b = diag_ref[hh]
            return s + jnp.concatenate([b, b], axis=1)

        grp = ATTN_GROUP

        def block(hh, j):
            return s_ref[hh, j * tk:(j + 1) * tk]

        def group(i, carry):
            fill(grp * i, grp)
            near = jnp.where(grp * i + grp - 1 == qi - 1, 1.0, 0.0)
            for hh in heads:
                accumulate(hh, grp * i, [block(hh, j) for j in range(grp - 1)]
                           + [neighbour(hh, block(hh, grp - 1), near)])
            return carry

        lax.fori_loop(0, qi // grp, group, 0)

        for r in range(grp):
            @pl.when(qi % grp == r)
            def _(r=r):
                fill(qi - r, r + 1)
                for hh in heads:
                    parts = [block(hh, j) for j in range(r)]
                    if r:
                        parts[-1] = neighbour(hh, parts[-1], 1.0)
                    accumulate(hh, qi - r, parts + [diagonal(hh, block(hh, r))])

        for hh in heads:
            acc = acc_ref[hh]
            ot = acc[:hd2] / acc[hd2:hd2 + 1]
            ot = ot[:, :tq] - lam_ref[0] * ot[:, tq:]
            ot = ot * lax.rsqrt(jnp.mean(ot * ot, axis=0, keepdims=True) + EPS)
            o_ref[0, rows, hh * hd2:(hh + 1) * hd2] = (ot.T * sub_ref[...] * out_scale).astype(o_ref.dtype)
        return carry

    lax.fori_loop(0, k_ref.shape[1] // tq, q_block, 0)


def _bias_tables(rel_bias, tq, tk):
    n = jnp.arange(2 * tq)
    nf = jnp.maximum(n, REL_MAX_EXACT).astype(f32)
    large = REL_MAX_EXACT + (jnp.log(nf / REL_MAX_EXACT) / math.log(REL_MAX_DIST / REL_MAX_EXACT)
                             * (REL_BUCKETS - REL_MAX_EXACT)).astype(jnp.int32)
    large = jnp.minimum(large, REL_BUCKETS - 1)
    bucket = jnp.where(n < REL_MAX_EXACT, n, large)
    onehot = (bucket[:, None] == jnp.arange(REL_BUCKETS)[None, :]).astype(f32)
    by_dist = jnp.dot(onehot, rel_bias.astype(f32), precision=lax.Precision.HIGHEST).T * LOG2E
    n_heads = by_dist.shape[0]
    neg = jnp.full((n_heads, tk), NEG_INF, f32)
    period = tq + tk
    tiles = []
    for d, tail in ((0, neg), (1, by_dist[:, tq - tk:tq])):
        r = jnp.concatenate([by_dist[:, d * tq:(d + 1) * tq], tail], axis=1)
        t = jnp.tile(r, (1, tk))[:, :tk * (period - 1)].reshape(n_heads, tk, period - 1)
        tiles.append(t[:, :, :tq])
    far = rel_bias.astype(f32)[REL_BUCKETS - 1] * LOG2E
    corner = tiles[1][:, tk - CORNER:, :CORNER] - far[:, None, None]
    return tiles[0] - far[:, None, None], corner, far


def _far_bucket_from():
    n = REL_MAX_EXACT
    while REL_MAX_EXACT + int(math.log(n / REL_MAX_EXACT) / math.log(REL_MAX_DIST / REL_MAX_EXACT)
                              * (REL_BUCKETS - REL_MAX_EXACT)) < REL_BUCKETS - 1:
        n += 1
    return n


def _diff_attention(q, k, vt, diag, corner, far, lam, subln, out_scale, bsz, seqlen, tq, tk):
    assert tq == tk and tq - CORNER + 1 >= _far_bucket_from()
    hd2 = 2 * HEAD_DIM
    q3 = q.reshape(bsz, seqlen, N_HEADS * hd2)
    k3 = k.reshape(bsz, seqlen, N_HEADS * hd2)
    smem = pl.BlockSpec(memory_space=pltpu.SMEM)
    hps = ATTN_HEADS_PER_STEP
    o = pl.pallas_call(
        functools.partial(_attn_kernel, tq, tk, out_scale),
        grid=(bsz, N_HEADS // hps),
        in_specs=[smem, smem,
                  pl.BlockSpec((1, seqlen, hps * hd2), lambda b, h: (b, 0, h)),
                  pl.BlockSpec((1, seqlen, hps * hd2), lambda b, h: (b, 0, h)),
                  pl.BlockSpec((1, hps, V_ROWS, seqlen), lambda b, h: (b, h, 0, 0)),
                  pl.BlockSpec((hps, tk, tq), lambda b, h: (h, 0, 0)),
                  pl.BlockSpec((hps, CORNER, CORNER), lambda b, h: (h, 0, 0)),
                  pl.BlockSpec((1, hd2), lambda b, h: (0, 0))],
        out_specs=pl.BlockSpec((1, seqlen, hps * hd2), lambda b, h: (b, 0, h)),
        out_shape=jax.ShapeDtypeStruct((bsz, seqlen, N_HEADS * hd2), bf16),
        scratch_shapes=[pltpu.VMEM((hps, 2 * tq, hd2), bf16), pltpu.VMEM((hps, 1, 2 * tq), f32),
                        pltpu.VMEM((hps, V_ROWS, 2 * tq), f32), pltpu.VMEM((hps, ATTN_GROUP * tk, 2 * tq), f32)],
        compiler_params=_cparams("parallel", "parallel"),
        name="diff_attn",
    )(lam.reshape(1).astype(f32), far, q3, k3, vt, diag, corner, subln.astype(f32)[None, :])
    return o.reshape(bsz * seqlen, N_HEADS * hd2)


def _ffn_ple_kernel(tiles_per_seq, has_attn, has_q, has_kv, *refs):
    refs = list(refs)
    x_ref, p_ref = refs[:2]
    del refs[:2]
    if has_attn:
        a_ref, wo_ref = refs[:2]
        del refs[:2]
    fg_ref, wup_ref, cw_ref, cb_ref, wdn_ref, pg_ref, wgate_ref, wproj_ref = refs[:8]
    del refs[:8]
    if has_q or has_kv:
        seg_ref = refs.pop(0)
    if has_q:
        qg_ref, wq_ref, qgain_ref = refs[:3]
        del refs[:3]
    if has_kv:
        kvg_ref, wk_ref, wvt_ref, kgain_ref = refs[:4]
        del refs[:4]
    o_ref = refs.pop(0)
    if has_q:
        q_ref = refs.pop(0)
    if has_kv:
        k_ref, vt_ref = refs[:2]
        del refs[:2]
    carry_ref, act_ref, h_ref = refs

    i = pl.program_id(0)
    tm = x_ref.shape[0]
    d_ff = wdn_ref.shape[0]
    fc = FFN_COLS
    halo = SUBLANES

    @pl.when(i % tiles_per_seq == 0)
    def _():
        carry_ref[...] = jnp.zeros_like(carry_ref)

    x = x_ref[...]
    if has_attn:
        x = x + jnp.dot(a_ref[...], wo_ref[...], preferred_element_type=f32)
    h_ref[...] = _rms(x, fg_ref[...]).astype(bf16)
    row = lax.broadcasted_iota(jnp.int32, (halo, fc), 0)

    def up_proj(col):
        return jnp.dot(h_ref[...], wup_ref[:, col:col + fc], preferred_element_type=f32)

    def conv(col, up):
        prev = carry_ref[:, col:col + fc]
        carry_ref[:, col:col + fc] = up[tm - halo:]
        c = cb_ref[:, col:col + fc] + cw_ref[2:3, col:col + fc] * up
        for back in range(1, CONV_WIDTH):
            r = pltpu.roll(up, back, 0)
            head = jnp.where(row < back, pltpu.roll(prev, back, 0), r[:halo])
            c = c + cw_ref[2 - back:3 - back, col:col + fc] * jnp.concatenate([head, r[halo:]], axis=0)
        return c

    cols = list(range(0, d_ff, fc))
    ups = (up_proj(cols[0]), up_proj(d_ff + cols[0]))
    for idx, col in enumerate(cols):
        nxt = (up_proj(cols[idx + 1]), up_proj(d_ff + cols[idx + 1])) if idx + 1 < len(cols) else None
        act_ref[:, col:col + fc] = (_gelu_tanh(conv(col, ups[0])) * conv(d_ff + col, ups[1])).astype(bf16)
        ups = nxt
    halves = [slice(0, tm // 2), slice(tm // 2, tm)]
    xs = [x[r] + jnp.dot(act_ref[r, :], wdn_ref[...], preferred_element_type=f32) for r in halves]
    projs = [jnp.dot(p_ref[r, :].astype(bf16), wproj_ref[...], preferred_element_type=f32) for r in halves]
    gates = [jax.nn.sigmoid(jnp.dot(_rms(xh, pg_ref[...]).astype(bf16), wgate_ref[...],
                                    preferred_element_type=f32)) for xh in xs]
    xs = [xh + g * pr for xh, g, pr in zip(xs, gates, projs)]
    for r, xh in zip(halves, xs):
        o_ref[r, :] = xh
    if has_q or has_kv:
        xn = jnp.concatenate([xh * lax.rsqrt(jnp.mean(xh * xh, axis=-1, keepdims=True) + EPS) for xh in xs],
                             axis=0)
    if has_q:
        q = jnp.dot((xn * qg_ref[...]).astype(bf16), wq_ref[...], preferred_element_type=f32)
        q_ref[...] = _seg_norm(q, seg_ref, qgain_ref).astype(q_ref.dtype)
    if has_kv:
        hk = (xn * kvg_ref[...]).astype(bf16)
        k = jnp.dot(hk, wk_ref[...], preferred_element_type=f32)
        k_ref[...] = _seg_norm(k, seg_ref, kgain_ref).astype(k_ref.dtype)
        _store_vt(vt_ref, lax.dot_general(wvt_ref[...], hk, (((1,), (1,)), ((), ())),
                                          preferred_element_type=f32))


def _ffn_ple(x, p, layer, ffn, ple, seqlen, attn=None, next_q=None, next_kv=None):
    n, d = x.shape
    ffn_g, w_up, conv_w, conv_b, w_down = ffn
    ple_g, w_gate, w_proj = ple
    d_ff = w_down.shape[-2]
    tm = FFN_ROWS
    bsz = n // seqlen
    tps = seqlen // tm
    row = lambda i: (i, 0)

    def full(a):
        if a.ndim == 3:
            return pl.BlockSpec((None,) + a.shape[1:], lambda i: (layer, 0, 0), pipeline_mode=pl.Buffered(1))
        return pl.BlockSpec(a.shape, lambda i: (0, 0), pipeline_mode=pl.Buffered(1))

    vec = lambda a: a.astype(f32)[None, :]
    args = [x, p]
    in_specs = [pl.BlockSpec((tm, d), row), pl.BlockSpec((None, tm, p.shape[2]), lambda i: (layer, i, 0))]
    if attn is not None:
        args += [attn[0], attn[1]]
        in_specs += [pl.BlockSpec((tm, attn[0].shape[1]), row), full(attn[1])]
    weights = [vec(ffn_g), w_up, conv_w.astype(f32), vec(conv_b), w_down, vec(ple_g), w_gate, w_proj]
    out_specs = [pl.BlockSpec((tm, d), row)]
    out_shape = [jax.ShapeDtypeStruct((n, d), f32)]
    if next_q is not None or next_kv is not None:
        weights.append(_segment_ones())
    if next_q is not None:
        weights += [vec(next_q[0]), next_q[1], _segment_gain(next_q[2])]
        out_specs.append(pl.BlockSpec((tm, next_q[1].shape[1]), row))
        out_shape.append(jax.ShapeDtypeStruct((n, next_q[1].shape[1]), bf16))
    if next_kv is not None:
        weights += [vec(next_kv[0]), next_kv[1], next_kv[2], _segment_gain(next_kv[3])]
        out_specs += [pl.BlockSpec((tm, next_kv[1].shape[1]), row),
                      pl.BlockSpec((1, N_HEADS, V_ROWS, tm), lambda i: (i // tps, 0, 0, i % tps))]
        out_shape += [jax.ShapeDtypeStruct((n, next_kv[1].shape[1]), bf16),
                      jax.ShapeDtypeStruct((bsz, N_HEADS, V_ROWS, seqlen), bf16)]
    args += weights
    in_specs += [full(w) for w in weights]
    return pl.pallas_call(
        functools.partial(_ffn_ple_kernel, tps, attn is not None, next_q is not None, next_kv is not None),
        grid=(n // tm,),
        in_specs=in_specs,
        out_specs=out_specs,
        out_shape=out_shape,
        scratch_shapes=[pltpu.VMEM((SUBLANES, 2 * d_ff), f32),
                        pltpu.VMEM((tm, d_ff), bf16),
                        pltpu.VMEM((tm, d), bf16)],
        compiler_params=_cparams("arbitrary"),
        name="ffn_ple",
    )(*args)


def kernel(x, p, ssm_norm, ssm_w_in, ssm_log_dt, ssm_lambda_re, ssm_lambda_im, ssm_b_re, ssm_b_im, ssm_c_re, ssm_c_im, ssm_d, ssm_w_glu, kv_norm, kv_w, k_norm, attn_norm, attn_w_q, q_norm, lambda_q1, lambda_k1, lambda_q2, lambda_k2, subln, attn_w_o, rel_bias, ffn_norm, ffn_w_up, ffn_conv_w, ffn_conv_b, ffn_w_down, ple_norm, ple_w_gate, ple_w_proj):
    bsz, seqlen, d = x.shape
    depth = p.shape[0]
    n_a = ssm_norm.shape[0]
    n = bsz * seqlen
    x = x.reshape(n, d).astype(f32)
    p = p.reshape(depth, n, p.shape[-1])
    nk = N_HEADS * 2 * HEAD_DIM
    assert 1 <= n_a < depth
    diag, corner, far = _bias_tables(rel_bias, ATTN_BLOCK, ATTN_BLOCK)
    tables = jax.vmap(functools.partial(_ssm_tables, n_chunks=seqlen // SSM_T))(
        ssm_log_dt, ssm_lambda_re, ssm_lambda_im, ssm_b_re, ssm_b_im, ssm_c_re, ssm_c_im)
    w_in_t = jnp.swapaxes(ssm_w_in, 1, 2).astype(bf16)
    w_glu, w_q, w_o = ssm_w_glu.astype(bf16), attn_w_q.astype(bf16), attn_w_o.astype(bf16)
    w_up, w_down = ffn_w_up.astype(bf16), ffn_w_down.astype(bf16)
    w_gate, w_proj = ple_w_gate.astype(bf16), ple_w_proj.astype(bf16)
    lam_all = (jnp.exp(jnp.sum(lambda_q1.astype(f32) * lambda_k1.astype(f32), axis=-1))
               - jnp.exp(jnp.sum(lambda_q2.astype(f32) * lambda_k2.astype(f32), axis=-1)))
    q = k = vt = None
    for i in range(depth):
        attn = None
        if i < n_a:
            x3 = x.reshape(bsz, seqlen, d)
            ut = _win_t(x3, ssm_norm[i], w_in_t[i])
            yt = _ssm(ut, tables, i)
            x = _glu_res(x3, yt, ut, ssm_d[i], w_glu[i]).reshape(n, d)
        else:
            j = i - n_a
            lam_init = 0.8 - 0.6 * math.exp(-0.3 * i)
            o = _diff_attention(q, k, vt, diag, corner, far, lam_all[j] + lam_init, subln[j], 1.0 - lam_init,
                                bsz, seqlen, ATTN_BLOCK, ATTN_BLOCK)
            attn = (o, w_o[j])
        j_next = i + 1 - n_a
        next_q = next_kv = None
        if 0 <= j_next < depth - n_a:
            next_q = (attn_norm[j_next], w_q[j_next], q_norm[j_next])
            if j_next == 0:
                next_kv = (kv_norm, kv_w[:, :nk].astype(bf16), kv_w[:, nk:].T.astype(bf16), k_norm)
        outs = _ffn_ple(x, p, i,
                        (ffn_norm[i], w_up, ffn_conv_w[i], ffn_conv_b[i], w_down),
                        (ple_norm[i], w_gate, w_proj),
                        seqlen, attn=attn, next_q=next_q, next_kv=next_kv)
        x = outs[0]
        if next_q is not None:
            q = outs[1]
        if next_kv is not None:
            k, vt = outs[2], outs[3]
    return x.reshape(bsz, seqlen, d)
```
